```python
import math
import jax
import jax.numpy as jnp
from jax import lax
import numpy as np

D_MODEL = 1024
BATCH = 2
SEQ = 8192
DEPTH = 4
DEC_BATCH = 128
DEC_SEQ = 8
PAST_LEN = 2048
PAGE_SIZE = 128

N_A_LAYERS = DEPTH // 2
N_B_LAYERS = DEPTH - N_A_LAYERS
SSM_EXPAND = 2
D_INNER = SSM_EXPAND * D_MODEL
SSM_HEAD_DIM = 64
SSM_HEADS = D_INNER // SSM_HEAD_DIM
SSM_GROUPS = 4
SSM_HEADS_PER_GROUP = SSM_HEADS // SSM_GROUPS
D_STATE = 128
CONV_WIDTH = 4
CONV_DIM = D_INNER + 2 * SSM_GROUPS * D_STATE
SSD_CHUNK = 128
N_HEADS = 16
HEAD_DIM = D_MODEL // N_HEADS
N_KV_HEADS = 8
GQA_GROUP = N_HEADS // N_KV_HEADS
MOBA_BLOCK = 256
MOBA_TOPK = 3
Q_CHUNK = 32
N_EXPERT_GROUPS = 4
EXPERTS_PER_GROUP = 4
N_EXPERTS = N_EXPERT_GROUPS * EXPERTS_PER_GROUP
D_EXPERT = 256
EXPERT_TOPK = 2
EPS = 1e-6

kernel_name = 'yoco_ssd_moba_hmoe_step'


def rms_normalize(x):
    xf = x.astype(jnp.float32)
    return (xf * lax.rsqrt(jnp.mean(xf * xf, axis=-1, keepdims=True) + EPS)).astype(x.dtype)


def rms_norm(x, w):
    return rms_normalize(x) * w


def alibi_slopes():
    return jnp.exp2(-8.0 * jnp.arange(1, N_HEADS + 1, dtype=jnp.float32) / N_HEADS)


def causal_depthwise_conv(xbc, buf, conv_w, conv_b):
    T = xbc.shape[1]
    xpad = jnp.concatenate([buf.astype(xbc.dtype), xbc], axis=1)
    out = conv_b + xpad[:, 0:T] * conv_w[0]
    for k in range(1, CONV_WIDTH):
        out = out + xpad[:, k:k + T] * conv_w[k]
    return out, xpad[:, T:]


def ssd_chunked(x, dt, A, Bm, Cm, h0):
    b, T, H, P = x.shape
    L = min(SSD_CHUNK, T)
    nc = T // L
    G, R = SSM_GROUPS, SSM_HEADS_PER_GROUP
    xs = x.astype(jnp.float32).reshape(b, nc, L, G, R, P)
    dts = dt.reshape(b, nc, L, G, R)
    Bc = Bm.astype(jnp.float32).reshape(b, nc, L, G, D_STATE)
    Cc = Cm.astype(jnp.float32).reshape(b, nc, L, G, D_STATE)
    a_cum = jnp.cumsum(dts * A.reshape(G, R), axis=2)
    causal = jnp.tril(jnp.ones((L, L), dtype=bool))[:, :, None, None]
    seg = a_cum[:, :, :, None] - a_cum[:, :, None, :]
    decay = jnp.exp(jnp.where(causal, seg, -jnp.inf))
    xdt = xs * dts[..., None]
    cb = jnp.einsum('bclgn,bcsgn->bclsg', Cc, Bc)
    y_diag = jnp.einsum('bclsg,bclsgr,bcsgrp->bclgrp', cb, decay, xdt)
    decay_end = jnp.exp(a_cum[:, :, -1:] - a_cum)
    states = jnp.einsum('bclgn,bclgr,bclgrp->bcgrpn', Bc, decay_end, xdt)
    chunk_decay = jnp.exp(a_cum[:, :, -1])

    def step(h, inp):
        s, d = inp
        return h * d[..., None, None] + s, h

    h_init = h0.astype(jnp.float32).reshape(b, G, R, P, D_STATE)
    h_final, h_prev = lax.scan(step, h_init, (jnp.moveaxis(states, 1, 0), jnp.moveaxis(chunk_decay, 1, 0)))
    h_prev = jnp.moveaxis(h_prev, 0, 1)
    y_off = jnp.einsum('bclgn,bcgrpn,bclgr->bclgrp', Cc, h_prev, jnp.exp(a_cum))
    y = (y_diag + y_off).reshape(b, T, H, P).astype(x.dtype)
    return y, h_final.reshape(b, H, P, D_STATE).astype(h0.dtype)


def mamba_mixer(u, conv_buf, ssm_state, w_in, conv_w, conv_b, dt_bias, a_log, d_skip, norm_w, w_out):
    b, T, _ = u.shape
    proj = u @ w_in
    z, xbc, dt_raw = jnp.split(proj, [D_INNER, D_INNER + CONV_DIM], axis=-1)
    xbc, new_buf = causal_depthwise_conv(xbc, conv_buf, conv_w, conv_b)
    xbc = jax.nn.silu(xbc)
    xs, Bm, Cm = jnp.split(xbc, [D_INNER, D_INNER + SSM_GROUPS * D_STATE], axis=-1)
    xs = xs.reshape(b, T, SSM_HEADS, SSM_HEAD_DIM)
    Bm = Bm.reshape(b, T, SSM_GROUPS, D_STATE)
    Cm = Cm.reshape(b, T, SSM_GROUPS, D_STATE)
    dt = jax.nn.softplus((dt_raw + dt_bias).astype(jnp.float32))
    A = -jnp.exp(a_log.astype(jnp.float32))
    y, new_state = ssd_chunked(xs, dt, A, Bm, Cm, ssm_state)
    y = (y + d_skip[:, None] * xs).reshape(b, T, D_INNER)
    g = (y * jax.nn.silu(z)).reshape(b, T, SSM_GROUPS, D_INNER // SSM_GROUPS)
    g = rms_normalize(g).reshape(b, T, D_INNER) * norm_w
    return g @ w_out, new_buf, new_state


def hmoe(x, w_grp, b_grp, w_exp, b_exp, w1, w3, w2):
    grp_logits = (x @ w_grp + b_grp).astype(jnp.float32)
    grp_idx = jnp.argmax(grp_logits, axis=-1)
    p_grp = jnp.take_along_axis(jax.nn.softmax(grp_logits, axis=-1), grp_idx[..., None], axis=-1)
    exp_logits = (x @ w_exp + b_exp).astype(jnp.float32)
    exp_logits = exp_logits.reshape(exp_logits.shape[:-1] + (N_EXPERT_GROUPS, EXPERTS_PER_GROUP))
    exp_logits = jnp.take_along_axis(exp_logits, grp_idx[..., None, None], axis=-2)[..., 0, :]
    top_v, top_i = lax.top_k(exp_logits, EXPERT_TOPK)
    top_w = jax.nn.softmax(top_v, axis=-1) * p_grp
    expert_id = grp_idx[..., None] * EXPERTS_PER_GROUP + top_i
    gates = jnp.einsum('btk,btke->bte', top_w,
                       jax.nn.one_hot(expert_id, N_EXPERTS, dtype=jnp.float32)).astype(x.dtype)
    act = jax.nn.silu(jnp.einsum('btd,edf->btef', x, w1)) * jnp.einsum('btd,edf->btef', x, w3)
    return jnp.einsum('btef,bte,efd->btd', act, gates, w2)


def shared_kv(h, c_act, kv_w_mod, kv_b_mod, kv_norm_w, w_k, w_v, k_norm_w):
    b, T, _ = h.shape
    shift, scale = jnp.split((c_act @ kv_w_mod + kv_b_mod)[:, None, :], 2, axis=-1)
    u = rms_norm(h, kv_norm_w) * (1 + scale) + shift
    k = rms_norm((u @ w_k).reshape(b, T, N_KV_HEADS, HEAD_DIM), k_norm_w)
    v = (u @ w_v).reshape(b, T, N_KV_HEADS, HEAD_DIM)
    return k, v


def moba_blocks(k_new, v_new, past_k, past_v):
    if past_k is None:
        k_all, v_all = k_new, v_new
    else:
        k_all = jnp.concatenate([past_k.astype(k_new.dtype), k_new], axis=1)
        v_all = jnp.concatenate([past_v.astype(v_new.dtype), v_new], axis=1)
    b, L = k_all.shape[0], k_all.shape[1]
    nb = -(-L // MOBA_BLOCK)
    pad = ((0, 0), (0, nb * MOBA_BLOCK - L), (0, 0), (0, 0))
    kb = jnp.pad(k_all, pad).reshape(b, nb, MOBA_BLOCK, N_KV_HEADS, HEAD_DIM)
    vb = jnp.pad(v_all, pad).reshape(b, nb, MOBA_BLOCK, N_KV_HEADS, HEAD_DIM)
    kmean = jnp.mean(kb.astype(jnp.float32), axis=2)
    return kb, vb, kmean


def moba_attend(q, kb, vb, kmean, q_pos0):
    T = q.shape[0]
    nb = kb.shape[0]
    qc = Q_CHUNK if T % Q_CHUNK == 0 else T
    n_sel = min(MOBA_TOPK, nb)
    kv_head = jnp.arange(N_HEADS) // GQA_GROUP
    kmean_h = kmean[:, kv_head]
    slopes = alibi_slopes()
    blk_ids = jnp.arange(nb)
    offs = jnp.arange(MOBA_BLOCK)
    scale = HEAD_DIM ** -0.5

    def attend_chunk(args):
        qq, pos = args
        own = pos // MOBA_BLOCK
        gate = jnp.einsum('thd,nhd->thn', qq.astype(jnp.float32), kmean_h)
        is_past = blk_ids[None, :] < own[:, None]
        gate = jnp.where(is_past[:, None, :], gate, -jnp.inf)
        _, sel = lax.top_k(gate, n_sel)
        sel_ok = jnp.arange(n_sel)[None, :] < own[:, None]
        k_sel = kb[sel, :, kv_head[None, :, None]]
        v_sel = vb[sel, :, kv_head[None, :, None]]
        s_sel = jnp.einsum('thd,thnjd->thnj', qq, k_sel).astype(jnp.float32) * scale
        dist_sel = pos[:, None, None, None] - (sel[..., None] * MOBA_BLOCK + offs)
        s_sel = s_sel - slopes[None, :, None, None] * dist_sel
        s_sel = jnp.where(sel_ok[:, None, :, None], s_sel, -jnp.inf)
        k_own = kb[own][:, :, kv_head]
        v_own = vb[own][:, :, kv_head]
        s_own = jnp.einsum('thd,tjhd->thj', qq, k_own).astype(jnp.float32) * scale
        dist_own = pos[:, None] - (own[:, None] * MOBA_BLOCK + offs)
        s_own = s_own - slopes[None, :, None] * dist_own[:, None, :]
        s_own = jnp.where((dist_own >= 0)[:, None, :], s_own, -jnp.inf)
        s = jnp.concatenate([s_sel.reshape(qc, N_HEADS, n_sel * MOBA_BLOCK), s_own], axis=-1)
        p = jax.nn.softmax(s, axis=-1).astype(qq.dtype)
        p_sel = p[..., :n_sel * MOBA_BLOCK].reshape(qc, N_HEADS, n_sel, MOBA_BLOCK)
        p_own = p[..., n_sel * MOBA_BLOCK:]
        return (jnp.einsum('thnj,thnjd->thd', p_sel, v_sel)
                + jnp.einsum('thj,tjhd->thd', p_own, v_own))

    qs = q.reshape(T // qc, qc, N_HEADS, HEAD_DIM)
    ps = (q_pos0 + jnp.arange(T)).reshape(T // qc, qc)
    return lax.map(attend_chunk, (qs, ps)).reshape(T, N_HEADS, HEAD_DIM)


def moba_mixer(u, kb, vb, kmean, q_pos0, w_q, qn_w, w_o):
    b, T, _ = u.shape
    q = rms_norm((u @ w_q).reshape(b, T, N_HEADS, HEAD_DIM), qn_w)
    o = lax.map(lambda a: moba_attend(a[0], a[1], a[2], a[3], q_pos0), (q, kb, vb, kmean))
    return o.reshape(b, T, N_HEADS * HEAD_DIM) @ w_o


def trunk(x, c, conv0, ssm0, past_k, past_v, q_pos0,
          w_mod, b_mod, norm_mix_w, norm_ffn_w,
          ssm_w_in, ssm_conv_w, ssm_conv_b, ssm_dt_bias, ssm_a_log, ssm_d, ssm_norm_w, ssm_w_out,
          kv_w_mod, kv_b_mod, kv_norm_w, w_k, w_v, k_norm_w,
          attn_w_q, q_norm_w, attn_w_o,
          moe_w_grp, moe_b_grp, moe_w_exp, moe_b_exp, moe_w1, moe_w3, moe_w2):
    c_act = jax.nn.silu(c)
    h = x
    conv_out, ssm_out = [], []
    blocks = None
    k_new = v_new = None
    for layer in range(DEPTH):
        mod = (c_act @ w_mod[layer] + b_mod[layer])[:, None, :]
        sh_m, sc_m, g_m, sh_f, sc_f, g_f = jnp.split(mod, 6, axis=-1)
        u = rms_norm(h, norm_mix_w[layer]) * (1 + sc_m) + sh_m
        if layer < N_A_LAYERS:
            y, cb, st = mamba_mixer(u, conv0[layer], ssm0[layer], ssm_w_in[layer], ssm_conv_w[layer],
                                    ssm_conv_b[layer], ssm_dt_bias[layer], ssm_a_log[layer], ssm_d[layer],
                                    ssm_norm_w[layer], ssm_w_out[layer])
            conv_out.append(cb)
            ssm_out.append(st)
        else:
            j = layer - N_A_LAYERS
            y = moba_mixer(u, blocks[0], blocks[1], blocks[2], q_pos0, attn_w_q[j], q_norm_w[j], attn_w_o[j])
        h = h + g_m * y
        u = rms_norm(h, norm_ffn_w[layer]) * (1 + sc_f) + sh_f
        h = h + g_f * hmoe(u, moe_w_grp[layer], moe_b_grp[layer], moe_w_exp[layer], moe_b_exp[layer],
                           moe_w1[layer], moe_w3[layer], moe_w2[layer])
        if layer == N_A_LAYERS - 1:
            k_new, v_new = shared_kv(h, c_act, kv_w_mod, kv_b_mod, kv_norm_w, w_k, w_v, k_norm_w)
            blocks = moba_blocks(k_new, v_new, past_k, past_v)
    return h, jnp.stack(conv_out), jnp.stack(ssm_out), k_new, v_new


def setup_inputs(seed: int = 0) -> dict:
    key = jax.random.key(seed)
    ks = iter(jax.random.split(key, 48))

    def nrm(shape, s=1.0):
        return jax.random.normal(next(ks), shape, jnp.float32) * s

    n_pages = PAST_LEN // PAGE_SIZE
    n_used = DEC_BATCH * n_pages
    n_pool = n_used + max(1, n_used // 4)
    page_table = jax.random.permutation(next(ks), n_pool)[:n_used].reshape(DEC_BATCH, n_pages).astype(jnp.int32)
    dt0 = jnp.exp(jax.random.uniform(next(ks), (N_A_LAYERS, SSM_HEADS), jnp.float32,
                                     minval=math.log(1e-3), maxval=math.log(1e-1)))
    dt_bias = dt0 + jnp.log(-jnp.expm1(-dt0))
    a_log = jnp.log(jax.random.uniform(next(ks), (N_A_LAYERS, SSM_HEADS), jnp.float32, minval=1.0, maxval=16.0))
    d_in = D_MODEL ** -0.5
    return {
        'x_prompt': nrm((BATCH, SEQ, D_MODEL)),
        'x_sample': nrm((DEC_BATCH, DEC_SEQ, D_MODEL)),
        'state_conv': nrm((N_A_LAYERS, DEC_BATCH, CONV_WIDTH - 1, CONV_DIM)),
        'state_ssm': nrm((N_A_LAYERS, DEC_BATCH, SSM_HEADS, SSM_HEAD_DIM, D_STATE), 0.1),
        'cache_k': nrm((n_pool, PAGE_SIZE, N_KV_HEADS, HEAD_DIM)),
        'cache_v': nrm((n_pool, PAGE_SIZE, N_KV_HEADS, HEAD_DIM)),
        'page_table': page_table,
        'c_prompt': nrm((BATCH, D_MODEL)),
        'c_sample': nrm((DEC_BATCH, D_MODEL)),
        'w_mod': nrm((DEPTH, D_MODEL, 6 * D_MODEL), 0.5 * d_in),
        'b_mod': nrm((DEPTH, 6 * D_MODEL), 0.02),
        'norm_mix_w': 1.0 + nrm((DEPTH, D_MODEL), 0.02),
        'norm_ffn_w': 1.0 + nrm((DEPTH, D_MODEL), 0.02),
        'ssm_w_in': nrm((N_A_LAYERS, D_MODEL, D_INNER + CONV_DIM + SSM_HEADS), d_in),
        'ssm_conv_w': nrm((N_A_LAYERS, CONV_WIDTH, CONV_DIM), CONV_WIDTH ** -0.5),
        'ssm_conv_b': nrm((N_A_LAYERS, CONV_DIM), 0.02),
        'ssm_dt_bias': dt_bias,
        'ssm_a_log': a_log,
        'ssm_d': 1.0 + nrm((N_A_LAYERS, SSM_HEADS), 0.1),
        'ssm_norm_w': 1.0 + nrm((N_A_LAYERS, D_INNER), 0.02),
        'ssm_w_out': nrm((N_A_LAYERS, D_INNER, D_MODEL), D_INNER ** -0.5),
        'kv_w_mod': nrm((D_MODEL, 2 * D_MODEL), 0.5 * d_in),
        'kv_b_mod': nrm((2 * D_MODEL,), 0.02),
        'kv_norm_w': 1.0 + nrm((D_MODEL,), 0.02),
        'w_k': nrm((D_MODEL, N_KV_HEADS * HEAD_DIM), d_in),
        'w_v': nrm((D_MODEL, N_KV_HEADS * HEAD_DIM), d_in),
        'k_norm_w': 1.0 + nrm((HEAD_DIM,), 0.02),
        'attn_w_q': nrm((N_B_LAYERS, D_MODEL, N_HEADS * HEAD_DIM), d_in),
        'q_norm_w': 1.0 + nrm((N_B_LAYERS, HEAD_DIM), 0.02),
        'attn_w_o': nrm((N_B_LAYERS, N_HEADS * HEAD_DIM, D_MODEL), (N_HEADS * HEAD_DIM) ** -0.5),
        'moe_w_grp': nrm((DEPTH, D_MODEL, N_EXPERT_GROUPS), d_in),
        'moe_b_grp': nrm((DEPTH, N_EXPERT_GROUPS), 0.01),
        'moe_w_exp': nrm((DEPTH, D_MODEL, N_EXPERTS), d_in),
        'moe_b_exp': nrm((DEPTH, N_EXPERTS), 0.01),
        'moe_w1': nrm((DEPTH, N_EXPERTS, D_MODEL, D_EXPERT), d_in),
        'moe_w3': nrm((DEPTH, N_EXPERTS, D_MODEL, D_EXPERT), d_in),
        'moe_w2': nrm((DEPTH, N_EXPERTS, D_EXPERT, D_MODEL), D_EXPERT ** -0.5),
    }


def reference(x_prompt, x_sample, state_conv, state_ssm, cache_k, cache_v, page_table, c_prompt, c_sample,
              w_mod, b_mod, norm_mix_w, norm_ffn_w,
              ssm_w_in, ssm_conv_w, ssm_conv_b, ssm_dt_bias, ssm_a_log, ssm_d, ssm_norm_w, ssm_w_out,
              kv_w_mod, kv_b_mod, kv_norm_w, w_k, w_v, k_norm_w,
              attn_w_q, q_norm_w, attn_w_o,
              moe_w_grp, moe_b_grp, moe_w_exp, moe_b_exp, moe_w1, moe_w3, moe_w2):
    weights = (w_mod, b_mod, norm_mix_w, norm_ffn_w,
               ssm_w_in, ssm_conv_w, ssm_conv_b, ssm_dt_bias, ssm_a_log, ssm_d, ssm_norm_w, ssm_w_out,
               kv_w_mod, kv_b_mod, kv_norm_w, w_k, w_v, k_norm_w,
               attn_w_q, q_norm_w, attn_w_o,
               moe_w_grp, moe_b_grp, moe_w_exp, moe_b_exp, moe_w1, moe_w3, moe_w2)
    bp = x_prompt.shape[0]
    conv0_p = jnp.zeros((N_A_LAYERS, bp, CONV_WIDTH - 1, CONV_DIM), x_prompt.dtype)
    ssm0_p = jnp.zeros((N_A_LAYERS, bp, SSM_HEADS, SSM_HEAD_DIM, D_STATE), x_prompt.dtype)
    y_prompt, conv_prompt, ssm_prompt, k_prompt, v_prompt = trunk(
        x_prompt, c_prompt, conv0_p, ssm0_p, None, None, 0, *weights)
    bs = page_table.shape[0]
    n_pages = PAST_LEN // PAGE_SIZE
    past_k = cache_k[page_table].reshape(bs, n_pages * PAGE_SIZE, N_KV_HEADS, HEAD_DIM)
    past_v = cache_v[page_table].reshape(bs, n_pages * PAGE_SIZE, N_KV_HEADS, HEAD_DIM)
    y_sample, conv_sample, ssm_sample, k_sample, v_sample = trunk(
        x_sample, c_sample, state_conv, state_ssm, past_k, past_v, PAST_LEN, *weights)
    return (y_prompt, y_sample, conv_prompt, ssm_prompt, k_prompt, v_prompt,
            conv_sample, ssm_sample, k_sample, v_sample)
```

```python
import functools

import jax
import jax.numpy as jnp
from jax import lax
from jax.experimental import pallas as pl
from jax.experimental.pallas import tpu as pltpu

F32 = jnp.float32
BF16 = jnp.bfloat16
HIGHEST = lax.Precision.HIGHEST

D_MODEL = 1024
DEPTH = 4
N_A_LAYERS = 2
D_INNER = 2048
SSM_HEADS = 32
SSM_HEAD_DIM = 64
SSM_GROUPS = 4
D_STATE = 128
CONV_WIDTH = 4
CONV_DIM = D_INNER + 2 * SSM_GROUPS * D_STATE
SSD_CHUNK = 128
N_HEADS = 16
HEAD_DIM = 64
N_KV_HEADS = 8
MOBA_BLOCK = 256
MOBA_TOPK = 3
N_EXPERT_GROUPS = 4
EXPERTS_PER_GROUP = 4
N_EXPERTS = 16
D_EXPERT = 256
PAGE_SIZE = 128
EPS = 1e-6

LANES = 128
SUBLANES = 8
MIB = 1024 * 1024
NEG_BIG = -1e30

NT_DIMS = (((1,), (1,)), ((), ()))


def _cparams(sem, vmem_mib):
    return pltpu.CompilerParams(dimension_semantics=sem, vmem_limit_bytes=vmem_mib * MIB)


def _sigmoid(x):
    return 1.0 / (1.0 + jnp.exp(-x))


def _silu(x):
    return x * _sigmoid(x)


def _norm_mod(h, nw, sc, sh):
    ms = jnp.mean(h * h, axis=-1, keepdims=True)
    return (h * lax.rsqrt(ms + EPS)) * nw * (1.0 + sc) + sh


def _split_bf16(x):
    hi = x.astype(BF16)
    lo = (x - hi.astype(F32)).astype(BF16)
    return hi, lo


def _head_rmsnorm(x, r, rt):
    sq = x * x
    hi, lo = _split_bf16(sq)
    ss = jnp.dot(hi, r, preferred_element_type=F32) + jnp.dot(lo, r, preferred_element_type=F32)
    inv = lax.rsqrt(ss * (1.0 / HEAD_DIM) + EPS)
    ih, il = _split_bf16(inv)
    invx = jnp.dot(ih, rt, preferred_element_type=F32) + jnp.dot(il, rt, preferred_element_type=F32)
    return x * invx


def _mod_kernel(c_ref, w_ref, b_ref, o_ref):
    ca = _silu(c_ref[...])
    o_ref[0] = jnp.dot(ca, w_ref[0], precision=HIGHEST, preferred_element_type=F32) + b_ref[0]


def _mod_vectors(c_all, w, b, tn):
    n_layers, d, n = w.shape
    m = c_all.shape[0]
    return pl.pallas_call(
        _mod_kernel,
        grid=(n_layers, n // tn),
        in_specs=[pl.BlockSpec((m, d), lambda l, j: (0, 0)),
                  pl.BlockSpec((1, d, tn), lambda l, j: (l, 0, j)),
                  pl.BlockSpec((1, 1, tn), lambda l, j: (l, 0, j))],
        out_specs=pl.BlockSpec((1, m, tn), lambda l, j: (l, 0, j)),
        out_shape=jax.ShapeDtypeStruct((n_layers, m, n), F32),
        compiler_params=_cparams(("arbitrary", "arbitrary"), 40),
        name="mod_vectors",
    )(c_all, w, b)


def _mod_spec(mod, tm):
    d = mod.shape[-1]
    if mod.shape[1] == 1:
        return pl.BlockSpec((1, 1, d), lambda b, i, *_: (b, 0, 0))
    return pl.BlockSpec((1, tm, d), lambda b, i, *_: (b, i, 0))


def _nm_matmul_kernel(h_ref, nw_ref, sc_ref, sh_ref, w_ref, o_ref, u_ref):
    @pl.when(pl.program_id(2) == 0)
    def _():
        u_ref[...] = _norm_mod(h_ref[0], nw_ref[...], sc_ref[0], sh_ref[0]).astype(BF16)

    o_ref[0] = jnp.dot(u_ref[...], w_ref[...], preferred_element_type=F32)


def _nm_matmul(h, nw, sc, sh, w, tm, tn):
    bsz, t, d = h.shape
    n = w.shape[1]
    return pl.pallas_call(
        _nm_matmul_kernel,
        grid=(bsz, t // tm, n // tn),
        in_specs=[pl.BlockSpec((1, tm, d), lambda b, i, j: (b, i, 0)),
                  pl.BlockSpec((1, d), lambda b, i, j: (0, 0)),
                  _mod_spec(sc, tm), _mod_spec(sh, tm),
                  pl.BlockSpec((d, tn), lambda b, i, j: (0, j))],
        out_specs=pl.BlockSpec((1, tm, tn), lambda b, i, j: (b, i, j)),
        out_shape=jax.ShapeDtypeStruct((bsz, t, n), F32),
        scratch_shapes=[pltpu.VMEM((tm, d), BF16)],
        compiler_params=_cparams(("parallel", "parallel", "arbitrary"), 48),
        name="norm_mod_matmul",
    )(h, nw, sc, sh, w)


def _mm_res_kernel(a_ref, w_ref, h_ref, g_ref, o_ref):
    acc = jnp.dot(a_ref[0].astype(BF16), w_ref[...], preferred_element_type=F32)
    o_ref[0] = h_ref[0] + g_ref[0] * acc


def _mm_res(a, w, h, gate, tm):
    bsz, t, k = a.shape
    d = w.shape[1]
    return pl.pallas_call(
        _mm_res_kernel,
        grid=(bsz, t // tm),
        in_specs=[pl.BlockSpec((1, tm, k), lambda b, i: (b, i, 0)),
                  pl.BlockSpec((k, d), lambda b, i: (0, 0)),
                  pl.BlockSpec((1, tm, d), lambda b, i: (b, i, 0)),
                  _mod_spec(gate, tm)],
        out_specs=pl.BlockSpec((1, tm, d), lambda b, i: (b, i, 0)),
        out_shape=jax.ShapeDtypeStruct((bsz, t, d), F32),
        compiler_params=_cparams(("parallel", "parallel"), 48),
        name="matmul_residual",
    )(a, w, h, gate)


HALO = SUBLANES
XC_TILE = 512


def _ssd_kernel(z_ref, xbc_ref, dt_ref, cs_ref, h0_ref, cw_ref, cb_ref, hp_ref, dsk_ref, nw_ref,
                g_ref, cso_ref, st_ref,
                xs_ref, xc_ref, xt_ref, y_ref, *, valid, has_init):
    L = SSD_CHUNK
    c = pl.program_id(1)

    @pl.when(c == 0)
    def _init():
        xs_ref[...] = jnp.zeros_like(xs_ref)
        xs_ref[0:HALO, :] = cs_ref[0]
        if has_init:
            st_ref[0] = h0_ref[0]
        else:
            st_ref[0] = jnp.zeros(st_ref.shape[1:], F32)

    xs_ref[HALO:HALO + valid, :] = xbc_ref[0]

    for j in range(CONV_DIM // XC_TILE):
        sl = slice(j * XC_TILE, (j + 1) * XC_TILE)
        first = HALO - (CONV_WIDTH - 1)
        acc = cb_ref[:, sl] + xs_ref[first:first + L, sl] * cw_ref[0:1, sl]
        for k in range(1, CONV_WIDTH):
            acc = acc + xs_ref[first + k:first + k + L, sl] * cw_ref[k:k + 1, sl]
        xc_ref[:, sl] = _silu(acc)

    cso_ref[0] = xs_ref[valid:valid + HALO, :]
    xs_ref[0:HALO, :] = xs_ref[L:L + HALO, :]

    dtr = dt_ref[0]
    if valid < L:
        dtr = jnp.concatenate([dtr, jnp.zeros((L - valid, LANES), F32)], axis=0)
    xx = dtr + hp_ref[0:1, :]
    dt = jnp.maximum(xx, 0.0) + jnp.log(1.0 + jnp.exp(-jnp.abs(xx)))
    row = lax.broadcasted_iota(jnp.int32, (L, L), 0)
    col = lax.broadcasted_iota(jnp.int32, (L, L), 1)
    if valid < L:
        dt = jnp.where(row < valid, dt, 0.0)
    a = dt * (-jnp.exp(hp_ref[1:2, :]))
    causal = row >= col
    a_cum = jnp.dot(causal.astype(F32), a, precision=HIGHEST, preferred_element_type=F32)
    a_cum_t = a_cum.T
    dt_t = dt.T
    lane_lo = col < HEAD_DIM
    sub_lo = row < HEAD_DIM

    for j in range(D_INNER // LANES):
        xt_ref[j * LANES:(j + 1) * LANES, :] = xc_ref[:, j * LANES:(j + 1) * LANES].T

    pairs_per_group = SSM_HEADS // SSM_GROUPS // 2
    grp_w = D_INNER // SSM_GROUPS
    for g in range(SSM_GROUPS):
        b_g = xc_ref[:, D_INNER + g * D_STATE:D_INNER + (g + 1) * D_STATE].astype(BF16)
        c_g = xc_ref[:, D_INNER + (SSM_GROUPS + g) * D_STATE:
                     D_INNER + (SSM_GROUPS + g + 1) * D_STATE].astype(BF16)
        cb = lax.dot_general(c_g, b_g, NT_DIMS, preferred_element_type=F32)
        st_g = st_ref[0, g * grp_w:(g + 1) * grp_w, :]
        y_off = lax.dot_general(c_g, st_g.astype(BF16), NT_DIMS, preferred_element_type=F32)
        xw_parts, dec_parts = [], []
        for jj in range(pairs_per_group):
            pair = g * pairs_per_group + jj
            s_mats, e_cols, w_rows, d_end = [], [], [], []
            for h in (2 * pair, 2 * pair + 1):
                colb = jnp.broadcast_to(a_cum[:, h:h + 1], (L, L))
                rowb = a_cum_t[h:h + 1, :]
                dec = jnp.where(causal, jnp.exp(colb - rowb), 0.0)
                s_mats.append((cb * dec * dt_t[h:h + 1, :]).astype(BF16))
                e_cols.append(jnp.exp(colb))
                a_last = a_cum_t[h:h + 1, L - 1:L]
                w_rows.append(dt_t[h:h + 1, :] * jnp.exp(a_last - rowb))
                d_end.append(jnp.broadcast_to(jnp.exp(a_last), (LANES, D_STATE)))
            psl = slice(pair * LANES, (pair + 1) * LANES)
            xp = xc_ref[:, psl]
            xpb = xp.astype(BF16)
            y_d = jnp.where(lane_lo,
                            jnp.dot(s_mats[0], xpb, preferred_element_type=F32),
                            jnp.dot(s_mats[1], xpb, preferred_element_type=F32))
            y_o = y_off[:, jj * LANES:(jj + 1) * LANES] * jnp.where(lane_lo, e_cols[0], e_cols[1])
            y_ref[:, psl] = y_d + y_o + dsk_ref[:, psl] * xp
            xw_parts.append((xt_ref[psl, :] * jnp.where(sub_lo, w_rows[0], w_rows[1])).astype(BF16))
            dec_parts.append(jnp.where(sub_lo, d_end[0], d_end[1]))
        xw_g = jnp.concatenate(xw_parts, axis=0)
        new = jnp.dot(xw_g, b_g, preferred_element_type=F32)
        st_ref[0, g * grp_w:(g + 1) * grp_w, :] = st_g * jnp.concatenate(dec_parts, axis=0) + new

    zz = z_ref[0]
    gg = y_ref[0:valid, :] * _silu(zz)
    for g in range(SSM_GROUPS):
        sl = slice(g * grp_w, (g + 1) * grp_w)
        seg = gg[:, sl]
        ms = jnp.mean(seg * seg, axis=-1, keepdims=True)
        g_ref[0, :, sl] = (seg * lax.rsqrt(ms + EPS) * nw_ref[:, sl]).astype(g_ref.dtype)


def _ssd(z, xbc, dtr, conv_in, h0, conv_w, conv_b, head_params, d_skip, norm_w, g_dtype):
    bsz, t, _ = z.shape
    L = SSD_CHUNK
    valid = min(L, t)
    assert t % valid == 0 and valid % SUBLANES == 0
    nc = t // valid
    has_init = h0 is not None
    if h0 is None:
        h0 = jnp.zeros((1, SUBLANES, D_STATE), F32)
        h0_spec = pl.BlockSpec((1, SUBLANES, D_STATE), lambda b, c: (0, 0, 0))
    else:
        h0_spec = pl.BlockSpec((1, D_INNER, D_STATE), lambda b, c: (b, 0, 0))
    full = lambda shape: pl.BlockSpec(shape, lambda b, c: (0,) * len(shape))
    return pl.pallas_call(
        functools.partial(_ssd_kernel, valid=valid, has_init=has_init),
        grid=(bsz, nc),
        in_specs=[pl.BlockSpec((1, valid, D_INNER), lambda b, c: (b, c, 0)),
                  pl.BlockSpec((1, valid, CONV_DIM), lambda b, c: (b, c, 0)),
                  pl.BlockSpec((1, valid, LANES), lambda b, c: (b, c, 0)),
                  pl.BlockSpec((1, HALO, CONV_DIM), lambda b, c: (b, 0, 0)),
                  h0_spec,
                  full((CONV_WIDTH, CONV_DIM)), full((1, CONV_DIM)), full((SUBLANES, LANES)),
                  full((1, D_INNER)), full((1, D_INNER))],
        out_specs=[pl.BlockSpec((1, valid, D_INNER), lambda b, c: (b, c, 0)),
                   pl.BlockSpec((1, HALO, CONV_DIM), lambda b, c: (b, 0, 0)),
                   pl.BlockSpec((1, D_INNER, D_STATE), lambda b, c: (b, 0, 0))],
        out_shape=[jax.ShapeDtypeStruct((bsz, t, D_INNER), g_dtype),
                   jax.ShapeDtypeStruct((bsz, HALO, CONV_DIM), F32),
                   jax.ShapeDtypeStruct((bsz, D_INNER, D_STATE), F32)],
        scratch_shapes=[pltpu.VMEM((L + HALO, CONV_DIM), F32),
                        pltpu.VMEM((L, CONV_DIM), F32),
                        pltpu.VMEM((D_INNER, L), F32),
                        pltpu.VMEM((L, D_INNER), F32)],
        compiler_params=_cparams(("parallel", "arbitrary"), 48),
        name="ssd_chunk_scan",
    )(z, xbc, dtr, conv_in, h0, conv_w, conv_b, head_params, d_skip, norm_w)


ROUTER_GRP0 = 0
ROUTER_EXP0 = N_EXPERT_GROUPS


def _route(logits):
    lane = lax.broadcasted_iota(jnp.int32, logits.shape, 1)
    big = jnp.int32(2 * LANES)
    is_grp = lane < N_EXPERT_GROUPS
    gl = jnp.where(is_grp, logits, -jnp.inf)
    gmax = jnp.max(gl, axis=-1, keepdims=True)
    gidx = jnp.min(jnp.where(gl == gmax, lane, big), axis=-1, keepdims=True)
    p_grp = 1.0 / jnp.sum(jnp.where(is_grp, jnp.exp(gl - gmax), 0.0), axis=-1, keepdims=True)
    e_rel = lane - ROUTER_EXP0
    in_grp = (e_rel >= gidx * EXPERTS_PER_GROUP) & (e_rel < (gidx + 1) * EXPERTS_PER_GROUP)
    el = jnp.where(in_grp, logits, -jnp.inf)
    m1 = jnp.max(el, axis=-1, keepdims=True)
    i1 = jnp.min(jnp.where(el == m1, lane, big), axis=-1, keepdims=True)
    el2 = jnp.where(lane == i1, -jnp.inf, el)
    m2 = jnp.max(el2, axis=-1, keepdims=True)
    i2 = jnp.min(jnp.where(el2 == m2, lane, big), axis=-1, keepdims=True)
    e2 = jnp.exp(m2 - m1)
    den = 1.0 + e2
    w1 = (1.0 / den) * p_grp
    w2 = (e2 / den) * p_grp
    return jnp.where(lane == i1, w1, jnp.where(lane == i2, w2, 0.0))


def _moe_kernel(h_ref, nw_ref, sc_ref, sh_ref, gf_ref, wr_ref, br_ref, w1_ref, w3_ref, w2_ref,
                o_ref, u_ref, gates_ref, acc_ref):
    e = pl.program_id(2)

    @pl.when(e == 0)
    def _():
        u = _norm_mod(h_ref[0], nw_ref[...], sc_ref[0], sh_ref[0])
        logits = jnp.dot(u, wr_ref[...], precision=HIGHEST, preferred_element_type=F32) + br_ref[...]
        gates_ref[...] = _route(logits)
        u_ref[...] = u.astype(BF16)
        acc_ref[...] = jnp.zeros_like(acc_ref)

    u = u_ref[...]
    a = _silu(jnp.dot(u, w1_ref[0], preferred_element_type=F32)) * jnp.dot(u, w3_ref[0], preferred_element_type=F32)
    gates = gates_ref[...]
    lane = lax.broadcasted_iota(jnp.int32, gates.shape, 1)
    gcol = jnp.sum(jnp.where(lane == e + ROUTER_EXP0, gates, 0.0), axis=-1, keepdims=True)
    acc_ref[...] += jnp.dot((a * gcol).astype(BF16), w2_ref[0], preferred_element_type=F32)

    @pl.when(e == pl.num_programs(2) - 1)
    def _():
        o_ref[0] = h_ref[0] + gf_ref[0] * acc_ref[...]


def _moe(h, nw, sc, sh, gf, wr, br, w1, w3, w2, tm):
    bsz, t, d = h.shape
    n_e, _, f = w1.shape
    return pl.pallas_call(
        _moe_kernel,
        grid=(bsz, t // tm, n_e),
        in_specs=[pl.BlockSpec((1, tm, d), lambda b, i, e: (b, i, 0)),
                  pl.BlockSpec((1, d), lambda b, i, e: (0, 0)),
                  _mod_spec(sc, tm), _mod_spec(sh, tm), _mod_spec(gf, tm),
                  pl.BlockSpec((d, LANES), lambda b, i, e: (0, 0)),
                  pl.BlockSpec((1, LANES), lambda b, i, e: (0, 0)),
                  pl.BlockSpec((1, d, f), lambda b, i, e: (e, 0, 0)),
                  pl.BlockSpec((1, d, f), lambda b, i, e: (e, 0, 0)),
                  pl.BlockSpec((1, f, d), lambda b, i, e: (e, 0, 0))],
        out_specs=pl.BlockSpec((1, tm, d), lambda b, i, e: (b, i, 0)),
        out_shape=jax.ShapeDtypeStruct((bsz, t, d), F32),
        scratch_shapes=[pltpu.VMEM((tm, d), BF16), pltpu.VMEM((tm, LANES), F32), pltpu.VMEM((tm, d), F32)],
        compiler_params=_cparams(("parallel", "parallel", "arbitrary"), 48),
        name="hmoe",
    )(h, nw, sc, sh, gf, wr, br, w1, w3, w2)


def _dup_heads(x):
    lane = lax.broadcasted_iota(jnp.int32, x.shape, 1)
    rolled = pltpu.roll(x, HEAD_DIM, 1)
    lo = lane < HEAD_DIM
    return jnp.where(lo, x, rolled), jnp.where(lo, rolled, x)


def _kv_kernel(h_ref, nw_ref, sc_ref, sh_ref, wk_ref, wv_ref, knw_ref, r_ref, rt_ref,
               kf_ref, vf_ref, *dup_refs, tm, with_dup):
    u = _norm_mod(h_ref[0], nw_ref[...], sc_ref[0], sh_ref[0]).astype(BF16)
    k = jnp.dot(u, wk_ref[...], preferred_element_type=F32)
    v = jnp.dot(u, wv_ref[...], preferred_element_type=F32)
    kn = _head_rmsnorm(k, r_ref[...], rt_ref[...]) * knw_ref[...]
    kf_ref[0] = kn
    vf_ref[0] = v
    if with_dup:
        kd_ref, vd_ref, km_ref = dup_refs
        for cidx in range(N_KV_HEADS // 2):
            sl = slice(cidx * LANES, (cidx + 1) * LANES)
            for src, dst in ((kn, kd_ref), (v, vd_ref)):
                d0, d1 = _dup_heads(src[:, sl])
                for hh, dd in ((2 * cidx, d0), (2 * cidx + 1, d1)):
                    hsl = slice(hh * LANES, (hh + 1) * LANES)
                    dst[0, :, hsl] = dd.astype(BF16)
                    if dst is kd_ref:
                        for blk in range(tm // MOBA_BLOCK):
                            km_ref[0, blk, :, hsl] = jnp.mean(
                                dd[blk * MOBA_BLOCK:(blk + 1) * MOBA_BLOCK], axis=0, keepdims=True)


def _kv_proj(h, nw, sc, sh, wk, wv, knw, r, rt, tm, with_dup):
    bsz, t, d = h.shape
    kvw = wk.shape[1]
    const = lambda shape: pl.BlockSpec(shape, lambda b, i: (0,) * len(shape))
    out_specs = [pl.BlockSpec((1, tm, kvw), lambda b, i: (b, i, 0))] * 2
    out_shape = [jax.ShapeDtypeStruct((bsz, t, kvw), F32)] * 2
    if with_dup:
        dupw = N_KV_HEADS * LANES
        out_specs += [pl.BlockSpec((1, tm, dupw), lambda b, i: (b, i, 0))] * 2
        out_specs += [pl.BlockSpec((1, tm // MOBA_BLOCK, 1, dupw), lambda b, i: (b, i, 0, 0))]
        out_shape += [jax.ShapeDtypeStruct((bsz, t, dupw), BF16)] * 2
        out_shape += [jax.ShapeDtypeStruct((bsz, t // MOBA_BLOCK, 1, dupw), F32)]
    return pl.pallas_call(
        functools.partial(_kv_kernel, tm=tm, with_dup=with_dup),
        grid=(bsz, t // tm),
        in_specs=[pl.BlockSpec((1, tm, d), lambda b, i: (b, i, 0)),
                  const((1, d)), _mod_spec(sc, tm), _mod_spec(sh, tm),
                  const((d, kvw)), const((d, kvw)), const((1, kvw)),
                  const((kvw, LANES)), const((LANES, kvw))],
        out_specs=out_specs,
        out_shape=out_shape,
        compiler_params=_cparams(("parallel", "parallel"), 48),
        name="shared_kv",
    )(h, nw, sc, sh, wk, wv, knw, r, rt)


def _q_kernel(h_ref, nw_ref, sc_ref, sh_ref, wq_ref, qnw_ref, r_ref, rt_ref, q_ref):
    u = _norm_mod(h_ref[0], nw_ref[...], sc_ref[0], sh_ref[0]).astype(BF16)
    q = jnp.dot(u, wq_ref[...], preferred_element_type=F32)
    q_ref[0] = (_head_rmsnorm(q, r_ref[...], rt_ref[...]) * qnw_ref[...]).astype(q_ref.dtype)


def _q_proj(h, nw, sc, sh, wq, qnw, r, rt, tm, out_dtype):
    bsz, t, d = h.shape
    const = lambda shape: pl.BlockSpec(shape, lambda b, i: (0,) * len(shape))
    return pl.pallas_call(
        _q_kernel,
        grid=(bsz, t // tm),
        in_specs=[pl.BlockSpec((1, tm, d), lambda b, i: (b, i, 0)),
                  const((1, d)), _mod_spec(sc, tm), _mod_spec(sh, tm),
                  const((d, d)), const((1, d)), const((d, LANES)), const((LANES, d))],
        out_specs=pl.BlockSpec((1, tm, d), lambda b, i: (b, i, 0)),
        out_shape=jax.ShapeDtypeStruct((bsz, t, d), out_dtype),
        compiler_params=_cparams(("parallel", "parallel"), 48),
        name="q_proj",
    )(h, nw, sc, sh, wq, qnw, r, rt)


def _top_blocks(gate, n_valid):
    lane = lax.broadcasted_iota(jnp.int32, gate.shape, 1)
    gm = jnp.where(lane < n_valid, gate, -jnp.inf)
    sel = jnp.zeros(gate.shape, F32)
    for _ in range(MOBA_TOPK):
        mx = jnp.max(gm, axis=-1, keepdims=True)
        cand = (gm == mx) & (mx > -jnp.inf)
        idx = jnp.min(jnp.where(cand, lane, jnp.int32(2 * LANES)), axis=-1, keepdims=True)
        pick = lane == idx
        sel = jnp.where(pick, 1.0, sel)
        gm = jnp.where(pick, -jnp.inf, gm)
    return sel


def _alibi_slope(head):
    return jnp.exp2(-8.0 * (head + 1).astype(F32) / N_HEADS)


def _attn_prompt_kernel(q_ref, k_ref, v_ref, km_ref, o_ref, q2_ref, sb_ref, sel_ref, m_ref, l_ref, acc_ref):
    g = pl.program_id(1)
    i = pl.program_id(2)
    blk = MOBA_BLOCK
    rows = 2 * blk
    qt = q_ref[0]
    lane = lax.broadcasted_iota(jnp.int32, (blk, LANES), 1)
    zero = jnp.zeros_like(qt)
    q2 = jnp.concatenate([jnp.where(lane < HEAD_DIM, qt, zero), jnp.where(lane >= HEAD_DIM, qt, zero)], axis=0)
    q2f = q2.astype(F32)
    q2_ref[...] = (q2f * (HEAD_DIM ** -0.5)).astype(BF16)

    r1 = lax.broadcasted_iota(jnp.int32, (rows, 1), 0)
    slope = _alibi_slope(2 * g + (r1 >= blk).astype(jnp.int32))
    gate = lax.dot_general(q2f, km_ref[0], NT_DIMS, precision=HIGHEST, preferred_element_type=F32)
    sel_ref[...] = _top_blocks(gate, i)

    rr = lax.broadcasted_iota(jnp.int32, (rows, blk), 0)
    cc = lax.broadcasted_iota(jnp.int32, (rows, blk), 1)
    d0 = (jnp.where(rr >= blk, rr - blk, rr) - cc).astype(F32)
    sb_ref[...] = -slope * d0

    own = pl.multiple_of(i * blk, blk)
    s = lax.dot_general(q2_ref[...], k_ref[0, pl.ds(own, blk), :], NT_DIMS, preferred_element_type=F32) + sb_ref[...]
    s = jnp.where(d0 >= 0.0, s, -jnp.inf)
    m0 = jnp.max(s, axis=-1, keepdims=True)
    p = jnp.exp(s - m0)
    m_ref[...] = m0
    l_ref[...] = jnp.sum(p, axis=-1, keepdims=True)
    acc_ref[...] = jnp.dot(p.astype(BF16), v_ref[0, pl.ds(own, blk), :], preferred_element_type=F32)

    def body(n, carry):
        start = pl.multiple_of(n * blk, blk)
        sel = sel_ref[...]
        lane_s = lax.broadcasted_iota(jnp.int32, sel.shape, 1)
        picked = jnp.sum(jnp.where(lane_s == n, sel, 0.0), axis=-1, keepdims=True) > 0.0
        s = lax.dot_general(q2_ref[...], k_ref[0, pl.ds(start, blk), :], NT_DIMS, preferred_element_type=F32)
        s = s + sb_ref[...] - slope * ((i - n) * blk).astype(F32)
        s = jnp.where(picked, s, NEG_BIG)
        m_old = m_ref[...]
        m_new = jnp.maximum(m_old, jnp.max(s, axis=-1, keepdims=True))
        alpha = jnp.exp(m_old - m_new)
        p = jnp.exp(s - m_new)
        l_ref[...] = alpha * l_ref[...] + jnp.sum(p, axis=-1, keepdims=True)
        acc_ref[...] = alpha * acc_ref[...] + jnp.dot(p.astype(BF16), v_ref[0, pl.ds(start, blk), :],
                                                      preferred_element_type=F32)
        m_ref[...] = m_new
        return carry

    lax.fori_loop(0, i, body, 0)
    o = acc_ref[...] / l_ref[...]
    o_ref[0] = jnp.where(lane < HEAD_DIM, o[0:blk], o[blk:rows]).astype(o_ref.dtype)


def _attn_prompt(q, kd, vd, kmd):
    bsz, t, d = q.shape
    blk = MOBA_BLOCK
    nb = t // blk
    rows = 2 * blk
    return pl.pallas_call(
        _attn_prompt_kernel,
        grid=(bsz, N_KV_HEADS, nb),
        in_specs=[pl.BlockSpec((1, blk, LANES), lambda b, g, i: (b, i, g)),
                  pl.BlockSpec((1, t, LANES), lambda b, g, i: (b, 0, g)),
                  pl.BlockSpec((1, t, LANES), lambda b, g, i: (b, 0, g)),
                  pl.BlockSpec((1, LANES, LANES), lambda b, g, i: (b, 0, g))],
        out_specs=pl.BlockSpec((1, blk, LANES), lambda b, g, i: (b, i, g)),
        out_shape=jax.ShapeDtypeStruct((bsz, t, d), BF16),
        scratch_shapes=[pltpu.VMEM((rows, LANES), BF16), pltpu.VMEM((rows, blk), F32),
                        pltpu.VMEM((rows, LANES), F32), pltpu.VMEM((rows, 1), F32),
                        pltpu.VMEM((rows, 1), F32), pltpu.VMEM((rows, LANES), F32)],
        compiler_params=_cparams(("parallel", "parallel", "arbitrary"), 48),
        name="moba_prompt",
    )(q, kd, vd, kmd)


def _attn_sample_kernel(pt_ref, q_ref, kn_ref, vn_ref, *rest, n_pages, past_len):
    del pt_ref
    k_refs = rest[:n_pages]
    v_refs = rest[n_pages:2 * n_pages]
    o_ref = rest[2 * n_pages]
    s_ref = rest[2 * n_pages + 1]
    t_new = q_ref.shape[1]
    kvw = N_KV_HEADS * HEAD_DIM
    rows = N_HEADS * t_new
    n_past_blk = past_len // MOBA_BLOCK
    pages_per_blk = MOBA_BLOCK // PAGE_SIZE
    q = q_ref[0]
    lane8 = lax.broadcasted_iota(jnp.int32, (t_new, LANES), 1)
    zeros8 = jnp.zeros((t_new, LANES), F32)
    row_blocks = []
    for h in range(N_HEADS):
        g = h // 2
        tile = q[:, g * LANES:(g + 1) * LANES]
        if h % 2 != g % 2:
            tile = pltpu.roll(tile, HEAD_DIM, 1)
        keep = (lane8 < HEAD_DIM) if g % 2 == 0 else (lane8 >= HEAD_DIM)
        tile = jnp.where(keep, tile, 0.0)
        row_blocks.append(jnp.concatenate([tile if cidx == g // 2 else zeros8 for cidx in range(kvw // LANES)], axis=1))
    qm = jnp.concatenate(row_blocks, axis=0)
    qmb = (qm * (HEAD_DIM ** -0.5)).astype(BF16)

    r1 = lax.broadcasted_iota(jnp.int32, (rows, 1), 0)
    slope = _alibi_slope(r1 // t_new)
    rr = lax.broadcasted_iota(jnp.int32, (rows, PAGE_SIZE), 0)
    cc = lax.broadcasted_iota(jnp.int32, (rows, PAGE_SIZE), 1)
    tok = rr % t_new
    d_page0 = (past_len + tok - cc).astype(F32)

    rows_km = lax.broadcasted_iota(jnp.int32, (LANES, kvw), 0)
    km = jnp.zeros((LANES, kvw), F32)
    for n in range(n_past_blk):
        acc = jnp.zeros((1, kvw), F32)
        for pp in range(pages_per_blk):
            acc = acc + jnp.sum(k_refs[n * pages_per_blk + pp][0], axis=0, keepdims=True)
        km = jnp.where(rows_km == n, acc * (1.0 / MOBA_BLOCK), km)
    gate = lax.dot_general(qm, km, NT_DIMS, precision=HIGHEST, preferred_element_type=F32)
    sel = _top_blocks(gate, n_past_blk)
    lane_s = lax.broadcasted_iota(jnp.int32, sel.shape, 1)

    for p in range(n_pages):
        kp = k_refs[p][0].astype(BF16)
        s = lax.dot_general(qmb, kp, NT_DIMS, preferred_element_type=F32)
        s = s - slope * (d_page0 - float(p * PAGE_SIZE))
        picked = jnp.sum(jnp.where(lane_s == p // pages_per_blk, sel, 0.0), axis=-1, keepdims=True) > 0.0
        s_ref[:, p * PAGE_SIZE:(p + 1) * PAGE_SIZE] = jnp.where(picked, s, -jnp.inf)
    pad = jnp.zeros((PAGE_SIZE - t_new, kvw), F32)
    k_new = jnp.concatenate([kn_ref[0], pad], axis=0).astype(BF16)
    v_new = jnp.concatenate([vn_ref[0], pad], axis=0).astype(BF16)
    d_own = (tok - cc).astype(F32)
    s = lax.dot_general(qmb, k_new, NT_DIMS, preferred_element_type=F32) - slope * d_own
    s_ref[:, n_pages * PAGE_SIZE:] = jnp.where((d_own >= 0.0) & (cc < t_new), s, -jnp.inf)

    s_all = s_ref[...]
    m = jnp.max(s_all, axis=-1, keepdims=True)
    p_all = jnp.exp(s_all - m)
    l = jnp.sum(p_all, axis=-1, keepdims=True)
    pb = p_all.astype(BF16)
    out = jnp.dot(pb[:, n_pages * PAGE_SIZE:], v_new, preferred_element_type=F32)
    for p in range(n_pages):
        out = out + jnp.dot(pb[:, p * PAGE_SIZE:(p + 1) * PAGE_SIZE], v_refs[p][0].astype(BF16),
                            preferred_element_type=F32)
    out = out / l

    lo8 = lane8 < HEAD_DIM
    for cidx in range(N_HEADS // 2):
        src = slice((cidx // 2) * LANES, (cidx // 2 + 1) * LANES)
        a = out[(2 * cidx) * t_new:(2 * cidx + 1) * t_new, src]
        b = out[(2 * cidx + 1) * t_new:(2 * cidx + 2) * t_new, src]
        if cidx % 2 == 1:
            a = pltpu.roll(a, HEAD_DIM, 1)
        else:
            b = pltpu.roll(b, HEAD_DIM, 1)
        o_ref[0, :, cidx * LANES:(cidx + 1) * LANES] = jnp.where(lo8, a, b)


def _attn_sample(q, k_new, v_new, cache_k, cache_v, page_table, t_new):
    n_seq, n_pages = page_table.shape
    d = q.shape[-1]
    kvw = k_new.shape[-1]
    past_len = n_pages * PAGE_SIZE
    rows = N_HEADS * t_new
    page_spec = lambda p: pl.BlockSpec((1, PAGE_SIZE, kvw), lambda s, pt: (pt[s, p], 0, 0))
    grid_spec = pltpu.PrefetchScalarGridSpec(
        num_scalar_prefetch=1,
        grid=(n_seq,),
        in_specs=[pl.BlockSpec((1, t_new, d), lambda s, pt: (0, s, 0)),
                  pl.BlockSpec((1, t_new, kvw), lambda s, pt: (0, s, 0)),
                  pl.BlockSpec((1, t_new, kvw), lambda s, pt: (0, s, 0))]
        + [page_spec(p) for p in range(n_pages)] + [page_spec(p) for p in range(n_pages)],
        out_specs=pl.BlockSpec((1, t_new, d), lambda s, pt: (0, s, 0)),
        scratch_shapes=[pltpu.VMEM((rows, (n_pages + 1) * PAGE_SIZE), F32)],
    )
    return pl.pallas_call(
        functools.partial(_attn_sample_kernel, n_pages=n_pages, past_len=past_len),
        grid_spec=grid_spec,
        out_shape=jax.ShapeDtypeStruct(q.shape, F32),
        compiler_params=_cparams(("arbitrary",), 48),
        name="moba_sample",
    )(page_table, q, k_new, v_new, *([cache_k] * n_pages), *([cache_v] * n_pages))


def _head_sum_matrices(n_heads):
    w = n_heads * HEAD_DIM
    head_of_lane = jnp.arange(w) // HEAD_DIM
    r = (head_of_lane[:, None] == jnp.arange(LANES)[None, :]).astype(BF16)
    return r, r.T


def _prep_weights(p):
    bf = lambda x: x.astype(BF16)
    w_in = p['ssm_w_in']
    wdt = jnp.pad(w_in[:, :, D_INNER + CONV_DIM:], ((0, 0), (0, 0), (0, LANES - SSM_HEADS)))
    pad_heads = lambda x: jnp.pad(x, ((0, 0), (0, LANES - SSM_HEADS)))
    head_params = jnp.stack([pad_heads(p['ssm_dt_bias']), pad_heads(p['ssm_a_log'])], axis=1)
    head_params = jnp.pad(head_params, ((0, 0), (0, SUBLANES - 2), (0, 0)))
    router_w = jnp.concatenate([p['moe_w_grp'], p['moe_w_exp']], axis=-1)
    router_w = jnp.pad(router_w, ((0, 0), (0, 0), (0, LANES - router_w.shape[-1])))
    router_b = jnp.concatenate([p['moe_b_grp'], p['moe_b_exp']], axis=-1)
    router_b = jnp.pad(router_b, ((0, 0), (0, LANES - router_b.shape[-1])))[:, None, :]
    rq, rqt = _head_sum_matrices(N_HEADS)
    rk, rkt = _head_sum_matrices(N_KV_HEADS)
    return dict(
        wz=bf(w_in[:, :, :D_INNER]), wxbc=bf(w_in[:, :, D_INNER:D_INNER + CONV_DIM]), wdt=bf(wdt),
        head_params=head_params, d_skip=jnp.repeat(p['ssm_d'], SSM_HEAD_DIM, axis=-1)[:, None, :],
        conv_w=p['ssm_conv_w'], conv_b=p['ssm_conv_b'][:, None, :], ssm_norm_w=p['ssm_norm_w'][:, None, :],
        w_out=bf(p['ssm_w_out']),
        norm_mix_w=p['norm_mix_w'][:, None, :], norm_ffn_w=p['norm_ffn_w'][:, None, :],
        kv_norm_w=p['kv_norm_w'][None, :], wk=bf(p['w_k']), wv=bf(p['w_v']),
        k_norm_w=jnp.tile(p['k_norm_w'], N_KV_HEADS)[None, :],
        wq=bf(p['attn_w_q']), q_norm_w=jnp.tile(p['q_norm_w'], (1, N_HEADS))[:, None, :], wo=bf(p['attn_w_o']),
        router_w=router_w, router_b=router_b,
        w1=bf(p['moe_w1']), w3=bf(p['moe_w3']), w2=bf(p['moe_w2']),
        rq=rq, rqt=rqt, rk=rk, rkt=rkt,
    )


def _trunk(h, mods, kv_mod, w, *, sample, conv0=None, ssm0=None, cache=None):
    bsz, t, _ = h.shape
    tm = min(t, 1024)
    conv_out, ssm_out = [], []
    k_new = v_new = kd = vd = kmd = None
    for layer in range(DEPTH):
        sh_m, sc_m, g_m, sh_f, sc_f, g_f = mods[layer]
        nmw = w['norm_mix_w'][layer]
        if layer < N_A_LAYERS:
            z = _nm_matmul(h, nmw, sc_m, sh_m, w['wz'][layer], tm, 1024)
            xbc = _nm_matmul(h, nmw, sc_m, sh_m, w['wxbc'][layer], tm, 1024)
            dtr = _nm_matmul(h, nmw, sc_m, sh_m, w['wdt'][layer], tm, LANES)
            ssd_args = (w['conv_w'][layer], w['conv_b'][layer], w['head_params'][layer], w['d_skip'][layer],
                        w['ssm_norm_w'][layer])
            if sample:
                n_seq, t_new = conv0.shape[1], cache[3]
                seqs = lambda x: x.reshape(n_seq, t_new, x.shape[-1])
                conv_in = jnp.pad(conv0[layer], ((0, 0), (HALO - (CONV_WIDTH - 1), 0), (0, 0)))
                g, conv8, st = _ssd(seqs(z), seqs(xbc), seqs(dtr), conv_in,
                                    ssm0[layer].reshape(n_seq, D_INNER, D_STATE), *ssd_args, g_dtype=F32)
                g = g.reshape(1, t, D_INNER)
            else:
                conv_in = jnp.zeros((bsz, HALO, CONV_DIM), F32)
                g, conv8, st = _ssd(z, xbc, dtr, conv_in, None, *ssd_args, g_dtype=BF16)
            conv_out.append(conv8[:, HALO - (CONV_WIDTH - 1):])
            ssm_out.append(st.reshape(st.shape[0], SSM_HEADS, SSM_HEAD_DIM, D_STATE))
            h = _mm_res(g, w['w_out'][layer], h, g_m, tm)
        else:
            j = layer - N_A_LAYERS
            if sample:
                q = _q_proj(h, nmw, sc_m, sh_m, w['wq'][j], w['q_norm_w'][j], w['rq'], w['rqt'], tm, F32)
                o = _attn_sample(q, k_new, v_new, cache[0], cache[1], cache[2], cache[3])
            else:
                q = _q_proj(h, nmw, sc_m, sh_m, w['wq'][j], w['q_norm_w'][j], w['rq'], w['rqt'], tm, BF16)
                o = _attn_prompt(q, kd, vd, kmd)
            h = _mm_res(o, w['wo'][j], h, g_m, tm)
        h = _moe(h, w['norm_ffn_w'][layer], sc_f, sh_f, g_f, w['router_w'][layer], w['router_b'][layer],
                 w['w1'][layer], w['w3'][layer], w['w2'][layer], tm)
        if layer == N_A_LAYERS - 1:
            kv_tm = min(t, 512)
            res = _kv_proj(h, w['kv_norm_w'], kv_mod[1], kv_mod[0], w['wk'], w['wv'], w['k_norm_w'],
                           w['rk'], w['rkt'], kv_tm, with_dup=not sample)
            k_new, v_new = res[0], res[1]
            if not sample:
                kd, vd = res[2], res[3]
                kmd = res[4].reshape(bsz, t // MOBA_BLOCK, N_KV_HEADS * LANES)
                kmd = jnp.pad(kmd, ((0, 0), (0, LANES - kmd.shape[1]), (0, 0)))
    return h, jnp.stack(conv_out), jnp.stack(ssm_out), k_new, v_new


def kernel(x_prompt, x_sample, state_conv, state_ssm, cache_k, cache_v, page_table, c_prompt, c_sample, w_mod, b_mod, norm_mix_w, norm_ffn_w, ssm_w_in, ssm_conv_w, ssm_conv_b, ssm_dt_bias, ssm_a_log, ssm_d, ssm_norm_w, ssm_w_out, kv_w_mod, kv_b_mod, kv_norm_w, w_k, w_v, k_norm_w, attn_w_q, q_norm_w, attn_w_o, moe_w_grp, moe_b_grp, moe_w_exp, moe_b_exp, moe_w1, moe_w3, moe_w2):
    params = dict(ssm_w_in=ssm_w_in, ssm_conv_w=ssm_conv_w, ssm_conv_b=ssm_conv_b, ssm_dt_bias=ssm_dt_bias,
                  ssm_a_log=ssm_a_log, ssm_d=ssm_d, ssm_norm_w=ssm_norm_w, ssm_w_out=ssm_w_out,
                  norm_mix_w=norm_mix_w, norm_ffn_w=norm_ffn_w, kv_norm_w=kv_norm_w, w_k=w_k, w_v=w_v,
                  k_norm_w=k_norm_w, attn_w_q=attn_w_q, q_norm_w=q_norm_w, attn_w_o=attn_w_o,
                  moe_w_grp=moe_w_grp, moe_b_grp=moe_b_grp, moe_w_exp=moe_w_exp, moe_b_exp=moe_b_exp,
                  moe_w1=moe_w1, moe_w3=moe_w3, moe_w2=moe_w2)
    w = _prep_weights(params)

    bp, seq, d = x_prompt.shape
    n_seq, t_new, _ = x_sample.shape
    n_pages = page_table.shape[1]
    past_len = n_pages * PAGE_SIZE
    assert seq % MOBA_BLOCK == 0 and seq // MOBA_BLOCK <= LANES
    assert past_len % MOBA_BLOCK == 0 and t_new <= MOBA_BLOCK and t_new % SUBLANES == 0

    n_c = bp + n_seq
    n_c_pad = -(-n_c // SUBLANES) * SUBLANES
    c_all = jnp.pad(jnp.concatenate([c_prompt, c_sample], axis=0), ((0, n_c_pad - n_c), (0, 0)))
    mod_all = _mod_vectors(c_all, w_mod, b_mod[:, None, :], 1536)
    kv_all = _mod_vectors(c_all, kv_w_mod[None], kv_b_mod[None, None, :], 1024)[0]

    def group_mods(lo, hi, per_token_repeat):
        def shape(x):
            if per_token_repeat:
                return jnp.repeat(x, per_token_repeat, axis=0)[None]
            return x[:, None, :]
        mods = [[shape(mod_all[l, lo:hi, k * d:(k + 1) * d]) for k in range(6)] for l in range(DEPTH)]
        kvm = [shape(kv_all[lo:hi, k * d:(k + 1) * d]) for k in range(2)]
        return mods, kvm

    mods_p, kv_p = group_mods(0, bp, 0)
    y_p, conv_p, ssm_p, k_p, v_p = _trunk(x_prompt, mods_p, kv_p, w, sample=False)

    mods_s, kv_s = group_mods(bp, n_c, t_new)
    kvw = N_KV_HEADS * HEAD_DIM
    cache = (cache_k.reshape(-1, PAGE_SIZE, kvw), cache_v.reshape(-1, PAGE_SIZE, kvw), page_table, t_new)
    y_s, conv_s, ssm_s, k_s, v_s = _trunk(x_sample.reshape(1, n_seq * t_new, d), mods_s, kv_s, w,
                                          sample=True, conv0=state_conv, ssm0=state_ssm, cache=cache)

    heads = lambda x, b, t: x.reshape(b, t, N_KV_HEADS, HEAD_DIM)
    return (y_p, y_s.reshape(n_seq, t_new, d), conv_p, ssm_p, heads(k_p, bp, seq), heads(v_p, bp, seq),
            conv_s, ssm_s, heads(k_s, n_seq, t_new), heads(v_s, n_seq, t_new))
```

```python
import functools

import jax
import jax.numpy as jnp
from jax import lax
from jax.experimental import pallas as pl
from jax.experimental.pallas import tpu as pltpu

F32 = jnp.float32
BF16 = jnp.bfloat16
HIGHEST = lax.Precision.HIGHEST

D_MODEL = 1024
DEPTH = 4
N_A_LAYERS = 2
D_INNER = 2048
SSM_HEADS = 32
SSM_HEAD_DIM = 64
SSM_GROUPS = 4
D_STATE = 128
CONV_WIDTH = 4
CONV_DIM = D_INNER + 2 * SSM_GROUPS * D_STATE
SSD_CHUNK = 128
N_HEADS = 16
HEAD_DIM = 64
N_KV_HEADS = 8
MOBA_BLOCK = 256
MOBA_TOPK = 3
N_EXPERT_GROUPS = 4
EXPERTS_PER_GROUP = 4
N_EXPERTS = 16
D_EXPERT = 256
PAGE_SIZE = 128
EPS = 1e-6

LANES = 128
SUBLANES = 8
MIB = 1024 * 1024
NEG_BIG = -1e30

NT_DIMS = (((1,), (1,)), ((), ()))


def _cparams(sem, vmem_mib):
    return pltpu.CompilerParams(dimension_semantics=sem, vmem_limit_bytes=vmem_mib * MIB)


def _sigmoid(x):
    return 1.0 / (1.0 + jnp.exp(-x))


def _silu(x):
    return x * _sigmoid(x)


def _norm_mod(h, nw, sc, sh):
    ms = jnp.mean(h * h, axis=-1, keepdims=True)
    return (h * lax.rsqrt(ms + EPS)) * nw * (1.0 + sc) + sh


def _split_bf16(x):
    hi = x.astype(BF16)
    lo = (x - hi.astype(F32)).astype(BF16)
    return hi, lo


def _head_rmsnorm(x, r, rt):
    sq = x * x
    hi, lo = _split_bf16(sq)
    ss = jnp.dot(hi, r, preferred_element_type=F32) + jnp.dot(lo, r, preferred_element_type=F32)
    inv = lax.rsqrt(ss * (1.0 / HEAD_DIM) + EPS)
    ih, il = _split_bf16(inv)
    invx = jnp.dot(ih, rt, preferred_element_type=F32) + jnp.dot(il, rt, preferred_element_type=F32)
    return x * invx


def _mod_kernel(c_ref, w_ref, b_ref, o_ref):
    ca = _silu(c_ref[...])
    o_ref[0] = jnp.dot(ca, w_ref[0], precision=HIGHEST, preferred_element_type=F32) + b_ref[0]


def _mod_vectors(c_all, w, b, tn):
    n_layers, d, n = w.shape
    m = c_all.shape[0]
    return pl.pallas_call(
        _mod_kernel,
        grid=(n_layers, n // tn),
        in_specs=[pl.BlockSpec((m, d), lambda l, j: (0, 0)),
                  pl.BlockSpec((1, d, tn), lambda l, j: (l, 0, j)),
                  pl.BlockSpec((1, 1, tn), lambda l, j: (l, 0, j))],
        out_specs=pl.BlockSpec((1, m, tn), lambda l, j: (l, 0, j)),
        out_shape=jax.ShapeDtypeStruct((n_layers, m, n), F32),
        compiler_params=_cparams(("arbitrary", "arbitrary"), 40),
        name="mod_vectors",
    )(c_all, w, b)


def _mod_spec(mod, tm):
    d = mod.shape[-1]
    if mod.shape[1] == 1:
        return pl.BlockSpec((1, 1, d), lambda b, i, *_: (b, 0, 0))
    return pl.BlockSpec((1, tm, d), lambda b, i, *_: (b, i, 0))


def _nm_matmul_kernel(h_ref, nw_ref, sc_ref, sh_ref, w_ref, o_ref, u_ref):
    @pl.when(pl.program_id(2) == 0)
    def _():
        u_ref[...] = _norm_mod(h_ref[0], nw_ref[...], sc_ref[0], sh_ref[0]).astype(BF16)

    o_ref[0] = jnp.dot(u_ref[...], w_ref[...], preferred_element_type=F32)


def _nm_matmul(h, nw, sc, sh, w, tm, tn):
    bsz, t, d = h.shape
    n = w.shape[1]
    return pl.pallas_call(
        _nm_matmul_kernel,
        grid=(bsz, t // tm, n // tn),
        in_specs=[pl.BlockSpec((1, tm, d), lambda b, i, j: (b, i, 0)),
                  pl.BlockSpec((1, d), lambda b, i, j: (0, 0)),
                  _mod_spec(sc, tm), _mod_spec(sh, tm),
                  pl.BlockSpec((d, tn), lambda b, i, j: (0, j))],
        out_specs=pl.BlockSpec((1, tm, tn), lambda b, i, j: (b, i, j)),
        out_shape=jax.ShapeDtypeStruct((bsz, t, n), F32),
        scratch_shapes=[pltpu.VMEM((tm, d), BF16)],
        compiler_params=_cparams(("parallel", "parallel", "arbitrary"), 48),
        name="norm_mod_matmul",
    )(h, nw, sc, sh, w)


def _mm_res_kernel(a_ref, w_ref, h_ref, g_ref, o_ref):
    acc = jnp.dot(a_ref[0].astype(BF16), w_ref[...], preferred_element_type=F32)
    o_ref[0] = h_ref[0] + g_ref[0] * acc


def _mm_res(a, w, h, gate, tm):
    bsz, t, k = a.shape
    d = w.shape[1]
    return pl.pallas_call(
        _mm_res_kernel,
        grid=(bsz, t // tm),
        in_specs=[pl.BlockSpec((1, tm, k), lambda b, i: (b, i, 0)),
                  pl.BlockSpec((k, d), lambda b, i: (0, 0)),
                  pl.BlockSpec((1, tm, d), lambda b, i: (b, i, 0)),
                  _mod_spec(gate, tm)],
        out_specs=pl.BlockSpec((1, tm, d), lambda b, i: (b, i, 0)),
        out_shape=jax.ShapeDtypeStruct((bsz, t, d), F32),
        compiler_params=_cparams(("parallel", "parallel"), 48),
        name="matmul_residual",
    )(a, w, h, gate)


HALO = SUBLANES
XC_TILE = 512


def _ssd_kernel(z_ref, xbc_ref, dt_ref, cs_ref, h0_ref, cw_ref, cb_ref, hp_ref, dsk_ref, nw_ref,
                g_ref, cso_ref, st_ref,
                xs_ref, xc_ref, xt_ref, y_ref, *, valid, has_init):
    L = SSD_CHUNK
    c = pl.program_id(1)

    @pl.when(c == 0)
    def _init():
        xs_ref[...] = jnp.zeros_like(xs_ref)
        xs_ref[0:HALO, :] = cs_ref[0]
        if has_init:
            st_ref[0] = h0_ref[0]
        else:
            st_ref[0] = jnp.zeros(st_ref.shape[1:], F32)

    xs_ref[HALO:HALO + valid, :] = xbc_ref[0]

    for j in range(CONV_DIM // XC_TILE):
        sl = slice(j * XC_TILE, (j + 1) * XC_TILE)
        first = HALO - (CONV_WIDTH - 1)
        acc = cb_ref[:, sl] + xs_ref[first:first + L, sl] * cw_ref[0:1, sl]
        for k in range(1, CONV_WIDTH):
            acc = acc + xs_ref[first + k:first + k + L, sl] * cw_ref[k:k + 1, sl]
        xc_ref[:, sl] = _silu(acc)

    cso_ref[0] = xs_ref[valid:valid + HALO, :]
    xs_ref[0:HALO, :] = xs_ref[L:L + HALO, :]

    dtr = dt_ref[0]
    if valid < L:
        dtr = jnp.concatenate([dtr, jnp.zeros((L - valid, LANES), F32)], axis=0)
    xx = dtr + hp_ref[0:1, :]
    dt = jnp.maximum(xx, 0.0) + jnp.log(1.0 + jnp.exp(-jnp.abs(xx)))
    row = lax.broadcasted_iota(jnp.int32, (L, L), 0)
    col = lax.broadcasted_iota(jnp.int32, (L, L), 1)
    if valid < L:
        dt = jnp.where(row < valid, dt, 0.0)
    a = dt * (-jnp.exp(hp_ref[1:2, :]))
    causal = row >= col
    a_cum = jnp.dot(causal.astype(F32), a, precision=HIGHEST, preferred_element_type=F32)
    a_cum_t = a_cum.T
    dt_t = dt.T
    lane_lo = col < HEAD_DIM
    sub_lo = row < HEAD_DIM

    for j in range(D_INNER // LANES):
        xt_ref[j * LANES:(j + 1) * LANES, :] = xc_ref[:, j * LANES:(j + 1) * LANES].T

    pairs_per_group = SSM_HEADS // SSM_GROUPS // 2
    grp_w = D_INNER // SSM_GROUPS
    for g in range(SSM_GROUPS):
        b_g = xc_ref[:, D_INNER + g * D_STATE:D_INNER + (g + 1) * D_STATE].astype(BF16)
        c_g = xc_ref[:, D_INNER + (SSM_GROUPS + g) * D_STATE:
                     D_INNER + (SSM_GROUPS + g + 1) * D_STATE].astype(BF16)
        cb = lax.dot_general(c_g, b_g, NT_DIMS, preferred_element_type=F32)
        st_g = st_ref[0, g * grp_w:(g + 1) * grp_w, :]
        y_off = lax.dot_general(c_g, st_g.astype(BF16), NT_DIMS, preferred_element_type=F32)
        xw_parts, dec_parts = [], []
        for jj in range(pairs_per_group):
            pair = g * pairs_per_group + jj
            s_mats, e_cols, w_rows, d_end = [], [], [], []
            for h in (2 * pair, 2 * pair + 1):
                colb = jnp.broadcast_to(a_cum[:, h:h + 1], (L, L))
                rowb = a_cum_t[h:h + 1, :]
                dec = jnp.where(causal, jnp.exp(colb - rowb), 0.0)
                s_mats.append((cb * dec * dt_t[h:h + 1, :]).astype(BF16))
                e_cols.append(jnp.exp(colb))
                a_last = a_cum_t[h:h + 1, L - 1:L]
                w_rows.append(dt_t[h:h + 1, :] * jnp.exp(a_last - rowb))
                d_end.append(jnp.broadcast_to(jnp.exp(a_last), (LANES, D_STATE)))
            psl = slice(pair * LANES, (pair + 1) * LANES)
            xp = xc_ref[:, psl]
            xpb = xp.astype(BF16)
            y_d = jnp.where(lane_lo,
                            jnp.dot(s_mats[0], xpb, preferred_element_type=F32),
                            jnp.dot(s_mats[1], xpb, preferred_element_type=F32))
            y_o = y_off[:, jj * LANES:(jj + 1) * LANES] * jnp.where(lane_lo, e_cols[0], e_cols[1])
            y_ref[:, psl] = y_d + y_o + dsk_ref[:, psl] * xp
            xw_parts.append((xt_ref[psl, :] * jnp.where(sub_lo, w_rows[0], w_rows[1])).astype(BF16))
            dec_parts.append(jnp.where(sub_lo, d_end[0], d_end[1]))
        xw_g = jnp.concatenate(xw_parts, axis=0)
        new = jnp.dot(xw_g, b_g, preferred_element_type=F32)
        st_ref[0, g * grp_w:(g + 1) * grp_w, :] = st_g * jnp.concatenate(dec_parts, axis=0) + new

    zz = z_ref[0]
    gg = y_ref[0:valid, :] * _silu(zz)
    for g in range(SSM_GROUPS):
        sl = slice(g * grp_w, (g + 1) * grp_w)
        seg = gg[:, sl]
        ms = jnp.mean(seg * seg, axis=-1, keepdims=True)
        g_ref[0, :, sl] = (seg * lax.rsqrt(ms + EPS) * nw_ref[:, sl]).astype(g_ref.dtype)


def _ssd(z, xbc, dtr, conv_in, h0, conv_w, conv_b, head_params, d_skip, norm_w, g_dtype):
    bsz, t, _ = z.shape
    L = SSD_CHUNK
    valid = min(L, t)
    assert t % valid == 0 and valid % SUBLANES == 0
    nc = t // valid
    has_init = h0 is not None
    if h0 is None:
        h0 = jnp.zeros((1, SUBLANES, D_STATE), F32)
        h0_spec = pl.BlockSpec((1, SUBLANES, D_STATE), lambda b, c: (0, 0, 0))
    else:
        h0_spec = pl.BlockSpec((1, D_INNER, D_STATE), lambda b, c: (b, 0, 0))
    full = lambda shape: pl.BlockSpec(shape, lambda b, c: (0,) * len(shape))
    return pl.pallas_call(
        functools.partial(_ssd_kernel, valid=valid, has_init=has_init),
        grid=(bsz, nc),
        in_specs=[pl.BlockSpec((1, valid, D_INNER), lambda b, c: (b, c, 0)),
                  pl.BlockSpec((1, valid, CONV_DIM), lambda b, c: (b, c, 0)),
                  pl.BlockSpec((1, valid, LANES), lambda b, c: (b, c, 0)),
                  pl.BlockSpec((1, HALO, CONV_DIM), lambda b, c: (b, 0, 0)),
                  h0_spec,
                  full((CONV_WIDTH, CONV_DIM)), full((1, CONV_DIM)), full((SUBLANES, LANES)),
                  full((1, D_INNER)), full((1, D_INNER))],
        out_specs=[pl.BlockSpec((1, valid, D_INNER), lambda b, c: (b, c, 0)),
                   pl.BlockSpec((1, HALO, CONV_DIM), lambda b, c: (b, 0, 0)),
                   pl.BlockSpec((1, D_INNER, D_STATE), lambda b, c: (b, 0, 0))],
        out_shape=[jax.ShapeDtypeStruct((bsz, t, D_INNER), g_dtype),
                   jax.ShapeDtypeStruct((bsz, HALO, CONV_DIM), F32),
                   jax.ShapeDtypeStruct((bsz, D_INNER, D_STATE), F32)],
        scratch_shapes=[pltpu.VMEM((L + HALO, CONV_DIM), F32),
                        pltpu.VMEM((L, CONV_DIM), F32),
                        pltpu.VMEM((D_INNER, L), F32),
                        pltpu.VMEM((L, D_INNER), F32)],
        compiler_params=_cparams(("parallel", "arbitrary"), 48),
        name="ssd_chunk_scan",
    )(z, xbc, dtr, conv_in, h0, conv_w, conv_b, head_params, d_skip, norm_w)


ROUTER_GRP0 = 0
ROUTER_EXP0 = N_EXPERT_GROUPS


def _route(logits):
    lane = lax.broadcasted_iota(jnp.int32, logits.shape, 1)
    big = jnp.int32(2 * LANES)
    is_grp = lane < N_EXPERT_GROUPS
    gl = jnp.where(is_grp, logits, -jnp.inf)
    gmax = jnp.max(gl, axis=-1, keepdims=True)
    gidx = jnp.min(jnp.where(gl == gmax, lane, big), axis=-1, keepdims=True)
    p_grp = 1.0 / jnp.sum(jnp.where(is_grp, jnp.exp(gl - gmax), 0.0), axis=-1, keepdims=True)
    e_rel = lane - ROUTER_EXP0
    in_grp = (e_rel >= gidx * EXPERTS_PER_GROUP) & (e_rel < (gidx + 1) * EXPERTS_PER_GROUP)
    el = jnp.where(in_grp, logits, -jnp.inf)
    m1 = jnp.max(el, axis=-1, keepdims=True)
    i1 = jnp.min(jnp.where(el == m1, lane, big), axis=-1, keepdims=True)
    el2 = jnp.where(lane == i1, -jnp.inf, el)
    m2 = jnp.max(el2, axis=-1, keepdims=True)
    i2 = jnp.min(jnp.where(el2 == m2, lane, big), axis=-1, keepdims=True)
    e2 = jnp.exp(m2 - m1)
    den = 1.0 + e2
    w1 = (1.0 / den) * p_grp
    w2 = (e2 / den) * p_grp
    return jnp.where(lane == i1, w1, jnp.where(lane == i2, w2, 0.0))


def _moe_kernel(h_ref, nw_ref, sc_ref, sh_ref, gf_ref, wr_ref, br_ref, w1_ref, w3_ref, w2_ref,
                o_ref, u_ref, gates_ref, acc_ref):
    e = pl.program_id(2)

    @pl.when(e == 0)
    def _():
        u = _norm_mod(h_ref[0], nw_ref[...], sc_ref[0], sh_ref[0])
        logits = jnp.dot(u, wr_ref[...], precision=HIGHEST, preferred_element_type=F32) + br_ref[...]
        gates_ref[...] = _route(logits)
        u_ref[...] = u.astype(BF16)
        acc_ref[...] = jnp.zeros_like(acc_ref)

    u = u_ref[...]
    a = _silu(jnp.dot(u, w1_ref[0], preferred_element_type=F32)) * jnp.dot(u, w3_ref[0], preferred_element_type=F32)
    gates = gates_ref[...]
    lane = lax.broadcasted_iota(jnp.int32, gates.shape, 1)
    gcol = jnp.sum(jnp.where(lane == e + ROUTER_EXP0, gates, 0.0), axis=-1, keepdims=True)
    acc_ref[...] += jnp.dot((a * gcol).astype(BF16), w2_ref[0], preferred_element_type=F32)

    @pl.when(e == pl.num_programs(2) - 1)
    def _():
        o_ref[0] = h_ref[0] + gf_ref[0] * acc_ref[...]


def _moe(h, nw, sc, sh, gf, wr, br, w1, w3, w2, tm):
    bsz, t, d = h.shape
    n_e, _, f = w1.shape
    return pl.pallas_call(
        _moe_kernel,
        grid=(bsz, t // tm, n_e),
        in_specs=[pl.BlockSpec((1, tm, d), lambda b, i, e: (b, i, 0)),
                  pl.BlockSpec((1, d), lambda b, i, e: (0, 0)),
                  _mod_spec(sc, tm), _mod_spec(sh, tm), _mod_spec(gf, tm),
                  pl.BlockSpec((d, LANES), lambda b, i, e: (0, 0)),
                  pl.BlockSpec((1, LANES), lambda b, i, e: (0, 0)),
                  pl.BlockSpec((1, d, f), lambda b, i, e: (e, 0, 0)),
                  pl.BlockSpec((1, d, f), lambda b, i, e: (e, 0, 0)),
                  pl.BlockSpec((1, f, d), lambda b, i, e: (e, 0, 0))],
        out_specs=pl.BlockSpec((1, tm, d), lambda b, i, e: (b, i, 0)),
        out_shape=jax.ShapeDtypeStruct((bsz, t, d), F32),
        scratch_shapes=[pltpu.VMEM((tm, d), BF16), pltpu.VMEM((tm, LANES), F32), pltpu.VMEM((tm, d), F32)],
        compiler_params=_cparams(("parallel", "parallel", "arbitrary"), 48),
        name="hmoe",
    )(h, nw, sc, sh, gf, wr, br, w1, w3, w2)


N_SPLIT = 3
FEAT_POS_LO = HEAD_DIM
FEAT_POS_HI = FEAT_POS_LO + N_SPLIT
FEAT_ONE = FEAT_POS_HI + N_SPLIT


def _split3(x):
    h1 = x.astype(BF16).astype(F32)
    r = x - h1
    h2 = r.astype(BF16).astype(F32)
    h3 = (r - h2).astype(BF16).astype(F32)
    return h1, h2, h3


def _feature_lanes(lane, base, parts, other):
    out = other
    for k, part in enumerate(parts):
        out = jnp.where(lane == base + k, part, out)
    return out


def _key_tiles(x, feat):
    lane = lax.broadcasted_iota(jnp.int32, x.shape, 1)
    rolled = pltpu.roll(x, HEAD_DIM, 1)
    lo = lane < HEAD_DIM
    return jnp.where(lo, x, feat), jnp.where(lo, rolled, feat)


def _kv_kernel(h_ref, nw_ref, sc_ref, sh_ref, wk_ref, wv_ref, knw_ref, r_ref, rt_ref,
               kf_ref, vf_ref, *dup_refs, tm, with_dup):
    u = _norm_mod(h_ref[0], nw_ref[...], sc_ref[0], sh_ref[0]).astype(BF16)
    k = jnp.dot(u, wk_ref[...], preferred_element_type=F32)
    v = jnp.dot(u, wv_ref[...], preferred_element_type=F32)
    kn = _head_rmsnorm(k, r_ref[...], rt_ref[...]) * knw_ref[...]
    kf_ref[0] = kn
    vf_ref[0] = v
    if with_dup:
        kd_ref, vt_ref, km_ref = dup_refs
        lane = lax.broadcasted_iota(jnp.int32, (tm, LANES), 1)
        pos = pl.program_id(1) * tm + lax.broadcasted_iota(jnp.int32, (tm, LANES), 0)
        pos_lo = pos % MOBA_BLOCK
        feat = jnp.zeros((tm, LANES), F32)
        feat = _feature_lanes(lane, FEAT_POS_LO, [pos_lo.astype(F32)] * N_SPLIT, feat)
        feat = _feature_lanes(lane, FEAT_POS_HI, [(pos - pos_lo).astype(F32)] * N_SPLIT, feat)
        feat = _feature_lanes(lane, FEAT_ONE, [jnp.ones((tm, LANES), F32)] * N_SPLIT, feat)
        lane_row = lax.broadcasted_iota(jnp.int32, (1, LANES), 1)
        for cidx in range(N_KV_HEADS // 2):
            d0, d1 = _key_tiles(kn[:, cidx * LANES:(cidx + 1) * LANES], feat)
            for hh, dd in ((2 * cidx, d0), (2 * cidx + 1, d1)):
                hsl = slice(hh * LANES, (hh + 1) * LANES)
                kd_ref[0, :, hsl] = dd.astype(BF16)
                for blk in range(tm // MOBA_BLOCK):
                    mean = jnp.mean(dd[blk * MOBA_BLOCK:(blk + 1) * MOBA_BLOCK], axis=0, keepdims=True)
                    km_ref[0, blk, :, hsl] = jnp.where(lane_row < HEAD_DIM, mean, 0.0)
        for blk in range(tm // MOBA_BLOCK):
            vt_ref[0, blk] = v[blk * MOBA_BLOCK:(blk + 1) * MOBA_BLOCK].T.astype(BF16)


def _kv_proj(h, nw, sc, sh, wk, wv, knw, r, rt, tm, with_dup):
    bsz, t, d = h.shape
    kvw = wk.shape[1]
    const = lambda shape: pl.BlockSpec(shape, lambda b, i: (0,) * len(shape))
    out_specs = [pl.BlockSpec((1, tm, kvw), lambda b, i: (b, i, 0))] * 2
    out_shape = [jax.ShapeDtypeStruct((bsz, t, kvw), F32)] * 2
    if with_dup:
        dupw = N_KV_HEADS * LANES
        nblk = tm // MOBA_BLOCK
        out_specs += [pl.BlockSpec((1, tm, dupw), lambda b, i: (b, i, 0)),
                      pl.BlockSpec((1, nblk, kvw, MOBA_BLOCK), lambda b, i: (b, i, 0, 0)),
                      pl.BlockSpec((1, nblk, 1, dupw), lambda b, i: (b, i, 0, 0))]
        out_shape += [jax.ShapeDtypeStruct((bsz, t, dupw), BF16),
                      jax.ShapeDtypeStruct((bsz, t // MOBA_BLOCK, kvw, MOBA_BLOCK), BF16),
                      jax.ShapeDtypeStruct((bsz, t // MOBA_BLOCK, 1, dupw), F32)]
    return pl.pallas_call(
        functools.partial(_kv_kernel, tm=tm, with_dup=with_dup),
        grid=(bsz, t // tm),
        in_specs=[pl.BlockSpec((1, tm, d), lambda b, i: (b, i, 0)),
                  const((1, d)), _mod_spec(sc, tm), _mod_spec(sh, tm),
                  const((d, kvw)), const((d, kvw)), const((1, kvw)),
                  const((kvw, LANES)), const((LANES, kvw))],
        out_specs=out_specs,
        out_shape=out_shape,
        compiler_params=_cparams(("parallel", "parallel"), 48),
        name="shared_kv",
    )(h, nw, sc, sh, wk, wv, knw, r, rt)


def _q_kernel(h_ref, nw_ref, sc_ref, sh_ref, wq_ref, qnw_ref, r_ref, rt_ref, q_ref):
    u = _norm_mod(h_ref[0], nw_ref[...], sc_ref[0], sh_ref[0]).astype(BF16)
    q = jnp.dot(u, wq_ref[...], preferred_element_type=F32)
    q_ref[0] = (_head_rmsnorm(q, r_ref[...], rt_ref[...]) * qnw_ref[...]).astype(q_ref.dtype)


def _q_proj(h, nw, sc, sh, wq, qnw, r, rt, tm, out_dtype):
    bsz, t, d = h.shape
    const = lambda shape: pl.BlockSpec(shape, lambda b, i: (0,) * len(shape))
    return pl.pallas_call(
        _q_kernel,
        grid=(bsz, t // tm),
        in_specs=[pl.BlockSpec((1, tm, d), lambda b, i: (b, i, 0)),
                  const((1, d)), _mod_spec(sc, tm), _mod_spec(sh, tm),
                  const((d, d)), const((1, d)), const((d, LANES)), const((LANES, d))],
        out_specs=pl.BlockSpec((1, tm, d), lambda b, i: (b, i, 0)),
        out_shape=jax.ShapeDtypeStruct((bsz, t, d), out_dtype),
        compiler_params=_cparams(("parallel", "parallel"), 48),
        name="q_proj",
    )(h, nw, sc, sh, wq, qnw, r, rt)


def _top_blocks(gate, n_valid, axis):
    pos = lax.broadcasted_iota(jnp.int32, gate.shape, axis)
    gm = jnp.where(pos < n_valid, gate, -jnp.inf)
    sel = jnp.zeros(gate.shape, F32)
    for _ in range(MOBA_TOPK):
        mx = jnp.max(gm, axis=axis, keepdims=True)
        cand = (gm == mx) & (mx > -jnp.inf)
        idx = jnp.min(jnp.where(cand, pos, jnp.int32(2 * LANES)), axis=axis, keepdims=True)
        pick = pos == idx
        sel = jnp.where(pick, 1.0, sel)
        gm = jnp.where(pick, -jnp.inf, gm)
    return sel


def _alibi_slope(head):
    return jnp.exp2(-8.0 * (head + 1).astype(F32) / N_HEADS)


ATT_GROUP = 4


def _attn_prompt_kernel(q_ref, k_ref, vt_ref, km_ref, o_ref,
                        qt_ref, sa_ref, sb_ref, sel_ref, m_ref, l_ref, acc_ref):
    g = pl.program_id(1)
    i = pl.program_id(2)
    blk = MOBA_BLOCK
    cols = 2 * blk
    grp = ATT_GROUP

    qt = q_ref[0].astype(F32)
    lane = lax.broadcasted_iota(jnp.int32, (blk, LANES), 1)
    q2 = jnp.concatenate([jnp.where(lane < HEAD_DIM, qt, 0.0),
                          jnp.where(lane < HEAD_DIM, pltpu.roll(qt, HEAD_DIM, 1), 0.0)], axis=0)
    r2 = lax.broadcasted_iota(jnp.int32, (cols, LANES), 0)
    l2 = lax.broadcasted_iota(jnp.int32, (cols, LANES), 1)
    second = r2 >= blk
    slope = _alibi_slope(2 * g + second.astype(jnp.int32))
    pos_q = jnp.where(second, r2 - blk, r2) + i * blk
    slope_parts = _split3(slope)
    q2 = _feature_lanes(l2, FEAT_POS_LO, slope_parts, q2)
    q2 = _feature_lanes(l2, FEAT_POS_HI, slope_parts, q2)
    q2 = _feature_lanes(l2, FEAT_ONE, _split3(-slope * pos_q.astype(F32)), q2)
    q2t = q2.T
    gate = jnp.dot(km_ref[0], q2t, precision=HIGHEST, preferred_element_type=F32)
    sel_ref[...] = _top_blocks(gate, i, 0)
    is_q = lax.broadcasted_iota(jnp.int32, (LANES, 1), 0) < HEAD_DIM
    qt_ref[...] = (q2t * jnp.where(is_q, HEAD_DIM ** -0.5, 1.0)).astype(BF16)

    def scores(n):
        start = pl.multiple_of(n * blk, blk)
        return jnp.dot(k_ref[0, pl.ds(start, blk), :], qt_ref[...], preferred_element_type=F32)

    def block_stats(s, n):
        mb = jnp.max(s, axis=0, keepdims=True)
        p = jnp.exp(s - mb)
        return mb, jnp.sum(p, axis=0, keepdims=True), jnp.dot(vt_ref[0, n], p.astype(BF16),
                                                               preferred_element_type=F32)

    def merge(parts):
        m_old = m_ref[...]
        m_new = m_old
        for mb, _, _ in parts:
            m_new = jnp.maximum(m_new, mb)
        alpha = jnp.exp(m_old - m_new)
        l_new = alpha * l_ref[...]
        acc_new = alpha * acc_ref[...]
        for mb, lb, ab in parts:
            wgt = jnp.exp(mb - m_new)
            l_new = l_new + wgt * lb
            acc_new = acc_new + wgt * ab
        m_ref[...] = m_new
        l_ref[...] = l_new
        acc_ref[...] = acc_new

    rr = lax.broadcasted_iota(jnp.int32, (blk, cols), 0)
    cc = lax.broadcasted_iota(jnp.int32, (blk, cols), 1)
    visible = jnp.where(cc >= blk, cc - blk, cc) >= rr
    m0, l0, a0 = block_stats(jnp.where(visible, scores(i), -jnp.inf), i)
    m_ref[...] = m0
    l_ref[...] = l0
    acc_ref[...] = a0

    n_groups = (i + grp - 1) // grp

    def score_group(kk, s_ref):
        for b in range(grp):
            n = jnp.minimum(kk * grp + b, i - 1)
            s_ref[b * blk:(b + 1) * blk, :] = scores(n)

    def softmax_group(kk, s_ref):
        parts = []
        for b in range(grp):
            n_raw = kk * grp + b
            n = jnp.minimum(n_raw, i - 1)
            mb, lb, ab = block_stats(s_ref[b * blk:(b + 1) * blk, :], n)
            counted = sel_ref[pl.ds(n, 1), :] * (n_raw < i).astype(F32) > 0.0
            parts.append((jnp.where(counted, mb, NEG_BIG), lb, ab))
        merge(parts)

    @pl.when(n_groups > 0)
    def _first():
        score_group(0, sa_ref)

    def step(kk, carry):
        score_group(2 * kk + 1, sb_ref)
        softmax_group(2 * kk, sa_ref)
        score_group(2 * kk + 2, sa_ref)
        softmax_group(2 * kk + 1, sb_ref)
        return carry

    lax.fori_loop(0, n_groups // 2, step, 0)

    @pl.when(n_groups % 2 == 1)
    def _last():
        softmax_group(n_groups - 1, sa_ref)

    o = acc_ref[...] / l_ref[...]
    o_ref[0] = jnp.concatenate([o[:, 0:blk], o[:, blk:cols]], axis=0).T.astype(o_ref.dtype)


def _attn_prompt(q, kd, vt, kmd):
    bsz, t, d = q.shape
    blk = MOBA_BLOCK
    nb = t // blk
    cols = 2 * blk
    return pl.pallas_call(
        _attn_prompt_kernel,
        grid=(bsz, N_KV_HEADS, nb),
        in_specs=[pl.BlockSpec((1, blk, LANES), lambda b, g, i: (b, i, g)),
                  pl.BlockSpec((1, t, LANES), lambda b, g, i: (b, 0, g)),
                  pl.BlockSpec((1, nb, HEAD_DIM, blk), lambda b, g, i: (b, 0, g, 0)),
                  pl.BlockSpec((1, LANES, LANES), lambda b, g, i: (b, 0, g))],
        out_specs=pl.BlockSpec((1, blk, LANES), lambda b, g, i: (b, i, g)),
        out_shape=jax.ShapeDtypeStruct((bsz, t, d), BF16),
        scratch_shapes=[pltpu.VMEM((LANES, cols), BF16),
                        pltpu.VMEM((ATT_GROUP * blk, cols), F32), pltpu.VMEM((ATT_GROUP * blk, cols), F32),
                        pltpu.VMEM((LANES, cols), F32), pltpu.VMEM((1, cols), F32),
                        pltpu.VMEM((1, cols), F32), pltpu.VMEM((HEAD_DIM, cols), F32)],
        compiler_params=_cparams(("parallel", "parallel", "arbitrary"), 48),
        name="moba_prompt",
    )(q, kd, vt, kmd)


def _attn_sample_kernel(pt_ref, q_ref, kn_ref, vn_ref, *rest, n_pages, past_len):
    del pt_ref
    k_refs = rest[:n_pages]
    v_refs = rest[n_pages:2 * n_pages]
    o_ref = rest[2 * n_pages]
    s_ref = rest[2 * n_pages + 1]
    t_new = q_ref.shape[1]
    kvw = N_KV_HEADS * HEAD_DIM
    rows = N_HEADS * t_new
    n_past_blk = past_len // MOBA_BLOCK
    pages_per_blk = MOBA_BLOCK // PAGE_SIZE
    q = q_ref[0]
    lane8 = lax.broadcasted_iota(jnp.int32, (t_new, LANES), 1)
    zeros8 = jnp.zeros((t_new, LANES), F32)
    row_blocks = []
    for h in range(N_HEADS):
        g = h // 2
        tile = q[:, g * LANES:(g + 1) * LANES]
        if h % 2 != g % 2:
            tile = pltpu.roll(tile, HEAD_DIM, 1)
        keep = (lane8 < HEAD_DIM) if g % 2 == 0 else (lane8 >= HEAD_DIM)
        tile = jnp.where(keep, tile, 0.0)
        row_blocks.append(jnp.concatenate([tile if cidx == g // 2 else zeros8 for cidx in range(kvw // LANES)], axis=1))
    qm = jnp.concatenate(row_blocks, axis=0)
    qmb = (qm * (HEAD_DIM ** -0.5)).astype(BF16)

    r1 = lax.broadcasted_iota(jnp.int32, (rows, 1), 0)
    slope = _alibi_slope(r1 // t_new)
    rr = lax.broadcasted_iota(jnp.int32, (rows, PAGE_SIZE), 0)
    cc = lax.broadcasted_iota(jnp.int32, (rows, PAGE_SIZE), 1)
    tok = rr % t_new
    d_page0 = (past_len + tok - cc).astype(F32)

    rows_km = lax.broadcasted_iota(jnp.int32, (LANES, kvw), 0)
    km = jnp.zeros((LANES, kvw), F32)
    for n in range(n_past_blk):
        acc = jnp.zeros((1, kvw), F32)
        for pp in range(pages_per_blk):
            acc = acc + jnp.sum(k_refs[n * pages_per_blk + pp][0], axis=0, keepdims=True)
        km = jnp.where(rows_km == n, acc * (1.0 / MOBA_BLOCK), km)
    gate = lax.dot_general(qm, km, NT_DIMS, precision=HIGHEST, preferred_element_type=F32)
    sel = _top_blocks(gate, n_past_blk, 1)
    lane_s = lax.broadcasted_iota(jnp.int32, sel.shape, 1)

    for p in range(n_pages):
        kp = k_refs[p][0].astype(BF16)
        s = lax.dot_general(qmb, kp, NT_DIMS, preferred_element_type=F32)
        s = s - slope * (d_page0 - float(p * PAGE_SIZE))
        picked = jnp.sum(jnp.where(lane_s == p // pages_per_blk, sel, 0.0), axis=-1, keepdims=True) > 0.0
        s_ref[:, p * PAGE_SIZE:(p + 1) * PAGE_SIZE] = jnp.where(picked, s, -jnp.inf)
    pad = jnp.zeros((PAGE_SIZE - t_new, kvw), F32)
    k_new = jnp.concatenate([kn_ref[0], pad], axis=0).astype(BF16)
    v_new = jnp.concatenate([vn_ref[0], pad], axis=0).astype(BF16)
    d_own = (tok - cc).astype(F32)
    s = lax.dot_general(qmb, k_new, NT_DIMS, preferred_element_type=F32) - slope * d_own
    s_ref[:, n_pages * PAGE_SIZE:] = jnp.where((d_own >= 0.0) & (cc < t_new), s, -jnp.inf)

    s_all = s_ref[...]
    m = jnp.max(s_all, axis=-1, keepdims=True)
    p_all = jnp.exp(s_all - m)
    l = jnp.sum(p_all, axis=-1, keepdims=True)
    pb = p_all.astype(BF16)
    out = jnp.dot(pb[:, n_pages * PAGE_SIZE:], v_new, preferred_element_type=F32)
    for p in range(n_pages):
        out = out + jnp.dot(pb[:, p * PAGE_SIZE:(p + 1) * PAGE_SIZE], v_refs[p][0].astype(BF16),
                            preferred_element_type=F32)
    out = out / l

    lo8 = lane8 < HEAD_DIM
    for cidx in range(N_HEADS // 2):
        src = slice((cidx // 2) * LANES, (cidx // 2 + 1) * LANES)
        a = out[(2 * cidx) * t_new:(2 * cidx + 1) * t_new, src]
        b = out[(2 * cidx + 1) * t_new:(2 * cidx + 2) * t_new, src]
        if cidx % 2 == 1:
            a = pltpu.roll(a, HEAD_DIM, 1)
        else:
            b = pltpu.roll(b, HEAD_DIM, 1)
        o_ref[0, :, cidx * LANES:(cidx + 1) * LANES] = jnp.where(lo8, a, b)


def _attn_sample(q, k_new, v_new, cache_k, cache_v, page_table, t_new):
    n_seq, n_pages = page_table.shape
    d = q.shape[-1]
    kvw = k_new.shape[-1]
    past_len = n_pages * PAGE_SIZE
    rows = N_HEADS * t_new
    page_spec = lambda p: pl.BlockSpec((1, PAGE_SIZE, kvw), lambda s, pt: (pt[s, p], 0, 0))
    grid_spec = pltpu.PrefetchScalarGridSpec(
        num_scalar_prefetch=1,
        grid=(n_seq,),
        in_specs=[pl.BlockSpec((1, t_new, d), lambda s, pt: (0, s, 0)),
                  pl.BlockSpec((1, t_new, kvw), lambda s, pt: (0, s, 0)),
                  pl.BlockSpec((1, t_new, kvw), lambda s, pt: (0, s, 0))]
        + [page_spec(p) for p in range(n_pages)] + [page_spec(p) for p in range(n_pages)],
        out_specs=pl.BlockSpec((1, t_new, d), lambda s, pt: (0, s, 0)),
        scratch_shapes=[pltpu.VMEM((rows, (n_pages + 1) * PAGE_SIZE), F32)],
    )
    return pl.pallas_call(
        functools.partial(_attn_sample_kernel, n_pages=n_pages, past_len=past_len),
        grid_spec=grid_spec,
        out_shape=jax.ShapeDtypeStruct(q.shape, F32),
        compiler_params=_cparams(("arbitrary",), 48),
        name="moba_sample",
    )(page_table, q, k_new, v_new, *([cache_k] * n_pages), *([cache_v] * n_pages))


def _head_sum_matrices(n_heads):
    w = n_heads * HEAD_DIM
    head_of_lane = jnp.arange(w) // HEAD_DIM
    r = (head_of_lane[:, None] == jnp.arange(LANES)[None, :]).astype(BF16)
    return r, r.T


def _prep_weights(p):
    bf = lambda x: x.astype(BF16)
    w_in = p['ssm_w_in']
    wdt = jnp.pad(w_in[:, :, D_INNER + CONV_DIM:], ((0, 0), (0, 0), (0, LANES - SSM_HEADS)))
    pad_heads = lambda x: jnp.pad(x, ((0, 0), (0, LANES - SSM_HEADS)))
    head_params = jnp.stack([pad_heads(p['ssm_dt_bias']), pad_heads(p['ssm_a_log'])], axis=1)
    head_params = jnp.pad(head_params, ((0, 0), (0, SUBLANES - 2), (0, 0)))
    router_w = jnp.concatenate([p['moe_w_grp'], p['moe_w_exp']], axis=-1)
    router_w = jnp.pad(router_w, ((0, 0), (0, 0), (0, LANES - router_w.shape[-1])))
    router_b = jnp.concatenate([p['moe_b_grp'], p['moe_b_exp']], axis=-1)
    router_b = jnp.pad(router_b, ((0, 0), (0, LANES - router_b.shape[-1])))[:, None, :]
    rq, rqt = _head_sum_matrices(N_HEADS)
    rk, rkt = _head_sum_matrices(N_KV_HEADS)
    return dict(
        wz=bf(w_in[:, :, :D_INNER]), wxbc=bf(w_in[:, :, D_INNER:D_INNER + CONV_DIM]), wdt=bf(wdt),
        head_params=head_params, d_skip=jnp.repeat(p['ssm_d'], SSM_HEAD_DIM, axis=-1)[:, None, :],
        conv_w=p['ssm_conv_w'], conv_b=p['ssm_conv_b'][:, None, :], ssm_norm_w=p['ssm_norm_w'][:, None, :],
        w_out=bf(p['ssm_w_out']),
        norm_mix_w=p['norm_mix_w'][:, None, :], norm_ffn_w=p['norm_ffn_w'][:, None, :],
        kv_norm_w=p['kv_norm_w'][None, :], wk=bf(p['w_k']), wv=bf(p['w_v']),
        k_norm_w=jnp.tile(p['k_norm_w'], N_KV_HEADS)[None, :],
        wq=bf(p['attn_w_q']), q_norm_w=jnp.tile(p['q_norm_w'], (1, N_HEADS))[:, None, :], wo=bf(p['attn_w_o']),
        router_w=router_w, router_b=router_b,
        w1=bf(p['moe_w1']), w3=bf(p['moe_w3']), w2=bf(p['moe_w2']),
        rq=rq, rqt=rqt, rk=rk, rkt=rkt,
    )


def _trunk(h, mods, kv_mod, w, *, sample, conv0=None, ssm0=None, cache=None):
    bsz, t, _ = h.shape
    tm = min(t, 1024)
    conv_out, ssm_out = [], []
    k_new = v_new = kd = vt = kmd = None
    for layer in range(DEPTH):
        sh_m, sc_m, g_m, sh_f, sc_f, g_f = mods[layer]
        nmw = w['norm_mix_w'][layer]
        if layer < N_A_LAYERS:
            z = _nm_matmul(h, nmw, sc_m, sh_m, w['wz'][layer], tm, 1024)
            xbc = _nm_matmul(h, nmw, sc_m, sh_m, w['wxbc'][layer], tm, 1024)
            dtr = _nm_matmul(h, nmw, sc_m, sh_m, w['wdt'][layer], tm, LANES)
            ssd_args = (w['conv_w'][layer], w['conv_b'][layer], w['head_params'][layer], w['d_skip'][layer],
                        w['ssm_norm_w'][layer])
            if sample:
                n_seq, t_new = conv0.shape[1], cache[3]
                seqs = lambda x: x.reshape(n_seq, t_new, x.shape[-1])
                conv_in = jnp.pad(conv0[layer], ((0, 0), (HALO - (CONV_WIDTH - 1), 0), (0, 0)))
                g, conv8, st = _ssd(seqs(z), seqs(xbc), seqs(dtr), conv_in,
                                    ssm0[layer].reshape(n_seq, D_INNER, D_STATE), *ssd_args, g_dtype=F32)
                g = g.reshape(1, t, D_INNER)
            else:
                conv_in = jnp.zeros((bsz, HALO, CONV_DIM), F32)
                g, conv8, st = _ssd(z, xbc, dtr, conv_in, None, *ssd_args, g_dtype=BF16)
            conv_out.append(conv8[:, HALO - (CONV_WIDTH - 1):])
            ssm_out.append(st.reshape(st.shape[0], SSM_HEADS, SSM_HEAD_DIM, D_STATE))
            h = _mm_res(g, w['w_out'][layer], h, g_m, tm)
        else:
            j = layer - N_A_LAYERS
            if sample:
                q = _q_proj(h, nmw, sc_m, sh_m, w['wq'][j], w['q_norm_w'][j], w['rq'], w['rqt'], tm, F32)
                o = _attn_sample(q, k_new, v_new, cache[0], cache[1], cache[2], cache[3])
            else:
                q = _q_proj(h, nmw, sc_m, sh_m, w['wq'][j], w['q_norm_w'][j], w['rq'], w['rqt'], tm, BF16)
                o = _attn_prompt(q, kd, vt, kmd)
            h = _mm_res(o, w['wo'][j], h, g_m, tm)
        h = _moe(h, w['norm_ffn_w'][layer], sc_f, sh_f, g_f, w['router_w'][layer], w['router_b'][layer],
                 w['w1'][layer], w['w3'][layer], w['w2'][layer], tm)
        if layer == N_A_LAYERS - 1:
            kv_tm = min(t, 512)
            res = _kv_proj(h, w['kv_norm_w'], kv_mod[1], kv_mod[0], w['wk'], w['wv'], w['k_norm_w'],
                           w['rk'], w['rkt'], kv_tm, with_dup=not sample)
            k_new, v_new = res[0], res[1]
            if not sample:
                kd, vt = res[2], res[3]
                kmd = res[4].reshape(bsz, t // MOBA_BLOCK, N_KV_HEADS * LANES)
                kmd = jnp.pad(kmd, ((0, 0), (0, LANES - kmd.shape[1]), (0, 0)))
    return h, jnp.stack(conv_out), jnp.stack(ssm_out), k_new, v_new


def kernel(x_prompt, x_sample, state_conv, state_ssm, cache_k, cache_v, page_table, c_prompt, c_sample, w_mod, b_mod, norm_mix_w, norm_ffn_w, ssm_w_in, ssm_conv_w, ssm_conv_b, ssm_dt_bias, ssm_a_log, ssm_d, ssm_norm_w, ssm_w_out, kv_w_mod, kv_b_mod, kv_norm_w, w_k, w_v, k_norm_w, attn_w_q, q_norm_w, attn_w_o, moe_w_grp, moe_b_grp, moe_w_exp, moe_b_exp, moe_w1, moe_w3, moe_w2):
    params = dict(ssm_w_in=ssm_w_in, ssm_conv_w=ssm_conv_w, ssm_conv_b=ssm_conv_b, ssm_dt_bias=ssm_dt_bias,
                  ssm_a_log=ssm_a_log, ssm_d=ssm_d, ssm_norm_w=ssm_norm_w, ssm_w_out=ssm_w_out,
                  norm_mix_w=norm_mix_w, norm_ffn_w=norm_ffn_w, kv_norm_w=kv_norm_w, w_k=w_k, w_v=w_v,
                  k_norm_w=k_norm_w, attn_w_q=attn_w_q, q_norm_w=q_norm_w, attn_w_o=attn_w_o,
                  moe_w_grp=moe_w_grp, moe_b_grp=moe_b_grp, moe_w_exp=moe_w_exp, moe_b_exp=moe_b_exp,
                  moe_w1=moe_w1, moe_w3=moe_w3, moe_w2=moe_w2)
    w = _prep_weights(params)

    bp, seq, d = x_prompt.shape
    n_seq, t_new, _ = x_sample.shape
    n_pages = page_table.shape[1]
    past_len = n_pages * PAGE_SIZE
    assert seq % MOBA_BLOCK == 0 and seq // MOBA_BLOCK <= LANES
    assert past_len % MOBA_BLOCK == 0 and t_new <= MOBA_BLOCK and t_new % SUBLANES == 0

    n_c = bp + n_seq
    n_c_pad = -(-n_c // SUBLANES) * SUBLANES
    c_all = jnp.pad(jnp.concatenate([c_prompt, c_sample], axis=0), ((0, n_c_pad - n_c), (0, 0)))
    mod_all = _mod_vectors(c_all, w_mod, b_mod[:, None, :], 1536)
    kv_all = _mod_vectors(c_all, kv_w_mod[None], kv_b_mod[None, None, :], 1024)[0]

    def group_mods(lo, hi, per_token_repeat):
        def shape(x):
            if per_token_repeat:
                return jnp.repeat(x, per_token_repeat, axis=0)[None]
            return x[:, None, :]
        mods = [[shape(mod_all[l, lo:hi, k * d:(k + 1) * d]) for k in range(6)] for l in range(DEPTH)]
        kvm = [shape(kv_all[lo:hi, k * d:(k + 1) * d]) for k in range(2)]
        return mods, kvm

    mods_p, kv_p = group_mods(0, bp, 0)
    y_p, conv_p, ssm_p, k_p, v_p = _trunk(x_prompt, mods_p, kv_p, w, sample=False)

    mods_s, kv_s = group_mods(bp, n_c, t_new)
    kvw = N_KV_HEADS * HEAD_DIM
    cache = (cache_k.reshape(-1, PAGE_SIZE, kvw), cache_v.reshape(-1, PAGE_SIZE, kvw), page_table, t_new)
    y_s, conv_s, ssm_s, k_s, v_s = _trunk(x_sample.reshape(1, n_seq * t_new, d), mods_s, kv_s, w,
                                          sample=True, conv0=state_conv, ssm0=state_ssm, cache=cache)

    heads = lambda x, b, t: x.reshape(b, t, N_KV_HEADS, HEAD_DIM)
    return (y_p, y_s.reshape(n_seq, t_new, d), conv_p, ssm_p, heads(k_p, bp, seq), heads(v_p, bp, seq),
            conv_s, ssm_s, heads(k_s, n_seq, t_new), heads(v_s, n_seq, t_new))
```

```python
import functools

import jax
import jax.numpy as jnp
from jax import lax
from jax.experimental import pallas as pl
from jax.experimental.pallas import tpu as pltpu

F32 = jnp.float32
BF16 = jnp.bfloat16
HIGHEST = lax.Precision.HIGHEST

D_MODEL = 1024
DEPTH = 4
N_A_LAYERS = 2
D_INNER = 2048
SSM_HEADS = 32
SSM_HEAD_DIM = 64
SSM_GROUPS = 4
D_STATE = 128
CONV_WIDTH = 4
CONV_DIM = D_INNER + 2 * SSM_GROUPS * D_STATE
SSD_CHUNK = 128
N_HEADS = 16
HEAD_DIM = 64
N_KV_HEADS = 8
MOBA_BLOCK = 256
MOBA_TOPK = 3
N_EXPERT_GROUPS = 4
EXPERTS_PER_GROUP = 4
N_EXPERTS = 16
D_EXPERT = 256
PAGE_SIZE = 128
EPS = 1e-6

LANES = 128
SUBLANES = 8
MIB = 1024 * 1024
NEG_BIG = -1e30

NT_DIMS = (((1,), (1,)), ((), ()))


def _cparams(sem, vmem_mib):
    return pltpu.CompilerParams(dimension_semantics=sem, vmem_limit_bytes=vmem_mib * MIB)


def _sigmoid(x):
    return 1.0 / (1.0 + jnp.exp(-x))


def _silu(x):
    return x * _sigmoid(x)


def _norm_mod(h, nw, sc, sh):
    ms = jnp.mean(h * h, axis=-1, keepdims=True)
    return (h * lax.rsqrt(ms + EPS)) * nw * (1.0 + sc) + sh


def _split_bf16(x):
    hi = x.astype(BF16)
    lo = (x - hi.astype(F32)).astype(BF16)
    return hi, lo


def _head_rmsnorm(x, r, rt):
    sq = x * x
    hi, lo = _split_bf16(sq)
    ss = jnp.dot(hi, r, preferred_element_type=F32) + jnp.dot(lo, r, preferred_element_type=F32)
    inv = lax.rsqrt(ss * (1.0 / HEAD_DIM) + EPS)
    ih, il = _split_bf16(inv)
    invx = jnp.dot(ih, rt, preferred_element_type=F32) + jnp.dot(il, rt, preferred_element_type=F32)
    return x * invx


def _mod_kernel(c_ref, w_ref, b_ref, o_ref):
    ca = _silu(c_ref[...])
    o_ref[0] = jnp.dot(ca, w_ref[0], precision=HIGHEST, preferred_element_type=F32) + b_ref[0]


def _mod_vectors(c_all, w, b, tn):
    n_layers, d, n = w.shape
    m = c_all.shape[0]
    return pl.pallas_call(
        _mod_kernel,
        grid=(n_layers, n // tn),
        in_specs=[pl.BlockSpec((m, d), lambda l, j: (0, 0)),
                  pl.BlockSpec((1, d, tn), lambda l, j: (l, 0, j)),
                  pl.BlockSpec((1, 1, tn), lambda l, j: (l, 0, j))],
        out_specs=pl.BlockSpec((1, m, tn), lambda l, j: (l, 0, j)),
        out_shape=jax.ShapeDtypeStruct((n_layers, m, n), F32),
        compiler_params=_cparams(("arbitrary", "arbitrary"), 40),
        name="mod_vectors",
    )(c_all, w, b)


def _mod_spec(mod, tm):
    d = mod.shape[-1]
    if mod.shape[1] == 1:
        return pl.BlockSpec((1, 1, d), lambda b, i, *_: (b, 0, 0))
    return pl.BlockSpec((1, tm, d), lambda b, i, *_: (b, i, 0))


def _nm_matmul_kernel(h_ref, nw_ref, sc_ref, sh_ref, w_ref, o_ref, u_ref):
    @pl.when(pl.program_id(2) == 0)
    def _():
        u_ref[...] = _norm_mod(h_ref[0], nw_ref[...], sc_ref[0], sh_ref[0]).astype(BF16)

    o_ref[0] = jnp.dot(u_ref[...], w_ref[...], preferred_element_type=F32)


def _nm_matmul(h, nw, sc, sh, w, tm, tn):
    bsz, t, d = h.shape
    n = w.shape[1]
    return pl.pallas_call(
        _nm_matmul_kernel,
        grid=(bsz, t // tm, n // tn),
        in_specs=[pl.BlockSpec((1, tm, d), lambda b, i, j: (b, i, 0)),
                  pl.BlockSpec((1, d), lambda b, i, j: (0, 0)),
                  _mod_spec(sc, tm), _mod_spec(sh, tm),
                  pl.BlockSpec((d, tn), lambda b, i, j: (0, j))],
        out_specs=pl.BlockSpec((1, tm, tn), lambda b, i, j: (b, i, j)),
        out_shape=jax.ShapeDtypeStruct((bsz, t, n), F32),
        scratch_shapes=[pltpu.VMEM((tm, d), BF16)],
        compiler_params=_cparams(("parallel", "parallel", "arbitrary"), 48),
        name="norm_mod_matmul",
    )(h, nw, sc, sh, w)


def _mm_res_kernel(a_ref, w_ref, h_ref, g_ref, o_ref):
    acc = jnp.dot(a_ref[0].astype(BF16), w_ref[...], preferred_element_type=F32)
    o_ref[0] = h_ref[0] + g_ref[0] * acc


def _mm_res(a, w, h, gate, tm):
    bsz, t, k = a.shape
    d = w.shape[1]
    return pl.pallas_call(
        _mm_res_kernel,
        grid=(bsz, t // tm),
        in_specs=[pl.BlockSpec((1, tm, k), lambda b, i: (b, i, 0)),
                  pl.BlockSpec((k, d), lambda b, i: (0, 0)),
                  pl.BlockSpec((1, tm, d), lambda b, i: (b, i, 0)),
                  _mod_spec(gate, tm)],
        out_specs=pl.BlockSpec((1, tm, d), lambda b, i: (b, i, 0)),
        out_shape=jax.ShapeDtypeStruct((bsz, t, d), F32),
        compiler_params=_cparams(("parallel", "parallel"), 48),
        name="matmul_residual",
    )(a, w, h, gate)


HALO = SUBLANES
XC_TILE = 512


def _ssd_kernel(z_ref, xbc_ref, dt_ref, cs_ref, h0_ref, cw_ref, cb_ref, hp_ref, dsk_ref, nw_ref,
                g_ref, cso_ref, st_ref,
                xs_ref, xc_ref, xt_ref, y_ref, *, valid, has_init):
    L = SSD_CHUNK
    c = pl.program_id(1)

    @pl.when(c == 0)
    def _init():
        xs_ref[...] = jnp.zeros_like(xs_ref)
        xs_ref[0:HALO, :] = cs_ref[0]
        if has_init:
            st_ref[0] = h0_ref[0]
        else:
            st_ref[0] = jnp.zeros(st_ref.shape[1:], F32)

    xs_ref[HALO:HALO + valid, :] = xbc_ref[0]

    for j in range(CONV_DIM // XC_TILE):
        sl = slice(j * XC_TILE, (j + 1) * XC_TILE)
        first = HALO - (CONV_WIDTH - 1)
        acc = cb_ref[:, sl] + xs_ref[first:first + L, sl] * cw_ref[0:1, sl]
        for k in range(1, CONV_WIDTH):
            acc = acc + xs_ref[first + k:first + k + L, sl] * cw_ref[k:k + 1, sl]
        xc_ref[:, sl] = _silu(acc)

    cso_ref[0] = xs_ref[valid:valid + HALO, :]
    xs_ref[0:HALO, :] = xs_ref[L:L + HALO, :]

    dtr = dt_ref[0]
    if valid < L:
        dtr = jnp.concatenate([dtr, jnp.zeros((L - valid, LANES), F32)], axis=0)
    xx = dtr + hp_ref[0:1, :]
    dt = jnp.maximum(xx, 0.0) + jnp.log(1.0 + jnp.exp(-jnp.abs(xx)))
    row = lax.broadcasted_iota(jnp.int32, (L, L), 0)
    col = lax.broadcasted_iota(jnp.int32, (L, L), 1)
    if valid < L:
        dt = jnp.where(row < valid, dt, 0.0)
    a = dt * (-jnp.exp(hp_ref[1:2, :]))
    causal = row >= col
    a_cum = jnp.dot(causal.astype(F32), a, precision=HIGHEST, preferred_element_type=F32)
    a_cum_t = a_cum.T
    dt_t = dt.T
    lane_lo = col < HEAD_DIM
    sub_lo = row < HEAD_DIM

    for j in range(D_INNER // LANES):
        xt_ref[j * LANES:(j + 1) * LANES, :] = xc_ref[:, j * LANES:(j + 1) * LANES].T

    pairs_per_group = SSM_HEADS // SSM_GROUPS // 2
    grp_w = D_INNER // SSM_GROUPS
    for g in range(SSM_GROUPS):
        b_g = xc_ref[:, D_INNER + g * D_STATE:D_INNER + (g + 1) * D_STATE].astype(BF16)
        c_g = xc_ref[:, D_INNER + (SSM_GROUPS + g) * D_STATE:
                     D_INNER + (SSM_GROUPS + g + 1) * D_STATE].astype(BF16)
        cb = lax.dot_general(c_g, b_g, NT_DIMS, preferred_element_type=F32)
        st_g = st_ref[0, g * grp_w:(g + 1) * grp_w, :]
        y_off = lax.dot_general(c_g, st_g.astype(BF16), NT_DIMS, preferred_element_type=F32)
        xw_parts, dec_parts = [], []
        for jj in range(pairs_per_group):
            pair = g * pairs_per_group + jj
            s_mats, e_cols, w_rows, d_end = [], [], [], []
            for h in (2 * pair, 2 * pair + 1):
                colb = jnp.broadcast_to(a_cum[:, h:h + 1], (L, L))
                rowb = a_cum_t[h:h + 1, :]
                dec = jnp.where(causal, jnp.exp(colb - rowb), 0.0)
                s_mats.append((cb * dec * dt_t[h:h + 1, :]).astype(BF16))
                e_cols.append(jnp.exp(colb))
                a_last = a_cum_t[h:h + 1, L - 1:L]
                w_rows.append(dt_t[h:h + 1, :] * jnp.exp(a_last - rowb))
                d_end.append(jnp.broadcast_to(jnp.exp(a_last), (LANES, D_STATE)))
            psl = slice(pair * LANES, (pair + 1) * LANES)
            xp = xc_ref[:, psl]
            xpb = xp.astype(BF16)
            y_d = jnp.where(lane_lo,
                            jnp.dot(s_mats[0], xpb, preferred_element_type=F32),
                            jnp.dot(s_mats[1], xpb, preferred_element_type=F32))
            y_o = y_off[:, jj * LANES:(jj + 1) * LANES] * jnp.where(lane_lo, e_cols[0], e_cols[1])
            y_ref[:, psl] = y_d + y_o + dsk_ref[:, psl] * xp
            xw_parts.append((xt_ref[psl, :] * jnp.where(sub_lo, w_rows[0], w_rows[1])).astype(BF16))
            dec_parts.append(jnp.where(sub_lo, d_end[0], d_end[1]))
        xw_g = jnp.concatenate(xw_parts, axis=0)
        new = jnp.dot(xw_g, b_g, preferred_element_type=F32)
        st_ref[0, g * grp_w:(g + 1) * grp_w, :] = st_g * jnp.concatenate(dec_parts, axis=0) + new

    zz = z_ref[0]
    gg = y_ref[0:valid, :] * _silu(zz)
    for g in range(SSM_GROUPS):
        sl = slice(g * grp_w, (g + 1) * grp_w)
        seg = gg[:, sl]
        ms = jnp.mean(seg * seg, axis=-1, keepdims=True)
        g_ref[0, :, sl] = (seg * lax.rsqrt(ms + EPS) * nw_ref[:, sl]).astype(g_ref.dtype)


def _ssd(z, xbc, dtr, conv_in, h0, conv_w, conv_b, head_params, d_skip, norm_w, g_dtype):
    bsz, t, _ = z.shape
    L = SSD_CHUNK
    valid = min(L, t)
    assert t % valid == 0 and valid % SUBLANES == 0
    nc = t // valid
    has_init = h0 is not None
    if h0 is None:
        h0 = jnp.zeros((1, SUBLANES, D_STATE), F32)
        h0_spec = pl.BlockSpec((1, SUBLANES, D_STATE), lambda b, c: (0, 0, 0))
    else:
        h0_spec = pl.BlockSpec((1, D_INNER, D_STATE), lambda b, c: (b, 0, 0))
    full = lambda shape: pl.BlockSpec(shape, lambda b, c: (0,) * len(shape))
    return pl.pallas_call(
        functools.partial(_ssd_kernel, valid=valid, has_init=has_init),
        grid=(bsz, nc),
        in_specs=[pl.BlockSpec((1, valid, D_INNER), lambda b, c: (b, c, 0)),
                  pl.BlockSpec((1, valid, CONV_DIM), lambda b, c: (b, c, 0)),
                  pl.BlockSpec((1, valid, LANES), lambda b, c: (b, c, 0)),
                  pl.BlockSpec((1, HALO, CONV_DIM), lambda b, c: (b, 0, 0)),
                  h0_spec,
                  full((CONV_WIDTH, CONV_DIM)), full((1, CONV_DIM)), full((SUBLANES, LANES)),
                  full((1, D_INNER)), full((1, D_INNER))],
        out_specs=[pl.BlockSpec((1, valid, D_INNER), lambda b, c: (b, c, 0)),
                   pl.BlockSpec((1, HALO, CONV_DIM), lambda b, c: (b, 0, 0)),
                   pl.BlockSpec((1, D_INNER, D_STATE), lambda b, c: (b, 0, 0))],
        out_shape=[jax.ShapeDtypeStruct((bsz, t, D_INNER), g_dtype),
                   jax.ShapeDtypeStruct((bsz, HALO, CONV_DIM), F32),
                   jax.ShapeDtypeStruct((bsz, D_INNER, D_STATE), F32)],
        scratch_shapes=[pltpu.VMEM((L + HALO, CONV_DIM), F32),
                        pltpu.VMEM((L, CONV_DIM), F32),
                        pltpu.VMEM((D_INNER, L), F32),
                        pltpu.VMEM((L, D_INNER), F32)],
        compiler_params=_cparams(("parallel", "arbitrary"), 48),
        name="ssd_chunk_scan",
    )(z, xbc, dtr, conv_in, h0, conv_w, conv_b, head_params, d_skip, norm_w)


ROUTER_GRP0 = 0
ROUTER_EXP0 = N_EXPERT_GROUPS


def _route(logits):
    lane = lax.broadcasted_iota(jnp.int32, logits.shape, 1)
    big = jnp.int32(2 * LANES)
    is_grp = lane < N_EXPERT_GROUPS
    gl = jnp.where(is_grp, logits, -jnp.inf)
    gmax = jnp.max(gl, axis=-1, keepdims=True)
    gidx = jnp.min(jnp.where(gl == gmax, lane, big), axis=-1, keepdims=True)
    p_grp = 1.0 / jnp.sum(jnp.where(is_grp, jnp.exp(gl - gmax), 0.0), axis=-1, keepdims=True)
    e_rel = lane - ROUTER_EXP0
    in_grp = (e_rel >= gidx * EXPERTS_PER_GROUP) & (e_rel < (gidx + 1) * EXPERTS_PER_GROUP)
    el = jnp.where(in_grp, logits, -jnp.inf)
    m1 = jnp.max(el, axis=-1, keepdims=True)
    i1 = jnp.min(jnp.where(el == m1, lane, big), axis=-1, keepdims=True)
    el2 = jnp.where(lane == i1, -jnp.inf, el)
    m2 = jnp.max(el2, axis=-1, keepdims=True)
    i2 = jnp.min(jnp.where(el2 == m2, lane, big), axis=-1, keepdims=True)
    e2 = jnp.exp(m2 - m1)
    den = 1.0 + e2
    w1 = (1.0 / den) * p_grp
    w2 = (e2 / den) * p_grp
    return jnp.where(lane == i1, w1, jnp.where(lane == i2, w2, 0.0))


def _moe_kernel(h_ref, nw_ref, sc_ref, sh_ref, gf_ref, wr_ref, br_ref, w1_ref, w3_ref, w2_ref,
                o_ref, u_ref, gates_ref, acc_ref):
    e = pl.program_id(2)

    @pl.when(e == 0)
    def _():
        u = _norm_mod(h_ref[0], nw_ref[...], sc_ref[0], sh_ref[0])
        logits = jnp.dot(u, wr_ref[...], precision=HIGHEST, preferred_element_type=F32) + br_ref[...]
        gates_ref[...] = _route(logits)
        u_ref[...] = u.astype(BF16)
        acc_ref[...] = jnp.zeros_like(acc_ref)

    u = u_ref[...]
    a = _silu(jnp.dot(u, w1_ref[0], preferred_element_type=F32)) * jnp.dot(u, w3_ref[0], preferred_element_type=F32)
    gates = gates_ref[...]
    lane = lax.broadcasted_iota(jnp.int32, gates.shape, 1)
    gcol = jnp.sum(jnp.where(lane == e + ROUTER_EXP0, gates, 0.0), axis=-1, keepdims=True)
    acc_ref[...] += jnp.dot((a * gcol).astype(BF16), w2_ref[0], preferred_element_type=F32)

    @pl.when(e == pl.num_programs(2) - 1)
    def _():
        o_ref[0] = h_ref[0] + gf_ref[0] * acc_ref[...]


def _moe(h, nw, sc, sh, gf, wr, br, w1, w3, w2, tm):
    bsz, t, d = h.shape
    n_e, _, f = w1.shape
    return pl.pallas_call(
        _moe_kernel,
        grid=(bsz, t // tm, n_e),
        in_specs=[pl.BlockSpec((1, tm, d), lambda b, i, e: (b, i, 0)),
                  pl.BlockSpec((1, d), lambda b, i, e: (0, 0)),
                  _mod_spec(sc, tm), _mod_spec(sh, tm), _mod_spec(gf, tm),
                  pl.BlockSpec((d, LANES), lambda b, i, e: (0, 0)),
                  pl.BlockSpec((1, LANES), lambda b, i, e: (0, 0)),
                  pl.BlockSpec((1, d, f), lambda b, i, e: (e, 0, 0)),
                  pl.BlockSpec((1, d, f), lambda b, i, e: (e, 0, 0)),
                  pl.BlockSpec((1, f, d), lambda b, i, e: (e, 0, 0))],
        out_specs=pl.BlockSpec((1, tm, d), lambda b, i, e: (b, i, 0)),
        out_shape=jax.ShapeDtypeStruct((bsz, t, d), F32),
        scratch_shapes=[pltpu.VMEM((tm, d), BF16), pltpu.VMEM((tm, LANES), F32), pltpu.VMEM((tm, d), F32)],
        compiler_params=_cparams(("parallel", "parallel", "arbitrary"), 48),
        name="hmoe",
    )(h, nw, sc, sh, gf, wr, br, w1, w3, w2)


N_SPLIT = 3
FEAT_POS_LO = HEAD_DIM
FEAT_POS_HI = FEAT_POS_LO + N_SPLIT
FEAT_ONE = FEAT_POS_HI + N_SPLIT


def _split3(x):
    h1 = x.astype(BF16).astype(F32)
    r = x - h1
    h2 = r.astype(BF16).astype(F32)
    h3 = (r - h2).astype(BF16).astype(F32)
    return h1, h2, h3


def _feature_lanes(lane, base, parts, other):
    out = other
    for k, part in enumerate(parts):
        out = jnp.where(lane == base + k, part, out)
    return out


def _key_tiles(x, feat):
    lane = lax.broadcasted_iota(jnp.int32, x.shape, 1)
    rolled = pltpu.roll(x, HEAD_DIM, 1)
    lo = lane < HEAD_DIM
    return jnp.where(lo, x, feat), jnp.where(lo, rolled, feat)


def _kv_kernel(h_ref, nw_ref, sc_ref, sh_ref, wk_ref, wv_ref, knw_ref, r_ref, rt_ref,
               kf_ref, vf_ref, *dup_refs, tm, with_dup):
    u = _norm_mod(h_ref[0], nw_ref[...], sc_ref[0], sh_ref[0]).astype(BF16)
    k = jnp.dot(u, wk_ref[...], preferred_element_type=F32)
    v = jnp.dot(u, wv_ref[...], preferred_element_type=F32)
    kn = _head_rmsnorm(k, r_ref[...], rt_ref[...]) * knw_ref[...]
    if not with_dup:
        kf_ref[0] = kn
        vf_ref[0] = v
    else:
        for hh in range(N_KV_HEADS):
            rows = pl.ds(hh, tm, stride=N_KV_HEADS)
            for src, dst in ((kn, kf_ref), (v, vf_ref)):
                tile = src[:, (hh // 2) * LANES:(hh // 2 + 1) * LANES]
                if hh % 2 == 1:
                    tile = pltpu.roll(tile, HEAD_DIM, 1)
                dst[0, rows, :] = tile[:, :HEAD_DIM]
        kd_ref, vt_ref, km_ref = dup_refs
        lane = lax.broadcasted_iota(jnp.int32, (tm, LANES), 1)
        pos = pl.program_id(1) * tm + lax.broadcasted_iota(jnp.int32, (tm, LANES), 0)
        pos_lo = pos % MOBA_BLOCK
        feat = jnp.zeros((tm, LANES), F32)
        feat = _feature_lanes(lane, FEAT_POS_LO, [pos_lo.astype(F32)] * N_SPLIT, feat)
        feat = _feature_lanes(lane, FEAT_POS_HI, [(pos - pos_lo).astype(F32)] * N_SPLIT, feat)
        feat = _feature_lanes(lane, FEAT_ONE, [jnp.ones((tm, LANES), F32)] * N_SPLIT, feat)
        lane_row = lax.broadcasted_iota(jnp.int32, (1, LANES), 1)
        for cidx in range(N_KV_HEADS // 2):
            d0, d1 = _key_tiles(kn[:, cidx * LANES:(cidx + 1) * LANES], feat)
            for hh, dd in ((2 * cidx, d0), (2 * cidx + 1, d1)):
                hsl = slice(hh * LANES, (hh + 1) * LANES)
                kd_ref[0, :, hsl] = dd.astype(BF16)
                for blk in range(tm // MOBA_BLOCK):
                    mean = jnp.mean(dd[blk * MOBA_BLOCK:(blk + 1) * MOBA_BLOCK], axis=0, keepdims=True)
                    km_ref[0, blk, :, hsl] = jnp.where(lane_row < HEAD_DIM, mean, 0.0)
        for blk in range(tm // MOBA_BLOCK):
            vt_ref[0, blk] = v[blk * MOBA_BLOCK:(blk + 1) * MOBA_BLOCK].T.astype(BF16)


def _kv_proj(h, nw, sc, sh, wk, wv, knw, r, rt, tm, with_dup):
    bsz, t, d = h.shape
    kvw = wk.shape[1]
    const = lambda shape: pl.BlockSpec(shape, lambda b, i: (0,) * len(shape))
    if not with_dup:
        out_specs = [pl.BlockSpec((1, tm, kvw), lambda b, i: (b, i, 0))] * 2
        out_shape = [jax.ShapeDtypeStruct((bsz, t, kvw), F32)] * 2
    else:
        out_specs = [pl.BlockSpec((1, tm * N_KV_HEADS, HEAD_DIM), lambda b, i: (b, i, 0))] * 2
        out_shape = [jax.ShapeDtypeStruct((bsz, t * N_KV_HEADS, HEAD_DIM), F32)] * 2
        dupw = N_KV_HEADS * LANES
        nblk = tm // MOBA_BLOCK
        out_specs += [pl.BlockSpec((1, tm, dupw), lambda b, i: (b, i, 0)),
                      pl.BlockSpec((1, nblk, kvw, MOBA_BLOCK), lambda b, i: (b, i, 0, 0)),
                      pl.BlockSpec((1, nblk, 1, dupw), lambda b, i: (b, i, 0, 0))]
        out_shape += [jax.ShapeDtypeStruct((bsz, t, dupw), BF16),
                      jax.ShapeDtypeStruct((bsz, t // MOBA_BLOCK, kvw, MOBA_BLOCK), BF16),
                      jax.ShapeDtypeStruct((bsz, t // MOBA_BLOCK, 1, dupw), F32)]
    return pl.pallas_call(
        functools.partial(_kv_kernel, tm=tm, with_dup=with_dup),
        grid=(bsz, t // tm),
        in_specs=[pl.BlockSpec((1, tm, d), lambda b, i: (b, i, 0)),
                  const((1, d)), _mod_spec(sc, tm), _mod_spec(sh, tm),
                  const((d, kvw)), const((d, kvw)), const((1, kvw)),
                  const((kvw, LANES)), const((LANES, kvw))],
        out_specs=out_specs,
        out_shape=out_shape,
        compiler_params=_cparams(("parallel", "parallel"), 48),
        name="shared_kv",
    )(h, nw, sc, sh, wk, wv, knw, r, rt)


def _q_kernel(h_ref, nw_ref, sc_ref, sh_ref, wq_ref, qnw_ref, r_ref, rt_ref, q_ref):
    u = _norm_mod(h_ref[0], nw_ref[...], sc_ref[0], sh_ref[0]).astype(BF16)
    q = jnp.dot(u, wq_ref[...], preferred_element_type=F32)
    q_ref[0] = (_head_rmsnorm(q, r_ref[...], rt_ref[...]) * qnw_ref[...]).astype(q_ref.dtype)


def _q_proj(h, nw, sc, sh, wq, qnw, r, rt, tm, out_dtype):
    bsz, t, d = h.shape
    const = lambda shape: pl.BlockSpec(shape, lambda b, i: (0,) * len(shape))
    return pl.pallas_call(
        _q_kernel,
        grid=(bsz, t // tm),
        in_specs=[pl.BlockSpec((1, tm, d), lambda b, i: (b, i, 0)),
                  const((1, d)), _mod_spec(sc, tm), _mod_spec(sh, tm),
                  const((d, d)), const((1, d)), const((d, LANES)), const((LANES, d))],
        out_specs=pl.BlockSpec((1, tm, d), lambda b, i: (b, i, 0)),
        out_shape=jax.ShapeDtypeStruct((bsz, t, d), out_dtype),
        compiler_params=_cparams(("parallel", "parallel"), 48),
        name="q_proj",
    )(h, nw, sc, sh, wq, qnw, r, rt)


def _top_blocks(gate, n_valid, axis):
    pos = lax.broadcasted_iota(jnp.int32, gate.shape, axis)
    gm = jnp.where(pos < n_valid, gate, -jnp.inf)
    sel = jnp.zeros(gate.shape, F32)
    for _ in range(MOBA_TOPK):
        mx = jnp.max(gm, axis=axis, keepdims=True)
        cand = (gm == mx) & (mx > -jnp.inf)
        idx = jnp.min(jnp.where(cand, pos, jnp.int32(2 * LANES)), axis=axis, keepdims=True)
        pick = pos == idx
        sel = jnp.where(pick, 1.0, sel)
        gm = jnp.where(pick, -jnp.inf, gm)
    return sel


def _alibi_slope(head):
    return jnp.exp2(-8.0 * (head + 1).astype(F32) / N_HEADS)


ATT_GROUP = 4


def _attn_prompt_kernel(q_ref, k_ref, vt_ref, km_ref, o_ref,
                        qt_ref, sa_ref, sb_ref, sel_ref, m_ref, l_ref, acc_ref):
    g = pl.program_id(1)
    i = pl.program_id(2)
    blk = MOBA_BLOCK
    cols = 2 * blk
    grp = ATT_GROUP

    qt = q_ref[0].astype(F32)
    lane = lax.broadcasted_iota(jnp.int32, (blk, LANES), 1)
    q2 = jnp.concatenate([jnp.where(lane < HEAD_DIM, qt, 0.0),
                          jnp.where(lane < HEAD_DIM, pltpu.roll(qt, HEAD_DIM, 1), 0.0)], axis=0)
    r2 = lax.broadcasted_iota(jnp.int32, (cols, LANES), 0)
    l2 = lax.broadcasted_iota(jnp.int32, (cols, LANES), 1)
    second = r2 >= blk
    slope = _alibi_slope(2 * g + second.astype(jnp.int32))
    pos_q = jnp.where(second, r2 - blk, r2) + i * blk
    slope_parts = _split3(slope)
    q2 = _feature_lanes(l2, FEAT_POS_LO, slope_parts, q2)
    q2 = _feature_lanes(l2, FEAT_POS_HI, slope_parts, q2)
    q2 = _feature_lanes(l2, FEAT_ONE, _split3(-slope * pos_q.astype(F32)), q2)
    q2t = q2.T
    gate = jnp.dot(km_ref[0], q2t, precision=HIGHEST, preferred_element_type=F32)
    sel_ref[...] = _top_blocks(gate, i, 0)
    is_q = lax.broadcasted_iota(jnp.int32, (LANES, 1), 0) < HEAD_DIM
    qt_ref[...] = (q2t * jnp.where(is_q, HEAD_DIM ** -0.5, 1.0)).astype(BF16)

    def scores(n):
        start = pl.multiple_of(n * blk, blk)
        return jnp.dot(k_ref[0, pl.ds(start, blk), :], qt_ref[...], preferred_element_type=F32)

    def block_stats(s, n):
        mb = jnp.max(s, axis=0, keepdims=True)
        p = jnp.exp(s - mb)
        return mb, jnp.sum(p, axis=0, keepdims=True), jnp.dot(vt_ref[0, n], p.astype(BF16),
                                                               preferred_element_type=F32)

    def merge(parts):
        m_old = m_ref[...]
        m_new = m_old
        for mb, _, _ in parts:
            m_new = jnp.maximum(m_new, mb)
        alpha = jnp.exp(m_old - m_new)
        l_new = alpha * l_ref[...]
        acc_new = alpha * acc_ref[...]
        for mb, lb, ab in parts:
            wgt = jnp.exp(mb - m_new)
            l_new = l_new + wgt * lb
            acc_new = acc_new + wgt * ab
        m_ref[...] = m_new
        l_ref[...] = l_new
        acc_ref[...] = acc_new

    rr = lax.broadcasted_iota(jnp.int32, (blk, cols), 0)
    cc = lax.broadcasted_iota(jnp.int32, (blk, cols), 1)
    visible = jnp.where(cc >= blk, cc - blk, cc) >= rr
    m0, l0, a0 = block_stats(jnp.where(visible, scores(i), -jnp.inf), i)
    m_ref[...] = m0
    l_ref[...] = l0
    acc_ref[...] = a0

    n_groups = (i + grp - 1) // grp

    def score_group(kk, s_ref):
        for b in range(grp):
            n = jnp.minimum(kk * grp + b, i - 1)
            s_ref[b * blk:(b + 1) * blk, :] = scores(n)

    def softmax_group(kk, s_ref):
        parts = []
        for b in range(grp):
            n_raw = kk * grp + b
            n = jnp.minimum(n_raw, i - 1)
            mb, lb, ab = block_stats(s_ref[b * blk:(b + 1) * blk, :], n)
            counted = sel_ref[pl.ds(n, 1), :] * (n_raw < i).astype(F32) > 0.0
            parts.append((jnp.where(counted, mb, NEG_BIG), lb, ab))
        merge(parts)

    @pl.when(n_groups > 0)
    def _first():
        score_group(0, sa_ref)

    def step(kk, carry):
        score_group(2 * kk + 1, sb_ref)
        softmax_group(2 * kk, sa_ref)
        score_group(2 * kk + 2, sa_ref)
        softmax_group(2 * kk + 1, sb_ref)
        return carry

    lax.fori_loop(0, n_groups // 2, step, 0)

    @pl.when(n_groups % 2 == 1)
    def _last():
        softmax_group(n_groups - 1, sa_ref)

    o = acc_ref[...] / l_ref[...]
    o_ref[0] = jnp.concatenate([o[:, 0:blk], o[:, blk:cols]], axis=0).T.astype(o_ref.dtype)


def _attn_prompt(q, kd, vt, kmd):
    bsz, t, d = q.shape
    blk = MOBA_BLOCK
    nb = t // blk
    cols = 2 * blk
    return pl.pallas_call(
        _attn_prompt_kernel,
        grid=(bsz, N_KV_HEADS, nb),
        in_specs=[pl.BlockSpec((1, blk, LANES), lambda b, g, i: (b, i, g)),
                  pl.BlockSpec((1, t, LANES), lambda b, g, i: (b, 0, g)),
                  pl.BlockSpec((1, nb, HEAD_DIM, blk), lambda b, g, i: (b, 0, g, 0)),
                  pl.BlockSpec((1, LANES, LANES), lambda b, g, i: (b, 0, g))],
        out_specs=pl.BlockSpec((1, blk, LANES), lambda b, g, i: (b, i, g)),
        out_shape=jax.ShapeDtypeStruct((bsz, t, d), BF16),
        scratch_shapes=[pltpu.VMEM((LANES, cols), BF16),
                        pltpu.VMEM((ATT_GROUP * blk, cols), F32), pltpu.VMEM((ATT_GROUP * blk, cols), F32),
                        pltpu.VMEM((LANES, cols), F32), pltpu.VMEM((1, cols), F32),
                        pltpu.VMEM((1, cols), F32), pltpu.VMEM((HEAD_DIM, cols), F32)],
        compiler_params=_cparams(("parallel", "parallel", "arbitrary"), 48),
        name="moba_prompt",
    )(q, kd, vt, kmd)


def _cache_prep_kernel(pt_ref, *refs, n_pages):
    del pt_ref
    k_refs = refs[:n_pages]
    v_refs = refs[n_pages:2 * n_pages]
    kc_ref, vc_ref, km_ref = refs[2 * n_pages:]
    pages_per_blk = MOBA_BLOCK // PAGE_SIZE

    def lane_dense(page_ref):
        heads = [page_ref[pl.ds(g, PAGE_SIZE, stride=N_KV_HEADS), :] for g in range(N_KV_HEADS)]
        return jnp.concatenate(heads, axis=1)

    acc = None
    for p in range(n_pages):
        rows = slice(p * PAGE_SIZE, (p + 1) * PAGE_SIZE)
        kp = lane_dense(k_refs[p])
        kc_ref[0, rows, :] = kp.astype(BF16)
        vc_ref[0, rows, :] = lane_dense(v_refs[p]).astype(BF16)
        colsum = jnp.sum(kp, axis=0, keepdims=True)
        acc = colsum if p % pages_per_blk == 0 else acc + colsum
        if p % pages_per_blk == pages_per_blk - 1:
            n = p // pages_per_blk
            km_ref[0, n:n + 1, :] = acc * (1.0 / MOBA_BLOCK)


def _cache_prep(cache_k, cache_v, page_table):
    n_seq, n_pages = page_table.shape
    kvw = N_KV_HEADS * HEAD_DIM
    past_len = n_pages * PAGE_SIZE
    n_blk = past_len // MOBA_BLOCK
    page_rows = PAGE_SIZE * N_KV_HEADS
    page_spec = lambda p: pl.BlockSpec((page_rows, HEAD_DIM), lambda s, pt: (pt[s, p], 0))
    grid_spec = pltpu.PrefetchScalarGridSpec(
        num_scalar_prefetch=1,
        grid=(n_seq,),
        in_specs=[page_spec(p) for p in range(n_pages)] + [page_spec(p) for p in range(n_pages)],
        out_specs=[pl.BlockSpec((1, past_len, kvw), lambda s, pt: (s, 0, 0)),
                   pl.BlockSpec((1, past_len, kvw), lambda s, pt: (s, 0, 0)),
                   pl.BlockSpec((1, n_blk, kvw), lambda s, pt: (s, 0, 0))],
    )
    return pl.pallas_call(
        functools.partial(_cache_prep_kernel, n_pages=n_pages),
        grid_spec=grid_spec,
        out_shape=[jax.ShapeDtypeStruct((n_seq, past_len, kvw), BF16),
                   jax.ShapeDtypeStruct((n_seq, past_len, kvw), BF16),
                   jax.ShapeDtypeStruct((n_seq, n_blk, kvw), F32)],
        compiler_params=_cparams(("arbitrary",), 56),
        name="cache_prep",
    )(page_table, *([cache_k] * n_pages), *([cache_v] * n_pages))


def _attn_sample_kernel(q_ref, kn_ref, vn_ref, kc_ref, vc_ref, km_ref, o_ref, s_ref):
    t_new = q_ref.shape[1]
    past_len = kc_ref.shape[1]
    kvw = N_KV_HEADS * HEAD_DIM
    rows = N_HEADS * t_new
    n_past_blk = past_len // MOBA_BLOCK
    chunk = MOBA_BLOCK
    q = q_ref[0]
    lane8 = lax.broadcasted_iota(jnp.int32, (t_new, LANES), 1)
    zeros8 = jnp.zeros((t_new, LANES), F32)
    row_blocks = []
    for h in range(N_HEADS):
        g = h // 2
        tile = q[:, g * LANES:(g + 1) * LANES]
        if h % 2 != g % 2:
            tile = pltpu.roll(tile, HEAD_DIM, 1)
        keep = (lane8 < HEAD_DIM) if g % 2 == 0 else (lane8 >= HEAD_DIM)
        tile = jnp.where(keep, tile, 0.0)
        row_blocks.append(jnp.concatenate([tile if cidx == g // 2 else zeros8 for cidx in range(kvw // LANES)], axis=1))
    qm = jnp.concatenate(row_blocks, axis=0)
    qmb = (qm * (HEAD_DIM ** -0.5)).astype(BF16)

    r1 = lax.broadcasted_iota(jnp.int32, (rows, 1), 0)
    slope = _alibi_slope(r1 // t_new)
    rr = lax.broadcasted_iota(jnp.int32, (rows, chunk), 0)
    cc = lax.broadcasted_iota(jnp.int32, (rows, chunk), 1)
    tok = rr % t_new
    d_blk0 = (past_len + tok - cc).astype(F32)

    km = jnp.concatenate([km_ref[0], jnp.zeros((LANES - n_past_blk, kvw), F32)], axis=0)
    gate = lax.dot_general(qm, km, NT_DIMS, precision=HIGHEST, preferred_element_type=F32)
    sel = _top_blocks(gate, n_past_blk, 1)
    lane_s = lax.broadcasted_iota(jnp.int32, sel.shape, 1)

    for n in range(n_past_blk):
        ks = slice(n * chunk, (n + 1) * chunk)
        s = lax.dot_general(qmb, kc_ref[0, ks, :], NT_DIMS, preferred_element_type=F32)
        s = s - slope * (d_blk0 - float(n * chunk))
        picked = jnp.sum(jnp.where(lane_s == n, sel, 0.0), axis=-1, keepdims=True) > 0.0
        s_ref[:, ks] = jnp.where(picked, s, -jnp.inf)
    pad = jnp.zeros((LANES - t_new, kvw), F32)
    k_new = jnp.concatenate([kn_ref[0], pad], axis=0).astype(BF16)
    v_new = jnp.concatenate([vn_ref[0], pad], axis=0).astype(BF16)
    rr_own = lax.broadcasted_iota(jnp.int32, (rows, LANES), 0)
    cc_own = lax.broadcasted_iota(jnp.int32, (rows, LANES), 1)
    d_own = (rr_own % t_new - cc_own).astype(F32)
    s = lax.dot_general(qmb, k_new, NT_DIMS, preferred_element_type=F32) - slope * d_own
    s_ref[:, past_len:] = jnp.where((d_own >= 0.0) & (cc_own < t_new), s, -jnp.inf)

    s_all = s_ref[...]
    m = jnp.max(s_all, axis=-1, keepdims=True)
    p_all = jnp.exp(s_all - m)
    l = jnp.sum(p_all, axis=-1, keepdims=True)
    pb = p_all.astype(BF16)
    out = jnp.dot(pb[:, past_len:], v_new, preferred_element_type=F32)
    out = out + jnp.dot(pb[:, :past_len], vc_ref[0], preferred_element_type=F32)
    out = out / l

    lo8 = lane8 < HEAD_DIM
    for cidx in range(N_HEADS // 2):
        src = slice((cidx // 2) * LANES, (cidx // 2 + 1) * LANES)
        a = out[(2 * cidx) * t_new:(2 * cidx + 1) * t_new, src]
        b = out[(2 * cidx + 1) * t_new:(2 * cidx + 2) * t_new, src]
        if cidx % 2 == 1:
            a = pltpu.roll(a, HEAD_DIM, 1)
        else:
            b = pltpu.roll(b, HEAD_DIM, 1)
        o_ref[0, :, cidx * LANES:(cidx + 1) * LANES] = jnp.where(lo8, a, b)


def _attn_sample(q, k_new, v_new, kc, vc, km, t_new):
    n_seq, past_len, kvw = kc.shape
    d = q.shape[-1]
    n_blk = km.shape[1]
    rows = N_HEADS * t_new
    return pl.pallas_call(
        _attn_sample_kernel,
        grid=(n_seq,),
        in_specs=[pl.BlockSpec((1, t_new, d), lambda s: (0, s, 0)),
                  pl.BlockSpec((1, t_new, kvw), lambda s: (0, s, 0)),
                  pl.BlockSpec((1, t_new, kvw), lambda s: (0, s, 0)),
                  pl.BlockSpec((1, past_len, kvw), lambda s: (s, 0, 0)),
                  pl.BlockSpec((1, past_len, kvw), lambda s: (s, 0, 0)),
                  pl.BlockSpec((1, n_blk, kvw), lambda s: (s, 0, 0))],
        out_specs=pl.BlockSpec((1, t_new, d), lambda s: (0, s, 0)),
        out_shape=jax.ShapeDtypeStruct(q.shape, F32),
        scratch_shapes=[pltpu.VMEM((rows, past_len + LANES), F32)],
        compiler_params=_cparams(("parallel",), 48),
        name="moba_sample",
    )(q, k_new, v_new, kc, vc, km)


def _head_sum_matrices(n_heads):
    w = n_heads * HEAD_DIM
    head_of_lane = jnp.arange(w) // HEAD_DIM
    r = (head_of_lane[:, None] == jnp.arange(LANES)[None, :]).astype(BF16)
    return r, r.T


def _prep_weights(p):
    bf = lambda x: x.astype(BF16)
    w_in = p['ssm_w_in']
    wdt = jnp.pad(w_in[:, :, D_INNER + CONV_DIM:], ((0, 0), (0, 0), (0, LANES - SSM_HEADS)))
    pad_heads = lambda x: jnp.pad(x, ((0, 0), (0, LANES - SSM_HEADS)))
    head_params = jnp.stack([pad_heads(p['ssm_dt_bias']), pad_heads(p['ssm_a_log'])], axis=1)
    head_params = jnp.pad(head_params, ((0, 0), (0, SUBLANES - 2), (0, 0)))
    router_w = jnp.concatenate([p['moe_w_grp'], p['moe_w_exp']], axis=-1)
    router_w = jnp.pad(router_w, ((0, 0), (0, 0), (0, LANES - router_w.shape[-1])))
    router_b = jnp.concatenate([p['moe_b_grp'], p['moe_b_exp']], axis=-1)
    router_b = jnp.pad(router_b, ((0, 0), (0, LANES - router_b.shape[-1])))[:, None, :]
    rq, rqt = _head_sum_matrices(N_HEADS)
    rk, rkt = _head_sum_matrices(N_KV_HEADS)
    return dict(
        wz=bf(w_in[:, :, :D_INNER]), wxbc=bf(w_in[:, :, D_INNER:D_INNER + CONV_DIM]), wdt=bf(wdt),
        head_params=head_params, d_skip=jnp.repeat(p['ssm_d'], SSM_HEAD_DIM, axis=-1)[:, None, :],
        conv_w=p['ssm_conv_w'], conv_b=p['ssm_conv_b'][:, None, :], ssm_norm_w=p['ssm_norm_w'][:, None, :],
        w_out=bf(p['ssm_w_out']),
        norm_mix_w=p['norm_mix_w'][:, None, :], norm_ffn_w=p['norm_ffn_w'][:, None, :],
        kv_norm_w=p['kv_norm_w'][None, :], wk=bf(p['w_k']), wv=bf(p['w_v']),
        k_norm_w=jnp.tile(p['k_norm_w'], N_KV_HEADS)[None, :],
        wq=bf(p['attn_w_q']), q_norm_w=jnp.tile(p['q_norm_w'], (1, N_HEADS))[:, None, :], wo=bf(p['attn_w_o']),
        router_w=router_w, router_b=router_b,
        w1=bf(p['moe_w1']), w3=bf(p['moe_w3']), w2=bf(p['moe_w2']),
        rq=rq, rqt=rqt, rk=rk, rkt=rkt,
    )


def _trunk(h, mods, kv_mod, w, *, sample, conv0=None, ssm0=None, cache=None):
    bsz, t, _ = h.shape
    tm = min(t, 1024)
    conv_out, ssm_out = [], []
    k_new = v_new = kd = vt = kmd = None
    for layer in range(DEPTH):
        sh_m, sc_m, g_m, sh_f, sc_f, g_f = mods[layer]
        nmw = w['norm_mix_w'][layer]
        if layer < N_A_LAYERS:
            z = _nm_matmul(h, nmw, sc_m, sh_m, w['wz'][layer], tm, 1024)
            xbc = _nm_matmul(h, nmw, sc_m, sh_m, w['wxbc'][layer], tm, 1024)
            dtr = _nm_matmul(h, nmw, sc_m, sh_m, w['wdt'][layer], tm, LANES)
            ssd_args = (w['conv_w'][layer], w['conv_b'][layer], w['head_params'][layer], w['d_skip'][layer],
                        w['ssm_norm_w'][layer])
            if sample:
                n_seq, t_new = conv0.shape[1], cache[3]
                seqs = lambda x: x.reshape(n_seq, t_new, x.shape[-1])
                conv_in = jnp.pad(conv0[layer], ((0, 0), (HALO - (CONV_WIDTH - 1), 0), (0, 0)))
                g, conv8, st = _ssd(seqs(z), seqs(xbc), seqs(dtr), conv_in,
                                    ssm0[layer].reshape(n_seq, D_INNER, D_STATE), *ssd_args, g_dtype=F32)
                g = g.reshape(1, t, D_INNER)
            else:
                conv_in = jnp.zeros((bsz, HALO, CONV_DIM), F32)
                g, conv8, st = _ssd(z, xbc, dtr, conv_in, None, *ssd_args, g_dtype=BF16)
            conv_out.append(conv8[:, HALO - (CONV_WIDTH - 1):])
            ssm_out.append(st.reshape(st.shape[0], SSM_HEADS, SSM_HEAD_DIM, D_STATE))
            h = _mm_res(g, w['w_out'][layer], h, g_m, tm)
        else:
            j = layer - N_A_LAYERS
            if sample:
                q = _q_proj(h, nmw, sc_m, sh_m, w['wq'][j], w['q_norm_w'][j], w['rq'], w['rqt'], tm, F32)
                o = _attn_sample(q, k_new, v_new, cache[0], cache[1], cache[2], cache[3])
            else:
                q = _q_proj(h, nmw, sc_m, sh_m, w['wq'][j], w['q_norm_w'][j], w['rq'], w['rqt'], tm, BF16)
                o = _attn_prompt(q, kd, vt, kmd)
            h = _mm_res(o, w['wo'][j], h, g_m, tm)
        h = _moe(h, w['norm_ffn_w'][layer], sc_f, sh_f, g_f, w['router_w'][layer], w['router_b'][layer],
                 w['w1'][layer], w['w3'][layer], w['w2'][layer], tm)
        if layer == N_A_LAYERS - 1:
            kv_tm = min(t, 512)
            res = _kv_proj(h, w['kv_norm_w'], kv_mod[1], kv_mod[0], w['wk'], w['wv'], w['k_norm_w'],
                           w['rk'], w['rkt'], kv_tm, with_dup=not sample)
            k_new, v_new = res[0], res[1]
            if not sample:
                kd, vt = res[2], res[3]
                kmd = res[4].reshape(bsz, t // MOBA_BLOCK, N_KV_HEADS * LANES)
                kmd = jnp.pad(kmd, ((0, 0), (0, LANES - kmd.shape[1]), (0, 0)))
    return h, jnp.stack(conv_out), jnp.stack(ssm_out), k_new, v_new


def kernel(x_prompt, x_sample, state_conv, state_ssm, cache_k, cache_v, page_table, c_prompt, c_sample, w_mod, b_mod, norm_mix_w, norm_ffn_w, ssm_w_in, ssm_conv_w, ssm_conv_b, ssm_dt_bias, ssm_a_log, ssm_d, ssm_norm_w, ssm_w_out, kv_w_mod, kv_b_mod, kv_norm_w, w_k, w_v, k_norm_w, attn_w_q, q_norm_w, attn_w_o, moe_w_grp, moe_b_grp, moe_w_exp, moe_b_exp, moe_w1, moe_w3, moe_w2):
    params = dict(ssm_w_in=ssm_w_in, ssm_conv_w=ssm_conv_w, ssm_conv_b=ssm_conv_b, ssm_dt_bias=ssm_dt_bias,
                  ssm_a_log=ssm_a_log, ssm_d=ssm_d, ssm_norm_w=ssm_norm_w, ssm_w_out=ssm_w_out,
                  norm_mix_w=norm_mix_w, norm_ffn_w=norm_ffn_w, kv_norm_w=kv_norm_w, w_k=w_k, w_v=w_v,
                  k_norm_w=k_norm_w, attn_w_q=attn_w_q, q_norm_w=q_norm_w, attn_w_o=attn_w_o,
                  moe_w_grp=moe_w_grp, moe_b_grp=moe_b_grp, moe_w_exp=moe_w_exp, moe_b_exp=moe_b_exp,
                  moe_w1=moe_w1, moe_w3=moe_w3, moe_w2=moe_w2)
    w = _prep_weights(params)

    bp, seq, d = x_prompt.shape
    n_seq, t_new, _ = x_sample.shape
    n_pages = page_table.shape[1]
    past_len = n_pages * PAGE_SIZE
    assert seq % MOBA_BLOCK == 0 and seq // MOBA_BLOCK <= LANES
    assert past_len % MOBA_BLOCK == 0 and t_new <= MOBA_BLOCK and t_new % SUBLANES == 0

    n_c = bp + n_seq
    n_c_pad = -(-n_c // SUBLANES) * SUBLANES
    c_all = jnp.pad(jnp.concatenate([c_prompt, c_sample], axis=0), ((0, n_c_pad - n_c), (0, 0)))
    mod_all = _mod_vectors(c_all, w_mod, b_mod[:, None, :], 1536)
    kv_all = _mod_vectors(c_all, kv_w_mod[None], kv_b_mod[None, None, :], 1024)[0]

    def group_mods(lo, hi, per_token_repeat):
        def shape(x):
            if per_token_repeat:
                return jnp.repeat(x, per_token_repeat, axis=0)[None]
            return x[:, None, :]
        mods = [[shape(mod_all[l, lo:hi, k * d:(k + 1) * d]) for k in range(6)] for l in range(DEPTH)]
        kvm = [shape(kv_all[lo:hi, k * d:(k + 1) * d]) for k in range(2)]
        return mods, kvm

    mods_p, kv_p = group_mods(0, bp, 0)
    y_p, conv_p, ssm_p, k_p, v_p = _trunk(x_prompt, mods_p, kv_p, w, sample=False)

    mods_s, kv_s = group_mods(bp, n_c, t_new)
    kvw = N_KV_HEADS * HEAD_DIM
    cache = (*_cache_prep(cache_k.reshape(-1, HEAD_DIM), cache_v.reshape(-1, HEAD_DIM), page_table), t_new)
    y_s, conv_s, ssm_s, k_s, v_s = _trunk(x_sample.reshape(1, n_seq * t_new, d), mods_s, kv_s, w,
                                          sample=True, conv0=state_conv, ssm0=state_ssm, cache=cache)

    heads = lambda x, b, t: x.reshape(b, t, N_KV_HEADS, HEAD_DIM)
    return (y_p, y_s.reshape(n_seq, t_new, d), conv_p, ssm_p, heads(k_p, bp, seq), heads(v_p, bp, seq),
            conv_s, ssm_s, heads(k_s, n_seq, t_new), heads(v_s, n_seq, t_new))
```

```python
import functools

import jax
import jax.numpy as jnp
from jax import lax
from jax.experimental import pallas as pl
from jax.experimental.pallas import tpu as pltpu

F32 = jnp.float32
BF16 = jnp.bfloat16
HIGHEST = lax.Precision.HIGHEST

D_MODEL = 1024
DEPTH = 4
N_A_LAYERS = 2
D_INNER = 2048
SSM_HEADS = 32
SSM_HEAD_DIM = 64
SSM_GROUPS = 4
D_STATE = 128
CONV_WIDTH = 4
CONV_DIM = D_INNER + 2 * SSM_GROUPS * D_STATE
SSD_CHUNK = 128
N_HEADS = 16
HEAD_DIM = 64
N_KV_HEADS = 8
MOBA_BLOCK = 256
MOBA_TOPK = 3
N_EXPERT_GROUPS = 4
EXPERTS_PER_GROUP = 4
N_EXPERTS = 16
D_EXPERT = 256
PAGE_SIZE = 128
EPS = 1e-6

LANES = 128
SUBLANES = 8
MIB = 1024 * 1024
NEG_BIG = -1e30

NT_DIMS = (((1,), (1,)), ((), ()))


def _cparams(sem, vmem_mib):
    return pltpu.CompilerParams(dimension_semantics=sem, vmem_limit_bytes=vmem_mib * MIB)


def _sigmoid(x):
    return 1.0 / (1.0 + jnp.exp(-x))


def _silu(x):
    return x * _sigmoid(x)


def _norm_mod(h, nw, sc, sh):
    ms = jnp.mean(h * h, axis=-1, keepdims=True)
    return (h * lax.rsqrt(ms + EPS)) * nw * (1.0 + sc) + sh


def _split_bf16(x):
    hi = x.astype(BF16)
    lo = (x - hi.astype(F32)).astype(BF16)
    return hi, lo


def _head_rmsnorm(x, r, rt):
    sq = x * x
    hi, lo = _split_bf16(sq)
    ss = jnp.dot(hi, r, preferred_element_type=F32) + jnp.dot(lo, r, preferred_element_type=F32)
    inv = lax.rsqrt(ss * (1.0 / HEAD_DIM) + EPS)
    ih, il = _split_bf16(inv)
    invx = jnp.dot(ih, rt, preferred_element_type=F32) + jnp.dot(il, rt, preferred_element_type=F32)
    return x * invx


def _mod_kernel(c_ref, w_ref, b_ref, o_ref):
    ca = _silu(c_ref[...])
    o_ref[0] = jnp.dot(ca, w_ref[0], precision=HIGHEST, preferred_element_type=F32) + b_ref[0]


def _mod_vectors(c_all, w, b, tn):
    n_layers, d, n = w.shape
    m = c_all.shape[0]
    return pl.pallas_call(
        _mod_kernel,
        grid=(n_layers, n // tn),
        in_specs=[pl.BlockSpec((m, d), lambda l, j: (0, 0)),
                  pl.BlockSpec((1, d, tn), lambda l, j: (l, 0, j)),
                  pl.BlockSpec((1, 1, tn), lambda l, j: (l, 0, j))],
        out_specs=pl.BlockSpec((1, m, tn), lambda l, j: (l, 0, j)),
        out_shape=jax.ShapeDtypeStruct((n_layers, m, n), F32),
        compiler_params=_cparams(("arbitrary", "arbitrary"), 40),
        name="mod_vectors",
    )(c_all, w, b)


def _mod_spec(mod, tm):
    d = mod.shape[-1]
    if mod.shape[1] == 1:
        return pl.BlockSpec((1, 1, d), lambda b, i, *_: (b, 0, 0))
    return pl.BlockSpec((1, tm, d), lambda b, i, *_: (b, i, 0))


def _nm_matmul_kernel(h_ref, nw_ref, sc_ref, sh_ref, w_ref, o_ref, u_ref):
    @pl.when(pl.program_id(2) == 0)
    def _():
        u_ref[...] = _norm_mod(h_ref[0], nw_ref[...], sc_ref[0], sh_ref[0]).astype(BF16)

    o_ref[0] = jnp.dot(u_ref[...], w_ref[...], preferred_element_type=F32)


def _nm_matmul(h, nw, sc, sh, w, tm, tn):
    bsz, t, d = h.shape
    n = w.shape[1]
    return pl.pallas_call(
        _nm_matmul_kernel,
        grid=(bsz, t // tm, n // tn),
        in_specs=[pl.BlockSpec((1, tm, d), lambda b, i, j: (b, i, 0)),
                  pl.BlockSpec((1, d), lambda b, i, j: (0, 0)),
                  _mod_spec(sc, tm), _mod_spec(sh, tm),
                  pl.BlockSpec((d, tn), lambda b, i, j: (0, j))],
        out_specs=pl.BlockSpec((1, tm, tn), lambda b, i, j: (b, i, j)),
        out_shape=jax.ShapeDtypeStruct((bsz, t, n), F32),
        scratch_shapes=[pltpu.VMEM((tm, d), BF16)],
        compiler_params=_cparams(("parallel", "parallel", "arbitrary"), 48),
        name="norm_mod_matmul",
    )(h, nw, sc, sh, w)


def _mm_res_kernel(a_ref, w_ref, h_ref, g_ref, o_ref):
    acc = jnp.dot(a_ref[0].astype(BF16), w_ref[...], preferred_element_type=F32)
    o_ref[0] = h_ref[0] + g_ref[0] * acc


def _mm_res(a, w, h, gate, tm):
    bsz, t, k = a.shape
    d = w.shape[1]
    return pl.pallas_call(
        _mm_res_kernel,
        grid=(bsz, t // tm),
        in_specs=[pl.BlockSpec((1, tm, k), lambda b, i: (b, i, 0)),
                  pl.BlockSpec((k, d), lambda b, i: (0, 0)),
                  pl.BlockSpec((1, tm, d), lambda b, i: (b, i, 0)),
                  _mod_spec(gate, tm)],
        out_specs=pl.BlockSpec((1, tm, d), lambda b, i: (b, i, 0)),
        out_shape=jax.ShapeDtypeStruct((bsz, t, d), F32),
        compiler_params=_cparams(("parallel", "parallel"), 48),
        name="matmul_residual",
    )(a, w, h, gate)


HALO = SUBLANES
XC_TILE = 512


def _ssd_kernel(z_ref, xbc_ref, dt_ref, cs_ref, h0_ref, cw_ref, cb_ref, hp_ref, dsk_ref, nw_ref,
                *rest, valid, has_init, has_prev):
    g_ref, cso_ref, st_ref, xs_ref, xc_ref, xt_ref, y_ref = rest[1:] if has_prev else rest
    L = SSD_CHUNK
    c = pl.program_id(1)

    @pl.when(c == 0)
    def _init():
        xs_ref[...] = jnp.zeros_like(xs_ref)
        xs_ref[0:HALO, :] = cs_ref[0]
        if has_init:
            st_ref[0] = h0_ref[0]
        else:
            st_ref[0] = jnp.zeros(st_ref.shape[1:], F32)

    xs_ref[HALO:HALO + valid, :] = xbc_ref[0]

    for j in range(CONV_DIM // XC_TILE):
        sl = slice(j * XC_TILE, (j + 1) * XC_TILE)
        first = HALO - (CONV_WIDTH - 1)
        acc = cb_ref[:, sl] + xs_ref[first:first + L, sl] * cw_ref[0:1, sl]
        for k in range(1, CONV_WIDTH):
            acc = acc + xs_ref[first + k:first + k + L, sl] * cw_ref[k:k + 1, sl]
        xc_ref[:, sl] = _silu(acc)

    cso_ref[0] = xs_ref[valid:valid + HALO, :]
    xs_ref[0:HALO, :] = xs_ref[L:L + HALO, :]

    dtr = dt_ref[0]
    if valid < L:
        dtr = jnp.concatenate([dtr, jnp.zeros((L - valid, LANES), F32)], axis=0)
    xx = dtr + hp_ref[0:1, :]
    dt = jnp.maximum(xx, 0.0) + jnp.log(1.0 + jnp.exp(-jnp.abs(xx)))
    row = lax.broadcasted_iota(jnp.int32, (L, L), 0)
    col = lax.broadcasted_iota(jnp.int32, (L, L), 1)
    if valid < L:
        dt = jnp.where(row < valid, dt, 0.0)
    a = dt * (-jnp.exp(hp_ref[1:2, :]))
    causal = row >= col
    a_cum = jnp.dot(causal.astype(F32), a, precision=HIGHEST, preferred_element_type=F32)
    a_cum_t = a_cum.T
    dt_t = dt.T
    lane_lo = col < HEAD_DIM
    sub_lo = row < HEAD_DIM

    for j in range(D_INNER // LANES):
        xt_ref[j * LANES:(j + 1) * LANES, :] = xc_ref[:, j * LANES:(j + 1) * LANES].T

    pairs_per_group = SSM_HEADS // SSM_GROUPS // 2
    grp_w = D_INNER // SSM_GROUPS
    for g in range(SSM_GROUPS):
        b_g = xc_ref[:, D_INNER + g * D_STATE:D_INNER + (g + 1) * D_STATE].astype(BF16)
        c_g = xc_ref[:, D_INNER + (SSM_GROUPS + g) * D_STATE:
                     D_INNER + (SSM_GROUPS + g + 1) * D_STATE].astype(BF16)
        cb = lax.dot_general(c_g, b_g, NT_DIMS, preferred_element_type=F32)
        st_g = st_ref[0, g * grp_w:(g + 1) * grp_w, :]
        y_off = lax.dot_general(c_g, st_g.astype(BF16), NT_DIMS, preferred_element_type=F32)
        xw_parts, dec_parts = [], []
        for jj in range(pairs_per_group):
            pair = g * pairs_per_group + jj
            s_mats, e_cols, w_rows, d_end = [], [], [], []
            for h in (2 * pair, 2 * pair + 1):
                colb = jnp.broadcast_to(a_cum[:, h:h + 1], (L, L))
                rowb = a_cum_t[h:h + 1, :]
                dec = jnp.where(causal, jnp.exp(colb - rowb), 0.0)
                s_mats.append((cb * dec * dt_t[h:h + 1, :]).astype(BF16))
                e_cols.append(jnp.exp(colb))
                a_last = a_cum_t[h:h + 1, L - 1:L]
                w_rows.append(dt_t[h:h + 1, :] * jnp.exp(a_last - rowb))
                d_end.append(jnp.broadcast_to(jnp.exp(a_last), (LANES, D_STATE)))
            psl = slice(pair * LANES, (pair + 1) * LANES)
            xp = xc_ref[:, psl]
            xpb = xp.astype(BF16)
            y_d = jnp.where(lane_lo,
                            jnp.dot(s_mats[0], xpb, preferred_element_type=F32),
                            jnp.dot(s_mats[1], xpb, preferred_element_type=F32))
            y_o = y_off[:, jj * LANES:(jj + 1) * LANES] * jnp.where(lane_lo, e_cols[0], e_cols[1])
            y_ref[:, psl] = y_d + y_o + dsk_ref[:, psl] * xp
            xw_parts.append((xt_ref[psl, :] * jnp.where(sub_lo, w_rows[0], w_rows[1])).astype(BF16))
            dec_parts.append(jnp.where(sub_lo, d_end[0], d_end[1]))
        xw_g = jnp.concatenate(xw_parts, axis=0)
        new = jnp.dot(xw_g, b_g, preferred_element_type=F32)
        st_ref[0, g * grp_w:(g + 1) * grp_w, :] = st_g * jnp.concatenate(dec_parts, axis=0) + new

    zz = z_ref[0]
    gg = y_ref[0:valid, :] * _silu(zz)
    for g in range(SSM_GROUPS):
        sl = slice(g * grp_w, (g + 1) * grp_w)
        seg = gg[:, sl]
        ms = jnp.mean(seg * seg, axis=-1, keepdims=True)
        g_ref[0, :, sl] = (seg * lax.rsqrt(ms + EPS) * nw_ref[:, sl]).astype(g_ref.dtype)


def _ssd(z, xbc, dtr, conv_in, h0_all, st_all, layer, conv_w, conv_b, head_params, d_skip, norm_w, g_dtype):
    bsz, t, _ = z.shape
    L = SSD_CHUNK
    valid = min(L, t)
    assert t % valid == 0 and valid % SUBLANES == 0
    nc = t // valid
    has_init = h0_all is not None
    state_spec = pl.BlockSpec((None, 1, D_INNER, D_STATE), lambda b, c: (layer, b, 0, 0))
    if h0_all is None:
        h0_all = jnp.zeros((1, SUBLANES, D_STATE), F32)
        h0_spec = pl.BlockSpec((1, SUBLANES, D_STATE), lambda b, c: (0, 0, 0))
    else:
        h0_spec = state_spec
    full = lambda shape: pl.BlockSpec(shape, lambda b, c: (0,) * len(shape))
    operands = [z, xbc, dtr, conv_in, h0_all, conv_w, conv_b, head_params, d_skip, norm_w]
    in_specs = [pl.BlockSpec((1, valid, D_INNER), lambda b, c: (b, c, 0)),
                pl.BlockSpec((1, valid, CONV_DIM), lambda b, c: (b, c, 0)),
                pl.BlockSpec((1, valid, LANES), lambda b, c: (b, c, 0)),
                pl.BlockSpec((1, HALO, CONV_DIM), lambda b, c: (b, 0, 0)),
                h0_spec,
                full((CONV_WIDTH, CONV_DIM)), full((1, CONV_DIM)), full((SUBLANES, LANES)),
                full((1, D_INNER)), full((1, D_INNER))]
    aliases = {}
    if st_all is not None:
        aliases = {len(operands): 2}
        operands.append(st_all)
        in_specs.append(pl.BlockSpec(memory_space=pl.ANY))
    return pl.pallas_call(
        functools.partial(_ssd_kernel, valid=valid, has_init=has_init, has_prev=st_all is not None),
        grid=(bsz, nc),
        in_specs=in_specs,
        out_specs=[pl.BlockSpec((1, valid, D_INNER), lambda b, c: (b, c, 0)),
                   pl.BlockSpec((1, HALO, CONV_DIM), lambda b, c: (b, 0, 0)),
                   state_spec],
        out_shape=[jax.ShapeDtypeStruct((bsz, t, D_INNER), g_dtype),
                   jax.ShapeDtypeStruct((bsz, HALO, CONV_DIM), F32),
                   jax.ShapeDtypeStruct((N_A_LAYERS, bsz, D_INNER, D_STATE), F32)],
        input_output_aliases=aliases,
        scratch_shapes=[pltpu.VMEM((L + HALO, CONV_DIM), F32),
                        pltpu.VMEM((L, CONV_DIM), F32),
                        pltpu.VMEM((D_INNER, L), F32),
                        pltpu.VMEM((L, D_INNER), F32)],
        compiler_params=_cparams(("parallel", "arbitrary"), 48),
        name="ssd_chunk_scan",
    )(*operands)


ROUTER_GRP0 = 0
ROUTER_EXP0 = N_EXPERT_GROUPS


def _route(logits):
    lane = lax.broadcasted_iota(jnp.int32, logits.shape, 1)
    big = jnp.int32(2 * LANES)
    is_grp = lane < N_EXPERT_GROUPS
    gl = jnp.where(is_grp, logits, -jnp.inf)
    gmax = jnp.max(gl, axis=-1, keepdims=True)
    gidx = jnp.min(jnp.where(gl == gmax, lane, big), axis=-1, keepdims=True)
    p_grp = 1.0 / jnp.sum(jnp.where(is_grp, jnp.exp(gl - gmax), 0.0), axis=-1, keepdims=True)
    e_rel = lane - ROUTER_EXP0
    in_grp = (e_rel >= gidx * EXPERTS_PER_GROUP) & (e_rel < (gidx + 1) * EXPERTS_PER_GROUP)
    el = jnp.where(in_grp, logits, -jnp.inf)
    m1 = jnp.max(el, axis=-1, keepdims=True)
    i1 = jnp.min(jnp.where(el == m1, lane, big), axis=-1, keepdims=True)
    el2 = jnp.where(lane == i1, -jnp.inf, el)
    m2 = jnp.max(el2, axis=-1, keepdims=True)
    i2 = jnp.min(jnp.where(el2 == m2, lane, big), axis=-1, keepdims=True)
    e2 = jnp.exp(m2 - m1)
    den = 1.0 + e2
    w1 = (1.0 / den) * p_grp
    w2 = (e2 / den) * p_grp
    return jnp.where(lane == i1, w1, jnp.where(lane == i2, w2, 0.0))


def _moe_kernel(h_ref, nw_ref, sc_ref, sh_ref, gf_ref, wr_ref, br_ref, w1_ref, w3_ref, w2_ref,
                o_ref, u_ref, gates_ref, acc_ref):
    e = pl.program_id(2)

    @pl.when(e == 0)
    def _():
        u = _norm_mod(h_ref[0], nw_ref[...], sc_ref[0], sh_ref[0])
        logits = jnp.dot(u, wr_ref[...], precision=HIGHEST, preferred_element_type=F32) + br_ref[...]
        gates_ref[...] = _route(logits)
        u_ref[...] = u.astype(BF16)
        acc_ref[...] = jnp.zeros_like(acc_ref)

    u = u_ref[...]
    a = _silu(jnp.dot(u, w1_ref[0], preferred_element_type=F32)) * jnp.dot(u, w3_ref[0], preferred_element_type=F32)
    gates = gates_ref[...]
    lane = lax.broadcasted_iota(jnp.int32, gates.shape, 1)
    gcol = jnp.sum(jnp.where(lane == e + ROUTER_EXP0, gates, 0.0), axis=-1, keepdims=True)
    acc_ref[...] += jnp.dot((a * gcol).astype(BF16), w2_ref[0], preferred_element_type=F32)

    @pl.when(e == pl.num_programs(2) - 1)
    def _():
        o_ref[0] = h_ref[0] + gf_ref[0] * acc_ref[...]


def _moe(h, nw, sc, sh, gf, wr, br, w1, w3, w2, tm):
    bsz, t, d = h.shape
    n_e, _, f = w1.shape
    return pl.pallas_call(
        _moe_kernel,
        grid=(bsz, t // tm, n_e),
        in_specs=[pl.BlockSpec((1, tm, d), lambda b, i, e: (b, i, 0)),
                  pl.BlockSpec((1, d), lambda b, i, e: (0, 0)),
                  _mod_spec(sc, tm), _mod_spec(sh, tm), _mod_spec(gf, tm),
                  pl.BlockSpec((d, LANES), lambda b, i, e: (0, 0)),
                  pl.BlockSpec((1, LANES), lambda b, i, e: (0, 0)),
                  pl.BlockSpec((1, d, f), lambda b, i, e: (e, 0, 0)),
                  pl.BlockSpec((1, d, f), lambda b, i, e: (e, 0, 0)),
                  pl.BlockSpec((1, f, d), lambda b, i, e: (e, 0, 0))],
        out_specs=pl.BlockSpec((1, tm, d), lambda b, i, e: (b, i, 0)),
        out_shape=jax.ShapeDtypeStruct((bsz, t, d), F32),
        scratch_shapes=[pltpu.VMEM((tm, d), BF16), pltpu.VMEM((tm, LANES), F32), pltpu.VMEM((tm, d), F32)],
        compiler_params=_cparams(("parallel", "parallel", "arbitrary"), 48),
        name="hmoe",
    )(h, nw, sc, sh, gf, wr, br, w1, w3, w2)


N_SPLIT = 3
FEAT_POS_LO = HEAD_DIM
FEAT_POS_HI = FEAT_POS_LO + N_SPLIT
FEAT_ONE = FEAT_POS_HI + N_SPLIT


def _split3(x):
    h1 = x.astype(BF16).astype(F32)
    r = x - h1
    h2 = r.astype(BF16).astype(F32)
    h3 = (r - h2).astype(BF16).astype(F32)
    return h1, h2, h3


def _feature_lanes(lane, base, parts, other):
    out = other
    for k, part in enumerate(parts):
        out = jnp.where(lane == base + k, part, out)
    return out


def _key_tiles(x, feat):
    lane = lax.broadcasted_iota(jnp.int32, x.shape, 1)
    rolled = pltpu.roll(x, HEAD_DIM, 1)
    lo = lane < HEAD_DIM
    return jnp.where(lo, x, feat), jnp.where(lo, rolled, feat)


def _kv_kernel(h_ref, nw_ref, sc_ref, sh_ref, wk_ref, wv_ref, knw_ref, r_ref, rt_ref,
               kf_ref, vf_ref, *dup_refs, tm, with_dup):
    u = _norm_mod(h_ref[0], nw_ref[...], sc_ref[0], sh_ref[0]).astype(BF16)
    k = jnp.dot(u, wk_ref[...], preferred_element_type=F32)
    v = jnp.dot(u, wv_ref[...], preferred_element_type=F32)
    kn = _head_rmsnorm(k, r_ref[...], rt_ref[...]) * knw_ref[...]
    if not with_dup:
        kf_ref[0] = kn
        vf_ref[0] = v
    else:
        for hh in range(N_KV_HEADS):
            rows = pl.ds(hh, tm, stride=N_KV_HEADS)
            for src, dst in ((kn, kf_ref), (v, vf_ref)):
                tile = src[:, (hh // 2) * LANES:(hh // 2 + 1) * LANES]
                if hh % 2 == 1:
                    tile = pltpu.roll(tile, HEAD_DIM, 1)
                dst[0, rows, :] = tile[:, :HEAD_DIM]
        kd_ref, vt_ref, km_ref = dup_refs
        lane = lax.broadcasted_iota(jnp.int32, (tm, LANES), 1)
        pos = pl.program_id(1) * tm + lax.broadcasted_iota(jnp.int32, (tm, LANES), 0)
        pos_lo = pos % MOBA_BLOCK
        feat = jnp.zeros((tm, LANES), F32)
        feat = _feature_lanes(lane, FEAT_POS_LO, [pos_lo.astype(F32)] * N_SPLIT, feat)
        feat = _feature_lanes(lane, FEAT_POS_HI, [(pos - pos_lo).astype(F32)] * N_SPLIT, feat)
        feat = _feature_lanes(lane, FEAT_ONE, [jnp.ones((tm, LANES), F32)] * N_SPLIT, feat)
        lane_row = lax.broadcasted_iota(jnp.int32, (1, LANES), 1)
        for cidx in range(N_KV_HEADS // 2):
            d0, d1 = _key_tiles(kn[:, cidx * LANES:(cidx + 1) * LANES], feat)
            for hh, dd in ((2 * cidx, d0), (2 * cidx + 1, d1)):
                hsl = slice(hh * LANES, (hh + 1) * LANES)
                kd_ref[0, :, hsl] = dd.astype(BF16)
                for blk in range(tm // MOBA_BLOCK):
                    mean = jnp.mean(dd[blk * MOBA_BLOCK:(blk + 1) * MOBA_BLOCK], axis=0, keepdims=True)
                    km_ref[0, blk, :, hsl] = jnp.where(lane_row < HEAD_DIM, mean, 0.0)
        for blk in range(tm // MOBA_BLOCK):
            vt_ref[0, blk] = v[blk * MOBA_BLOCK:(blk + 1) * MOBA_BLOCK].T.astype(BF16)


def _kv_proj(h, nw, sc, sh, wk, wv, knw, r, rt, tm, with_dup):
    bsz, t, d = h.shape
    kvw = wk.shape[1]
    const = lambda shape: pl.BlockSpec(shape, lambda b, i: (0,) * len(shape))
    if not with_dup:
        out_specs = [pl.BlockSpec((1, tm, kvw), lambda b, i: (b, i, 0))] * 2
        out_shape = [jax.ShapeDtypeStruct((bsz, t, kvw), F32)] * 2
    else:
        out_specs = [pl.BlockSpec((1, tm * N_KV_HEADS, HEAD_DIM), lambda b, i: (b, i, 0))] * 2
        out_shape = [jax.ShapeDtypeStruct((bsz, t * N_KV_HEADS, HEAD_DIM), F32)] * 2
        dupw = N_KV_HEADS * LANES
        nblk = tm // MOBA_BLOCK
        out_specs += [pl.BlockSpec((1, tm, dupw), lambda b, i: (b, i, 0)),
                      pl.BlockSpec((1, nblk, kvw, MOBA_BLOCK), lambda b, i: (b, i, 0, 0)),
                      pl.BlockSpec((1, nblk, 1, dupw), lambda b, i: (b, i, 0, 0))]
        out_shape += [jax.ShapeDtypeStruct((bsz, t, dupw), BF16),
                      jax.ShapeDtypeStruct((bsz, t // MOBA_BLOCK, kvw, MOBA_BLOCK), BF16),
                      jax.ShapeDtypeStruct((bsz, t // MOBA_BLOCK, 1, dupw), F32)]
    return pl.pallas_call(
        functools.partial(_kv_kernel, tm=tm, with_dup=with_dup),
        grid=(bsz, t // tm),
        in_specs=[pl.BlockSpec((1, tm, d), lambda b, i: (b, i, 0)),
                  const((1, d)), _mod_spec(sc, tm), _mod_spec(sh, tm),
                  const((d, kvw)), const((d, kvw)), const((1, kvw)),
                  const((kvw, LANES)), const((LANES, kvw))],
        out_specs=out_specs,
        out_shape=out_shape,
        compiler_params=_cparams(("parallel", "parallel"), 48),
        name="shared_kv",
    )(h, nw, sc, sh, wk, wv, knw, r, rt)


def _q_kernel(h_ref, nw_ref, sc_ref, sh_ref, wq_ref, qnw_ref, r_ref, rt_ref, q_ref):
    u = _norm_mod(h_ref[0], nw_ref[...], sc_ref[0], sh_ref[0]).astype(BF16)
    q = jnp.dot(u, wq_ref[...], preferred_element_type=F32)
    q_ref[0] = (_head_rmsnorm(q, r_ref[...], rt_ref[...]) * qnw_ref[...]).astype(q_ref.dtype)


def _q_proj(h, nw, sc, sh, wq, qnw, r, rt, tm, out_dtype):
    bsz, t, d = h.shape
    const = lambda shape: pl.BlockSpec(shape, lambda b, i: (0,) * len(shape))
    return pl.pallas_call(
        _q_kernel,
        grid=(bsz, t // tm),
        in_specs=[pl.BlockSpec((1, tm, d), lambda b, i: (b, i, 0)),
                  const((1, d)), _mod_spec(sc, tm), _mod_spec(sh, tm),
                  const((d, d)), const((1, d)), const((d, LANES)), const((LANES, d))],
        out_specs=pl.BlockSpec((1, tm, d), lambda b, i: (b, i, 0)),
        out_shape=jax.ShapeDtypeStruct((bsz, t, d), out_dtype),
        compiler_params=_cparams(("parallel", "parallel"), 48),
        name="q_proj",
    )(h, nw, sc, sh, wq, qnw, r, rt)


def _top_blocks(gate, n_valid, axis):
    pos = lax.broadcasted_iota(jnp.int32, gate.shape, axis)
    gm = jnp.where(pos < n_valid, gate, -jnp.inf)
    sel = jnp.zeros(gate.shape, F32)
    for _ in range(MOBA_TOPK):
        mx = jnp.max(gm, axis=axis, keepdims=True)
        cand = (gm == mx) & (mx > -jnp.inf)
        idx = jnp.min(jnp.where(cand, pos, jnp.int32(2 * LANES)), axis=axis, keepdims=True)
        pick = pos == idx
        sel = jnp.where(pick, 1.0, sel)
        gm = jnp.where(pick, -jnp.inf, gm)
    return sel


def _alibi_slope(head):
    return jnp.exp2(-8.0 * (head + 1).astype(F32) / N_HEADS)


ATT_GROUP = 4


def _attn_prompt_kernel(q_ref, k_ref, vt_ref, km_ref, o_ref,
                        qt_ref, sa_ref, sb_ref, sel_ref, m_ref, l_ref, acc_ref):
    g = pl.program_id(1)
    i = pl.program_id(2)
    blk = MOBA_BLOCK
    cols = 2 * blk
    grp = ATT_GROUP

    qt = q_ref[0].astype(F32)
    lane = lax.broadcasted_iota(jnp.int32, (blk, LANES), 1)
    q2 = jnp.concatenate([jnp.where(lane < HEAD_DIM, qt, 0.0),
                          jnp.where(lane < HEAD_DIM, pltpu.roll(qt, HEAD_DIM, 1), 0.0)], axis=0)
    r2 = lax.broadcasted_iota(jnp.int32, (cols, LANES), 0)
    l2 = lax.broadcasted_iota(jnp.int32, (cols, LANES), 1)
    second = r2 >= blk
    slope = _alibi_slope(2 * g + second.astype(jnp.int32))
    pos_q = jnp.where(second, r2 - blk, r2) + i * blk
    slope_parts = _split3(slope)
    q2 = _feature_lanes(l2, FEAT_POS_LO, slope_parts, q2)
    q2 = _feature_lanes(l2, FEAT_POS_HI, slope_parts, q2)
    q2 = _feature_lanes(l2, FEAT_ONE, _split3(-slope * pos_q.astype(F32)), q2)
    q2t = q2.T
    gate = jnp.dot(km_ref[0], q2t, precision=HIGHEST, preferred_element_type=F32)
    sel_ref[...] = _top_blocks(gate, i, 0)
    is_q = lax.broadcasted_iota(jnp.int32, (LANES, 1), 0) < HEAD_DIM
    qt_ref[...] = (q2t * jnp.where(is_q, HEAD_DIM ** -0.5, 1.0)).astype(BF16)

    def scores(n):
        start = pl.multiple_of(n * blk, blk)
        return jnp.dot(k_ref[0, pl.ds(start, blk), :], qt_ref[...], preferred_element_type=F32)

    def block_stats(s, n):
        mb = jnp.max(s, axis=0, keepdims=True)
        p = jnp.exp(s - mb)
        return mb, jnp.sum(p, axis=0, keepdims=True), jnp.dot(vt_ref[0, n], p.astype(BF16),
                                                               preferred_element_type=F32)

    def merge(parts):
        m_old = m_ref[...]
        m_new = m_old
        for mb, _, _ in parts:
            m_new = jnp.maximum(m_new, mb)
        alpha = jnp.exp(m_old - m_new)
        l_new = alpha * l_ref[...]
        acc_new = alpha * acc_ref[...]
        for mb, lb, ab in parts:
            wgt = jnp.exp(mb - m_new)
            l_new = l_new + wgt * lb
            acc_new = acc_new + wgt * ab
        m_ref[...] = m_new
        l_ref[...] = l_new
        acc_ref[...] = acc_new

    rr = lax.broadcasted_iota(jnp.int32, (blk, cols), 0)
    cc = lax.broadcasted_iota(jnp.int32, (blk, cols), 1)
    visible = jnp.where(cc >= blk, cc - blk, cc) >= rr
    m0, l0, a0 = block_stats(jnp.where(visible, scores(i), -jnp.inf), i)
    m_ref[...] = m0
    l_ref[...] = l0
    acc_ref[...] = a0

    n_groups = (i + grp - 1) // grp

    def score_group(kk, s_ref):
        for b in range(grp):
            n = jnp.minimum(kk * grp + b, i - 1)
            s_ref[b * blk:(b + 1) * blk, :] = scores(n)

    def softmax_group(kk, s_ref):
        parts = []
        for b in range(grp):
            n_raw = kk * grp + b
            n = jnp.minimum(n_raw, i - 1)
            mb, lb, ab = block_stats(s_ref[b * blk:(b + 1) * blk, :], n)
            counted = sel_ref[pl.ds(n, 1), :] * (n_raw < i).astype(F32) > 0.0
            parts.append((jnp.where(counted, mb, NEG_BIG), lb, ab))
        merge(parts)

    @pl.when(n_groups > 0)
    def _first():
        score_group(0, sa_ref)

    def step(kk, carry):
        score_group(2 * kk + 1, sb_ref)
        softmax_group(2 * kk, sa_ref)
        score_group(2 * kk + 2, sa_ref)
        softmax_group(2 * kk + 1, sb_ref)
        return carry

    lax.fori_loop(0, n_groups // 2, step, 0)

    @pl.when(n_groups % 2 == 1)
    def _last():
        softmax_group(n_groups - 1, sa_ref)

    o = acc_ref[...] / l_ref[...]
    o_ref[0] = jnp.concatenate([o[:, 0:blk], o[:, blk:cols]], axis=0).T.astype(o_ref.dtype)


def _attn_prompt(q, kd, vt, kmd):
    bsz, t, d = q.shape
    blk = MOBA_BLOCK
    nb = t // blk
    cols = 2 * blk
    return pl.pallas_call(
        _attn_prompt_kernel,
        grid=(bsz, N_KV_HEADS, nb),
        in_specs=[pl.BlockSpec((1, blk, LANES), lambda b, g, i: (b, i, g)),
                  pl.BlockSpec((1, t, LANES), lambda b, g, i: (b, 0, g)),
                  pl.BlockSpec((1, nb, HEAD_DIM, blk), lambda b, g, i: (b, 0, g, 0)),
                  pl.BlockSpec((1, LANES, LANES), lambda b, g, i: (b, 0, g))],
        out_specs=pl.BlockSpec((1, blk, LANES), lambda b, g, i: (b, i, g)),
        out_shape=jax.ShapeDtypeStruct((bsz, t, d), BF16),
        scratch_shapes=[pltpu.VMEM((LANES, cols), BF16),
                        pltpu.VMEM((ATT_GROUP * blk, cols), F32), pltpu.VMEM((ATT_GROUP * blk, cols), F32),
                        pltpu.VMEM((LANES, cols), F32), pltpu.VMEM((1, cols), F32),
                        pltpu.VMEM((1, cols), F32), pltpu.VMEM((HEAD_DIM, cols), F32)],
        compiler_params=_cparams(("parallel", "parallel", "arbitrary"), 48),
        name="moba_prompt",
    )(q, kd, vt, kmd)


def _attn_sample_kernel(pt_ref, q_ref, kn_ref, vn_ref, *rest, n_pages):
    del pt_ref
    k_refs = rest[:n_pages]
    v_refs = rest[n_pages:2 * n_pages]
    o_ref = rest[2 * n_pages]
    s_ref = rest[2 * n_pages + 1]
    t_new = q_ref.shape[1]
    past_len = n_pages * PAGE_SIZE
    kvw = N_KV_HEADS * HEAD_DIM
    rows = N_HEADS * t_new
    n_past_blk = past_len // MOBA_BLOCK
    pages_per_blk = MOBA_BLOCK // PAGE_SIZE
    q = q_ref[0]
    lane8 = lax.broadcasted_iota(jnp.int32, (t_new, LANES), 1)
    zeros8 = jnp.zeros((t_new, LANES), F32)
    row_blocks = []
    for h in range(N_HEADS):
        g = h // 2
        tile = q[:, g * LANES:(g + 1) * LANES]
        if h % 2 != g % 2:
            tile = pltpu.roll(tile, HEAD_DIM, 1)
        keep = (lane8 < HEAD_DIM) if g % 2 == 0 else (lane8 >= HEAD_DIM)
        tile = jnp.where(keep, tile, 0.0)
        row_blocks.append(jnp.concatenate([tile if cidx == g // 2 else zeros8 for cidx in range(kvw // LANES)], axis=1))
    qm = jnp.concatenate(row_blocks, axis=0)
    qmb = (qm * (HEAD_DIM ** -0.5)).astype(BF16)

    r1 = lax.broadcasted_iota(jnp.int32, (rows, 1), 0)
    slope = _alibi_slope(r1 // t_new)
    rr = lax.broadcasted_iota(jnp.int32, (rows, PAGE_SIZE), 0)
    cc = lax.broadcasted_iota(jnp.int32, (rows, PAGE_SIZE), 1)
    d_page0 = (past_len + rr % t_new - cc).astype(F32)

    lane_b = lax.broadcasted_iota(jnp.int32, (kvw, LANES), 1)
    kmt = jnp.zeros((kvw, LANES), F32)
    for n in range(n_past_blk):
        acc = k_refs[n * pages_per_blk][0].reshape(kvw, PAGE_SIZE)
        for pp in range(1, pages_per_blk):
            acc = acc + k_refs[n * pages_per_blk + pp][0].reshape(kvw, PAGE_SIZE)
        mean = jnp.sum(acc, axis=1, keepdims=True) * (1.0 / MOBA_BLOCK)
        kmt = jnp.where(lane_b == n, mean, kmt)
    gate = jnp.dot(qm, kmt, precision=HIGHEST, preferred_element_type=F32)
    sel = _top_blocks(gate, n_past_blk, 1)
    lane_s = lax.broadcasted_iota(jnp.int32, sel.shape, 1)

    for p in range(n_pages):
        kp = k_refs[p][0].reshape(kvw, PAGE_SIZE).astype(BF16)
        s = jnp.dot(qmb, kp, preferred_element_type=F32)
        s = s - slope * (d_page0 - float(p * PAGE_SIZE))
        picked = jnp.sum(jnp.where(lane_s == p // pages_per_blk, sel, 0.0), axis=-1, keepdims=True) > 0.0
        s_ref[:, p * PAGE_SIZE:(p + 1) * PAGE_SIZE] = jnp.where(picked, s, -jnp.inf)
    pad = jnp.zeros((LANES - t_new, kvw), F32)
    k_new = jnp.concatenate([kn_ref[0], pad], axis=0).astype(BF16)
    v_new = jnp.concatenate([vn_ref[0], pad], axis=0).astype(BF16)
    rr_own = lax.broadcasted_iota(jnp.int32, (rows, LANES), 0)
    cc_own = lax.broadcasted_iota(jnp.int32, (rows, LANES), 1)
    d_own = (rr_own % t_new - cc_own).astype(F32)
    s = lax.dot_general(qmb, k_new, NT_DIMS, preferred_element_type=F32) - slope * d_own
    s_ref[:, past_len:] = jnp.where((d_own >= 0.0) & (cc_own < t_new), s, -jnp.inf)

    s_all = s_ref[...]
    m = jnp.max(s_all, axis=-1, keepdims=True)
    p_all = jnp.exp(s_all - m)
    l = jnp.sum(p_all, axis=-1, keepdims=True)
    pb = p_all.astype(BF16)
    out = jnp.dot(pb[:, past_len:], v_new, preferred_element_type=F32)
    for p in range(n_pages):
        vp = v_refs[p][0].reshape(kvw, PAGE_SIZE).astype(BF16)
        out = out + lax.dot_general(pb[:, p * PAGE_SIZE:(p + 1) * PAGE_SIZE], vp, NT_DIMS,
                                    preferred_element_type=F32)
    out = out / l

    lo8 = lane8 < HEAD_DIM
    for cidx in range(N_HEADS // 2):
        src = slice((cidx // 2) * LANES, (cidx // 2 + 1) * LANES)
        a = out[(2 * cidx) * t_new:(2 * cidx + 1) * t_new, src]
        b = out[(2 * cidx + 1) * t_new:(2 * cidx + 2) * t_new, src]
        if cidx % 2 == 1:
            a = pltpu.roll(a, HEAD_DIM, 1)
        else:
            b = pltpu.roll(b, HEAD_DIM, 1)
        o_ref[0, :, cidx * LANES:(cidx + 1) * LANES] = jnp.where(lo8, a, b)


def _attn_sample(q, k_new, v_new, cache_kt, cache_vt, page_table, t_new):
    n_seq, n_pages = page_table.shape
    d = q.shape[-1]
    kvw = k_new.shape[-1]
    rows = N_HEADS * t_new
    page_spec = lambda p: pl.BlockSpec((1, N_KV_HEADS, HEAD_DIM, PAGE_SIZE), lambda s, pt: (pt[s, p], 0, 0, 0))
    grid_spec = pltpu.PrefetchScalarGridSpec(
        num_scalar_prefetch=1,
        grid=(n_seq,),
        in_specs=[pl.BlockSpec((1, t_new, d), lambda s, pt: (0, s, 0)),
                  pl.BlockSpec((1, t_new, kvw), lambda s, pt: (0, s, 0)),
                  pl.BlockSpec((1, t_new, kvw), lambda s, pt: (0, s, 0))]
        + [page_spec(p) for p in range(n_pages)] + [page_spec(p) for p in range(n_pages)],
        out_specs=pl.BlockSpec((1, t_new, d), lambda s, pt: (0, s, 0)),
        scratch_shapes=[pltpu.VMEM((rows, n_pages * PAGE_SIZE + LANES), F32)],
    )
    return pl.pallas_call(
        functools.partial(_attn_sample_kernel, n_pages=n_pages),
        grid_spec=grid_spec,
        out_shape=jax.ShapeDtypeStruct(q.shape, F32),
        compiler_params=_cparams(("arbitrary",), 48),
        name="moba_sample",
    )(page_table, q, k_new, v_new, *([cache_kt] * n_pages), *([cache_vt] * n_pages))


def _head_sum_matrices(n_heads):
    w = n_heads * HEAD_DIM
    head_of_lane = jnp.arange(w) // HEAD_DIM
    r = (head_of_lane[:, None] == jnp.arange(LANES)[None, :]).astype(BF16)
    return r, r.T


def _prep_weights(p):
    bf = lambda x: x.astype(BF16)
    w_in = p['ssm_w_in']
    wdt = jnp.pad(w_in[:, :, D_INNER + CONV_DIM:], ((0, 0), (0, 0), (0, LANES - SSM_HEADS)))
    pad_heads = lambda x: jnp.pad(x, ((0, 0), (0, LANES - SSM_HEADS)))
    head_params = jnp.stack([pad_heads(p['ssm_dt_bias']), pad_heads(p['ssm_a_log'])], axis=1)
    head_params = jnp.pad(head_params, ((0, 0), (0, SUBLANES - 2), (0, 0)))
    router_w = jnp.concatenate([p['moe_w_grp'], p['moe_w_exp']], axis=-1)
    router_w = jnp.pad(router_w, ((0, 0), (0, 0), (0, LANES - router_w.shape[-1])))
    router_b = jnp.concatenate([p['moe_b_grp'], p['moe_b_exp']], axis=-1)
    router_b = jnp.pad(router_b, ((0, 0), (0, LANES - router_b.shape[-1])))[:, None, :]
    rq, rqt = _head_sum_matrices(N_HEADS)
    rk, rkt = _head_sum_matrices(N_KV_HEADS)
    return dict(
        wz=bf(w_in[:, :, :D_INNER]), wxbc=bf(w_in[:, :, D_INNER:D_INNER + CONV_DIM]), wdt=bf(wdt),
        head_params=head_params, d_skip=jnp.repeat(p['ssm_d'], SSM_HEAD_DIM, axis=-1)[:, None, :],
        conv_w=p['ssm_conv_w'], conv_b=p['ssm_conv_b'][:, None, :], ssm_norm_w=p['ssm_norm_w'][:, None, :],
        w_out=bf(p['ssm_w_out']),
        norm_mix_w=p['norm_mix_w'][:, None, :], norm_ffn_w=p['norm_ffn_w'][:, None, :],
        kv_norm_w=p['kv_norm_w'][None, :], wk=bf(p['w_k']), wv=bf(p['w_v']),
        k_norm_w=jnp.tile(p['k_norm_w'], N_KV_HEADS)[None, :],
        wq=bf(p['attn_w_q']), q_norm_w=jnp.tile(p['q_norm_w'], (1, N_HEADS))[:, None, :], wo=bf(p['attn_w_o']),
        router_w=router_w, router_b=router_b,
        w1=bf(p['moe_w1']), w3=bf(p['moe_w3']), w2=bf(p['moe_w2']),
        rq=rq, rqt=rqt, rk=rk, rkt=rkt,
    )


def _trunk(h, mods, kv_mod, w, *, sample, conv0=None, ssm0=None, cache=None):
    bsz, t, _ = h.shape
    tm = min(t, 1024)
    conv_out = []
    st_all = None
    k_new = v_new = kd = vt = kmd = None
    for layer in range(DEPTH):
        sh_m, sc_m, g_m, sh_f, sc_f, g_f = mods[layer]
        nmw = w['norm_mix_w'][layer]
        if layer < N_A_LAYERS:
            z = _nm_matmul(h, nmw, sc_m, sh_m, w['wz'][layer], tm, 1024)
            xbc = _nm_matmul(h, nmw, sc_m, sh_m, w['wxbc'][layer], tm, 1024)
            dtr = _nm_matmul(h, nmw, sc_m, sh_m, w['wdt'][layer], tm, LANES)
            ssd_args = (w['conv_w'][layer], w['conv_b'][layer], w['head_params'][layer], w['d_skip'][layer],
                        w['ssm_norm_w'][layer])
            if sample:
                n_seq, t_new = conv0.shape[1], cache[3]
                seqs = lambda x: x.reshape(n_seq, t_new, x.shape[-1])
                conv_in = jnp.pad(conv0[layer], ((0, 0), (HALO - (CONV_WIDTH - 1), 0), (0, 0)))
                h0_all = ssm0.reshape(N_A_LAYERS, n_seq, D_INNER, D_STATE)
                g, conv8, st_all = _ssd(seqs(z), seqs(xbc), seqs(dtr), conv_in, h0_all, st_all, layer,
                                        *ssd_args, g_dtype=F32)
                g = g.reshape(1, t, D_INNER)
            else:
                conv_in = jnp.zeros((bsz, HALO, CONV_DIM), F32)
                g, conv8, st_all = _ssd(z, xbc, dtr, conv_in, None, st_all, layer, *ssd_args, g_dtype=BF16)
            conv_out.append(conv8[:, HALO - (CONV_WIDTH - 1):])
            h = _mm_res(g, w['w_out'][layer], h, g_m, tm)
        else:
            j = layer - N_A_LAYERS
            if sample:
                q = _q_proj(h, nmw, sc_m, sh_m, w['wq'][j], w['q_norm_w'][j], w['rq'], w['rqt'], tm, F32)
                o = _attn_sample(q, k_new, v_new, cache[0], cache[1], cache[2], cache[3])
            else:
                q = _q_proj(h, nmw, sc_m, sh_m, w['wq'][j], w['q_norm_w'][j], w['rq'], w['rqt'], tm, BF16)
                o = _attn_prompt(q, kd, vt, kmd)
            h = _mm_res(o, w['wo'][j], h, g_m, tm)
        h = _moe(h, w['norm_ffn_w'][layer], sc_f, sh_f, g_f, w['router_w'][layer], w['router_b'][layer],
                 w['w1'][layer], w['w3'][layer], w['w2'][layer], tm)
        if layer == N_A_LAYERS - 1:
            kv_tm = min(t, 512)
            res = _kv_proj(h, w['kv_norm_w'], kv_mod[1], kv_mod[0], w['wk'], w['wv'], w['k_norm_w'],
                           w['rk'], w['rkt'], kv_tm, with_dup=not sample)
            k_new, v_new = res[0], res[1]
            if not sample:
                kd, vt = res[2], res[3]
                kmd = res[4].reshape(bsz, t // MOBA_BLOCK, N_KV_HEADS * LANES)
                kmd = jnp.pad(kmd, ((0, 0), (0, LANES - kmd.shape[1]), (0, 0)))
    ssm_out = st_all.reshape(N_A_LAYERS, st_all.shape[1], SSM_HEADS, SSM_HEAD_DIM, D_STATE)
    return h, jnp.stack(conv_out), ssm_out, k_new, v_new


def kernel(x_prompt, x_sample, state_conv, state_ssm, cache_k, cache_v, page_table, c_prompt, c_sample, w_mod, b_mod, norm_mix_w, norm_ffn_w, ssm_w_in, ssm_conv_w, ssm_conv_b, ssm_dt_bias, ssm_a_log, ssm_d, ssm_norm_w, ssm_w_out, kv_w_mod, kv_b_mod, kv_norm_w, w_k, w_v, k_norm_w, attn_w_q, q_norm_w, attn_w_o, moe_w_grp, moe_b_grp, moe_w_exp, moe_b_exp, moe_w1, moe_w3, moe_w2):
    params = dict(ssm_w_in=ssm_w_in, ssm_conv_w=ssm_conv_w, ssm_conv_b=ssm_conv_b, ssm_dt_bias=ssm_dt_bias,
                  ssm_a_log=ssm_a_log, ssm_d=ssm_d, ssm_norm_w=ssm_norm_w, ssm_w_out=ssm_w_out,
                  norm_mix_w=norm_mix_w, norm_ffn_w=norm_ffn_w, kv_norm_w=kv_norm_w, w_k=w_k, w_v=w_v,
                  k_norm_w=k_norm_w, attn_w_q=attn_w_q, q_norm_w=q_norm_w, attn_w_o=attn_w_o,
                  moe_w_grp=moe_w_grp, moe_b_grp=moe_b_grp, moe_w_exp=moe_w_exp, moe_b_exp=moe_b_exp,
                  moe_w1=moe_w1, moe_w3=moe_w3, moe_w2=moe_w2)
    w = _prep_weights(params)

    bp, seq, d = x_prompt.shape
    n_seq, t_new, _ = x_sample.shape
    n_pages = page_table.shape[1]
    past_len = n_pages * PAGE_SIZE
    assert seq % MOBA_BLOCK == 0 and seq // MOBA_BLOCK <= LANES
    assert past_len % MOBA_BLOCK == 0 and t_new <= MOBA_BLOCK and t_new % SUBLANES == 0

    n_c = bp + n_seq
    n_c_pad = -(-n_c // SUBLANES) * SUBLANES
    c_all = jnp.pad(jnp.concatenate([c_prompt, c_sample], axis=0), ((0, n_c_pad - n_c), (0, 0)))
    mod_all = _mod_vectors(c_all, w_mod, b_mod[:, None, :], 1536)
    kv_all = _mod_vectors(c_all, kv_w_mod[None], kv_b_mod[None, None, :], 1024)[0]

    def group_mods(lo, hi, per_token_repeat):
        def shape(x):
            if per_token_repeat:
                return jnp.repeat(x, per_token_repeat, axis=0)[None]
            return x[:, None, :]
        mods = [[shape(mod_all[l, lo:hi, k * d:(k + 1) * d]) for k in range(6)] for l in range(DEPTH)]
        kvm = [shape(kv_all[lo:hi, k * d:(k + 1) * d]) for k in range(2)]
        return mods, kvm

    mods_p, kv_p = group_mods(0, bp, 0)
    y_p, conv_p, ssm_p, k_p, v_p = _trunk(x_prompt, mods_p, kv_p, w, sample=False)

    mods_s, kv_s = group_mods(bp, n_c, t_new)
    kvw = N_KV_HEADS * HEAD_DIM
    cache = (jnp.transpose(cache_k, (0, 2, 3, 1)), jnp.transpose(cache_v, (0, 2, 3, 1)), page_table, t_new)
    y_s, conv_s, ssm_s, k_s, v_s = _trunk(x_sample.reshape(1, n_seq * t_new, d), mods_s, kv_s, w,
                                          sample=True, conv0=state_conv, ssm0=state_ssm, cache=cache)

    heads = lambda x, b, t: x.reshape(b, t, N_KV_HEADS, HEAD_DIM)
    return (y_p, y_s.reshape(n_seq, t_new, d), conv_p, ssm_p, heads(k_p, bp, seq), heads(v_p, bp, seq),
            conv_s, ssm_s, heads(k_s, n_seq, t_new), heads(v_s, n_seq, t_new))
```

```python
import functools

import jax
import jax.numpy as jnp
from jax import lax
from jax.experimental import pallas as pl
from jax.experimental.pallas import tpu as pltpu

F32 = jnp.float32
BF16 = jnp.bfloat16
HIGHEST = lax.Precision.HIGHEST

D_MODEL = 1024
DEPTH = 4
N_A_LAYERS = 2
D_INNER = 2048
SSM_HEADS = 32
SSM_HEAD_DIM = 64
SSM_GROUPS = 4
D_STATE = 128
CONV_WIDTH = 4
CONV_DIM = D_INNER + 2 * SSM_GROUPS * D_STATE
SSD_CHUNK = 128
N_HEADS = 16
HEAD_DIM = 64
N_KV_HEADS = 8
MOBA_BLOCK = 256
MOBA_TOPK = 3
N_EXPERT_GROUPS = 4
EXPERTS_PER_GROUP = 4
N_EXPERTS = 16
D_EXPERT = 256
PAGE_SIZE = 128
EPS = 1e-6

LANES = 128
SUBLANES = 8
MIB = 1024 * 1024
NEG_BIG = -1e30

NT_DIMS = (((1,), (1,)), ((), ()))


def _cparams(sem, vmem_mib):
    return pltpu.CompilerParams(dimension_semantics=sem, vmem_limit_bytes=vmem_mib * MIB)


def _sigmoid(x):
    return 1.0 / (1.0 + jnp.exp(-x))


def _silu(x):
    return x * _sigmoid(x)


def _norm_mod(h, nw, sc, sh):
    ms = jnp.mean(h * h, axis=-1, keepdims=True)
    return (h * lax.rsqrt(ms + EPS)) * nw * (1.0 + sc) + sh


def _split_bf16(x):
    hi = x.astype(BF16)
    lo = (x - hi.astype(F32)).astype(BF16)
    return hi, lo


def _head_rmsnorm(x, r, rt):
    sq = x * x
    hi, lo = _split_bf16(sq)
    ss = jnp.dot(hi, r, preferred_element_type=F32) + jnp.dot(lo, r, preferred_element_type=F32)
    inv = lax.rsqrt(ss * (1.0 / HEAD_DIM) + EPS)
    ih, il = _split_bf16(inv)
    invx = jnp.dot(ih, rt, preferred_element_type=F32) + jnp.dot(il, rt, preferred_element_type=F32)
    return x * invx


def _mod_kernel(c_ref, w_ref, b_ref, o_ref):
    ca = _silu(c_ref[...])
    o_ref[0] = jnp.dot(ca, w_ref[0], precision=HIGHEST, preferred_element_type=F32) + b_ref[0]


def _mod_vectors(c_all, w, b, tn):
    n_layers, d, n = w.shape
    m = c_all.shape[0]
    return pl.pallas_call(
        _mod_kernel,
        grid=(n_layers, n // tn),
        in_specs=[pl.BlockSpec((m, d), lambda l, j: (0, 0)),
                  pl.BlockSpec((1, d, tn), lambda l, j: (l, 0, j)),
                  pl.BlockSpec((1, 1, tn), lambda l, j: (l, 0, j))],
        out_specs=pl.BlockSpec((1, m, tn), lambda l, j: (l, 0, j)),
        out_shape=jax.ShapeDtypeStruct((n_layers, m, n), F32),
        compiler_params=_cparams(("arbitrary", "arbitrary"), 40),
        name="mod_vectors",
    )(c_all, w, b)


def _mod_spec(mod, tm):
    d = mod.shape[-1]
    if mod.shape[1] == 1:
        return pl.BlockSpec((1, 1, d), lambda b, i, *_: (b, 0, 0))
    return pl.BlockSpec((1, tm, d), lambda b, i, *_: (b, i, 0))


def _nm_matmul_kernel(h_ref, nw_ref, sc_ref, sh_ref, w_ref, o_ref, u_ref):
    @pl.when(pl.program_id(2) == 0)
    def _():
        u_ref[...] = _norm_mod(h_ref[0], nw_ref[...], sc_ref[0], sh_ref[0]).astype(BF16)

    o_ref[0] = jnp.dot(u_ref[...], w_ref[...], preferred_element_type=F32)


def _nm_matmul(h, nw, sc, sh, w, tm, tn):
    bsz, t, d = h.shape
    n = w.shape[1]
    return pl.pallas_call(
        _nm_matmul_kernel,
        grid=(bsz, t // tm, n // tn),
        in_specs=[pl.BlockSpec((1, tm, d), lambda b, i, j: (b, i, 0)),
                  pl.BlockSpec((1, d), lambda b, i, j: (0, 0)),
                  _mod_spec(sc, tm), _mod_spec(sh, tm),
                  pl.BlockSpec((d, tn), lambda b, i, j: (0, j))],
        out_specs=pl.BlockSpec((1, tm, tn), lambda b, i, j: (b, i, j)),
        out_shape=jax.ShapeDtypeStruct((bsz, t, n), F32),
        scratch_shapes=[pltpu.VMEM((tm, d), BF16)],
        compiler_params=_cparams(("parallel", "parallel", "arbitrary"), 48),
        name="norm_mod_matmul",
    )(h, nw, sc, sh, w)


def _mm_res_kernel(a_ref, w_ref, h_ref, g_ref, o_ref):
    acc = jnp.dot(a_ref[0].astype(BF16), w_ref[...], preferred_element_type=F32)
    o_ref[0] = h_ref[0] + g_ref[0] * acc


def _mm_res(a, w, h, gate, tm):
    bsz, t, k = a.shape
    d = w.shape[1]
    return pl.pallas_call(
        _mm_res_kernel,
        grid=(bsz, t // tm),
        in_specs=[pl.BlockSpec((1, tm, k), lambda b, i: (b, i, 0)),
                  pl.BlockSpec((k, d), lambda b, i: (0, 0)),
                  pl.BlockSpec((1, tm, d), lambda b, i: (b, i, 0)),
                  _mod_spec(gate, tm)],
        out_specs=pl.BlockSpec((1, tm, d), lambda b, i: (b, i, 0)),
        out_shape=jax.ShapeDtypeStruct((bsz, t, d), F32),
        compiler_params=_cparams(("parallel", "parallel"), 48),
        name="matmul_residual",
    )(a, w, h, gate)


HALO = SUBLANES
XC_TILE = 512


def _ssd_kernel(z_ref, xbc_ref, dt_ref, cs_ref, h0_ref, cw_ref, cb_ref, hp_ref, dsk_ref, nw_ref,
                *rest, valid, has_init, has_prev):
    g_ref, cso_ref, st_ref, xs_ref, xc_ref, xt_ref, y_ref = rest[1:] if has_prev else rest
    L = SSD_CHUNK
    c = pl.program_id(1)

    @pl.when(c == 0)
    def _init():
        xs_ref[0:HALO, :] = cs_ref[0]
        if valid < L:
            xc_ref[...] = jnp.zeros_like(xc_ref)
        if has_init:
            st_ref[0] = h0_ref[0]
        else:
            st_ref[0] = jnp.zeros(st_ref.shape[1:], F32)

    xs_ref[HALO:HALO + valid, :] = xbc_ref[0]

    for j in range(CONV_DIM // XC_TILE):
        sl = slice(j * XC_TILE, (j + 1) * XC_TILE)
        x_rows = xs_ref[0:HALO + valid, sl]
        acc = x_rows * cw_ref[0:1, sl]
        for k in range(1, CONV_WIDTH):
            acc = pltpu.roll(acc, 1, 0) + x_rows * cw_ref[k:k + 1, sl]
        xc_ref[0:valid, sl] = _silu(acc[HALO:HALO + valid] + cb_ref[:, sl])

    cso_ref[0] = xs_ref[valid:valid + HALO, :]
    xs_ref[0:HALO, :] = xs_ref[valid:valid + HALO, :]

    dtr = dt_ref[0]
    if valid < L:
        dtr = jnp.concatenate([dtr, jnp.zeros((L - valid, LANES), F32)], axis=0)
    xx = dtr + hp_ref[0:1, :]
    dt = jnp.maximum(xx, 0.0) + jnp.log(1.0 + jnp.exp(-jnp.abs(xx)))
    row = lax.broadcasted_iota(jnp.int32, (L, L), 0)
    col = lax.broadcasted_iota(jnp.int32, (L, L), 1)
    if valid < L:
        dt = jnp.where(row < valid, dt, 0.0)
    a = dt * (-jnp.exp(hp_ref[1:2, :]))
    causal = row >= col
    a_cum = jnp.dot(causal.astype(F32), a, precision=HIGHEST, preferred_element_type=F32)
    a_cum_t = a_cum.T
    dt_t = dt.T
    lane_lo = col < HEAD_DIM
    sub_lo = row < HEAD_DIM

    for j in range(D_INNER // LANES):
        xt_ref[j * LANES:(j + 1) * LANES, :] = xc_ref[:, j * LANES:(j + 1) * LANES].T

    pairs_per_group = SSM_HEADS // SSM_GROUPS // 2
    grp_w = D_INNER // SSM_GROUPS
    for g in range(SSM_GROUPS):
        b_g = xc_ref[:, D_INNER + g * D_STATE:D_INNER + (g + 1) * D_STATE].astype(BF16)
        c_g = xc_ref[:, D_INNER + (SSM_GROUPS + g) * D_STATE:
                     D_INNER + (SSM_GROUPS + g + 1) * D_STATE].astype(BF16)
        cb = lax.dot_general(c_g, b_g, NT_DIMS, preferred_element_type=F32)
        st_g = st_ref[0, g * grp_w:(g + 1) * grp_w, :]
        y_off = lax.dot_general(c_g, st_g.astype(BF16), NT_DIMS, preferred_element_type=F32)
        xw_parts, dec_parts = [], []
        for jj in range(pairs_per_group):
            pair = g * pairs_per_group + jj
            s_mats, e_cols, w_rows, d_end = [], [], [], []
            for h in (2 * pair, 2 * pair + 1):
                colb = jnp.broadcast_to(a_cum[:, h:h + 1], (L, L))
                rowb = a_cum_t[h:h + 1, :]
                dec = jnp.where(causal, jnp.exp(colb - rowb), 0.0)
                s_mats.append((cb * dec * dt_t[h:h + 1, :]).astype(BF16))
                e_cols.append(jnp.exp(colb))
                a_last = a_cum_t[h:h + 1, L - 1:L]
                w_rows.append(dt_t[h:h + 1, :] * jnp.exp(a_last - rowb))
                d_end.append(jnp.broadcast_to(jnp.exp(a_last), (LANES, D_STATE)))
            psl = slice(pair * LANES, (pair + 1) * LANES)
            xp = xc_ref[:, psl]
            xpb = xp.astype(BF16)
            y_d = jnp.where(lane_lo,
                            jnp.dot(s_mats[0], xpb, preferred_element_type=F32),
                            jnp.dot(s_mats[1], xpb, preferred_element_type=F32))
            y_o = y_off[:, jj * LANES:(jj + 1) * LANES] * jnp.where(lane_lo, e_cols[0], e_cols[1])
            y_ref[:, psl] = y_d + y_o + dsk_ref[:, psl] * xp
            xw_parts.append((xt_ref[psl, :] * jnp.where(sub_lo, w_rows[0], w_rows[1])).astype(BF16))
            dec_parts.append(jnp.where(sub_lo, d_end[0], d_end[1]))
        xw_g = jnp.concatenate(xw_parts, axis=0)
        new = jnp.dot(xw_g, b_g, preferred_element_type=F32)
        st_ref[0, g * grp_w:(g + 1) * grp_w, :] = st_g * jnp.concatenate(dec_parts, axis=0) + new

    zz = z_ref[0]
    gg = y_ref[0:valid, :] * _silu(zz)
    for g in range(SSM_GROUPS):
        sl = slice(g * grp_w, (g + 1) * grp_w)
        seg = gg[:, sl]
        ms = jnp.mean(seg * seg, axis=-1, keepdims=True)
        g_ref[0, :, sl] = (seg * lax.rsqrt(ms + EPS) * nw_ref[:, sl]).astype(g_ref.dtype)


def _ssd(z, xbc, dtr, conv_in, h0_all, st_all, layer, conv_w, conv_b, head_params, d_skip, norm_w, g_dtype):
    bsz, t, _ = z.shape
    L = SSD_CHUNK
    valid = min(L, t)
    assert t % valid == 0 and valid % SUBLANES == 0
    nc = t // valid
    has_init = h0_all is not None
    state_spec = pl.BlockSpec((None, 1, D_INNER, D_STATE), lambda b, c: (layer, b, 0, 0))
    if h0_all is None:
        h0_all = jnp.zeros((1, SUBLANES, D_STATE), F32)
        h0_spec = pl.BlockSpec((1, SUBLANES, D_STATE), lambda b, c: (0, 0, 0))
    else:
        h0_spec = state_spec
    full = lambda shape: pl.BlockSpec(shape, lambda b, c: (0,) * len(shape))
    operands = [z, xbc, dtr, conv_in, h0_all, conv_w, conv_b, head_params, d_skip, norm_w]
    in_specs = [pl.BlockSpec((1, valid, D_INNER), lambda b, c: (b, c, 0)),
                pl.BlockSpec((1, valid, CONV_DIM), lambda b, c: (b, c, 0)),
                pl.BlockSpec((1, valid, LANES), lambda b, c: (b, c, 0)),
                pl.BlockSpec((1, HALO, CONV_DIM), lambda b, c: (b, 0, 0)),
                h0_spec,
                full((CONV_WIDTH, CONV_DIM)), full((1, CONV_DIM)), full((SUBLANES, LANES)),
                full((1, D_INNER)), full((1, D_INNER))]
    aliases = {}
    if st_all is not None:
        aliases = {len(operands): 2}
        operands.append(st_all)
        in_specs.append(pl.BlockSpec(memory_space=pl.ANY))
    return pl.pallas_call(
        functools.partial(_ssd_kernel, valid=valid, has_init=has_init, has_prev=st_all is not None),
        grid=(bsz, nc),
        in_specs=in_specs,
        out_specs=[pl.BlockSpec((1, valid, D_INNER), lambda b, c: (b, c, 0)),
                   pl.BlockSpec((1, HALO, CONV_DIM), lambda b, c: (b, 0, 0)),
                   state_spec],
        out_shape=[jax.ShapeDtypeStruct((bsz, t, D_INNER), g_dtype),
                   jax.ShapeDtypeStruct((bsz, HALO, CONV_DIM), F32),
                   jax.ShapeDtypeStruct((N_A_LAYERS, bsz, D_INNER, D_STATE), F32)],
        input_output_aliases=aliases,
        scratch_shapes=[pltpu.VMEM((L + HALO, CONV_DIM), F32),
                        pltpu.VMEM((L, CONV_DIM), F32),
                        pltpu.VMEM((D_INNER, L), F32),
                        pltpu.VMEM((L, D_INNER), F32)],
        compiler_params=_cparams(("parallel", "arbitrary"), 48),
        name="ssd_chunk_scan",
    )(*operands)


ROUTER_GRP0 = 0
ROUTER_EXP0 = N_EXPERT_GROUPS


def _route(logits):
    lane = lax.broadcasted_iota(jnp.int32, logits.shape, 1)
    big = jnp.int32(2 * LANES)
    is_grp = lane < N_EXPERT_GROUPS
    gl = jnp.where(is_grp, logits, -jnp.inf)
    gmax = jnp.max(gl, axis=-1, keepdims=True)
    gidx = jnp.min(jnp.where(gl == gmax, lane, big), axis=-1, keepdims=True)
    p_grp = 1.0 / jnp.sum(jnp.where(is_grp, jnp.exp(gl - gmax), 0.0), axis=-1, keepdims=True)
    e_rel = lane - ROUTER_EXP0
    in_grp = (e_rel >= gidx * EXPERTS_PER_GROUP) & (e_rel < (gidx + 1) * EXPERTS_PER_GROUP)
    el = jnp.where(in_grp, logits, -jnp.inf)
    m1 = jnp.max(el, axis=-1, keepdims=True)
    i1 = jnp.min(jnp.where(el == m1, lane, big), axis=-1, keepdims=True)
    el2 = jnp.where(lane == i1, -jnp.inf, el)
    m2 = jnp.max(el2, axis=-1, keepdims=True)
    i2 = jnp.min(jnp.where(el2 == m2, lane, big), axis=-1, keepdims=True)
    e2 = jnp.exp(m2 - m1)
    den = 1.0 + e2
    w1 = (1.0 / den) * p_grp
    w2 = (e2 / den) * p_grp
    return jnp.where(lane == i1, w1, jnp.where(lane == i2, w2, 0.0))


def _moe_kernel(h_ref, nw_ref, sc_ref, sh_ref, gf_ref, wr_ref, br_ref, w1_ref, w3_ref, w2_ref,
                o_ref, u_ref, gates_ref, acc_ref):
    step = pl.program_id(2)
    per_step = w1_ref.shape[0]

    @pl.when(step == 0)
    def _():
        u = _norm_mod(h_ref[0], nw_ref[...], sc_ref[0], sh_ref[0])
        logits = jnp.dot(u, wr_ref[...], precision=HIGHEST, preferred_element_type=F32) + br_ref[...]
        gates_ref[...] = _route(logits)
        u_ref[...] = u.astype(BF16)
        acc_ref[...] = jnp.zeros_like(acc_ref)

    u = u_ref[...]
    gates = gates_ref[...]
    lane = lax.broadcasted_iota(jnp.int32, gates.shape, 1)
    acts = []
    for j in range(per_step):
        a = (_silu(jnp.dot(u, w1_ref[j], preferred_element_type=F32))
             * jnp.dot(u, w3_ref[j], preferred_element_type=F32))
        e = step * per_step + j
        gcol = jnp.sum(jnp.where(lane == e + ROUTER_EXP0, gates, 0.0), axis=-1, keepdims=True)
        acts.append((a * gcol).astype(BF16))
    w2 = w2_ref[...].reshape(per_step * w2_ref.shape[1], w2_ref.shape[2])
    acc_ref[...] += jnp.dot(jnp.concatenate(acts, axis=1), w2, preferred_element_type=F32)

    @pl.when(step == pl.num_programs(2) - 1)
    def _():
        o_ref[0] = h_ref[0] + gf_ref[0] * acc_ref[...]


def _moe(h, nw, sc, sh, gf, wr, br, w1, w3, w2, tm):
    bsz, t, d = h.shape
    n_e, _, f = w1.shape
    per_step = EXPERTS_PER_GROUP
    return pl.pallas_call(
        _moe_kernel,
        grid=(bsz, t // tm, n_e // per_step),
        in_specs=[pl.BlockSpec((1, tm, d), lambda b, i, e: (b, i, 0)),
                  pl.BlockSpec((1, d), lambda b, i, e: (0, 0)),
                  _mod_spec(sc, tm), _mod_spec(sh, tm), _mod_spec(gf, tm),
                  pl.BlockSpec((d, LANES), lambda b, i, e: (0, 0)),
                  pl.BlockSpec((1, LANES), lambda b, i, e: (0, 0)),
                  pl.BlockSpec((per_step, d, f), lambda b, i, e: (e, 0, 0)),
                  pl.BlockSpec((per_step, d, f), lambda b, i, e: (e, 0, 0)),
                  pl.BlockSpec((per_step, f, d), lambda b, i, e: (e, 0, 0))],
        out_specs=pl.BlockSpec((1, tm, d), lambda b, i, e: (b, i, 0)),
        out_shape=jax.ShapeDtypeStruct((bsz, t, d), F32),
        scratch_shapes=[pltpu.VMEM((tm, d), BF16), pltpu.VMEM((tm, LANES), F32), pltpu.VMEM((tm, d), F32)],
        compiler_params=_cparams(("parallel", "parallel", "arbitrary"), 48),
        name="hmoe",
    )(h, nw, sc, sh, gf, wr, br, w1, w3, w2)


N_SPLIT = 3
FEAT_POS_LO = HEAD_DIM
FEAT_POS_HI = FEAT_POS_LO + N_SPLIT
FEAT_ONE = FEAT_POS_HI + N_SPLIT


def _split3(x):
    h1 = x.astype(BF16).astype(F32)
    r = x - h1
    h2 = r.astype(BF16).astype(F32)
    h3 = (r - h2).astype(BF16).astype(F32)
    return h1, h2, h3


def _feature_lanes(lane, base, parts, other):
    out = other
    for k, part in enumerate(parts):
        out = jnp.where(lane == base + k, part, out)
    return out


def _key_tiles(x, feat):
    lane = lax.broadcasted_iota(jnp.int32, x.shape, 1)
    rolled = pltpu.roll(x, HEAD_DIM, 1)
    lo = lane < HEAD_DIM
    return jnp.where(lo, x, feat), jnp.where(lo, rolled, feat)


def _kv_kernel(h_ref, nw_ref, sc_ref, sh_ref, wk_ref, wv_ref, knw_ref, r_ref, rt_ref,
               kf_ref, vf_ref, *dup_refs, tm, with_dup):
    u = _norm_mod(h_ref[0], nw_ref[...], sc_ref[0], sh_ref[0]).astype(BF16)
    k = jnp.dot(u, wk_ref[...], preferred_element_type=F32)
    v = jnp.dot(u, wv_ref[...], preferred_element_type=F32)
    kn = _head_rmsnorm(k, r_ref[...], rt_ref[...]) * knw_ref[...]
    if not with_dup:
        kf_ref[0] = kn
        vf_ref[0] = v
    else:
        for hh in range(N_KV_HEADS):
            rows = pl.ds(hh, tm, stride=N_KV_HEADS)
            for src, dst in ((kn, kf_ref), (v, vf_ref)):
                tile = src[:, (hh // 2) * LANES:(hh // 2 + 1) * LANES]
                if hh % 2 == 1:
                    tile = pltpu.roll(tile, HEAD_DIM, 1)
                dst[0, rows, :] = tile[:, :HEAD_DIM]
        kd_ref, vt_ref, km_ref = dup_refs
        lane = lax.broadcasted_iota(jnp.int32, (tm, LANES), 1)
        pos = pl.program_id(1) * tm + lax.broadcasted_iota(jnp.int32, (tm, LANES), 0)
        pos_lo = pos % MOBA_BLOCK
        feat = jnp.zeros((tm, LANES), F32)
        feat = _feature_lanes(lane, FEAT_POS_LO, [pos_lo.astype(F32)] * N_SPLIT, feat)
        feat = _feature_lanes(lane, FEAT_POS_HI, [(pos - pos_lo).astype(F32)] * N_SPLIT, feat)
        feat = _feature_lanes(lane, FEAT_ONE, [jnp.ones((tm, LANES), F32)] * N_SPLIT, feat)
        lane_row = lax.broadcasted_iota(jnp.int32, (1, LANES), 1)
        for cidx in range(N_KV_HEADS // 2):
            d0, d1 = _key_tiles(kn[:, cidx * LANES:(cidx + 1) * LANES], feat)
            for hh, dd in ((2 * cidx, d0), (2 * cidx + 1, d1)):
                hsl = slice(hh * LANES, (hh + 1) * LANES)
                kd_ref[0, :, hsl] = dd.astype(BF16)
                for blk in range(tm // MOBA_BLOCK):
                    mean = jnp.mean(dd[blk * MOBA_BLOCK:(blk + 1) * MOBA_BLOCK], axis=0, keepdims=True)
                    km_ref[0, blk, :, hsl] = jnp.where(lane_row < HEAD_DIM, mean, 0.0)
        for blk in range(tm // MOBA_BLOCK):
            vt_ref[0, blk] = v[blk * MOBA_BLOCK:(blk + 1) * MOBA_BLOCK].T.astype(BF16)


def _kv_proj(h, nw, sc, sh, wk, wv, knw, r, rt, tm, with_dup):
    bsz, t, d = h.shape
    kvw = wk.shape[1]
    const = lambda shape: pl.BlockSpec(shape, lambda b, i: (0,) * len(shape))
    if not with_dup:
        out_specs = [pl.BlockSpec((1, tm, kvw), lambda b, i: (b, i, 0))] * 2
        out_shape = [jax.ShapeDtypeStruct((bsz, t, kvw), F32)] * 2
    else:
        out_specs = [pl.BlockSpec((1, tm * N_KV_HEADS, HEAD_DIM), lambda b, i: (b, i, 0))] * 2
        out_shape = [jax.ShapeDtypeStruct((bsz, t * N_KV_HEADS, HEAD_DIM), F32)] * 2
        dupw = N_KV_HEADS * LANES
        nblk = tm // MOBA_BLOCK
        out_specs += [pl.BlockSpec((1, tm, dupw), lambda b, i: (b, i, 0)),
                      pl.BlockSpec((1, nblk, kvw, MOBA_BLOCK), lambda b, i: (b, i, 0, 0)),
                      pl.BlockSpec((1, nblk, 1, dupw), lambda b, i: (b, i, 0, 0))]
        out_shape += [jax.ShapeDtypeStruct((bsz, t, dupw), BF16),
                      jax.ShapeDtypeStruct((bsz, t // MOBA_BLOCK, kvw, MOBA_BLOCK), BF16),
                      jax.ShapeDtypeStruct((bsz, t // MOBA_BLOCK, 1, dupw), F32)]
    return pl.pallas_call(
        functools.partial(_kv_kernel, tm=tm, with_dup=with_dup),
        grid=(bsz, t // tm),
        in_specs=[pl.BlockSpec((1, tm, d), lambda b, i: (b, i, 0)),
                  const((1, d)), _mod_spec(sc, tm), _mod_spec(sh, tm),
                  const((d, kvw)), const((d, kvw)), const((1, kvw)),
                  const((kvw, LANES)), const((LANES, kvw))],
        out_specs=out_specs,
        out_shape=out_shape,
        compiler_params=_cparams(("parallel", "parallel"), 48),
        name="shared_kv",
    )(h, nw, sc, sh, wk, wv, knw, r, rt)


def _q_kernel(h_ref, nw_ref, sc_ref, sh_ref, wq_ref, qnw_ref, r_ref, rt_ref, q_ref, *, out_scale):
    u = _norm_mod(h_ref[0], nw_ref[...], sc_ref[0], sh_ref[0]).astype(BF16)
    q = jnp.dot(u, wq_ref[...], preferred_element_type=F32)
    q = _head_rmsnorm(q, r_ref[...], rt_ref[...]) * qnw_ref[...]
    if out_scale != 1.0:
        q = q * out_scale
    q_ref[0] = q.astype(q_ref.dtype)


def _q_proj(h, nw, sc, sh, wq, qnw, r, rt, tm, out_dtype, out_scale=1.0):
    bsz, t, d = h.shape
    const = lambda shape: pl.BlockSpec(shape, lambda b, i: (0,) * len(shape))
    return pl.pallas_call(
        functools.partial(_q_kernel, out_scale=out_scale),
        grid=(bsz, t // tm),
        in_specs=[pl.BlockSpec((1, tm, d), lambda b, i: (b, i, 0)),
                  const((1, d)), _mod_spec(sc, tm), _mod_spec(sh, tm),
                  const((d, d)), const((1, d)), const((d, LANES)), const((LANES, d))],
        out_specs=pl.BlockSpec((1, tm, d), lambda b, i: (b, i, 0)),
        out_shape=jax.ShapeDtypeStruct((bsz, t, d), out_dtype),
        compiler_params=_cparams(("parallel", "parallel"), 48),
        name="q_proj",
    )(h, nw, sc, sh, wq, qnw, r, rt)


def _top_blocks(gate, n_valid, axis):
    pos = lax.broadcasted_iota(jnp.int32, gate.shape, axis)
    gm = jnp.where(pos < n_valid, gate, -jnp.inf)
    sel = jnp.zeros(gate.shape, F32)
    for _ in range(MOBA_TOPK):
        mx = jnp.max(gm, axis=axis, keepdims=True)
        cand = (gm == mx) & (mx > -jnp.inf)
        idx = jnp.min(jnp.where(cand, pos, jnp.int32(2 * LANES)), axis=axis, keepdims=True)
        pick = pos == idx
        sel = jnp.where(pick, 1.0, sel)
        gm = jnp.where(pick, -jnp.inf, gm)
    return sel


def _alibi_slope(head):
    return jnp.exp2(-8.0 * (head + 1).astype(F32) / N_HEADS)


ATT_GROUP = 4
ACC_ROWS = HEAD_DIM + 16
LOG2_E = 1.4426950408889634


def _attn_prompt_kernel(q_ref, k_ref, vt_ref, km_ref, o_ref,
                        qt_ref, sa_ref, sb_ref, sel_ref, m_ref, acc_ref):
    g = pl.program_id(1)
    i = pl.program_id(2)
    blk = MOBA_BLOCK
    cols = 2 * blk
    grp = ATT_GROUP

    qt = q_ref[0].astype(F32)
    lane = lax.broadcasted_iota(jnp.int32, (blk, LANES), 1)
    q2 = jnp.concatenate([jnp.where(lane < HEAD_DIM, qt, 0.0),
                          jnp.where(lane < HEAD_DIM, pltpu.roll(qt, HEAD_DIM, 1), 0.0)], axis=0)
    r2 = lax.broadcasted_iota(jnp.int32, (cols, LANES), 0)
    l2 = lax.broadcasted_iota(jnp.int32, (cols, LANES), 1)
    second = r2 >= blk
    slope = _alibi_slope(2 * g + second.astype(jnp.int32)) * LOG2_E
    pos_q = jnp.where(second, r2 - blk, r2) + i * blk
    slope_parts = _split3(slope)
    q2 = _feature_lanes(l2, FEAT_POS_LO, slope_parts, q2)
    q2 = _feature_lanes(l2, FEAT_POS_HI, slope_parts, q2)
    q2 = _feature_lanes(l2, FEAT_ONE, _split3(-slope * pos_q.astype(F32)), q2)
    q2t = q2.T
    gate = jnp.dot(km_ref[0], q2t, precision=HIGHEST, preferred_element_type=F32)
    sel_ref[...] = _top_blocks(gate, i, 0)
    is_q = lax.broadcasted_iota(jnp.int32, (LANES, 1), 0) < HEAD_DIM
    qt_ref[...] = (q2t * jnp.where(is_q, HEAD_DIM ** -0.5, 1.0)).astype(BF16)

    def scores(n):
        start = pl.multiple_of(n * blk, blk)
        return jnp.dot(k_ref[0, pl.ds(start, blk), :], qt_ref[...], preferred_element_type=F32)

    ones_rows = jnp.ones((ACC_ROWS - HEAD_DIM, blk), BF16)

    def block_stats(s, n):
        mb = jnp.max(s, axis=0, keepdims=True)
        p = jnp.exp2(s - mb).astype(BF16)
        vt_aug = jnp.concatenate([vt_ref[0, n], ones_rows], axis=0)
        return mb, jnp.dot(vt_aug, p, preferred_element_type=F32)

    def merge(parts):
        m_old = m_ref[...]
        m_new = m_old
        for mb, _ in parts:
            m_new = jnp.maximum(m_new, mb)
        acc_new = jnp.exp2(m_old - m_new) * acc_ref[...]
        for mb, ab in parts:
            acc_new = acc_new + jnp.exp2(mb - m_new) * ab
        m_ref[...] = m_new
        acc_ref[...] = acc_new

    rr = lax.broadcasted_iota(jnp.int32, (blk, cols), 0)
    cc = lax.broadcasted_iota(jnp.int32, (blk, cols), 1)
    visible = jnp.where(cc >= blk, cc - blk, cc) >= rr
    m0, a0 = block_stats(jnp.where(visible, scores(i), -jnp.inf), i)
    m_ref[...] = m0
    acc_ref[...] = a0

    n_groups = (i + grp - 1) // grp

    def score_group(kk, s_ref):
        for b in range(grp):
            n = jnp.minimum(kk * grp + b, i - 1)
            s_ref[b * blk:(b + 1) * blk, :] = scores(n)

    def softmax_group(kk, s_ref):
        parts = []
        for b in range(grp):
            n_raw = kk * grp + b
            n = jnp.minimum(n_raw, i - 1)
            mb, ab = block_stats(s_ref[b * blk:(b + 1) * blk, :], n)
            counted = sel_ref[pl.ds(n, 1), :] * (n_raw < i).astype(F32) > 0.0
            parts.append((jnp.where(counted, mb, NEG_BIG), ab))
        merge(parts)

    @pl.when(n_groups > 0)
    def _first():
        score_group(0, sa_ref)

    def step(kk, carry):
        score_group(2 * kk + 1, sb_ref)
        softmax_group(2 * kk, sa_ref)
        score_group(2 * kk + 2, sa_ref)
        softmax_group(2 * kk + 1, sb_ref)
        return carry

    lax.fori_loop(0, n_groups // 2, step, 0)

    @pl.when(n_groups % 2 == 1)
    def _last():
        softmax_group(n_groups - 1, sa_ref)

    o = acc_ref[0:HEAD_DIM, :] / acc_ref[HEAD_DIM:HEAD_DIM + 1, :]
    o_ref[0] = jnp.concatenate([o[:, 0:blk], o[:, blk:cols]], axis=0).T.astype(o_ref.dtype)


def _attn_prompt(q, kd, vt, kmd):
    bsz, t, d = q.shape
    blk = MOBA_BLOCK
    nb = t // blk
    nb_pad = kmd.shape[1]
    cols = 2 * blk
    return pl.pallas_call(
        _attn_prompt_kernel,
        grid=(bsz, N_KV_HEADS, nb),
        in_specs=[pl.BlockSpec((1, blk, LANES), lambda b, g, i: (b, i, g)),
                  pl.BlockSpec((1, t, LANES), lambda b, g, i: (b, 0, g)),
                  pl.BlockSpec((1, nb, HEAD_DIM, blk), lambda b, g, i: (b, 0, g, 0)),
                  pl.BlockSpec((1, nb_pad, LANES), lambda b, g, i: (b, 0, g))],
        out_specs=pl.BlockSpec((1, blk, LANES), lambda b, g, i: (b, i, g)),
        out_shape=jax.ShapeDtypeStruct((bsz, t, d), BF16),
        scratch_shapes=[pltpu.VMEM((LANES, cols), BF16),
                        pltpu.VMEM((ATT_GROUP * blk, cols), F32), pltpu.VMEM((ATT_GROUP * blk, cols), F32),
                        pltpu.VMEM((nb_pad, cols), F32), pltpu.VMEM((1, cols), F32),
                        pltpu.VMEM((ACC_ROWS, cols), F32)],
        compiler_params=_cparams(("parallel", "parallel", "arbitrary"), 48),
        name="moba_prompt",
    )(q, kd, vt, kmd)


def _attn_sample_kernel(pt_ref, q_ref, kn_ref, vn_ref, *rest, n_pages):
    del pt_ref
    k_refs = rest[:n_pages]
    v_refs = rest[n_pages:2 * n_pages]
    o_ref = rest[2 * n_pages]
    s_ref = rest[2 * n_pages + 1]
    t_new = q_ref.shape[1]
    past_len = n_pages * PAGE_SIZE
    kvw = N_KV_HEADS * HEAD_DIM
    rows = N_HEADS * t_new
    n_past_blk = past_len // MOBA_BLOCK
    pages_per_blk = MOBA_BLOCK // PAGE_SIZE
    q = q_ref[0]
    lane8 = lax.broadcasted_iota(jnp.int32, (t_new, LANES), 1)
    zeros8 = jnp.zeros((t_new, LANES), F32)
    row_blocks = []
    for h in range(N_HEADS):
        g = h // 2
        tile = q[:, g * LANES:(g + 1) * LANES]
        if h % 2 != g % 2:
            tile = pltpu.roll(tile, HEAD_DIM, 1)
        keep = (lane8 < HEAD_DIM) if g % 2 == 0 else (lane8 >= HEAD_DIM)
        tile = jnp.where(keep, tile, 0.0)
        row_blocks.append(jnp.concatenate([tile if cidx == g // 2 else zeros8 for cidx in range(kvw // LANES)], axis=1))
    qm = jnp.concatenate(row_blocks, axis=0)
    qmb = (qm * (HEAD_DIM ** -0.5)).astype(BF16)

    r1 = lax.broadcasted_iota(jnp.int32, (rows, 1), 0)
    slope = _alibi_slope(r1 // t_new)
    rr = lax.broadcasted_iota(jnp.int32, (rows, PAGE_SIZE), 0)
    cc = lax.broadcasted_iota(jnp.int32, (rows, PAGE_SIZE), 1)
    d_page0 = (past_len + rr % t_new - cc).astype(F32)

    lane_b = lax.broadcasted_iota(jnp.int32, (kvw, LANES), 1)
    kmt = jnp.zeros((kvw, LANES), F32)
    for n in range(n_past_blk):
        acc = k_refs[n * pages_per_blk][0].reshape(kvw, PAGE_SIZE)
        for pp in range(1, pages_per_blk):
            acc = acc + k_refs[n * pages_per_blk + pp][0].reshape(kvw, PAGE_SIZE)
        mean = jnp.sum(acc, axis=1, keepdims=True) * (1.0 / MOBA_BLOCK)
        kmt = jnp.where(lane_b == n, mean, kmt)
    gate = jnp.dot(qm, kmt, precision=HIGHEST, preferred_element_type=F32)
    sel = _top_blocks(gate, n_past_blk, 1)
    lane_s = lax.broadcasted_iota(jnp.int32, sel.shape, 1)

    for p in range(n_pages):
        kp = k_refs[p][0].reshape(kvw, PAGE_SIZE).astype(BF16)
        s = jnp.dot(qmb, kp, preferred_element_type=F32)
        s = s - slope * (d_page0 - float(p * PAGE_SIZE))
        picked = jnp.sum(jnp.where(lane_s == p // pages_per_blk, sel, 0.0), axis=-1, keepdims=True) > 0.0
        s_ref[:, p * PAGE_SIZE:(p + 1) * PAGE_SIZE] = jnp.where(picked, s, -jnp.inf)
    pad = jnp.zeros((LANES - t_new, kvw), F32)
    k_new = jnp.concatenate([kn_ref[0], pad], axis=0).astype(BF16)
    v_new = jnp.concatenate([vn_ref[0], pad], axis=0).astype(BF16)
    rr_own = lax.broadcasted_iota(jnp.int32, (rows, LANES), 0)
    cc_own = lax.broadcasted_iota(jnp.int32, (rows, LANES), 1)
    d_own = (rr_own % t_new - cc_own).astype(F32)
    s = lax.dot_general(qmb, k_new, NT_DIMS, preferred_element_type=F32) - slope * d_own
    s_ref[:, past_len:] = jnp.where((d_own >= 0.0) & (cc_own < t_new), s, -jnp.inf)

    s_all = s_ref[...]
    m = jnp.max(s_all, axis=-1, keepdims=True)
    p_all = jnp.exp(s_all - m)
    l = jnp.sum(p_all, axis=-1, keepdims=True)
    pb = p_all.astype(BF16)
    out = jnp.dot(pb[:, past_len:], v_new, preferred_element_type=F32)
    for p in range(n_pages):
        vp = v_refs[p][0].reshape(kvw, PAGE_SIZE).astype(BF16)
        out = out + lax.dot_general(pb[:, p * PAGE_SIZE:(p + 1) * PAGE_SIZE], vp, NT_DIMS,
                                    preferred_element_type=F32)
    out = out / l

    lo8 = lane8 < HEAD_DIM
    for cidx in range(N_HEADS // 2):
        src = slice((cidx // 2) * LANES, (cidx // 2 + 1) * LANES)
        a = out[(2 * cidx) * t_new:(2 * cidx + 1) * t_new, src]
        b = out[(2 * cidx + 1) * t_new:(2 * cidx + 2) * t_new, src]
        if cidx % 2 == 1:
            a = pltpu.roll(a, HEAD_DIM, 1)
        else:
            b = pltpu.roll(b, HEAD_DIM, 1)
        o_ref[0, :, cidx * LANES:(cidx + 1) * LANES] = jnp.where(lo8, a, b)


def _attn_sample(q, k_new, v_new, cache_kt, cache_vt, page_table, t_new):
    n_seq, n_pages = page_table.shape
    d = q.shape[-1]
    kvw = k_new.shape[-1]
    rows = N_HEADS * t_new
    page_spec = lambda p: pl.BlockSpec((1, N_KV_HEADS, HEAD_DIM, PAGE_SIZE), lambda s, pt: (pt[s, p], 0, 0, 0))
    grid_spec = pltpu.PrefetchScalarGridSpec(
        num_scalar_prefetch=1,
        grid=(n_seq,),
        in_specs=[pl.BlockSpec((1, t_new, d), lambda s, pt: (0, s, 0)),
                  pl.BlockSpec((1, t_new, kvw), lambda s, pt: (0, s, 0)),
                  pl.BlockSpec((1, t_new, kvw), lambda s, pt: (0, s, 0))]
        + [page_spec(p) for p in range(n_pages)] + [page_spec(p) for p in range(n_pages)],
        out_specs=pl.BlockSpec((1, t_new, d), lambda s, pt: (0, s, 0)),
        scratch_shapes=[pltpu.VMEM((rows, n_pages * PAGE_SIZE + LANES), F32)],
    )
    return pl.pallas_call(
        functools.partial(_attn_sample_kernel, n_pages=n_pages),
        grid_spec=grid_spec,
        out_shape=jax.ShapeDtypeStruct(q.shape, F32),
        compiler_params=_cparams(("arbitrary",), 48),
        name="moba_sample",
    )(page_table, q, k_new, v_new, *([cache_kt] * n_pages), *([cache_vt] * n_pages))


def _head_sum_matrices(n_heads):
    w = n_heads * HEAD_DIM
    head_of_lane = jnp.arange(w) // HEAD_DIM
    r = (head_of_lane[:, None] == jnp.arange(LANES)[None, :]).astype(BF16)
    return r, r.T


def _prep_weights(p):
    bf = lambda x: x.astype(BF16)
    w_in = p['ssm_w_in']
    wdt = jnp.pad(w_in[:, :, D_INNER + CONV_DIM:], ((0, 0), (0, 0), (0, LANES - SSM_HEADS)))
    pad_heads = lambda x: jnp.pad(x, ((0, 0), (0, LANES - SSM_HEADS)))
    head_params = jnp.stack([pad_heads(p['ssm_dt_bias']), pad_heads(p['ssm_a_log'])], axis=1)
    head_params = jnp.pad(head_params, ((0, 0), (0, SUBLANES - 2), (0, 0)))
    router_w = jnp.concatenate([p['moe_w_grp'], p['moe_w_exp']], axis=-1)
    router_w = jnp.pad(router_w, ((0, 0), (0, 0), (0, LANES - router_w.shape[-1])))
    router_b = jnp.concatenate([p['moe_b_grp'], p['moe_b_exp']], axis=-1)
    router_b = jnp.pad(router_b, ((0, 0), (0, LANES - router_b.shape[-1])))[:, None, :]
    rq, rqt = _head_sum_matrices(N_HEADS)
    rk, rkt = _head_sum_matrices(N_KV_HEADS)
    return dict(
        wz=bf(w_in[:, :, :D_INNER]), wxbc=bf(w_in[:, :, D_INNER:D_INNER + CONV_DIM]), wdt=bf(wdt),
        head_params=head_params, d_skip=jnp.repeat(p['ssm_d'], SSM_HEAD_DIM, axis=-1)[:, None, :],
        conv_w=p['ssm_conv_w'], conv_b=p['ssm_conv_b'][:, None, :], ssm_norm_w=p['ssm_norm_w'][:, None, :],
        w_out=bf(p['ssm_w_out']),
        norm_mix_w=p['norm_mix_w'][:, None, :], norm_ffn_w=p['norm_ffn_w'][:, None, :],
        kv_norm_w=p['kv_norm_w'][None, :], wk=bf(p['w_k']), wv=bf(p['w_v']),
        k_norm_w=jnp.tile(p['k_norm_w'], N_KV_HEADS)[None, :],
        wq=bf(p['attn_w_q']), q_norm_w=jnp.tile(p['q_norm_w'], (1, N_HEADS))[:, None, :], wo=bf(p['attn_w_o']),
        router_w=router_w, router_b=router_b,
        w1=bf(p['moe_w1']), w3=bf(p['moe_w3']), w2=bf(p['moe_w2']),
        rq=rq, rqt=rqt, rk=rk, rkt=rkt,
    )


def _trunk(h, mods, kv_mod, w, *, sample, conv0=None, ssm0=None, cache=None):
    bsz, t, _ = h.shape
    tm = min(t, 1024)
    conv_out = []
    st_all = None
    k_new = v_new = kd = vt = kmd = None
    for layer in range(DEPTH):
        sh_m, sc_m, g_m, sh_f, sc_f, g_f = mods[layer]
        nmw = w['norm_mix_w'][layer]
        if layer < N_A_LAYERS:
            z = _nm_matmul(h, nmw, sc_m, sh_m, w['wz'][layer], tm, 1024)
            xbc = _nm_matmul(h, nmw, sc_m, sh_m, w['wxbc'][layer], tm, 1024)
            dtr = _nm_matmul(h, nmw, sc_m, sh_m, w['wdt'][layer], tm, LANES)
            ssd_args = (w['conv_w'][layer], w['conv_b'][layer], w['head_params'][layer], w['d_skip'][layer],
                        w['ssm_norm_w'][layer])
            if sample:
                n_seq, t_new = conv0.shape[1], cache[3]
                seqs = lambda x: x.reshape(n_seq, t_new, x.shape[-1])
                conv_in = jnp.pad(conv0[layer], ((0, 0), (HALO - (CONV_WIDTH - 1), 0), (0, 0)))
                h0_all = ssm0.reshape(N_A_LAYERS, n_seq, D_INNER, D_STATE)
                g, conv8, st_all = _ssd(seqs(z), seqs(xbc), seqs(dtr), conv_in, h0_all, st_all, layer,
                                        *ssd_args, g_dtype=F32)
                g = g.reshape(1, t, D_INNER)
            else:
                conv_in = jnp.zeros((bsz, HALO, CONV_DIM), F32)
                g, conv8, st_all = _ssd(z, xbc, dtr, conv_in, None, st_all, layer, *ssd_args, g_dtype=BF16)
            conv_out.append(conv8[:, HALO - (CONV_WIDTH - 1):])
            h = _mm_res(g, w['w_out'][layer], h, g_m, tm)
        else:
            j = layer - N_A_LAYERS
            if sample:
                q = _q_proj(h, nmw, sc_m, sh_m, w['wq'][j], w['q_norm_w'][j], w['rq'], w['rqt'], tm, F32)
                o = _attn_sample(q, k_new, v_new, cache[0], cache[1], cache[2], cache[3])
            else:
                q = _q_proj(h, nmw, sc_m, sh_m, w['wq'][j], w['q_norm_w'][j], w['rq'], w['rqt'], tm, BF16,
                            out_scale=LOG2_E)
                o = _attn_prompt(q, kd, vt, kmd)
            h = _mm_res(o, w['wo'][j], h, g_m, tm)
        h = _moe(h, w['norm_ffn_w'][layer], sc_f, sh_f, g_f, w['router_w'][layer], w['router_b'][layer],
                 w['w1'][layer], w['w3'][layer], w['w2'][layer], tm)
        if layer == N_A_LAYERS - 1:
            kv_tm = min(t, 512)
            res = _kv_proj(h, w['kv_norm_w'], kv_mod[1], kv_mod[0], w['wk'], w['wv'], w['k_norm_w'],
                           w['rk'], w['rkt'], kv_tm, with_dup=not sample)
            k_new, v_new = res[0], res[1]
            if not sample:
                kd, vt = res[2], res[3]
                kmd = res[4].reshape(bsz, t // MOBA_BLOCK, N_KV_HEADS * LANES)
                kmd = jnp.pad(kmd, ((0, 0), (0, -kmd.shape[1] % SUBLANES), (0, 0)))
    ssm_out = st_all.reshape(N_A_LAYERS, st_all.shape[1], SSM_HEADS, SSM_HEAD_DIM, D_STATE)
    return h, jnp.stack(conv_out), ssm_out, k_new, v_new


def kernel(x_prompt, x_sample, state_conv, state_ssm, cache_k, cache_v, page_table, c_prompt, c_sample, w_mod, b_mod, norm_mix_w, norm_ffn_w, ssm_w_in, ssm_conv_w, ssm_conv_b, ssm_dt_bias, ssm_a_log, ssm_d, ssm_norm_w, ssm_w_out, kv_w_mod, kv_b_mod, kv_norm_w, w_k, w_v, k_norm_w, attn_w_q, q_norm_w, attn_w_o, moe_w_grp, moe_b_grp, moe_w_exp, moe_b_exp, moe_w1, moe_w3, moe_w2):
    params = dict(ssm_w_in=ssm_w_in, ssm_conv_w=ssm_conv_w, ssm_conv_b=ssm_conv_b, ssm_dt_bias=ssm_dt_bias,
                  ssm_a_log=ssm_a_log, ssm_d=ssm_d, ssm_norm_w=ssm_norm_w, ssm_w_out=ssm_w_out,
                  norm_mix_w=norm_mix_w, norm_ffn_w=norm_ffn_w, kv_norm_w=kv_norm_w, w_k=w_k, w_v=w_v,
                  k_norm_w=k_norm_w, attn_w_q=attn_w_q, q_norm_w=q_norm_w, attn_w_o=attn_w_o,
                  moe_w_grp=moe_w_grp, moe_b_grp=moe_b_grp, moe_w_exp=moe_w_exp, moe_b_exp=moe_b_exp,
                  moe_w1=moe_w1, moe_w3=moe_w3, moe_w2=moe_w2)
    w = _prep_weights(params)

    bp, seq, d = x_prompt.shape
    n_seq, t_new, _ = x_sample.shape
    n_pages = page_table.shape[1]
    past_len = n_pages * PAGE_SIZE
    assert seq % MOBA_BLOCK == 0 and seq // MOBA_BLOCK <= LANES
    assert past_len % MOBA_BLOCK == 0 and t_new <= MOBA_BLOCK and t_new % SUBLANES == 0

    n_c = bp + n_seq
    n_c_pad = -(-n_c // SUBLANES) * SUBLANES
    c_all = jnp.pad(jnp.concatenate([c_prompt, c_sample], axis=0), ((0, n_c_pad - n_c), (0, 0)))
    mod_all = _mod_vectors(c_all, w_mod, b_mod[:, None, :], 1536)
    kv_all = _mod_vectors(c_all, kv_w_mod[None], kv_b_mod[None, None, :], 1024)[0]

    def group_mods(lo, hi, per_token_repeat):
        def shape(x):
            if per_token_repeat:
                return jnp.repeat(x, per_token_repeat, axis=0)[None]
            return x[:, None, :]
        mods = [[shape(mod_all[l, lo:hi, k * d:(k + 1) * d]) for k in range(6)] for l in range(DEPTH)]
        kvm = [shape(kv_all[lo:hi, k * d:(k + 1) * d]) for k in range(2)]
        return mods, kvm

    mods_p, kv_p = group_mods(0, bp, 0)
    y_p, conv_p, ssm_p, k_p, v_p = _trunk(x_prompt, mods_p, kv_p, w, sample=False)

    mods_s, kv_s = group_mods(bp, n_c, t_new)
    kvw = N_KV_HEADS * HEAD_DIM
    cache = (jnp.transpose(cache_k, (0, 2, 3, 1)), jnp.transpose(cache_v, (0, 2, 3, 1)), page_table, t_new)
    y_s, conv_s, ssm_s, k_s, v_s = _trunk(x_sample.reshape(1, n_seq * t_new, d), mods_s, kv_s, w,
                                          sample=True, conv0=state_conv, ssm0=state_ssm, cache=cache)

    heads = lambda x, b, t: x.reshape(b, t, N_KV_HEADS, HEAD_DIM)
    return (y_p, y_s.reshape(n_seq, t_new, d), conv_p, ssm_p, heads(k_p, bp, seq), heads(v_p, bp, seq),
            conv_s, ssm_s, heads(k_s, n_seq, t_new), heads(v_s, n_seq, t_new))
```

```python
import functools

import jax
import jax.numpy as jnp
from jax import lax
from jax.experimental import pallas as pl
from jax.experimental.pallas import tpu as pltpu

F32 = jnp.float32
BF16 = jnp.bfloat16
HIGHEST = lax.Precision.HIGHEST

D_MODEL = 1024
DEPTH = 4
N_A_LAYERS = 2
D_INNER = 2048
SSM_HEADS = 32
SSM_HEAD_DIM = 64
SSM_GROUPS = 4
D_STATE = 128
CONV_WIDTH = 4
CONV_DIM = D_INNER + 2 * SSM_GROUPS * D_STATE
SSD_CHUNK = 128
N_HEADS = 16
HEAD_DIM = 64
N_KV_HEADS = 8
MOBA_BLOCK = 256
MOBA_TOPK = 3
N_EXPERT_GROUPS = 4
EXPERTS_PER_GROUP = 4
N_EXPERTS = 16
D_EXPERT = 256
PAGE_SIZE = 128
EPS = 1e-6

LANES = 128
SUBLANES = 8
BF16_ROWS = 16
MIB = 1024 * 1024
NEG_BIG = -1e30

NT_DIMS = (((1,), (1,)), ((), ()))


def _cparams(sem, vmem_mib):
    return pltpu.CompilerParams(dimension_semantics=sem, vmem_limit_bytes=vmem_mib * MIB)


def _sigmoid(x):
    return 1.0 / (1.0 + jnp.exp(-x))


def _silu(x):
    return x * _sigmoid(x)


def _norm_mod(h, nw, sc, sh):
    ms = jnp.mean(h * h, axis=-1, keepdims=True)
    return (h * lax.rsqrt(ms + EPS)) * nw * (1.0 + sc) + sh


def _split_bf16(x):
    hi = x.astype(BF16)
    lo = (x - hi.astype(F32)).astype(BF16)
    return hi, lo


def _head_rmsnorm(x, r, rt):
    sq = x * x
    hi, lo = _split_bf16(sq)
    ss = jnp.dot(hi, r, preferred_element_type=F32) + jnp.dot(lo, r, preferred_element_type=F32)
    inv = lax.rsqrt(ss * (1.0 / HEAD_DIM) + EPS)
    ih, il = _split_bf16(inv)
    invx = jnp.dot(ih, rt, preferred_element_type=F32) + jnp.dot(il, rt, preferred_element_type=F32)
    return x * invx


def _mod_kernel(c_ref, w_ref, b_ref, o_ref):
    ca = _silu(c_ref[...])
    o_ref[0] = jnp.dot(ca, w_ref[0], precision=HIGHEST, preferred_element_type=F32) + b_ref[0]


def _mod_vectors(c_all, w, b, tn):
    n_layers, d, n = w.shape
    m = c_all.shape[0]
    return pl.pallas_call(
        _mod_kernel,
        grid=(n_layers, n // tn),
        in_specs=[pl.BlockSpec((m, d), lambda l, j: (0, 0)),
                  pl.BlockSpec((1, d, tn), lambda l, j: (l, 0, j)),
                  pl.BlockSpec((1, 1, tn), lambda l, j: (l, 0, j))],
        out_specs=pl.BlockSpec((1, m, tn), lambda l, j: (l, 0, j)),
        out_shape=jax.ShapeDtypeStruct((n_layers, m, n), F32),
        compiler_params=_cparams(("arbitrary", "arbitrary"), 40),
        name="mod_vectors",
    )(c_all, w, b)


def _mod_spec(mod, tm):
    d = mod.shape[-1]
    if mod.shape[1] == 1:
        return pl.BlockSpec((1, 1, d), lambda b, i, *_: (b, 0, 0))
    return pl.BlockSpec((1, tm, d), lambda b, i, *_: (b, i, 0))


def _nm_matmul_kernel(h_ref, nw_ref, sc_ref, sh_ref, w_ref, o_ref, u_ref):
    @pl.when(pl.program_id(2) == 0)
    def _():
        u_ref[...] = _norm_mod(h_ref[0], nw_ref[...], sc_ref[0], sh_ref[0]).astype(BF16)

    o_ref[0] = jnp.dot(u_ref[...], w_ref[...], preferred_element_type=F32)


def _nm_matmul(h, nw, sc, sh, w, tm, tn):
    bsz, t, d = h.shape
    n = w.shape[1]
    return pl.pallas_call(
        _nm_matmul_kernel,
        grid=(bsz, t // tm, n // tn),
        in_specs=[pl.BlockSpec((1, tm, d), lambda b, i, j: (b, i, 0)),
                  pl.BlockSpec((1, d), lambda b, i, j: (0, 0)),
                  _mod_spec(sc, tm), _mod_spec(sh, tm),
                  pl.BlockSpec((d, tn), lambda b, i, j: (0, j))],
        out_specs=pl.BlockSpec((1, tm, tn), lambda b, i, j: (b, i, j)),
        out_shape=jax.ShapeDtypeStruct((bsz, t, n), F32),
        scratch_shapes=[pltpu.VMEM((tm, d), BF16)],
        compiler_params=_cparams(("parallel", "parallel", "arbitrary"), 48),
        name="norm_mod_matmul",
    )(h, nw, sc, sh, w)


def _mm_res_kernel(a_ref, w_ref, h_ref, g_ref, o_ref):
    acc = jnp.dot(a_ref[0].astype(BF16), w_ref[...], preferred_element_type=F32)
    o_ref[0] = h_ref[0] + g_ref[0] * acc


def _mm_res(a, w, h, gate, tm):
    bsz, t, k = a.shape
    d = w.shape[1]
    return pl.pallas_call(
        _mm_res_kernel,
        grid=(bsz, t // tm),
        in_specs=[pl.BlockSpec((1, tm, k), lambda b, i: (b, i, 0)),
                  pl.BlockSpec((k, d), lambda b, i: (0, 0)),
                  pl.BlockSpec((1, tm, d), lambda b, i: (b, i, 0)),
                  _mod_spec(gate, tm)],
        out_specs=pl.BlockSpec((1, tm, d), lambda b, i: (b, i, 0)),
        out_shape=jax.ShapeDtypeStruct((bsz, t, d), F32),
        compiler_params=_cparams(("parallel", "parallel"), 48),
        name="matmul_residual",
    )(a, w, h, gate)


HALO = SUBLANES
XC_TILE = 512


def _ssd_kernel(z_ref, xbc_ref, dt_ref, cs_ref, h0_ref, cw_ref, cb_ref, hp_ref, dsk_ref, nw_ref,
                *rest, valid, has_init, has_prev):
    g_ref, cso_ref, st_ref, xs_ref, xc_ref, xt_ref, y_ref = rest[1:] if has_prev else rest
    L = SSD_CHUNK
    c = pl.program_id(1)

    @pl.when(c == 0)
    def _init():
        xs_ref[0:HALO, :] = cs_ref[0]
        if valid < L:
            xc_ref[...] = jnp.zeros_like(xc_ref)
        if has_init:
            st_ref[0] = h0_ref[0]
        else:
            st_ref[0] = jnp.zeros(st_ref.shape[1:], F32)

    xs_ref[HALO:HALO + valid, :] = xbc_ref[0]

    for j in range(CONV_DIM // XC_TILE):
        sl = slice(j * XC_TILE, (j + 1) * XC_TILE)
        x_rows = xs_ref[0:HALO + valid, sl]
        acc = x_rows * cw_ref[0:1, sl]
        for k in range(1, CONV_WIDTH):
            acc = pltpu.roll(acc, 1, 0) + x_rows * cw_ref[k:k + 1, sl]
        xc_ref[0:valid, sl] = _silu(acc[HALO:HALO + valid] + cb_ref[:, sl])

    cso_ref[0] = xs_ref[valid:valid + HALO, :]
    xs_ref[0:HALO, :] = xs_ref[valid:valid + HALO, :]

    dtr = dt_ref[0]
    if valid < L:
        dtr = jnp.concatenate([dtr, jnp.zeros((L - valid, LANES), F32)], axis=0)
    xx = dtr + hp_ref[0:1, :]
    dt = jnp.maximum(xx, 0.0) + jnp.log(1.0 + jnp.exp(-jnp.abs(xx)))
    row = lax.broadcasted_iota(jnp.int32, (L, L), 0)
    col = lax.broadcasted_iota(jnp.int32, (L, L), 1)
    if valid < L:
        dt = jnp.where(row < valid, dt, 0.0)
    a = dt * (-jnp.exp(hp_ref[1:2, :]))
    causal = row >= col
    a_cum = jnp.dot(causal.astype(F32), a, precision=HIGHEST, preferred_element_type=F32)
    a_cum_t = a_cum.T
    dt_t = dt.T
    sub_lo = row < HEAD_DIM
    R = L if valid == L else -(-valid // BF16_ROWS) * BF16_ROWS
    row_r = lax.broadcasted_iota(jnp.int32, (R, L), 0)
    col_r = lax.broadcasted_iota(jnp.int32, (R, L), 1)
    causal_r = row_r >= col_r
    lane_lo = col_r < HEAD_DIM

    for j in range(D_INNER // LANES):
        xt_ref[j * LANES:(j + 1) * LANES, :] = xc_ref[:, j * LANES:(j + 1) * LANES].T

    pairs_per_group = SSM_HEADS // SSM_GROUPS // 2
    grp_w = D_INNER // SSM_GROUPS
    for g in range(SSM_GROUPS):
        b_g = xc_ref[:, D_INNER + g * D_STATE:D_INNER + (g + 1) * D_STATE].astype(BF16)
        c_g = xc_ref[:, D_INNER + (SSM_GROUPS + g) * D_STATE:
                     D_INNER + (SSM_GROUPS + g + 1) * D_STATE].astype(BF16)
        cb = lax.dot_general(c_g[0:R], b_g, NT_DIMS, preferred_element_type=F32)
        st_g = st_ref[0, g * grp_w:(g + 1) * grp_w, :]
        y_off = lax.dot_general(c_g[0:R], st_g.astype(BF16), NT_DIMS, preferred_element_type=F32)
        xw_parts, dec_parts = [], []
        for jj in range(pairs_per_group):
            pair = g * pairs_per_group + jj
            s_mats, e_cols, w_rows, d_end = [], [], [], []
            for h in (2 * pair, 2 * pair + 1):
                colb = jnp.broadcast_to(a_cum[0:R, h:h + 1], (R, L))
                rowb = a_cum_t[h:h + 1, :]
                dec = jnp.where(causal_r, jnp.exp(colb - rowb), 0.0)
                s_mats.append((cb * dec * dt_t[h:h + 1, :]).astype(BF16))
                e_cols.append(jnp.exp(colb))
                a_last = a_cum_t[h:h + 1, L - 1:L]
                w_rows.append(dt_t[h:h + 1, :] * jnp.exp(a_last - rowb))
                d_end.append(jnp.broadcast_to(jnp.exp(a_last), (LANES, D_STATE)))
            psl = slice(pair * LANES, (pair + 1) * LANES)
            xp = xc_ref[:, psl]
            xpb = xp.astype(BF16)
            y_d = jnp.where(lane_lo,
                            jnp.dot(s_mats[0], xpb, preferred_element_type=F32),
                            jnp.dot(s_mats[1], xpb, preferred_element_type=F32))
            y_o = y_off[:, jj * LANES:(jj + 1) * LANES] * jnp.where(lane_lo, e_cols[0], e_cols[1])
            y_ref[0:R, psl] = y_d + y_o + dsk_ref[:, psl] * xp[0:R]
            xw_parts.append((xt_ref[psl, :] * jnp.where(sub_lo, w_rows[0], w_rows[1])).astype(BF16))
            dec_parts.append(jnp.where(sub_lo, d_end[0], d_end[1]))
        xw_g = jnp.concatenate(xw_parts, axis=0)
        new = jnp.dot(xw_g, b_g, preferred_element_type=F32)
        st_ref[0, g * grp_w:(g + 1) * grp_w, :] = st_g * jnp.concatenate(dec_parts, axis=0) + new

    zz = z_ref[0]
    gg = y_ref[0:valid, :] * _silu(zz)
    for g in range(SSM_GROUPS):
        sl = slice(g * grp_w, (g + 1) * grp_w)
        seg = gg[:, sl]
        ms = jnp.mean(seg * seg, axis=-1, keepdims=True)
        g_ref[0, :, sl] = (seg * lax.rsqrt(ms + EPS) * nw_ref[:, sl]).astype(g_ref.dtype)


def _ssd(z, xbc, dtr, conv_in, h0_all, st_all, layer, conv_w, conv_b, head_params, d_skip, norm_w, g_dtype):
    bsz, t, _ = z.shape
    L = SSD_CHUNK
    valid = min(L, t)
    assert t % valid == 0 and valid % SUBLANES == 0
    nc = t // valid
    has_init = h0_all is not None
    state_spec = pl.BlockSpec((None, 1, D_INNER, D_STATE), lambda b, c: (layer, b, 0, 0))
    if h0_all is None:
        h0_all = jnp.zeros((1, SUBLANES, D_STATE), F32)
        h0_spec = pl.BlockSpec((1, SUBLANES, D_STATE), lambda b, c: (0, 0, 0))
    else:
        h0_spec = state_spec
    full = lambda shape: pl.BlockSpec(shape, lambda b, c: (0,) * len(shape))
    operands = [z, xbc, dtr, conv_in, h0_all, conv_w, conv_b, head_params, d_skip, norm_w]
    in_specs = [pl.BlockSpec((1, valid, D_INNER), lambda b, c: (b, c, 0)),
                pl.BlockSpec((1, valid, CONV_DIM), lambda b, c: (b, c, 0)),
                pl.BlockSpec((1, valid, LANES), lambda b, c: (b, c, 0)),
                pl.BlockSpec((1, HALO, CONV_DIM), lambda b, c: (b, 0, 0)),
                h0_spec,
                full((CONV_WIDTH, CONV_DIM)), full((1, CONV_DIM)), full((SUBLANES, LANES)),
                full((1, D_INNER)), full((1, D_INNER))]
    aliases = {}
    if st_all is not None:
        aliases = {len(operands): 2}
        operands.append(st_all)
        in_specs.append(pl.BlockSpec(memory_space=pl.ANY))
    return pl.pallas_call(
        functools.partial(_ssd_kernel, valid=valid, has_init=has_init, has_prev=st_all is not None),
        grid=(bsz, nc),
        in_specs=in_specs,
        out_specs=[pl.BlockSpec((1, valid, D_INNER), lambda b, c: (b, c, 0)),
                   pl.BlockSpec((1, HALO, CONV_DIM), lambda b, c: (b, 0, 0)),
                   state_spec],
        out_shape=[jax.ShapeDtypeStruct((bsz, t, D_INNER), g_dtype),
                   jax.ShapeDtypeStruct((bsz, HALO, CONV_DIM), F32),
                   jax.ShapeDtypeStruct((N_A_LAYERS, bsz, D_INNER, D_STATE), F32)],
        input_output_aliases=aliases,
        scratch_shapes=[pltpu.VMEM((L + HALO, CONV_DIM), F32),
                        pltpu.VMEM((L, CONV_DIM), F32),
                        pltpu.VMEM((D_INNER, L), F32),
                        pltpu.VMEM((L, D_INNER), F32)],
        compiler_params=_cparams(("parallel", "arbitrary"), 48),
        name="ssd_chunk_scan",
    )(*operands)


ROUTER_GRP0 = 0
ROUTER_EXP0 = N_EXPERT_GROUPS


def _route(logits):
    lane = lax.broadcasted_iota(jnp.int32, logits.shape, 1)
    big = jnp.int32(2 * LANES)
    is_grp = lane < N_EXPERT_GROUPS
    gl = jnp.where(is_grp, logits, -jnp.inf)
    gmax = jnp.max(gl, axis=-1, keepdims=True)
    gidx = jnp.min(jnp.where(gl == gmax, lane, big), axis=-1, keepdims=True)
    p_grp = 1.0 / jnp.sum(jnp.where(is_grp, jnp.exp(gl - gmax), 0.0), axis=-1, keepdims=True)
    e_rel = lane - ROUTER_EXP0
    in_grp = (e_rel >= gidx * EXPERTS_PER_GROUP) & (e_rel < (gidx + 1) * EXPERTS_PER_GROUP)
    el = jnp.where(in_grp, logits, -jnp.inf)
    m1 = jnp.max(el, axis=-1, keepdims=True)
    i1 = jnp.min(jnp.where(el == m1, lane, big), axis=-1, keepdims=True)
    el2 = jnp.where(lane == i1, -jnp.inf, el)
    m2 = jnp.max(el2, axis=-1, keepdims=True)
    i2 = jnp.min(jnp.where(el2 == m2, lane, big), axis=-1, keepdims=True)
    e2 = jnp.exp(m2 - m1)
    den = 1.0 + e2
    w1 = (1.0 / den) * p_grp
    w2 = (e2 / den) * p_grp
    return jnp.where(lane == i1, w1, jnp.where(lane == i2, w2, 0.0))


def _moe_kernel(h_ref, nw_ref, sc_ref, sh_ref, gf_ref, wr_ref, br_ref, w1_ref, w3_ref, w2_ref,
                o_ref, u_ref, gates_ref, acc_ref):
    step = pl.program_id(2)
    per_step = w1_ref.shape[0]

    @pl.when(step == 0)
    def _():
        u = _norm_mod(h_ref[0], nw_ref[...], sc_ref[0], sh_ref[0])
        logits = jnp.dot(u, wr_ref[...], precision=HIGHEST, preferred_element_type=F32) + br_ref[...]
        gates_ref[...] = _route(logits)
        u_ref[...] = u.astype(BF16)
        acc_ref[...] = jnp.zeros_like(acc_ref)

    u = u_ref[...]
    gates = gates_ref[...]
    lane = lax.broadcasted_iota(jnp.int32, gates.shape, 1)
    acts = []
    for j in range(per_step):
        a = (_silu(jnp.dot(u, w1_ref[j], preferred_element_type=F32))
             * jnp.dot(u, w3_ref[j], preferred_element_type=F32))
        e = step * per_step + j
        gcol = jnp.sum(jnp.where(lane == e + ROUTER_EXP0, gates, 0.0), axis=-1, keepdims=True)
        acts.append((a * gcol).astype(BF16))
    w2 = w2_ref[...].reshape(per_step * w2_ref.shape[1], w2_ref.shape[2])
    acc_ref[...] += jnp.dot(jnp.concatenate(acts, axis=1), w2, preferred_element_type=F32)

    @pl.when(step == pl.num_programs(2) - 1)
    def _():
        o_ref[0] = h_ref[0] + gf_ref[0] * acc_ref[...]


def _moe(h, nw, sc, sh, gf, wr, br, w1, w3, w2, tm):
    bsz, t, d = h.shape
    n_e, _, f = w1.shape
    per_step = EXPERTS_PER_GROUP
    return pl.pallas_call(
        _moe_kernel,
        grid=(bsz, t // tm, n_e // per_step),
        in_specs=[pl.BlockSpec((1, tm, d), lambda b, i, e: (b, i, 0)),
                  pl.BlockSpec((1, d), lambda b, i, e: (0, 0)),
                  _mod_spec(sc, tm), _mod_spec(sh, tm), _mod_spec(gf, tm),
                  pl.BlockSpec((d, LANES), lambda b, i, e: (0, 0)),
                  pl.BlockSpec((1, LANES), lambda b, i, e: (0, 0)),
                  pl.BlockSpec((per_step, d, f), lambda b, i, e: (e, 0, 0)),
                  pl.BlockSpec((per_step, d, f), lambda b, i, e: (e, 0, 0)),
                  pl.BlockSpec((per_step, f, d), lambda b, i, e: (e, 0, 0))],
        out_specs=pl.BlockSpec((1, tm, d), lambda b, i, e: (b, i, 0)),
        out_shape=jax.ShapeDtypeStruct((bsz, t, d), F32),
        scratch_shapes=[pltpu.VMEM((tm, d), BF16), pltpu.VMEM((tm, LANES), F32), pltpu.VMEM((tm, d), F32)],
        compiler_params=_cparams(("parallel", "parallel", "arbitrary"), 48),
        name="hmoe",
    )(h, nw, sc, sh, gf, wr, br, w1, w3, w2)


N_SPLIT = 3
FEAT_POS_LO = HEAD_DIM
FEAT_POS_HI = FEAT_POS_LO + N_SPLIT
FEAT_ONE = FEAT_POS_HI + N_SPLIT


def _split3(x):
    h1 = x.astype(BF16).astype(F32)
    r = x - h1
    h2 = r.astype(BF16).astype(F32)
    h3 = (r - h2).astype(BF16).astype(F32)
    return h1, h2, h3


def _feature_lanes(lane, base, parts, other):
    out = other
    for k, part in enumerate(parts):
        out = jnp.where(lane == base + k, part, out)
    return out


def _key_tiles(x, feat):
    lane = lax.broadcasted_iota(jnp.int32, x.shape, 1)
    rolled = pltpu.roll(x, HEAD_DIM, 1)
    lo = lane < HEAD_DIM
    return jnp.where(lo, x, feat), jnp.where(lo, rolled, feat)


def _kv_kernel(h_ref, nw_ref, sc_ref, sh_ref, wk_ref, wv_ref, knw_ref, r_ref, rt_ref,
               kf_ref, vf_ref, *dup_refs, tm, with_dup):
    u = _norm_mod(h_ref[0], nw_ref[...], sc_ref[0], sh_ref[0]).astype(BF16)
    k = jnp.dot(u, wk_ref[...], preferred_element_type=F32)
    v = jnp.dot(u, wv_ref[...], preferred_element_type=F32)
    kn = _head_rmsnorm(k, r_ref[...], rt_ref[...]) * knw_ref[...]
    if not with_dup:
        kf_ref[0] = kn
        vf_ref[0] = v
    else:
        for hh in range(N_KV_HEADS):
            rows = pl.ds(hh, tm, stride=N_KV_HEADS)
            for src, dst in ((kn, kf_ref), (v, vf_ref)):
                tile = src[:, (hh // 2) * LANES:(hh // 2 + 1) * LANES]
                if hh % 2 == 1:
                    tile = pltpu.roll(tile, HEAD_DIM, 1)
                dst[0, rows, :] = tile[:, :HEAD_DIM]
        kd_ref, vt_ref, km_ref = dup_refs
        lane = lax.broadcasted_iota(jnp.int32, (tm, LANES), 1)
        pos = pl.program_id(1) * tm + lax.broadcasted_iota(jnp.int32, (tm, LANES), 0)
        pos_lo = pos % MOBA_BLOCK
        feat = jnp.zeros((tm, LANES), F32)
        feat = _feature_lanes(lane, FEAT_POS_LO, [pos_lo.astype(F32)] * N_SPLIT, feat)
        feat = _feature_lanes(lane, FEAT_POS_HI, [(pos - pos_lo).astype(F32)] * N_SPLIT, feat)
        feat = _feature_lanes(lane, FEAT_ONE, [jnp.ones((tm, LANES), F32)] * N_SPLIT, feat)
        lane_row = lax.broadcasted_iota(jnp.int32, (1, LANES), 1)
        for cidx in range(N_KV_HEADS // 2):
            d0, d1 = _key_tiles(kn[:, cidx * LANES:(cidx + 1) * LANES], feat)
            for hh, dd in ((2 * cidx, d0), (2 * cidx + 1, d1)):
                hsl = slice(hh * LANES, (hh + 1) * LANES)
                kd_ref[0, :, hsl] = dd.astype(BF16)
                for blk in range(tm // MOBA_BLOCK):
                    mean = jnp.mean(dd[blk * MOBA_BLOCK:(blk + 1) * MOBA_BLOCK], axis=0, keepdims=True)
                    km_ref[0, blk, :, hsl] = jnp.where(lane_row < HEAD_DIM, mean, 0.0)
        for blk in range(tm // MOBA_BLOCK):
            vt_ref[0, blk] = v[blk * MOBA_BLOCK:(blk + 1) * MOBA_BLOCK].T.astype(BF16)


def _kv_proj(h, nw, sc, sh, wk, wv, knw, r, rt, tm, with_dup):
    bsz, t, d = h.shape
    kvw = wk.shape[1]
    const = lambda shape: pl.BlockSpec(shape, lambda b, i: (0,) * len(shape))
    if not with_dup:
        out_specs = [pl.BlockSpec((1, tm, kvw), lambda b, i: (b, i, 0))] * 2
        out_shape = [jax.ShapeDtypeStruct((bsz, t, kvw), F32)] * 2
    else:
        out_specs = [pl.BlockSpec((1, tm * N_KV_HEADS, HEAD_DIM), lambda b, i: (b, i, 0))] * 2
        out_shape = [jax.ShapeDtypeStruct((bsz, t * N_KV_HEADS, HEAD_DIM), F32)] * 2
        dupw = N_KV_HEADS * LANES
        nblk = tm // MOBA_BLOCK
        out_specs += [pl.BlockSpec((1, tm, dupw), lambda b, i: (b, i, 0)),
                      pl.BlockSpec((1, nblk, kvw, MOBA_BLOCK), lambda b, i: (b, i, 0, 0)),
                      pl.BlockSpec((1, nblk, 1, dupw), lambda b, i: (b, i, 0, 0))]
        out_shape += [jax.ShapeDtypeStruct((bsz, t, dupw), BF16),
                      jax.ShapeDtypeStruct((bsz, t // MOBA_BLOCK, kvw, MOBA_BLOCK), BF16),
                      jax.ShapeDtypeStruct((bsz, t // MOBA_BLOCK, 1, dupw), F32)]
    return pl.pallas_call(
        functools.partial(_kv_kernel, tm=tm, with_dup=with_dup),
        grid=(bsz, t // tm),
        in_specs=[pl.BlockSpec((1, tm, d), lambda b, i: (b, i, 0)),
                  const((1, d)), _mod_spec(sc, tm), _mod_spec(sh, tm),
                  const((d, kvw)), const((d, kvw)), const((1, kvw)),
                  const((kvw, LANES)), const((LANES, kvw))],
        out_specs=out_specs,
        out_shape=out_shape,
        compiler_params=_cparams(("parallel", "parallel"), 48),
        name="shared_kv",
    )(h, nw, sc, sh, wk, wv, knw, r, rt)


def _q_kernel(h_ref, nw_ref, sc_ref, sh_ref, wq_ref, qnw_ref, r_ref, rt_ref, q_ref, *, out_scale):
    u = _norm_mod(h_ref[0], nw_ref[...], sc_ref[0], sh_ref[0]).astype(BF16)
    q = jnp.dot(u, wq_ref[...], preferred_element_type=F32)
    q = _head_rmsnorm(q, r_ref[...], rt_ref[...]) * qnw_ref[...]
    if out_scale != 1.0:
        q = q * out_scale
    q_ref[0] = q.astype(q_ref.dtype)


def _q_proj(h, nw, sc, sh, wq, qnw, r, rt, tm, out_dtype, out_scale=1.0):
    bsz, t, d = h.shape
    const = lambda shape: pl.BlockSpec(shape, lambda b, i: (0,) * len(shape))
    return pl.pallas_call(
        functools.partial(_q_kernel, out_scale=out_scale),
        grid=(bsz, t // tm),
        in_specs=[pl.BlockSpec((1, tm, d), lambda b, i: (b, i, 0)),
                  const((1, d)), _mod_spec(sc, tm), _mod_spec(sh, tm),
                  const((d, d)), const((1, d)), const((d, LANES)), const((LANES, d))],
        out_specs=pl.BlockSpec((1, tm, d), lambda b, i: (b, i, 0)),
        out_shape=jax.ShapeDtypeStruct((bsz, t, d), out_dtype),
        compiler_params=_cparams(("parallel", "parallel"), 48),
        name="q_proj",
    )(h, nw, sc, sh, wq, qnw, r, rt)


def _top_blocks(gate, n_valid, axis):
    pos = lax.broadcasted_iota(jnp.int32, gate.shape, axis)
    gm = jnp.where(pos < n_valid, gate, -jnp.inf)
    sel = jnp.zeros(gate.shape, F32)
    for _ in range(MOBA_TOPK):
        mx = jnp.max(gm, axis=axis, keepdims=True)
        cand = (gm == mx) & (mx > -jnp.inf)
        idx = jnp.min(jnp.where(cand, pos, jnp.int32(2 * LANES)), axis=axis, keepdims=True)
        pick = pos == idx
        sel = jnp.where(pick, 1.0, sel)
        gm = jnp.where(pick, -jnp.inf, gm)
    return sel


def _alibi_slope(head):
    return jnp.exp2(-8.0 * (head + 1).astype(F32) / N_HEADS)


ATT_GROUP = 2
ACC_ROWS = HEAD_DIM + BF16_ROWS
LOG2_E = 1.4426950408889634


def _attn_prompt_kernel(q_ref, k_ref, vt_ref, km_ref, o_ref,
                        qt_ref, sa_ref, sb_ref, sel_ref, m_ref, acc_ref):
    g = pl.program_id(1)
    i = pl.program_id(2)
    blk = MOBA_BLOCK
    cols = 2 * blk
    grp = ATT_GROUP

    qt = q_ref[0].astype(F32)
    lane = lax.broadcasted_iota(jnp.int32, (blk, LANES), 1)
    q2 = jnp.concatenate([jnp.where(lane < HEAD_DIM, qt, 0.0),
                          jnp.where(lane < HEAD_DIM, pltpu.roll(qt, HEAD_DIM, 1), 0.0)], axis=0)
    r2 = lax.broadcasted_iota(jnp.int32, (cols, LANES), 0)
    l2 = lax.broadcasted_iota(jnp.int32, (cols, LANES), 1)
    second = r2 >= blk
    slope = _alibi_slope(2 * g + second.astype(jnp.int32)) * LOG2_E
    pos_q = jnp.where(second, r2 - blk, r2) + i * blk
    slope_parts = _split3(slope)
    q2 = _feature_lanes(l2, FEAT_POS_LO, slope_parts, q2)
    q2 = _feature_lanes(l2, FEAT_POS_HI, slope_parts, q2)
    q2 = _feature_lanes(l2, FEAT_ONE, _split3(-slope * pos_q.astype(F32)), q2)
    q2t = q2.T
    gate = jnp.dot(km_ref[0], q2t, precision=HIGHEST, preferred_element_type=F32)
    sel_ref[...] = _top_blocks(gate, i, 0)
    is_q = lax.broadcasted_iota(jnp.int32, (LANES, 1), 0) < HEAD_DIM
    qt_ref[...] = (q2t * jnp.where(is_q, HEAD_DIM ** -0.5, 1.0)).astype(BF16)

    def scores(n):
        start = pl.multiple_of(n * blk, blk)
        return jnp.dot(k_ref[0, pl.ds(start, blk), :], qt_ref[...], preferred_element_type=F32)

    ones_rows = jnp.ones((ACC_ROWS - HEAD_DIM, blk), BF16)

    def block_stats(s, n):
        mb = jnp.max(s, axis=0, keepdims=True)
        p = jnp.exp2((s - mb).astype(BF16))
        vt_aug = jnp.concatenate([vt_ref[0, n], ones_rows], axis=0)
        return mb, jnp.dot(vt_aug, p, preferred_element_type=F32)

    def merge(parts):
        m_old = m_ref[...]
        m_new = m_old
        for mb, _ in parts:
            m_new = jnp.maximum(m_new, mb)
        acc_new = jnp.exp2(m_old - m_new) * acc_ref[...]
        for mb, ab in parts:
            acc_new = acc_new + jnp.exp2(mb - m_new) * ab
        m_ref[...] = m_new
        acc_ref[...] = acc_new

    rr = lax.broadcasted_iota(jnp.int32, (blk, cols), 0)
    cc = lax.broadcasted_iota(jnp.int32, (blk, cols), 1)
    visible = jnp.where(cc >= blk, cc - blk, cc) >= rr
    m0, a0 = block_stats(jnp.where(visible, scores(i), -jnp.inf), i)
    m_ref[...] = m0
    acc_ref[...] = a0

    n_groups = (i + grp - 1) // grp

    def score_group(kk, s_ref):
        for b in range(grp):
            n = jnp.minimum(kk * grp + b, i - 1)
            s_ref[b * blk:(b + 1) * blk, :] = scores(n)

    def softmax_group(kk, s_ref):
        parts = []
        for b in range(grp):
            n_raw = kk * grp + b
            n = jnp.minimum(n_raw, i - 1)
            mb, ab = block_stats(s_ref[b * blk:(b + 1) * blk, :], n)
            counted = sel_ref[pl.ds(n, 1), :] * (n_raw < i).astype(F32) > 0.0
            parts.append((jnp.where(counted, mb, NEG_BIG), ab))
        merge(parts)

    @pl.when(n_groups > 0)
    def _first():
        score_group(0, sa_ref)

    def step(kk, carry):
        score_group(2 * kk + 1, sb_ref)
        softmax_group(2 * kk, sa_ref)
        score_group(2 * kk + 2, sa_ref)
        softmax_group(2 * kk + 1, sb_ref)
        return carry

    lax.fori_loop(0, n_groups // 2, step, 0)

    @pl.when(n_groups % 2 == 1)
    def _last():
        softmax_group(n_groups - 1, sa_ref)

    o = acc_ref[0:HEAD_DIM, :] / acc_ref[HEAD_DIM:HEAD_DIM + 1, :]
    o_ref[0] = jnp.concatenate([o[:, 0:blk], o[:, blk:cols]], axis=0).T.astype(o_ref.dtype)


def _attn_prompt(q, kd, vt, kmd):
    bsz, t, d = q.shape
    blk = MOBA_BLOCK
    nb = t // blk
    nb_pad = kmd.shape[1]
    cols = 2 * blk
    return pl.pallas_call(
        _attn_prompt_kernel,
        grid=(bsz, N_KV_HEADS, nb),
        in_specs=[pl.BlockSpec((1, blk, LANES), lambda b, g, i: (b, i, g)),
                  pl.BlockSpec((1, t, LANES), lambda b, g, i: (b, 0, g)),
                  pl.BlockSpec((1, nb, HEAD_DIM, blk), lambda b, g, i: (b, 0, g, 0)),
                  pl.BlockSpec((1, nb_pad, LANES), lambda b, g, i: (b, 0, g))],
        out_specs=pl.BlockSpec((1, blk, LANES), lambda b, g, i: (b, i, g)),
        out_shape=jax.ShapeDtypeStruct((bsz, t, d), BF16),
        scratch_shapes=[pltpu.VMEM((LANES, cols), BF16),
                        pltpu.VMEM((ATT_GROUP * blk, cols), F32), pltpu.VMEM((ATT_GROUP * blk, cols), F32),
                        pltpu.VMEM((nb_pad, cols), F32), pltpu.VMEM((1, cols), F32),
                        pltpu.VMEM((ACC_ROWS, cols), F32)],
        compiler_params=_cparams(("parallel", "parallel", "arbitrary"), 48),
        name="moba_prompt",
    )(q, kd, vt, kmd)


def _attn_sample_kernel(pt_ref, q_ref, kn_ref, vn_ref, *rest, n_pages):
    del pt_ref
    k_refs = rest[:n_pages]
    v_refs = rest[n_pages:2 * n_pages]
    o_ref = rest[2 * n_pages]
    s_ref = rest[2 * n_pages + 1]
    t_new = q_ref.shape[1]
    past_len = n_pages * PAGE_SIZE
    kvw = N_KV_HEADS * HEAD_DIM
    rows = N_HEADS * t_new
    n_past_blk = past_len // MOBA_BLOCK
    pages_per_blk = MOBA_BLOCK // PAGE_SIZE
    q = q_ref[0]
    lane8 = lax.broadcasted_iota(jnp.int32, (t_new, LANES), 1)
    zeros8 = jnp.zeros((t_new, LANES), F32)
    row_blocks = []
    for h in range(N_HEADS):
        g = h // 2
        tile = q[:, g * LANES:(g + 1) * LANES]
        if h % 2 != g % 2:
            tile = pltpu.roll(tile, HEAD_DIM, 1)
        keep = (lane8 < HEAD_DIM) if g % 2 == 0 else (lane8 >= HEAD_DIM)
        tile = jnp.where(keep, tile, 0.0)
        row_blocks.append(jnp.concatenate([tile if cidx == g // 2 else zeros8 for cidx in range(kvw // LANES)], axis=1))
    qm = jnp.concatenate(row_blocks, axis=0)
    qmb = (qm * (HEAD_DIM ** -0.5)).astype(BF16)

    r1 = lax.broadcasted_iota(jnp.int32, (rows, 1), 0)
    slope = _alibi_slope(r1 // t_new)
    rr = lax.broadcasted_iota(jnp.int32, (rows, PAGE_SIZE), 0)
    cc = lax.broadcasted_iota(jnp.int32, (rows, PAGE_SIZE), 1)
    d_page0 = (past_len + rr % t_new - cc).astype(F32)

    lane_b = lax.broadcasted_iota(jnp.int32, (kvw, LANES), 1)
    kmt = jnp.zeros((kvw, LANES), F32)
    for n in range(n_past_blk):
        acc = k_refs[n * pages_per_blk][0].reshape(kvw, PAGE_SIZE)
        for pp in range(1, pages_per_blk):
            acc = acc + k_refs[n * pages_per_blk + pp][0].reshape(kvw, PAGE_SIZE)
        mean = jnp.sum(acc, axis=1, keepdims=True) * (1.0 / MOBA_BLOCK)
        kmt = jnp.where(lane_b == n, mean, kmt)
    gate = jnp.dot(qm, kmt, precision=HIGHEST, preferred_element_type=F32)
    sel = _top_blocks(gate, n_past_blk, 1)
    lane_s = lax.broadcasted_iota(jnp.int32, sel.shape, 1)

    for p in range(n_pages):
        kp = k_refs[p][0].reshape(kvw, PAGE_SIZE).astype(BF16)
        s = jnp.dot(qmb, kp, preferred_element_type=F32)
        s = s - slope * (d_page0 - float(p * PAGE_SIZE))
        picked = jnp.sum(jnp.where(lane_s == p // pages_per_blk, sel, 0.0), axis=-1, keepdims=True) > 0.0
        s_ref[:, p * PAGE_SIZE:(p + 1) * PAGE_SIZE] = jnp.where(picked, s, -jnp.inf)
    pad = jnp.zeros((LANES - t_new, kvw), F32)
    k_new = jnp.concatenate([kn_ref[0], pad], axis=0).astype(BF16)
    v_new = jnp.concatenate([vn_ref[0], pad], axis=0).astype(BF16)
    rr_own = lax.broadcasted_iota(jnp.int32, (rows, LANES), 0)
    cc_own = lax.broadcasted_iota(jnp.int32, (rows, LANES), 1)
    d_own = (rr_own % t_new - cc_own).astype(F32)
    s = lax.dot_general(qmb, k_new, NT_DIMS, preferred_element_type=F32) - slope * d_own
    s_ref[:, past_len:] = jnp.where((d_own >= 0.0) & (cc_own < t_new), s, -jnp.inf)

    s_all = s_ref[...]
    m = jnp.max(s_all, axis=-1, keepdims=True)
    p_all = jnp.exp(s_all - m)
    l = jnp.sum(p_all, axis=-1, keepdims=True)
    pb = p_all.astype(BF16)
    out = jnp.dot(pb[:, past_len:], v_new, preferred_element_type=F32)
    for p in range(n_pages):
        vp = v_refs[p][0].reshape(kvw, PAGE_SIZE).astype(BF16)
        out = out + lax.dot_general(pb[:, p * PAGE_SIZE:(p + 1) * PAGE_SIZE], vp, NT_DIMS,
                                    preferred_element_type=F32)
    out = out / l

    lo8 = lane8 < HEAD_DIM
    for cidx in range(N_HEADS // 2):
        src = slice((cidx // 2) * LANES, (cidx // 2 + 1) * LANES)
        a = out[(2 * cidx) * t_new:(2 * cidx + 1) * t_new, src]
        b = out[(2 * cidx + 1) * t_new:(2 * cidx + 2) * t_new, src]
        if cidx % 2 == 1:
            a = pltpu.roll(a, HEAD_DIM, 1)
        else:
            b = pltpu.roll(b, HEAD_DIM, 1)
        o_ref[0, :, cidx * LANES:(cidx + 1) * LANES] = jnp.where(lo8, a, b)


def _attn_sample(q, k_new, v_new, cache_kt, cache_vt, page_table, t_new):
    n_seq, n_pages = page_table.shape
    d = q.shape[-1]
    kvw = k_new.shape[-1]
    rows = N_HEADS * t_new
    page_spec = lambda p: pl.BlockSpec((1, N_KV_HEADS, HEAD_DIM, PAGE_SIZE), lambda s, pt: (pt[s, p], 0, 0, 0))
    grid_spec = pltpu.PrefetchScalarGridSpec(
        num_scalar_prefetch=1,
        grid=(n_seq,),
        in_specs=[pl.BlockSpec((1, t_new, d), lambda s, pt: (0, s, 0)),
                  pl.BlockSpec((1, t_new, kvw), lambda s, pt: (0, s, 0)),
                  pl.BlockSpec((1, t_new, kvw), lambda s, pt: (0, s, 0))]
        + [page_spec(p) for p in range(n_pages)] + [page_spec(p) for p in range(n_pages)],
        out_specs=pl.BlockSpec((1, t_new, d), lambda s, pt: (0, s, 0)),
        scratch_shapes=[pltpu.VMEM((rows, n_pages * PAGE_SIZE + LANES), F32)],
    )
    return pl.pallas_call(
        functools.partial(_attn_sample_kernel, n_pages=n_pages),
        grid_spec=grid_spec,
        out_shape=jax.ShapeDtypeStruct(q.shape, F32),
        compiler_params=_cparams(("arbitrary",), 48),
        name="moba_sample",
    )(page_table, q, k_new, v_new, *([cache_kt] * n_pages), *([cache_vt] * n_pages))


def _head_sum_matrices(n_heads):
    w = n_heads * HEAD_DIM
    head_of_lane = jnp.arange(w) // HEAD_DIM
    r = (head_of_lane[:, None] == jnp.arange(LANES)[None, :]).astype(BF16)
    return r, r.T


def _prep_weights(p):
    bf = lambda x: x.astype(BF16)
    w_in = p['ssm_w_in']
    wdt = jnp.pad(w_in[:, :, D_INNER + CONV_DIM:], ((0, 0), (0, 0), (0, LANES - SSM_HEADS)))
    pad_heads = lambda x: jnp.pad(x, ((0, 0), (0, LANES - SSM_HEADS)))
    head_params = jnp.stack([pad_heads(p['ssm_dt_bias']), pad_heads(p['ssm_a_log'])], axis=1)
    head_params = jnp.pad(head_params, ((0, 0), (0, SUBLANES - 2), (0, 0)))
    router_w = jnp.concatenate([p['moe_w_grp'], p['moe_w_exp']], axis=-1)
    router_w = jnp.pad(router_w, ((0, 0), (0, 0), (0, LANES - router_w.shape[-1])))
    router_b = jnp.concatenate([p['moe_b_grp'], p['moe_b_exp']], axis=-1)
    router_b = jnp.pad(router_b, ((0, 0), (0, LANES - router_b.shape[-1])))[:, None, :]
    rq, rqt = _head_sum_matrices(N_HEADS)
    rk, rkt = _head_sum_matrices(N_KV_HEADS)
    return dict(
        wz=bf(w_in[:, :, :D_INNER]), wxbc=bf(w_in[:, :, D_INNER:D_INNER + CONV_DIM]), wdt=bf(wdt),
        head_params=head_params, d_skip=jnp.repeat(p['ssm_d'], SSM_HEAD_DIM, axis=-1)[:, None, :],
        conv_w=p['ssm_conv_w'], conv_b=p['ssm_conv_b'][:, None, :], ssm_norm_w=p['ssm_norm_w'][:, None, :],
        w_out=bf(p['ssm_w_out']),
        norm_mix_w=p['norm_mix_w'][:, None, :], norm_ffn_w=p['norm_ffn_w'][:, None, :],
        kv_norm_w=p['kv_norm_w'][None, :], wk=bf(p['w_k']), wv=bf(p['w_v']),
        k_norm_w=jnp.tile(p['k_norm_w'], N_KV_HEADS)[None, :],
        wq=bf(p['attn_w_q']), q_norm_w=jnp.tile(p['q_norm_w'], (1, N_HEADS))[:, None, :], wo=bf(p['attn_w_o']),
        router_w=router_w, router_b=router_b,
        w1=bf(p['moe_w1']), w3=bf(p['moe_w3']), w2=bf(p['moe_w2']),
        rq=rq, rqt=rqt, rk=rk, rkt=rkt,
    )


def _trunk(h, mods, kv_mod, w, *, sample, conv0=None, ssm0=None, cache=None):
    bsz, t, _ = h.shape
    tm = min(t, 1024)
    conv_out = []
    st_all = None
    k_new = v_new = kd = vt = kmd = None
    for layer in range(DEPTH):
        sh_m, sc_m, g_m, sh_f, sc_f, g_f = mods[layer]
        nmw = w['norm_mix_w'][layer]
        if layer < N_A_LAYERS:
            z = _nm_matmul(h, nmw, sc_m, sh_m, w['wz'][layer], tm, 1024)
            xbc = _nm_matmul(h, nmw, sc_m, sh_m, w['wxbc'][layer], tm, 1024)
            dtr = _nm_matmul(h, nmw, sc_m, sh_m, w['wdt'][layer], tm, LANES)
            ssd_args = (w['conv_w'][layer], w['conv_b'][layer], w['head_params'][layer], w['d_skip'][layer],
                        w['ssm_norm_w'][layer])
            if sample:
                n_seq, t_new = conv0.shape[1], cache[3]
                seqs = lambda x: x.reshape(n_seq, t_new, x.shape[-1])
                conv_in = jnp.pad(conv0[layer], ((0, 0), (HALO - (CONV_WIDTH - 1), 0), (0, 0)))
                h0_all = ssm0.reshape(N_A_LAYERS, n_seq, D_INNER, D_STATE)
                g, conv8, st_all = _ssd(seqs(z), seqs(xbc), seqs(dtr), conv_in, h0_all, st_all, layer,
                                        *ssd_args, g_dtype=F32)
                g = g.reshape(1, t, D_INNER)
            else:
                conv_in = jnp.zeros((bsz, HALO, CONV_DIM), F32)
                g, conv8, st_all = _ssd(z, xbc, dtr, conv_in, None, st_all, layer, *ssd_args, g_dtype=BF16)
            conv_out.append(conv8[:, HALO - (CONV_WIDTH - 1):])
            h = _mm_res(g, w['w_out'][layer], h, g_m, tm)
        else:
            j = layer - N_A_LAYERS
            if sample:
                q = _q_proj(h, nmw, sc_m, sh_m, w['wq'][j], w['q_norm_w'][j], w['rq'], w['rqt'], tm, F32)
                o = _attn_sample(q, k_new, v_new, cache[0], cache[1], cache[2], cache[3])
            else:
                q = _q_proj(h, nmw, sc_m, sh_m, w['wq'][j], w['q_norm_w'][j], w['rq'], w['rqt'], tm, BF16,
                            out_scale=LOG2_E)
                o = _attn_prompt(q, kd, vt, kmd)
            h = _mm_res(o, w['wo'][j], h, g_m, tm)
        h = _moe(h, w['norm_ffn_w'][layer], sc_f, sh_f, g_f, w['router_w'][layer], w['router_b'][layer],
                 w['w1'][layer], w['w3'][layer], w['w2'][layer], tm)
        if layer == N_A_LAYERS - 1:
            kv_tm = min(t, 512)
            res = _kv_proj(h, w['kv_norm_w'], kv_mod[1], kv_mod[0], w['wk'], w['wv'], w['k_norm_w'],
                           w['rk'], w['rkt'], kv_tm, with_dup=not sample)
            k_new, v_new = res[0], res[1]
            if not sample:
                kd, vt = res[2], res[3]
                kmd = res[4].reshape(bsz, t // MOBA_BLOCK, N_KV_HEADS * LANES)
                kmd = jnp.pad(kmd, ((0, 0), (0, -kmd.shape[1] % SUBLANES), (0, 0)))
    ssm_out = st_all.reshape(N_A_LAYERS, st_all.shape[1], SSM_HEADS, SSM_HEAD_DIM, D_STATE)
    return h, jnp.stack(conv_out), ssm_out, k_new, v_new


def kernel(x_prompt, x_sample, state_conv, state_ssm, cache_k, cache_v, page_table, c_prompt, c_sample, w_mod, b_mod, norm_mix_w, norm_ffn_w, ssm_w_in, ssm_conv_w, ssm_conv_b, ssm_dt_bias, ssm_a_log, ssm_d, ssm_norm_w, ssm_w_out, kv_w_mod, kv_b_mod, kv_norm_w, w_k, w_v, k_norm_w, attn_w_q, q_norm_w, attn_w_o, moe_w_grp, moe_b_grp, moe_w_exp, moe_b_exp, moe_w1, moe_w3, moe_w2):
    params = dict(ssm_w_in=ssm_w_in, ssm_conv_w=ssm_conv_w, ssm_conv_b=ssm_conv_b, ssm_dt_bias=ssm_dt_bias,
                  ssm_a_log=ssm_a_log, ssm_d=ssm_d, ssm_norm_w=ssm_norm_w, ssm_w_out=ssm_w_out,
                  norm_mix_w=norm_mix_w, norm_ffn_w=norm_ffn_w, kv_norm_w=kv_norm_w, w_k=w_k, w_v=w_v,
                  k_norm_w=k_norm_w, attn_w_q=attn_w_q, q_norm_w=q_norm_w, attn_w_o=attn_w_o,
                  moe_w_grp=moe_w_grp, moe_b_grp=moe_b_grp, moe_w_exp=moe_w_exp, moe_b_exp=moe_b_exp,
                  moe_w1=moe_w1, moe_w3=moe_w3, moe_w2=moe_w2)
    w = _prep_weights(params)

    bp, seq, d = x_prompt.shape
    n_seq, t_new, _ = x_sample.shape
    n_pages = page_table.shape[1]
    past_len = n_pages * PAGE_SIZE
    assert seq % MOBA_BLOCK == 0 and seq // MOBA_BLOCK <= LANES
    assert past_len % MOBA_BLOCK == 0 and t_new <= MOBA_BLOCK and t_new % SUBLANES == 0

    n_c = bp + n_seq
    n_c_pad = -(-n_c // SUBLANES) * SUBLANES
    c_all = jnp.pad(jnp.concatenate([c_prompt, c_sample], axis=0), ((0, n_c_pad - n_c), (0, 0)))
    mod_all = _mod_vectors(c_all, w_mod, b_mod[:, None, :], 1536)
    kv_all = _mod_vectors(c_all, kv_w_mod[None], kv_b_mod[None, None, :], 1024)[0]

    def group_mods(lo, hi, per_token_repeat):
        def shape(x):
            if per_token_repeat:
                return jnp.repeat(x, per_token_repeat, axis=0)[None]
            return x[:, None, :]
        mods = [[shape(mod_all[l, lo:hi, k * d:(k + 1) * d]) for k in range(6)] for l in range(DEPTH)]
        kvm = [shape(kv_all[lo:hi, k * d:(k + 1) * d]) for k in range(2)]
        return mods, kvm

    mods_p, kv_p = group_mods(0, bp, 0)
    y_p, conv_p, ssm_p, k_p, v_p = _trunk(x_prompt, mods_p, kv_p, w, sample=False)

    mods_s, kv_s = group_mods(bp, n_c, t_new)
    kvw = N_KV_HEADS * HEAD_DIM
    cache = (jnp.transpose(cache_k, (0, 2, 3, 1)), jnp.transpose(cache_v, (0, 2, 3, 1)), page_table, t_new)
    y_s, conv_s, ssm_s, k_s, v_s = _trunk(x_sample.reshape(1, n_seq * t_new, d), mods_s, kv_s, w,
                                          sample=True, conv0=state_conv, ssm0=state_ssm, cache=cache)

    heads = lambda x, b, t: x.reshape(b, t, N_KV_HEADS, HEAD_DIM)
    return (y_p, y_s.reshape(n_seq, t_new, d), conv_p, ssm_p, heads(k_p, bp, seq), heads(v_p, bp, seq),
            conv_s, ssm_s, heads(k_s, n_seq, t_new), heads(v_s, n_seq, t_new))
```

```python
import functools

import jax
import jax.numpy as jnp
from jax import lax
from jax.experimental import pallas as pl
from jax.experimental.pallas import tpu as pltpu

F32 = jnp.float32
BF16 = jnp.bfloat16
HIGHEST = lax.Precision.HIGHEST

D_MODEL = 1024
DEPTH = 4
N_A_LAYERS = 2
D_INNER = 2048
SSM_HEADS = 32
SSM_HEAD_DIM = 64
SSM_GROUPS = 4
D_STATE = 128
CONV_WIDTH = 4
CONV_DIM = D_INNER + 2 * SSM_GROUPS * D_STATE
SSD_CHUNK = 128
N_HEADS = 16
HEAD_DIM = 64
N_KV_HEADS = 8
MOBA_BLOCK = 256
MOBA_TOPK = 3
N_EXPERT_GROUPS = 4
EXPERTS_PER_GROUP = 4
N_EXPERTS = 16
D_EXPERT = 256
PAGE_SIZE = 128
EPS = 1e-6

LANES = 128
SUBLANES = 8
BF16_ROWS = 16
MIB = 1024 * 1024
NEG_BIG = -1e30

NT_DIMS = (((1,), (1,)), ((), ()))


def _cparams(sem, vmem_mib):
    return pltpu.CompilerParams(dimension_semantics=sem, vmem_limit_bytes=vmem_mib * MIB)


def _sigmoid(x):
    return 1.0 / (1.0 + jnp.exp(-x))


def _silu(x):
    return x * _sigmoid(x)


def _norm_mod(h, nw, sc, sh):
    ms = jnp.mean(h * h, axis=-1, keepdims=True)
    return (h * lax.rsqrt(ms + EPS)) * nw * (1.0 + sc) + sh


def _split_bf16(x):
    hi = x.astype(BF16)
    lo = (x - hi.astype(F32)).astype(BF16)
    return hi, lo


def _head_rmsnorm(x, r, rt):
    sq = x * x
    hi, lo = _split_bf16(sq)
    ss = jnp.dot(hi, r, preferred_element_type=F32) + jnp.dot(lo, r, preferred_element_type=F32)
    inv = lax.rsqrt(ss * (1.0 / HEAD_DIM) + EPS)
    ih, il = _split_bf16(inv)
    invx = jnp.dot(ih, rt, preferred_element_type=F32) + jnp.dot(il, rt, preferred_element_type=F32)
    return x * invx


def _mod_kernel(c_ref, w_ref, b_ref, o_ref):
    ca = _silu(c_ref[...])
    o_ref[0] = jnp.dot(ca, w_ref[0], precision=HIGHEST, preferred_element_type=F32) + b_ref[0]


def _mod_vectors(c_all, w, b, tn):
    n_layers, d, n = w.shape
    m = c_all.shape[0]
    return pl.pallas_call(
        _mod_kernel,
        grid=(n_layers, n // tn),
        in_specs=[pl.BlockSpec((m, d), lambda l, j: (0, 0)),
                  pl.BlockSpec((1, d, tn), lambda l, j: (l, 0, j)),
                  pl.BlockSpec((1, 1, tn), lambda l, j: (l, 0, j))],
        out_specs=pl.BlockSpec((1, m, tn), lambda l, j: (l, 0, j)),
        out_shape=jax.ShapeDtypeStruct((n_layers, m, n), F32),
        compiler_params=_cparams(("arbitrary", "arbitrary"), 40),
        name="mod_vectors",
    )(c_all, w, b)


def _mod_spec(mod, tm):
    d = mod.shape[-1]
    if mod.shape[1] == 1:
        return pl.BlockSpec((1, 1, d), lambda b, i, *_: (b, 0, 0))
    return pl.BlockSpec((1, tm, d), lambda b, i, *_: (b, i, 0))


def _nm_matmul_kernel(h_ref, nw_ref, sc_ref, sh_ref, w_ref, o_ref, u_ref):
    @pl.when(pl.program_id(2) == 0)
    def _():
        u_ref[...] = _norm_mod(h_ref[0], nw_ref[...], sc_ref[0], sh_ref[0]).astype(BF16)

    o_ref[0] = jnp.dot(u_ref[...], w_ref[...], preferred_element_type=F32)


def _nm_matmul(h, nw, sc, sh, w, tm, tn):
    bsz, t, d = h.shape
    n = w.shape[1]
    return pl.pallas_call(
        _nm_matmul_kernel,
        grid=(bsz, t // tm, n // tn),
        in_specs=[pl.BlockSpec((1, tm, d), lambda b, i, j: (b, i, 0)),
                  pl.BlockSpec((1, d), lambda b, i, j: (0, 0)),
                  _mod_spec(sc, tm), _mod_spec(sh, tm),
                  pl.BlockSpec((d, tn), lambda b, i, j: (0, j))],
        out_specs=pl.BlockSpec((1, tm, tn), lambda b, i, j: (b, i, j)),
        out_shape=jax.ShapeDtypeStruct((bsz, t, n), F32),
        scratch_shapes=[pltpu.VMEM((tm, d), BF16)],
        compiler_params=_cparams(("parallel", "parallel", "arbitrary"), 48),
        name="norm_mod_matmul",
    )(h, nw, sc, sh, w)


def _mm_res_kernel(a_ref, w_ref, h_ref, g_ref, o_ref):
    acc = jnp.dot(a_ref[0].astype(BF16), w_ref[...], preferred_element_type=F32)
    o_ref[0] = h_ref[0] + g_ref[0] * acc


def _mm_res(a, w, h, gate, tm):
    bsz, t, k = a.shape
    d = w.shape[1]
    return pl.pallas_call(
        _mm_res_kernel,
        grid=(bsz, t // tm),
        in_specs=[pl.BlockSpec((1, tm, k), lambda b, i: (b, i, 0)),
                  pl.BlockSpec((k, d), lambda b, i: (0, 0)),
                  pl.BlockSpec((1, tm, d), lambda b, i: (b, i, 0)),
                  _mod_spec(gate, tm)],
        out_specs=pl.BlockSpec((1, tm, d), lambda b, i: (b, i, 0)),
        out_shape=jax.ShapeDtypeStruct((bsz, t, d), F32),
        compiler_params=_cparams(("parallel", "parallel"), 48),
        name="matmul_residual",
    )(a, w, h, gate)


HALO = SUBLANES
XC_TILE = 512


def _ssd_kernel(z_ref, xbc_ref, dt_ref, cs_ref, h0_ref, cw_ref, cb_ref, hp_ref, dsk_ref, nw_ref,
                *rest, valid, has_init, has_prev):
    g_ref, cso_ref, st_ref, xs_ref, xc_ref, xt_ref, y_ref = rest[1:] if has_prev else rest
    L = SSD_CHUNK
    c = pl.program_id(1)

    @pl.when(c == 0)
    def _init():
        xs_ref[0:HALO, :] = cs_ref[0]
        if valid < L:
            xc_ref[...] = jnp.zeros_like(xc_ref)
        if has_init:
            st_ref[0] = h0_ref[0]
        else:
            st_ref[0] = jnp.zeros(st_ref.shape[1:], F32)

    xs_ref[HALO:HALO + valid, :] = xbc_ref[0]

    for j in range(CONV_DIM // XC_TILE):
        sl = slice(j * XC_TILE, (j + 1) * XC_TILE)
        x_rows = xs_ref[0:HALO + valid, sl]
        acc = x_rows * cw_ref[0:1, sl]
        for k in range(1, CONV_WIDTH):
            acc = pltpu.roll(acc, 1, 0) + x_rows * cw_ref[k:k + 1, sl]
        xc_ref[0:valid, sl] = _silu(acc[HALO:HALO + valid] + cb_ref[:, sl])

    cso_ref[0] = xs_ref[valid:valid + HALO, :]
    xs_ref[0:HALO, :] = xs_ref[valid:valid + HALO, :]

    dtr = dt_ref[0]
    if valid < L:
        dtr = jnp.concatenate([dtr, jnp.zeros((L - valid, LANES), F32)], axis=0)
    xx = dtr + hp_ref[0:1, :]
    dt = jnp.maximum(xx, 0.0) + jnp.log(1.0 + jnp.exp(-jnp.abs(xx)))
    row = lax.broadcasted_iota(jnp.int32, (L, L), 0)
    col = lax.broadcasted_iota(jnp.int32, (L, L), 1)
    if valid < L:
        dt = jnp.where(row < valid, dt, 0.0)
    a = dt * (-jnp.exp(hp_ref[1:2, :]))
    causal = row >= col
    a_cum = jnp.dot(causal.astype(F32), a, precision=HIGHEST, preferred_element_type=F32)
    a_cum_t = a_cum.T
    dt_t = dt.T
    sub_lo = row < HEAD_DIM
    R = L if valid == L else -(-valid // BF16_ROWS) * BF16_ROWS
    row_r = lax.broadcasted_iota(jnp.int32, (R, L), 0)
    col_r = lax.broadcasted_iota(jnp.int32, (R, L), 1)
    causal_r = row_r >= col_r
    lane_lo = col_r < HEAD_DIM

    for j in range(D_INNER // LANES):
        xt_ref[j * LANES:(j + 1) * LANES, :] = xc_ref[:, j * LANES:(j + 1) * LANES].T

    pairs_per_group = SSM_HEADS // SSM_GROUPS // 2
    grp_w = D_INNER // SSM_GROUPS
    for g in range(SSM_GROUPS):
        b_g = xc_ref[:, D_INNER + g * D_STATE:D_INNER + (g + 1) * D_STATE].astype(BF16)
        c_g = xc_ref[:, D_INNER + (SSM_GROUPS + g) * D_STATE:
                     D_INNER + (SSM_GROUPS + g + 1) * D_STATE].astype(BF16)
        cb = lax.dot_general(c_g[0:R], b_g, NT_DIMS, preferred_element_type=F32)
        st_g = st_ref[0, g * grp_w:(g + 1) * grp_w, :]
        y_off = lax.dot_general(c_g[0:R], st_g.astype(BF16), NT_DIMS, preferred_element_type=F32)
        xw_parts, dec_parts = [], []
        for jj in range(pairs_per_group):
            pair = g * pairs_per_group + jj
            s_mats, e_cols, w_rows, d_end = [], [], [], []
            for h in (2 * pair, 2 * pair + 1):
                colb = jnp.broadcast_to(a_cum[0:R, h:h + 1], (R, L))
                rowb = a_cum_t[h:h + 1, :]
                dec = jnp.where(causal_r, jnp.exp(colb - rowb), 0.0)
                s_mats.append((cb * dec * dt_t[h:h + 1, :]).astype(BF16))
                e_cols.append(jnp.exp(colb))
                a_last = a_cum_t[h:h + 1, L - 1:L]
                w_rows.append(dt_t[h:h + 1, :] * jnp.exp(a_last - rowb))
                d_end.append(jnp.broadcast_to(jnp.exp(a_last), (LANES, D_STATE)))
            psl = slice(pair * LANES, (pair + 1) * LANES)
            xp = xc_ref[:, psl]
            xpb = xp.astype(BF16)
            y_d = jnp.where(lane_lo,
                            jnp.dot(s_mats[0], xpb, preferred_element_type=F32),
                            jnp.dot(s_mats[1], xpb, preferred_element_type=F32))
            y_o = y_off[:, jj * LANES:(jj + 1) * LANES] * jnp.where(lane_lo, e_cols[0], e_cols[1])
            y_ref[0:R, psl] = y_d + y_o + dsk_ref[:, psl] * xp[0:R]
            xw_parts.append((xt_ref[psl, :] * jnp.where(sub_lo, w_rows[0], w_rows[1])).astype(BF16))
            dec_parts.append(jnp.where(sub_lo, d_end[0], d_end[1]))
        xw_g = jnp.concatenate(xw_parts, axis=0)
        new = jnp.dot(xw_g, b_g, preferred_element_type=F32)
        st_ref[0, g * grp_w:(g + 1) * grp_w, :] = st_g * jnp.concatenate(dec_parts, axis=0) + new

    zz = z_ref[0]
    gg = y_ref[0:valid, :] * _silu(zz)
    for g in range(SSM_GROUPS):
        sl = slice(g * grp_w, (g + 1) * grp_w)
        seg = gg[:, sl]
        ms = jnp.mean(seg * seg, axis=-1, keepdims=True)
        g_ref[0, :, sl] = (seg * lax.rsqrt(ms + EPS) * nw_ref[:, sl]).astype(g_ref.dtype)


def _ssd(z, xbc, dtr, conv_in, h0_all, st_all, layer, conv_w, conv_b, head_params, d_skip, norm_w, g_dtype):
    bsz, t, _ = z.shape
    L = SSD_CHUNK
    valid = min(L, t)
    assert t % valid == 0 and valid % SUBLANES == 0
    nc = t // valid
    has_init = h0_all is not None
    state_spec = pl.BlockSpec((None, 1, D_INNER, D_STATE), lambda b, c: (layer, b, 0, 0))
    if h0_all is None:
        h0_all = jnp.zeros((1, SUBLANES, D_STATE), F32)
        h0_spec = pl.BlockSpec((1, SUBLANES, D_STATE), lambda b, c: (0, 0, 0))
    else:
        h0_spec = state_spec
    full = lambda shape: pl.BlockSpec(shape, lambda b, c: (0,) * len(shape))
    operands = [z, xbc, dtr, conv_in, h0_all, conv_w, conv_b, head_params, d_skip, norm_w]
    in_specs = [pl.BlockSpec((1, valid, D_INNER), lambda b, c: (b, c, 0)),
                pl.BlockSpec((1, valid, CONV_DIM), lambda b, c: (b, c, 0)),
                pl.BlockSpec((1, valid, LANES), lambda b, c: (b, c, 0)),
                pl.BlockSpec((1, HALO, CONV_DIM), lambda b, c: (b, 0, 0)),
                h0_spec,
                full((CONV_WIDTH, CONV_DIM)), full((1, CONV_DIM)), full((SUBLANES, LANES)),
                full((1, D_INNER)), full((1, D_INNER))]
    aliases = {}
    if st_all is not None:
        aliases = {len(operands): 2}
        operands.append(st_all)
        in_specs.append(pl.BlockSpec(memory_space=pl.ANY))
    return pl.pallas_call(
        functools.partial(_ssd_kernel, valid=valid, has_init=has_init, has_prev=st_all is not None),
        grid=(bsz, nc),
        in_specs=in_specs,
        out_specs=[pl.BlockSpec((1, valid, D_INNER), lambda b, c: (b, c, 0)),
                   pl.BlockSpec((1, HALO, CONV_DIM), lambda b, c: (b, 0, 0)),
                   state_spec],
        out_shape=[jax.ShapeDtypeStruct((bsz, t, D_INNER), g_dtype),
                   jax.ShapeDtypeStruct((bsz, HALO, CONV_DIM), F32),
                   jax.ShapeDtypeStruct((N_A_LAYERS, bsz, D_INNER, D_STATE), F32)],
        input_output_aliases=aliases,
        scratch_shapes=[pltpu.VMEM((L + HALO, CONV_DIM), F32),
                        pltpu.VMEM((L, CONV_DIM), F32),
                        pltpu.VMEM((D_INNER, L), F32),
                        pltpu.VMEM((L, D_INNER), F32)],
        compiler_params=_cparams(("parallel", "arbitrary"), 48),
        name="ssd_chunk_scan",
    )(*operands)


ROUTER_GRP0 = 0
ROUTER_EXP0 = N_EXPERT_GROUPS


def _route(logits):
    lane = lax.broadcasted_iota(jnp.int32, logits.shape, 1)
    big = jnp.int32(2 * LANES)
    is_grp = lane < N_EXPERT_GROUPS
    gl = jnp.where(is_grp, logits, -jnp.inf)
    gmax = jnp.max(gl, axis=-1, keepdims=True)
    gidx = jnp.min(jnp.where(gl == gmax, lane, big), axis=-1, keepdims=True)
    p_grp = 1.0 / jnp.sum(jnp.where(is_grp, jnp.exp(gl - gmax), 0.0), axis=-1, keepdims=True)
    e_rel = lane - ROUTER_EXP0
    in_grp = (e_rel >= gidx * EXPERTS_PER_GROUP) & (e_rel < (gidx + 1) * EXPERTS_PER_GROUP)
    el = jnp.where(in_grp, logits, -jnp.inf)
    m1 = jnp.max(el, axis=-1, keepdims=True)
    i1 = jnp.min(jnp.where(el == m1, lane, big), axis=-1, keepdims=True)
    el2 = jnp.where(lane == i1, -jnp.inf, el)
    m2 = jnp.max(el2, axis=-1, keepdims=True)
    i2 = jnp.min(jnp.where(el2 == m2, lane, big), axis=-1, keepdims=True)
    e2 = jnp.exp(m2 - m1)
    den = 1.0 + e2
    w1 = (1.0 / den) * p_grp
    w2 = (e2 / den) * p_grp
    return jnp.where(lane == i1, w1, jnp.where(lane == i2, w2, 0.0))


def _moe_kernel(h_ref, nw_ref, sc_ref, sh_ref, gf_ref, wr_ref, br_ref, w1_ref, w3_ref, w2_ref,
                o_ref, u_ref, gates_ref, acc_ref):
    step = pl.program_id(2)
    per_step = w1_ref.shape[0]

    @pl.when(step == 0)
    def _():
        u = _norm_mod(h_ref[0], nw_ref[...], sc_ref[0], sh_ref[0])
        logits = jnp.dot(u, wr_ref[...], precision=HIGHEST, preferred_element_type=F32) + br_ref[...]
        gates_ref[...] = _route(logits)
        u_ref[...] = u.astype(BF16)
        acc_ref[...] = jnp.zeros_like(acc_ref)

    u = u_ref[...]
    gates = gates_ref[...]
    lane = lax.broadcasted_iota(jnp.int32, gates.shape, 1)
    acts = []
    for j in range(per_step):
        a = (_silu(jnp.dot(u, w1_ref[j], preferred_element_type=F32))
             * jnp.dot(u, w3_ref[j], preferred_element_type=F32))
        e = step * per_step + j
        gcol = jnp.sum(jnp.where(lane == e + ROUTER_EXP0, gates, 0.0), axis=-1, keepdims=True)
        acts.append((a * gcol).astype(BF16))
    w2 = w2_ref[...].reshape(per_step * w2_ref.shape[1], w2_ref.shape[2])
    acc_ref[...] += jnp.dot(jnp.concatenate(acts, axis=1), w2, preferred_element_type=F32)

    @pl.when(step == pl.num_programs(2) - 1)
    def _():
        o_ref[0] = h_ref[0] + gf_ref[0] * acc_ref[...]


def _moe(h, nw, sc, sh, gf, wr, br, w1, w3, w2, tm):
    bsz, t, d = h.shape
    n_e, _, f = w1.shape
    per_step = EXPERTS_PER_GROUP
    return pl.pallas_call(
        _moe_kernel,
        grid=(bsz, t // tm, n_e // per_step),
        in_specs=[pl.BlockSpec((1, tm, d), lambda b, i, e: (b, i, 0)),
                  pl.BlockSpec((1, d), lambda b, i, e: (0, 0)),
                  _mod_spec(sc, tm), _mod_spec(sh, tm), _mod_spec(gf, tm),
                  pl.BlockSpec((d, LANES), lambda b, i, e: (0, 0)),
                  pl.BlockSpec((1, LANES), lambda b, i, e: (0, 0)),
                  pl.BlockSpec((per_step, d, f), lambda b, i, e: (e, 0, 0)),
                  pl.BlockSpec((per_step, d, f), lambda b, i, e: (e, 0, 0)),
                  pl.BlockSpec((per_step, f, d), lambda b, i, e: (e, 0, 0))],
        out_specs=pl.BlockSpec((1, tm, d), lambda b, i, e: (b, i, 0)),
        out_shape=jax.ShapeDtypeStruct((bsz, t, d), F32),
        scratch_shapes=[pltpu.VMEM((tm, d), BF16), pltpu.VMEM((tm, LANES), F32), pltpu.VMEM((tm, d), F32)],
        compiler_params=_cparams(("parallel", "parallel", "arbitrary"), 48),
        name="hmoe",
    )(h, nw, sc, sh, gf, wr, br, w1, w3, w2)


N_SPLIT = 3
FEAT_POS_LO = HEAD_DIM
FEAT_POS_HI = FEAT_POS_LO + N_SPLIT
FEAT_ONE = FEAT_POS_HI + N_SPLIT


def _split3(x):
    h1 = x.astype(BF16).astype(F32)
    r = x - h1
    h2 = r.astype(BF16).astype(F32)
    h3 = (r - h2).astype(BF16).astype(F32)
    return h1, h2, h3


def _feature_lanes(lane, base, parts, other):
    out = other
    for k, part in enumerate(parts):
        out = jnp.where(lane == base + k, part, out)
    return out


def _key_tiles(x, feat):
    lane = lax.broadcasted_iota(jnp.int32, x.shape, 1)
    rolled = pltpu.roll(x, HEAD_DIM, 1)
    lo = lane < HEAD_DIM
    return jnp.where(lo, x, feat), jnp.where(lo, rolled, feat)


def _kv_kernel(h_ref, nw_ref, sc_ref, sh_ref, wk_ref, wv_ref, knw_ref, r_ref, rt_ref,
               kf_ref, vf_ref, *dup_refs, tm, with_dup):
    u = _norm_mod(h_ref[0], nw_ref[...], sc_ref[0], sh_ref[0]).astype(BF16)
    k = jnp.dot(u, wk_ref[...], preferred_element_type=F32)
    v = jnp.dot(u, wv_ref[...], preferred_element_type=F32)
    kn = _head_rmsnorm(k, r_ref[...], rt_ref[...]) * knw_ref[...]
    if not with_dup:
        kf_ref[0] = kn
        vf_ref[0] = v
    else:
        for hh in range(N_KV_HEADS):
            rows = pl.ds(hh, tm, stride=N_KV_HEADS)
            for src, dst in ((kn, kf_ref), (v, vf_ref)):
                tile = src[:, (hh // 2) * LANES:(hh // 2 + 1) * LANES]
                if hh % 2 == 1:
                    tile = pltpu.roll(tile, HEAD_DIM, 1)
                dst[0, rows, :] = tile[:, :HEAD_DIM]
        kd_ref, vt_ref, km_ref = dup_refs
        lane = lax.broadcasted_iota(jnp.int32, (tm, LANES), 1)
        pos = pl.program_id(1) * tm + lax.broadcasted_iota(jnp.int32, (tm, LANES), 0)
        pos_lo = pos % MOBA_BLOCK
        feat = jnp.zeros((tm, LANES), F32)
        feat = _feature_lanes(lane, FEAT_POS_LO, [pos_lo.astype(F32)] * N_SPLIT, feat)
        feat = _feature_lanes(lane, FEAT_POS_HI, [(pos - pos_lo).astype(F32)] * N_SPLIT, feat)
        feat = _feature_lanes(lane, FEAT_ONE, [jnp.ones((tm, LANES), F32)] * N_SPLIT, feat)
        lane_row = lax.broadcasted_iota(jnp.int32, (1, LANES), 1)
        for cidx in range(N_KV_HEADS // 2):
            d0, d1 = _key_tiles(kn[:, cidx * LANES:(cidx + 1) * LANES], feat)
            for hh, dd in ((2 * cidx, d0), (2 * cidx + 1, d1)):
                hsl = slice(hh * LANES, (hh + 1) * LANES)
                kd_ref[0, :, hsl] = dd.astype(BF16)
                for blk in range(tm // MOBA_BLOCK):
                    mean = jnp.mean(dd[blk * MOBA_BLOCK:(blk + 1) * MOBA_BLOCK], axis=0, keepdims=True)
                    for part, piece in enumerate(_split3(jnp.where(lane_row < HEAD_DIM, mean, 0.0))):
                        km_ref[0, part, blk, :, hsl] = piece
        for blk in range(tm // MOBA_BLOCK):
            vt_ref[0, blk] = v[blk * MOBA_BLOCK:(blk + 1) * MOBA_BLOCK].T.astype(BF16)


def _kv_proj(h, nw, sc, sh, wk, wv, knw, r, rt, tm, with_dup):
    bsz, t, d = h.shape
    kvw = wk.shape[1]
    const = lambda shape: pl.BlockSpec(shape, lambda b, i: (0,) * len(shape))
    if not with_dup:
        out_specs = [pl.BlockSpec((1, tm, kvw), lambda b, i: (b, i, 0))] * 2
        out_shape = [jax.ShapeDtypeStruct((bsz, t, kvw), F32)] * 2
    else:
        out_specs = [pl.BlockSpec((1, tm * N_KV_HEADS, HEAD_DIM), lambda b, i: (b, i, 0))] * 2
        out_shape = [jax.ShapeDtypeStruct((bsz, t * N_KV_HEADS, HEAD_DIM), F32)] * 2
        dupw = N_KV_HEADS * LANES
        nblk = tm // MOBA_BLOCK
        out_specs += [pl.BlockSpec((1, tm, dupw), lambda b, i: (b, i, 0)),
                      pl.BlockSpec((1, nblk, kvw, MOBA_BLOCK), lambda b, i: (b, i, 0, 0)),
                      pl.BlockSpec((1, N_SPLIT, nblk, 1, dupw), lambda b, i: (b, 0, i, 0, 0))]
        out_shape += [jax.ShapeDtypeStruct((bsz, t, dupw), BF16),
                      jax.ShapeDtypeStruct((bsz, t // MOBA_BLOCK, kvw, MOBA_BLOCK), BF16),
                      jax.ShapeDtypeStruct((bsz, N_SPLIT, t // MOBA_BLOCK, 1, dupw), F32)]
    return pl.pallas_call(
        functools.partial(_kv_kernel, tm=tm, with_dup=with_dup),
        grid=(bsz, t // tm),
        in_specs=[pl.BlockSpec((1, tm, d), lambda b, i: (b, i, 0)),
                  const((1, d)), _mod_spec(sc, tm), _mod_spec(sh, tm),
                  const((d, kvw)), const((d, kvw)), const((1, kvw)),
                  const((kvw, LANES)), const((LANES, kvw))],
        out_specs=out_specs,
        out_shape=out_shape,
        compiler_params=_cparams(("parallel", "parallel"), 48),
        name="shared_kv",
    )(h, nw, sc, sh, wk, wv, knw, r, rt)


def _q_kernel(h_ref, nw_ref, sc_ref, sh_ref, wq_ref, qnw_ref, r_ref, rt_ref, q_ref, *, out_scale):
    u = _norm_mod(h_ref[0], nw_ref[...], sc_ref[0], sh_ref[0]).astype(BF16)
    q = jnp.dot(u, wq_ref[...], preferred_element_type=F32)
    q = _head_rmsnorm(q, r_ref[...], rt_ref[...]) * qnw_ref[...]
    if out_scale != 1.0:
        q = q * out_scale
    q_ref[0] = q.astype(q_ref.dtype)


def _q_proj(h, nw, sc, sh, wq, qnw, r, rt, tm, out_dtype, out_scale=1.0):
    bsz, t, d = h.shape
    const = lambda shape: pl.BlockSpec(shape, lambda b, i: (0,) * len(shape))
    return pl.pallas_call(
        functools.partial(_q_kernel, out_scale=out_scale),
        grid=(bsz, t // tm),
        in_specs=[pl.BlockSpec((1, tm, d), lambda b, i: (b, i, 0)),
                  const((1, d)), _mod_spec(sc, tm), _mod_spec(sh, tm),
                  const((d, d)), const((1, d)), const((d, LANES)), const((LANES, d))],
        out_specs=pl.BlockSpec((1, tm, d), lambda b, i: (b, i, 0)),
        out_shape=jax.ShapeDtypeStruct((bsz, t, d), out_dtype),
        compiler_params=_cparams(("parallel", "parallel"), 48),
        name="q_proj",
    )(h, nw, sc, sh, wq, qnw, r, rt)


def _top_blocks(gate, n_valid, axis):
    pos = lax.broadcasted_iota(jnp.int32, gate.shape, axis)
    gm = jnp.where(pos < n_valid, gate, -jnp.inf)
    sel = jnp.zeros(gate.shape, F32)
    for _ in range(MOBA_TOPK):
        mx = jnp.max(gm, axis=axis, keepdims=True)
        cand = (gm == mx) & (mx > -jnp.inf)
        idx = jnp.min(jnp.where(cand, pos, jnp.int32(2 * LANES)), axis=axis, keepdims=True)
        pick = pos == idx
        sel = jnp.where(pick, 1.0, sel)
        gm = jnp.where(pick, -jnp.inf, gm)
    return sel


def _alibi_slope(head):
    return jnp.exp2(-8.0 * (head + 1).astype(F32) / N_HEADS)


ATT_GROUP = 2
ACC_ROWS = HEAD_DIM + BF16_ROWS
LOG2_E = 1.4426950408889634


def _attn_prompt_kernel(q_ref, k_ref, vt_ref, km_ref, o_ref,
                        qt_ref, fq_ref, sa_ref, sb_ref, sel_ref, m_ref, acc_ref):
    g = pl.program_id(1)
    i = pl.program_id(2)
    blk = MOBA_BLOCK
    cols = 2 * blk
    grp = ATT_GROUP

    l2 = lax.broadcasted_iota(jnp.int32, (cols, LANES), 1)

    @pl.when(i == 0)
    def _features():
        r2 = lax.broadcasted_iota(jnp.int32, (cols, LANES), 0)
        second = r2 >= blk
        slope = _alibi_slope(2 * g + second.astype(jnp.int32)) * LOG2_E
        offs_q = jnp.where(second, r2 - blk, r2).astype(F32)
        slope_parts = _split3(slope)
        feat = jnp.zeros((cols, LANES), F32)
        feat = _feature_lanes(l2, FEAT_POS_LO, slope_parts, feat)
        feat = _feature_lanes(l2, FEAT_POS_HI, slope_parts, feat)
        fq_ref[...] = _feature_lanes(l2, FEAT_ONE, _split3(-slope * offs_q), feat)

    c1 = lax.broadcasted_iota(jnp.int32, (1, cols), 1)
    col_const = -(_alibi_slope(2 * g + (c1 >= blk).astype(jnp.int32)) * LOG2_E) * (i * blk).astype(F32)

    qt = q_ref[0].astype(F32)
    q2 = jnp.concatenate([qt, pltpu.roll(qt, HEAD_DIM, 1)], axis=0)
    q2t = jnp.where(l2 < HEAD_DIM, q2, fq_ref[...]).T
    is_q = lax.broadcasted_iota(jnp.int32, (LANES, 1), 0) < HEAD_DIM
    qt_ref[...] = (q2t * jnp.where(is_q, HEAD_DIM ** -0.5, 1.0)).astype(BF16)
    gate3 = jnp.dot(km_ref[0], qt_ref[...], preferred_element_type=F32)
    nbp = sel_ref.shape[0]
    gate = gate3[0:nbp] + gate3[nbp:2 * nbp] + gate3[2 * nbp:3 * nbp]
    sel_ref[...] = _top_blocks(gate, i, 0)

    def scores(n):
        start = pl.multiple_of(n * blk, blk)
        return jnp.dot(k_ref[0, pl.ds(start, blk), :], qt_ref[...], preferred_element_type=F32)

    ones_rows = jnp.ones((ACC_ROWS - HEAD_DIM, blk), BF16)

    def block_stats(s, n):
        mb = jnp.max(s, axis=0, keepdims=True)
        p = jnp.exp2((s - mb).astype(BF16))
        vt_aug = jnp.concatenate([vt_ref[0, n], ones_rows], axis=0)
        return mb, jnp.dot(vt_aug, p, preferred_element_type=F32)

    def merge(parts):
        m_old = m_ref[...]
        m_new = m_old
        for mb, _ in parts:
            m_new = jnp.maximum(m_new, mb)
        acc_new = jnp.exp2(m_old - m_new) * acc_ref[...]
        for mb, ab in parts:
            acc_new = acc_new + jnp.exp2(mb - m_new) * ab
        m_ref[...] = m_new
        acc_ref[...] = acc_new

    rr = lax.broadcasted_iota(jnp.int32, (blk, cols), 0)
    cc = lax.broadcasted_iota(jnp.int32, (blk, cols), 1)
    visible = jnp.where(cc >= blk, cc - blk, cc) >= rr
    m0, a0 = block_stats(jnp.where(visible, scores(i), -jnp.inf), i)
    m_ref[...] = m0 + col_const
    acc_ref[...] = a0

    n_groups = (i + grp - 1) // grp

    def clamp(n_raw):
        return jnp.maximum(jnp.minimum(n_raw, i - 1), 0)

    def score_group(kk, s_ref):
        for b in range(grp):
            s_ref[b * blk:(b + 1) * blk, :] = scores(clamp(kk * grp + b))

    def softmax_group(kk, s_ref):
        parts = []
        for b in range(grp):
            n_raw = kk * grp + b
            n = clamp(n_raw)
            mb, ab = block_stats(s_ref[b * blk:(b + 1) * blk, :], n)
            counted = sel_ref[pl.ds(n, 1), :] * (n_raw < i).astype(F32) > 0.0
            parts.append((jnp.where(counted, mb + col_const, NEG_BIG), ab))
        merge(parts)

    score_group(0, sa_ref)

    def step(kk, carry):
        score_group(2 * kk + 1, sb_ref)
        softmax_group(2 * kk, sa_ref)
        score_group(2 * kk + 2, sa_ref)
        softmax_group(2 * kk + 1, sb_ref)
        return carry

    lax.fori_loop(0, n_groups // 2, step, 0)

    @pl.when(n_groups % 2 == 1)
    def _last():
        softmax_group(n_groups - 1, sa_ref)

    o = acc_ref[0:HEAD_DIM, :] / acc_ref[HEAD_DIM:HEAD_DIM + 1, :]
    o_ref[0] = jnp.concatenate([o[:, 0:blk], o[:, blk:cols]], axis=0).T.astype(o_ref.dtype)


def _attn_prompt(q, kd, vt, kmd):
    bsz, t, d = q.shape
    blk = MOBA_BLOCK
    nb = t // blk
    nb_pad = kmd.shape[1] // N_SPLIT
    cols = 2 * blk
    return pl.pallas_call(
        _attn_prompt_kernel,
        grid=(bsz, N_KV_HEADS, nb),
        in_specs=[pl.BlockSpec((1, blk, LANES), lambda b, g, i: (b, i, g)),
                  pl.BlockSpec((1, t, LANES), lambda b, g, i: (b, 0, g)),
                  pl.BlockSpec((1, nb, HEAD_DIM, blk), lambda b, g, i: (b, 0, g, 0)),
                  pl.BlockSpec((1, N_SPLIT * nb_pad, LANES), lambda b, g, i: (b, 0, g))],
        out_specs=pl.BlockSpec((1, blk, LANES), lambda b, g, i: (b, i, g)),
        out_shape=jax.ShapeDtypeStruct((bsz, t, d), BF16),
        scratch_shapes=[pltpu.VMEM((LANES, cols), BF16), pltpu.VMEM((cols, LANES), F32),
                        pltpu.VMEM((ATT_GROUP * blk, cols), F32), pltpu.VMEM((ATT_GROUP * blk, cols), F32),
                        pltpu.VMEM((nb_pad, cols), F32), pltpu.VMEM((1, cols), F32),
                        pltpu.VMEM((ACC_ROWS, cols), F32)],
        compiler_params=_cparams(("parallel", "parallel", "arbitrary"), 48),
        name="moba_prompt",
    )(q, kd, vt, kmd)


def _attn_sample_kernel(pt_ref, q_ref, kn_ref, vn_ref, *rest, n_pages):
    del pt_ref
    k_refs = rest[:n_pages]
    v_refs = rest[n_pages:2 * n_pages]
    o_ref = rest[2 * n_pages]
    s_ref = rest[2 * n_pages + 1]
    t_new = q_ref.shape[1]
    past_len = n_pages * PAGE_SIZE
    kvw = N_KV_HEADS * HEAD_DIM
    rows = N_HEADS * t_new
    n_past_blk = past_len // MOBA_BLOCK
    pages_per_blk = MOBA_BLOCK // PAGE_SIZE
    q = q_ref[0]
    lane8 = lax.broadcasted_iota(jnp.int32, (t_new, LANES), 1)
    zeros8 = jnp.zeros((t_new, LANES), F32)
    row_blocks = []
    for h in range(N_HEADS):
        g = h // 2
        tile = q[:, g * LANES:(g + 1) * LANES]
        if h % 2 != g % 2:
            tile = pltpu.roll(tile, HEAD_DIM, 1)
        keep = (lane8 < HEAD_DIM) if g % 2 == 0 else (lane8 >= HEAD_DIM)
        tile = jnp.where(keep, tile, 0.0)
        row_blocks.append(jnp.concatenate([tile if cidx == g // 2 else zeros8 for cidx in range(kvw // LANES)], axis=1))
    qm = jnp.concatenate(row_blocks, axis=0)
    qmb = (qm * (HEAD_DIM ** -0.5)).astype(BF16)

    r1 = lax.broadcasted_iota(jnp.int32, (rows, 1), 0)
    slope = _alibi_slope(r1 // t_new)
    rr = lax.broadcasted_iota(jnp.int32, (rows, PAGE_SIZE), 0)
    cc = lax.broadcasted_iota(jnp.int32, (rows, PAGE_SIZE), 1)
    d_page0 = (past_len + rr % t_new - cc).astype(F32)

    lane_b = lax.broadcasted_iota(jnp.int32, (kvw, LANES), 1)
    kmt = jnp.zeros((kvw, LANES), F32)
    for n in range(n_past_blk):
        acc = k_refs[n * pages_per_blk][0].reshape(kvw, PAGE_SIZE)
        for pp in range(1, pages_per_blk):
            acc = acc + k_refs[n * pages_per_blk + pp][0].reshape(kvw, PAGE_SIZE)
        mean = jnp.sum(acc, axis=1, keepdims=True) * (1.0 / MOBA_BLOCK)
        kmt = jnp.where(lane_b == n, mean, kmt)
    gate = jnp.dot(qm, kmt, precision=HIGHEST, preferred_element_type=F32)
    sel = _top_blocks(gate, n_past_blk, 1)
    lane_s = lax.broadcasted_iota(jnp.int32, sel.shape, 1)

    for p in range(n_pages):
        kp = k_refs[p][0].reshape(kvw, PAGE_SIZE).astype(BF16)
        s = jnp.dot(qmb, kp, preferred_element_type=F32)
        s = s - slope * (d_page0 - float(p * PAGE_SIZE))
        picked = jnp.sum(jnp.where(lane_s == p // pages_per_blk, sel, 0.0), axis=-1, keepdims=True) > 0.0
        s_ref[:, p * PAGE_SIZE:(p + 1) * PAGE_SIZE] = jnp.where(picked, s, -jnp.inf)
    pad = jnp.zeros((LANES - t_new, kvw), F32)
    k_new = jnp.concatenate([kn_ref[0], pad], axis=0).astype(BF16)
    v_new = jnp.concatenate([vn_ref[0], pad], axis=0).astype(BF16)
    rr_own = lax.broadcasted_iota(jnp.int32, (rows, LANES), 0)
    cc_own = lax.broadcasted_iota(jnp.int32, (rows, LANES), 1)
    d_own = (rr_own % t_new - cc_own).astype(F32)
    s = lax.dot_general(qmb, k_new, NT_DIMS, preferred_element_type=F32) - slope * d_own
    s_ref[:, past_len:] = jnp.where((d_own >= 0.0) & (cc_own < t_new), s, -jnp.inf)

    s_all = s_ref[...]
    m = jnp.max(s_all, axis=-1, keepdims=True)
    p_all = jnp.exp(s_all - m)
    l = jnp.sum(p_all, axis=-1, keepdims=True)
    pb = p_all.astype(BF16)
    out = jnp.dot(pb[:, past_len:], v_new, preferred_element_type=F32)
    for p in range(n_pages):
        vp = v_refs[p][0].reshape(kvw, PAGE_SIZE).astype(BF16)
        out = out + lax.dot_general(pb[:, p * PAGE_SIZE:(p + 1) * PAGE_SIZE], vp, NT_DIMS,
                                    preferred_element_type=F32)
    out = out / l

    lo8 = lane8 < HEAD_DIM
    for cidx in range(N_HEADS // 2):
        src = slice((cidx // 2) * LANES, (cidx // 2 + 1) * LANES)
        a = out[(2 * cidx) * t_new:(2 * cidx + 1) * t_new, src]
        b = out[(2 * cidx + 1) * t_new:(2 * cidx + 2) * t_new, src]
        if cidx % 2 == 1:
            a = pltpu.roll(a, HEAD_DIM, 1)
        else:
            b = pltpu.roll(b, HEAD_DIM, 1)
        o_ref[0, :, cidx * LANES:(cidx + 1) * LANES] = jnp.where(lo8, a, b)


def _attn_sample(q, k_new, v_new, cache_kt, cache_vt, page_table, t_new):
    n_seq, n_pages = page_table.shape
    d = q.shape[-1]
    kvw = k_new.shape[-1]
    rows = N_HEADS * t_new
    page_spec = lambda p: pl.BlockSpec((1, N_KV_HEADS, HEAD_DIM, PAGE_SIZE), lambda s, pt: (pt[s, p], 0, 0, 0))
    grid_spec = pltpu.PrefetchScalarGridSpec(
        num_scalar_prefetch=1,
        grid=(n_seq,),
        in_specs=[pl.BlockSpec((1, t_new, d), lambda s, pt: (0, s, 0)),
                  pl.BlockSpec((1, t_new, kvw), lambda s, pt: (0, s, 0)),
                  pl.BlockSpec((1, t_new, kvw), lambda s, pt: (0, s, 0))]
        + [page_spec(p) for p in range(n_pages)] + [page_spec(p) for p in range(n_pages)],
        out_specs=pl.BlockSpec((1, t_new, d), lambda s, pt: (0, s, 0)),
        scratch_shapes=[pltpu.VMEM((rows, n_pages * PAGE_SIZE + LANES), F32)],
    )
    return pl.pallas_call(
        functools.partial(_attn_sample_kernel, n_pages=n_pages),
        grid_spec=grid_spec,
        out_shape=jax.ShapeDtypeStruct(q.shape, F32),
        compiler_params=_cparams(("arbitrary",), 48),
        name="moba_sample",
    )(page_table, q, k_new, v_new, *([cache_kt] * n_pages), *([cache_vt] * n_pages))


def _head_sum_matrices(n_heads):
    w = n_heads * HEAD_DIM
    head_of_lane = jnp.arange(w) // HEAD_DIM
    r = (head_of_lane[:, None] == jnp.arange(LANES)[None, :]).astype(BF16)
    return r, r.T


def _prep_weights(p):
    bf = lambda x: x.astype(BF16)
    w_in = p['ssm_w_in']
    wdt = jnp.pad(w_in[:, :, D_INNER + CONV_DIM:], ((0, 0), (0, 0), (0, LANES - SSM_HEADS)))
    pad_heads = lambda x: jnp.pad(x, ((0, 0), (0, LANES - SSM_HEADS)))
    head_params = jnp.stack([pad_heads(p['ssm_dt_bias']), pad_heads(p['ssm_a_log'])], axis=1)
    head_params = jnp.pad(head_params, ((0, 0), (0, SUBLANES - 2), (0, 0)))
    router_w = jnp.concatenate([p['moe_w_grp'], p['moe_w_exp']], axis=-1)
    router_w = jnp.pad(router_w, ((0, 0), (0, 0), (0, LANES - router_w.shape[-1])))
    router_b = jnp.concatenate([p['moe_b_grp'], p['moe_b_exp']], axis=-1)
    router_b = jnp.pad(router_b, ((0, 0), (0, LANES - router_b.shape[-1])))[:, None, :]
    rq, rqt = _head_sum_matrices(N_HEADS)
    rk, rkt = _head_sum_matrices(N_KV_HEADS)
    return dict(
        wz=bf(w_in[:, :, :D_INNER]), wxbc=bf(w_in[:, :, D_INNER:D_INNER + CONV_DIM]), wdt=bf(wdt),
        head_params=head_params, d_skip=jnp.repeat(p['ssm_d'], SSM_HEAD_DIM, axis=-1)[:, None, :],
        conv_w=p['ssm_conv_w'], conv_b=p['ssm_conv_b'][:, None, :], ssm_norm_w=p['ssm_norm_w'][:, None, :],
        w_out=bf(p['ssm_w_out']),
        norm_mix_w=p['norm_mix_w'][:, None, :], norm_ffn_w=p['norm_ffn_w'][:, None, :],
        kv_norm_w=p['kv_norm_w'][None, :], wk=bf(p['w_k']), wv=bf(p['w_v']),
        k_norm_w=jnp.tile(p['k_norm_w'], N_KV_HEADS)[None, :],
        wq=bf(p['attn_w_q']), q_norm_w=jnp.tile(p['q_norm_w'], (1, N_HEADS))[:, None, :], wo=bf(p['attn_w_o']),
        router_w=router_w, router_b=router_b,
        w1=bf(p['moe_w1']), w3=bf(p['moe_w3']), w2=bf(p['moe_w2']),
        rq=rq, rqt=rqt, rk=rk, rkt=rkt,
    )


def _trunk(h, mods, kv_mod, w, *, sample, conv0=None, ssm0=None, cache=None):
    bsz, t, _ = h.shape
    tm = min(t, 1024)
    conv_out = []
    st_all = None
    k_new = v_new = kd = vt = kmd = None
    for layer in range(DEPTH):
        sh_m, sc_m, g_m, sh_f, sc_f, g_f = mods[layer]
        nmw = w['norm_mix_w'][layer]
        if layer < N_A_LAYERS:
            z = _nm_matmul(h, nmw, sc_m, sh_m, w['wz'][layer], tm, 1024)
            xbc = _nm_matmul(h, nmw, sc_m, sh_m, w['wxbc'][layer], tm, 1024)
            dtr = _nm_matmul(h, nmw, sc_m, sh_m, w['wdt'][layer], tm, LANES)
            ssd_args = (w['conv_w'][layer], w['conv_b'][layer], w['head_params'][layer], w['d_skip'][layer],
                        w['ssm_norm_w'][layer])
            if sample:
                n_seq, t_new = conv0.shape[1], cache[3]
                seqs = lambda x: x.reshape(n_seq, t_new, x.shape[-1])
                conv_in = jnp.pad(conv0[layer], ((0, 0), (HALO - (CONV_WIDTH - 1), 0), (0, 0)))
                h0_all = ssm0.reshape(N_A_LAYERS, n_seq, D_INNER, D_STATE)
                g, conv8, st_all = _ssd(seqs(z), seqs(xbc), seqs(dtr), conv_in, h0_all, st_all, layer,
                                        *ssd_args, g_dtype=F32)
                g = g.reshape(1, t, D_INNER)
            else:
                conv_in = jnp.zeros((bsz, HALO, CONV_DIM), F32)
                g, conv8, st_all = _ssd(z, xbc, dtr, conv_in, None, st_all, layer, *ssd_args, g_dtype=BF16)
            conv_out.append(conv8[:, HALO - (CONV_WIDTH - 1):])
            h = _mm_res(g, w['w_out'][layer], h, g_m, tm)
        else:
            j = layer - N_A_LAYERS
            if sample:
                q = _q_proj(h, nmw, sc_m, sh_m, w['wq'][j], w['q_norm_w'][j], w['rq'], w['rqt'], tm, F32)
                o = _attn_sample(q, k_new, v_new, cache[0], cache[1], cache[2], cache[3])
            else:
                q = _q_proj(h, nmw, sc_m, sh_m, w['wq'][j], w['q_norm_w'][j], w['rq'], w['rqt'], tm, BF16,
                            out_scale=LOG2_E)
                o = _attn_prompt(q, kd, vt, kmd)
            h = _mm_res(o, w['wo'][j], h, g_m, tm)
        h = _moe(h, w['norm_ffn_w'][layer], sc_f, sh_f, g_f, w['router_w'][layer], w['router_b'][layer],
                 w['w1'][layer], w['w3'][layer], w['w2'][layer], tm)
        if layer == N_A_LAYERS - 1:
            kv_tm = min(t, 512)
            res = _kv_proj(h, w['kv_norm_w'], kv_mod[1], kv_mod[0], w['wk'], w['wv'], w['k_norm_w'],
                           w['rk'], w['rkt'], kv_tm, with_dup=not sample)
            k_new, v_new = res[0], res[1]
            if not sample:
                kd, vt = res[2], res[3]
                kmd = res[4].reshape(bsz, N_SPLIT, t // MOBA_BLOCK, N_KV_HEADS * LANES)
                kmd = jnp.pad(kmd, ((0, 0), (0, 0), (0, -kmd.shape[2] % BF16_ROWS), (0, 0)))
                kmd = kmd.reshape(bsz, -1, N_KV_HEADS * LANES).astype(BF16)
    ssm_out = st_all.reshape(N_A_LAYERS, st_all.shape[1], SSM_HEADS, SSM_HEAD_DIM, D_STATE)
    return h, jnp.stack(conv_out), ssm_out, k_new, v_new


def kernel(x_prompt, x_sample, state_conv, state_ssm, cache_k, cache_v, page_table, c_prompt, c_sample, w_mod, b_mod, norm_mix_w, norm_ffn_w, ssm_w_in, ssm_conv_w, ssm_conv_b, ssm_dt_bias, ssm_a_log, ssm_d, ssm_norm_w, ssm_w_out, kv_w_mod, kv_b_mod, kv_norm_w, w_k, w_v, k_norm_w, attn_w_q, q_norm_w, attn_w_o, moe_w_grp, moe_b_grp, moe_w_exp, moe_b_exp, moe_w1, moe_w3, moe_w2):
    params = dict(ssm_w_in=ssm_w_in, ssm_conv_w=ssm_conv_w, ssm_conv_b=ssm_conv_b, ssm_dt_bias=ssm_dt_bias,
                  ssm_a_log=ssm_a_log, ssm_d=ssm_d, ssm_norm_w=ssm_norm_w, ssm_w_out=ssm_w_out,
                  norm_mix_w=norm_mix_w, norm_ffn_w=norm_ffn_w, kv_norm_w=kv_norm_w, w_k=w_k, w_v=w_v,
                  k_norm_w=k_norm_w, attn_w_q=attn_w_q, q_norm_w=q_norm_w, attn_w_o=attn_w_o,
                  moe_w_grp=moe_w_grp, moe_b_grp=moe_b_grp, moe_w_exp=moe_w_exp, moe_b_exp=moe_b_exp,
                  moe_w1=moe_w1, moe_w3=moe_w3, moe_w2=moe_w2)
    w = _prep_weights(params)

    bp, seq, d = x_prompt.shape
    n_seq, t_new, _ = x_sample.shape
    n_pages = page_table.shape[1]
    past_len = n_pages * PAGE_SIZE
    assert seq % MOBA_BLOCK == 0 and seq // MOBA_BLOCK <= LANES
    assert past_len % MOBA_BLOCK == 0 and t_new <= MOBA_BLOCK and t_new % SUBLANES == 0

    n_c = bp + n_seq
    n_c_pad = -(-n_c // SUBLANES) * SUBLANES
    c_all = jnp.pad(jnp.concatenate([c_prompt, c_sample], axis=0), ((0, n_c_pad - n_c), (0, 0)))
    mod_all = _mod_vectors(c_all, w_mod, b_mod[:, None, :], 1536)
    kv_all = _mod_vectors(c_all, kv_w_mod[None], kv_b_mod[None, None, :], 1024)[0]

    def group_mods(lo, hi, per_token_repeat):
        def shape(x):
            if per_token_repeat:
                return jnp.repeat(x, per_token_repeat, axis=0)[None]
            return x[:, None, :]
        mods = [[shape(mod_all[l, lo:hi, k * d:(k + 1) * d]) for k in range(6)] for l in range(DEPTH)]
        kvm = [shape(kv_all[lo:hi, k * d:(k + 1) * d]) for k in range(2)]
        return mods, kvm

    mods_p, kv_p = group_mods(0, bp, 0)
    y_p, conv_p, ssm_p, k_p, v_p = _trunk(x_prompt, mods_p, kv_p, w, sample=False)

    mods_s, kv_s = group_mods(bp, n_c, t_new)
    kvw = N_KV_HEADS * HEAD_DIM
    cache = (jnp.transpose(cache_k, (0, 2, 3, 1)), jnp.transpose(cache_v, (0, 2, 3, 1)), page_table, t_new)
    y_s, conv_s, ssm_s, k_s, v_s = _trunk(x_sample.reshape(1, n_seq * t_new, d), mods_s, kv_s, w,
                                          sample=True, conv0=state_conv, ssm0=state_ssm, cache=cache)

    heads = lambda x, b, t: x.reshape(b, t, N_KV_HEADS, HEAD_DIM)
    return (y_p, y_s.reshape(n_seq, t_new, d), conv_p, ssm_p, heads(k_p, bp, seq), heads(v_p, bp, seq),
            conv_s, ssm_s, heads(k_s, n_seq, t_new), heads(v_s, n_seq, t_new))
```

```python
import functools

import jax
import jax.numpy as jnp
from jax import lax
from jax.experimental import pallas as pl
from jax.experimental.pallas import tpu as pltpu

F32 = jnp.float32
BF16 = jnp.bfloat16
HIGHEST = lax.Precision.HIGHEST

D_MODEL = 1024
DEPTH = 4
N_A_LAYERS = 2
D_INNER = 2048
SSM_HEADS = 32
SSM_HEAD_DIM = 64
SSM_GROUPS = 4
D_STATE = 128
CONV_WIDTH = 4
CONV_DIM = D_INNER + 2 * SSM_GROUPS * D_STATE
SSD_CHUNK = 128
N_HEADS = 16
HEAD_DIM = 64
N_KV_HEADS = 8
MOBA_BLOCK = 256
MOBA_TOPK = 3
N_EXPERT_GROUPS = 4
EXPERTS_PER_GROUP = 4
N_EXPERTS = 16
D_EXPERT = 256
PAGE_SIZE = 128
EPS = 1e-6

LANES = 128
SUBLANES = 8
BF16_ROWS = 16
MIB = 1024 * 1024
NEG_BIG = -1e30

NT_DIMS = (((1,), (1,)), ((), ()))


def _cparams(sem, vmem_mib):
    return pltpu.CompilerParams(dimension_semantics=sem, vmem_limit_bytes=vmem_mib * MIB)


def _sigmoid(x):
    return 1.0 / (1.0 + jnp.exp(-x))


def _silu(x):
    return x * _sigmoid(x)


def _norm_mod(h, nw, sc, sh):
    ms = jnp.mean(h * h, axis=-1, keepdims=True)
    return (h * lax.rsqrt(ms + EPS)) * nw * (1.0 + sc) + sh


def _split_bf16(x):
    hi = x.astype(BF16)
    lo = (x - hi.astype(F32)).astype(BF16)
    return hi, lo


def _head_rmsnorm(x, r, rt):
    sq = x * x
    hi, lo = _split_bf16(sq)
    ss = jnp.dot(hi, r, preferred_element_type=F32) + jnp.dot(lo, r, preferred_element_type=F32)
    inv = lax.rsqrt(ss * (1.0 / HEAD_DIM) + EPS)
    ih, il = _split_bf16(inv)
    invx = jnp.dot(ih, rt, preferred_element_type=F32) + jnp.dot(il, rt, preferred_element_type=F32)
    return x * invx


def _mod_kernel(c_ref, w_ref, b_ref, o_ref):
    ca = _silu(c_ref[...])
    o_ref[0] = jnp.dot(ca, w_ref[0], precision=HIGHEST, preferred_element_type=F32) + b_ref[0]


def _mod_vectors(c_all, w, b, tn):
    n_layers, d, n = w.shape
    m = c_all.shape[0]
    return pl.pallas_call(
        _mod_kernel,
        grid=(n_layers, n // tn),
        in_specs=[pl.BlockSpec((m, d), lambda l, j: (0, 0)),
                  pl.BlockSpec((1, d, tn), lambda l, j: (l, 0, j)),
                  pl.BlockSpec((1, 1, tn), lambda l, j: (l, 0, j))],
        out_specs=pl.BlockSpec((1, m, tn), lambda l, j: (l, 0, j)),
        out_shape=jax.ShapeDtypeStruct((n_layers, m, n), F32),
        compiler_params=_cparams(("arbitrary", "arbitrary"), 40),
        name="mod_vectors",
    )(c_all, w, b)


def _mod_spec(mod, tm):
    d = mod.shape[-1]
    if mod.shape[1] == 1:
        return pl.BlockSpec((1, 1, d), lambda b, i, *_: (b, 0, 0))
    return pl.BlockSpec((1, tm, d), lambda b, i, *_: (b, i, 0))


def _nm_matmul_kernel(h_ref, nw_ref, sc_ref, sh_ref, w_ref, o_ref, u_ref):
    @pl.when(pl.program_id(2) == 0)
    def _():
        u_ref[...] = _norm_mod(h_ref[0], nw_ref[...], sc_ref[0], sh_ref[0]).astype(BF16)

    o_ref[0] = jnp.dot(u_ref[...], w_ref[...], preferred_element_type=F32)


def _nm_matmul(h, nw, sc, sh, w, tm, tn):
    bsz, t, d = h.shape
    n = w.shape[1]
    return pl.pallas_call(
        _nm_matmul_kernel,
        grid=(bsz, t // tm, n // tn),
        in_specs=[pl.BlockSpec((1, tm, d), lambda b, i, j: (b, i, 0)),
                  pl.BlockSpec((1, d), lambda b, i, j: (0, 0)),
                  _mod_spec(sc, tm), _mod_spec(sh, tm),
                  pl.BlockSpec((d, tn), lambda b, i, j: (0, j))],
        out_specs=pl.BlockSpec((1, tm, tn), lambda b, i, j: (b, i, j)),
        out_shape=jax.ShapeDtypeStruct((bsz, t, n), F32),
        scratch_shapes=[pltpu.VMEM((tm, d), BF16)],
        compiler_params=_cparams(("parallel", "parallel", "arbitrary"), 48),
        name="norm_mod_matmul",
    )(h, nw, sc, sh, w)


def _in_proj_kernel(h_ref, nw_ref, sc_ref, sh_ref, wz_ref, wx_ref, wd_ref, z_ref, xbc_ref, dt_ref):
    u = _norm_mod(h_ref[0], nw_ref[...], sc_ref[0], sh_ref[0]).astype(BF16)
    z_ref[0] = jnp.dot(u, wz_ref[...], preferred_element_type=F32)
    xbc_ref[0] = jnp.dot(u, wx_ref[...], preferred_element_type=F32)
    dt_ref[0] = jnp.dot(u, wd_ref[...], preferred_element_type=F32)


def _in_proj(h, nw, sc, sh, wz, wxbc, wdt, tm):
    bsz, t, d = h.shape
    const = lambda w: pl.BlockSpec(w.shape, lambda b, i: (0, 0))
    out = lambda w: pl.BlockSpec((1, tm, w.shape[1]), lambda b, i: (b, i, 0))
    return pl.pallas_call(
        _in_proj_kernel,
        grid=(bsz, t // tm),
        in_specs=[pl.BlockSpec((1, tm, d), lambda b, i: (b, i, 0)), pl.BlockSpec((1, d), lambda b, i: (0, 0)),
                  _mod_spec(sc, tm), _mod_spec(sh, tm), const(wz), const(wxbc), const(wdt)],
        out_specs=[out(wz), out(wxbc), out(wdt)],
        out_shape=[jax.ShapeDtypeStruct((bsz, t, w.shape[1]), F32) for w in (wz, wxbc, wdt)],
        compiler_params=_cparams(("parallel", "parallel"), 56),
        name="ssm_in_proj",
    )(h, nw, sc, sh, wz, wxbc, wdt)


def _mm_res_kernel(a_ref, w_ref, h_ref, g_ref, o_ref):
    acc = jnp.dot(a_ref[0].astype(BF16), w_ref[...], preferred_element_type=F32)
    o_ref[0] = h_ref[0] + g_ref[0] * acc


def _mm_res(a, w, h, gate, tm):
    bsz, t, k = a.shape
    d = w.shape[1]
    return pl.pallas_call(
        _mm_res_kernel,
        grid=(bsz, t // tm),
        in_specs=[pl.BlockSpec((1, tm, k), lambda b, i: (b, i, 0)),
                  pl.BlockSpec((k, d), lambda b, i: (0, 0)),
                  pl.BlockSpec((1, tm, d), lambda b, i: (b, i, 0)),
                  _mod_spec(gate, tm)],
        out_specs=pl.BlockSpec((1, tm, d), lambda b, i: (b, i, 0)),
        out_shape=jax.ShapeDtypeStruct((bsz, t, d), F32),
        compiler_params=_cparams(("parallel", "parallel"), 48),
        name="matmul_residual",
    )(a, w, h, gate)


HALO = SUBLANES
XC_TILE = 512


def _ssd_kernel(z_ref, xbc_ref, dt_ref, cs_ref, h0_ref, cw_ref, cb_ref, hp_ref, dsk_ref, nw_ref,
                *rest, valid, has_init, has_prev):
    g_ref, cso_ref, st_ref, xs_ref, xc_ref, xt_ref, y_ref = rest[1:] if has_prev else rest
    L = SSD_CHUNK
    c = pl.program_id(1)

    @pl.when(c == 0)
    def _init():
        xs_ref[0:HALO, :] = cs_ref[0]
        if valid < L:
            xc_ref[...] = jnp.zeros_like(xc_ref)
        if has_init:
            st_ref[0] = h0_ref[0]
        else:
            st_ref[0] = jnp.zeros(st_ref.shape[1:], F32)

    xs_ref[HALO:HALO + valid, :] = xbc_ref[0]

    for j in range(CONV_DIM // XC_TILE):
        sl = slice(j * XC_TILE, (j + 1) * XC_TILE)
        x_rows = xs_ref[0:HALO + valid, sl]
        acc = x_rows * cw_ref[0:1, sl]
        for k in range(1, CONV_WIDTH):
            acc = pltpu.roll(acc, 1, 0) + x_rows * cw_ref[k:k + 1, sl]
        xc_ref[0:valid, sl] = _silu(acc[HALO:HALO + valid] + cb_ref[:, sl])

    cso_ref[0] = xs_ref[valid:valid + HALO, :]
    xs_ref[0:HALO, :] = xs_ref[valid:valid + HALO, :]

    dtr = dt_ref[0]
    if valid < L:
        dtr = jnp.concatenate([dtr, jnp.zeros((L - valid, LANES), F32)], axis=0)
    xx = dtr + hp_ref[0:1, :]
    dt = jnp.maximum(xx, 0.0) + jnp.log(1.0 + jnp.exp(-jnp.abs(xx)))
    row = lax.broadcasted_iota(jnp.int32, (L, L), 0)
    col = lax.broadcasted_iota(jnp.int32, (L, L), 1)
    if valid < L:
        dt = jnp.where(row < valid, dt, 0.0)
    a = dt * (-jnp.exp(hp_ref[1:2, :]))
    causal = row >= col
    a_cum = jnp.dot(causal.astype(F32), a, precision=HIGHEST, preferred_element_type=F32)
    a_cum_t = a_cum.T
    dt_t = dt.T
    sub_lo = row < HEAD_DIM
    R = L if valid == L else -(-valid // BF16_ROWS) * BF16_ROWS
    row_r = lax.broadcasted_iota(jnp.int32, (R, L), 0)
    col_r = lax.broadcasted_iota(jnp.int32, (R, L), 1)
    causal_r = row_r >= col_r
    lane_lo = col_r < HEAD_DIM

    for j in range(D_INNER // LANES):
        xt_ref[j * LANES:(j + 1) * LANES, :] = xc_ref[:, j * LANES:(j + 1) * LANES].T

    pairs_per_group = SSM_HEADS // SSM_GROUPS // 2
    grp_w = D_INNER // SSM_GROUPS
    for g in range(SSM_GROUPS):
        b_g = xc_ref[:, D_INNER + g * D_STATE:D_INNER + (g + 1) * D_STATE].astype(BF16)
        c_g = xc_ref[:, D_INNER + (SSM_GROUPS + g) * D_STATE:
                     D_INNER + (SSM_GROUPS + g + 1) * D_STATE].astype(BF16)
        cb = lax.dot_general(c_g[0:R], b_g, NT_DIMS, preferred_element_type=F32)
        st_g = st_ref[0, g * grp_w:(g + 1) * grp_w, :]
        y_off = lax.dot_general(c_g[0:R], st_g.astype(BF16), NT_DIMS, preferred_element_type=F32)
        xw_parts, dec_parts = [], []
        for jj in range(pairs_per_group):
            pair = g * pairs_per_group + jj
            s_mats, e_cols, w_rows, d_end = [], [], [], []
            for h in (2 * pair, 2 * pair + 1):
                colb = jnp.broadcast_to(a_cum[0:R, h:h + 1], (R, L))
                rowb = a_cum_t[h:h + 1, :]
                dec = jnp.where(causal_r, jnp.exp(colb - rowb), 0.0)
                s_mats.append((cb * dec * dt_t[h:h + 1, :]).astype(BF16))
                e_cols.append(jnp.exp(colb))
                a_last = a_cum_t[h:h + 1, L - 1:L]
                w_rows.append(dt_t[h:h + 1, :] * jnp.exp(a_last - rowb))
                d_end.append(jnp.broadcast_to(jnp.exp(a_last), (LANES, D_STATE)))
            psl = slice(pair * LANES, (pair + 1) * LANES)
            xp = xc_ref[:, psl]
            xpb = xp.astype(BF16)
            y_d = jnp.where(lane_lo,
                            jnp.dot(s_mats[0], xpb, preferred_element_type=F32),
                            jnp.dot(s_mats[1], xpb, preferred_element_type=F32))
            y_o = y_off[:, jj * LANES:(jj + 1) * LANES] * jnp.where(lane_lo, e_cols[0], e_cols[1])
            y_ref[0:R, psl] = y_d + y_o + dsk_ref[:, psl] * xp[0:R]
            xw_parts.append((xt_ref[psl, :] * jnp.where(sub_lo, w_rows[0], w_rows[1])).astype(BF16))
            dec_parts.append(jnp.where(sub_lo, d_end[0], d_end[1]))
        xw_g = jnp.concatenate(xw_parts, axis=0)
        new = jnp.dot(xw_g, b_g, preferred_element_type=F32)
        st_ref[0, g * grp_w:(g + 1) * grp_w, :] = st_g * jnp.concatenate(dec_parts, axis=0) + new

    zz = z_ref[0]
    gg = y_ref[0:valid, :] * _silu(zz)
    for g in range(SSM_GROUPS):
        sl = slice(g * grp_w, (g + 1) * grp_w)
        seg = gg[:, sl]
        ms = jnp.mean(seg * seg, axis=-1, keepdims=True)
        g_ref[0, :, sl] = (seg * lax.rsqrt(ms + EPS) * nw_ref[:, sl]).astype(g_ref.dtype)


def _ssd(z, xbc, dtr, conv_in, h0_all, st_all, layer, conv_w, conv_b, head_params, d_skip, norm_w, g_dtype):
    bsz, t, _ = z.shape
    L = SSD_CHUNK
    valid = min(L, t)
    assert t % valid == 0 and valid % SUBLANES == 0
    nc = t // valid
    has_init = h0_all is not None
    state_spec = pl.BlockSpec((None, 1, D_INNER, D_STATE), lambda b, c: (layer, b, 0, 0))
    if h0_all is None:
        h0_all = jnp.zeros((1, SUBLANES, D_STATE), F32)
        h0_spec = pl.BlockSpec((1, SUBLANES, D_STATE), lambda b, c: (0, 0, 0))
    else:
        h0_spec = state_spec
    full = lambda shape: pl.BlockSpec(shape, lambda b, c: (0,) * len(shape))
    operands = [z, xbc, dtr, conv_in, h0_all, conv_w, conv_b, head_params, d_skip, norm_w]
    in_specs = [pl.BlockSpec((1, valid, D_INNER), lambda b, c: (b, c, 0)),
                pl.BlockSpec((1, valid, CONV_DIM), lambda b, c: (b, c, 0)),
                pl.BlockSpec((1, valid, LANES), lambda b, c: (b, c, 0)),
                pl.BlockSpec((1, HALO, CONV_DIM), lambda b, c: (b, 0, 0)),
                h0_spec,
                full((CONV_WIDTH, CONV_DIM)), full((1, CONV_DIM)), full((SUBLANES, LANES)),
                full((1, D_INNER)), full((1, D_INNER))]
    aliases = {}
    if st_all is not None:
        aliases = {len(operands): 2}
        operands.append(st_all)
        in_specs.append(pl.BlockSpec(memory_space=pl.ANY))
    return pl.pallas_call(
        functools.partial(_ssd_kernel, valid=valid, has_init=has_init, has_prev=st_all is not None),
        grid=(bsz, nc),
        in_specs=in_specs,
        out_specs=[pl.BlockSpec((1, valid, D_INNER), lambda b, c: (b, c, 0)),
                   pl.BlockSpec((1, HALO, CONV_DIM), lambda b, c: (b, 0, 0)),
                   state_spec],
        out_shape=[jax.ShapeDtypeStruct((bsz, t, D_INNER), g_dtype),
                   jax.ShapeDtypeStruct((bsz, HALO, CONV_DIM), F32),
                   jax.ShapeDtypeStruct((N_A_LAYERS, bsz, D_INNER, D_STATE), F32)],
        input_output_aliases=aliases,
        scratch_shapes=[pltpu.VMEM((L + HALO, CONV_DIM), F32),
                        pltpu.VMEM((L, CONV_DIM), F32),
                        pltpu.VMEM((D_INNER, L), F32),
                        pltpu.VMEM((L, D_INNER), F32)],
        compiler_params=_cparams(("parallel", "arbitrary"), 48),
        name="ssd_chunk_scan",
    )(*operands)


ROUTER_GRP0 = 0
ROUTER_EXP0 = N_EXPERT_GROUPS


def _route(logits):
    lane = lax.broadcasted_iota(jnp.int32, logits.shape, 1)
    big = jnp.int32(2 * LANES)
    is_grp = lane < N_EXPERT_GROUPS
    gl = jnp.where(is_grp, logits, -jnp.inf)
    gmax = jnp.max(gl, axis=-1, keepdims=True)
    gidx = jnp.min(jnp.where(gl == gmax, lane, big), axis=-1, keepdims=True)
    p_grp = 1.0 / jnp.sum(jnp.where(is_grp, jnp.exp(gl - gmax), 0.0), axis=-1, keepdims=True)
    e_rel = lane - ROUTER_EXP0
    in_grp = (e_rel >= gidx * EXPERTS_PER_GROUP) & (e_rel < (gidx + 1) * EXPERTS_PER_GROUP)
    el = jnp.where(in_grp, logits, -jnp.inf)
    m1 = jnp.max(el, axis=-1, keepdims=True)
    i1 = jnp.min(jnp.where(el == m1, lane, big), axis=-1, keepdims=True)
    el2 = jnp.where(lane == i1, -jnp.inf, el)
    m2 = jnp.max(el2, axis=-1, keepdims=True)
    i2 = jnp.min(jnp.where(el2 == m2, lane, big), axis=-1, keepdims=True)
    e2 = jnp.exp(m2 - m1)
    den = 1.0 + e2
    w1 = (1.0 / den) * p_grp
    w2 = (e2 / den) * p_grp
    return jnp.where(lane == i1, w1, jnp.where(lane == i2, w2, 0.0))


def _moe_kernel(h_ref, nw_ref, sc_ref, sh_ref, gf_ref, wr_ref, br_ref, w1_ref, w3_ref, w2_ref,
                o_ref, u_ref, gates_ref, acc_ref):
    step = pl.program_id(2)
    per_step = w1_ref.shape[0]

    @pl.when(step == 0)
    def _():
        u = _norm_mod(h_ref[0], nw_ref[...], sc_ref[0], sh_ref[0])
        logits = jnp.dot(u, wr_ref[...], precision=HIGHEST, preferred_element_type=F32) + br_ref[...]
        gates_ref[...] = _route(logits)
        u_ref[...] = u.astype(BF16)
        acc_ref[...] = jnp.zeros_like(acc_ref)

    u = u_ref[...]
    gates = gates_ref[...]
    lane = lax.broadcasted_iota(jnp.int32, gates.shape, 1)
    acts = []
    for j in range(per_step):
        a = (_silu(jnp.dot(u, w1_ref[j], preferred_element_type=F32))
             * jnp.dot(u, w3_ref[j], preferred_element_type=F32))
        e = step * per_step + j
        gcol = jnp.sum(jnp.where(lane == e + ROUTER_EXP0, gates, 0.0), axis=-1, keepdims=True)
        acts.append((a * gcol).astype(BF16))
    w2 = w2_ref[...].reshape(per_step * w2_ref.shape[1], w2_ref.shape[2])
    acc_ref[...] += jnp.dot(jnp.concatenate(acts, axis=1), w2, preferred_element_type=F32)

    @pl.when(step == pl.num_programs(2) - 1)
    def _():
        o_ref[0] = h_ref[0] + gf_ref[0] * acc_ref[...]


def _moe(h, nw, sc, sh, gf, wr, br, w1, w3, w2, tm):
    bsz, t, d = h.shape
    n_e, _, f = w1.shape
    per_step = EXPERTS_PER_GROUP
    return pl.pallas_call(
        _moe_kernel,
        grid=(bsz, t // tm, n_e // per_step),
        in_specs=[pl.BlockSpec((1, tm, d), lambda b, i, e: (b, i, 0)),
                  pl.BlockSpec((1, d), lambda b, i, e: (0, 0)),
                  _mod_spec(sc, tm), _mod_spec(sh, tm), _mod_spec(gf, tm),
                  pl.BlockSpec((d, LANES), lambda b, i, e: (0, 0)),
                  pl.BlockSpec((1, LANES), lambda b, i, e: (0, 0)),
                  pl.BlockSpec((per_step, d, f), lambda b, i, e: (e, 0, 0)),
                  pl.BlockSpec((per_step, d, f), lambda b, i, e: (e, 0, 0)),
                  pl.BlockSpec((per_step, f, d), lambda b, i, e: (e, 0, 0))],
        out_specs=pl.BlockSpec((1, tm, d), lambda b, i, e: (b, i, 0)),
        out_shape=jax.ShapeDtypeStruct((bsz, t, d), F32),
        scratch_shapes=[pltpu.VMEM((tm, d), BF16), pltpu.VMEM((tm, LANES), F32), pltpu.VMEM((tm, d), F32)],
        compiler_params=_cparams(("parallel", "parallel", "arbitrary"), 48),
        name="hmoe",
    )(h, nw, sc, sh, gf, wr, br, w1, w3, w2)


N_SPLIT = 3
FEAT_POS_LO = HEAD_DIM
FEAT_POS_HI = FEAT_POS_LO + N_SPLIT
FEAT_ONE = FEAT_POS_HI + N_SPLIT


def _split3(x):
    h1 = x.astype(BF16).astype(F32)
    r = x - h1
    h2 = r.astype(BF16).astype(F32)
    h3 = (r - h2).astype(BF16).astype(F32)
    return h1, h2, h3


def _feature_lanes(lane, base, parts, other):
    out = other
    for k, part in enumerate(parts):
        out = jnp.where(lane == base + k, part, out)
    return out


def _key_tiles(x, feat):
    lane = lax.broadcasted_iota(jnp.int32, x.shape, 1)
    rolled = pltpu.roll(x, HEAD_DIM, 1)
    lo = lane < HEAD_DIM
    return jnp.where(lo, x, feat), jnp.where(lo, rolled, feat)


def _kv_kernel(h_ref, nw_ref, sc_ref, sh_ref, wk_ref, wv_ref, knw_ref, r_ref, rt_ref,
               kf_ref, vf_ref, *dup_refs, tm, with_dup):
    u = _norm_mod(h_ref[0], nw_ref[...], sc_ref[0], sh_ref[0]).astype(BF16)
    k = jnp.dot(u, wk_ref[...], preferred_element_type=F32)
    v = jnp.dot(u, wv_ref[...], preferred_element_type=F32)
    kn = _head_rmsnorm(k, r_ref[...], rt_ref[...]) * knw_ref[...]
    if not with_dup:
        kf_ref[0] = kn
        vf_ref[0] = v
    else:
        for hh in range(N_KV_HEADS):
            rows = pl.ds(hh, tm, stride=N_KV_HEADS)
            for src, dst in ((kn, kf_ref), (v, vf_ref)):
                tile = src[:, (hh // 2) * LANES:(hh // 2 + 1) * LANES]
                if hh % 2 == 1:
                    tile = pltpu.roll(tile, HEAD_DIM, 1)
                dst[0, rows, :] = tile[:, :HEAD_DIM]
        kd_ref, vt_ref, km_ref = dup_refs
        lane = lax.broadcasted_iota(jnp.int32, (tm, LANES), 1)
        pos = pl.program_id(1) * tm + lax.broadcasted_iota(jnp.int32, (tm, LANES), 0)
        pos_lo = pos % MOBA_BLOCK
        feat = jnp.zeros((tm, LANES), F32)
        feat = _feature_lanes(lane, FEAT_POS_LO, [pos_lo.astype(F32)] * N_SPLIT, feat)
        feat = _feature_lanes(lane, FEAT_POS_HI, [(pos - pos_lo).astype(F32)] * N_SPLIT, feat)
        feat = _feature_lanes(lane, FEAT_ONE, [jnp.ones((tm, LANES), F32)] * N_SPLIT, feat)
        lane_row = lax.broadcasted_iota(jnp.int32, (1, LANES), 1)
        for cidx in range(N_KV_HEADS // 2):
            d0, d1 = _key_tiles(kn[:, cidx * LANES:(cidx + 1) * LANES], feat)
            for hh, dd in ((2 * cidx, d0), (2 * cidx + 1, d1)):
                hsl = slice(hh * LANES, (hh + 1) * LANES)
                kd_ref[0, :, hsl] = dd.astype(BF16)
                for blk in range(tm // MOBA_BLOCK):
                    mean = jnp.mean(dd[blk * MOBA_BLOCK:(blk + 1) * MOBA_BLOCK], axis=0, keepdims=True)
                    for part, piece in enumerate(_split3(jnp.where(lane_row < HEAD_DIM, mean, 0.0))):
                        km_ref[0, part, blk, :, hsl] = piece
        for blk in range(tm // MOBA_BLOCK):
            vt_ref[0, blk] = v[blk * MOBA_BLOCK:(blk + 1) * MOBA_BLOCK].T.astype(BF16)


def _kv_proj(h, nw, sc, sh, wk, wv, knw, r, rt, tm, with_dup):
    bsz, t, d = h.shape
    kvw = wk.shape[1]
    const = lambda shape: pl.BlockSpec(shape, lambda b, i: (0,) * len(shape))
    if not with_dup:
        out_specs = [pl.BlockSpec((1, tm, kvw), lambda b, i: (b, i, 0))] * 2
        out_shape = [jax.ShapeDtypeStruct((bsz, t, kvw), F32)] * 2
    else:
        out_specs = [pl.BlockSpec((1, tm * N_KV_HEADS, HEAD_DIM), lambda b, i: (b, i, 0))] * 2
        out_shape = [jax.ShapeDtypeStruct((bsz, t * N_KV_HEADS, HEAD_DIM), F32)] * 2
        dupw = N_KV_HEADS * LANES
        nblk = tm // MOBA_BLOCK
        out_specs += [pl.BlockSpec((1, tm, dupw), lambda b, i: (b, i, 0)),
                      pl.BlockSpec((1, nblk, kvw, MOBA_BLOCK), lambda b, i: (b, i, 0, 0)),
                      pl.BlockSpec((1, N_SPLIT, nblk, 1, dupw), lambda b, i: (b, 0, i, 0, 0))]
        out_shape += [jax.ShapeDtypeStruct((bsz, t, dupw), BF16),
                      jax.ShapeDtypeStruct((bsz, t // MOBA_BLOCK, kvw, MOBA_BLOCK), BF16),
                      jax.ShapeDtypeStruct((bsz, N_SPLIT, t // MOBA_BLOCK, 1, dupw), F32)]
    return pl.pallas_call(
        functools.partial(_kv_kernel, tm=tm, with_dup=with_dup),
        grid=(bsz, t // tm),
        in_specs=[pl.BlockSpec((1, tm, d), lambda b, i: (b, i, 0)),
                  const((1, d)), _mod_spec(sc, tm), _mod_spec(sh, tm),
                  const((d, kvw)), const((d, kvw)), const((1, kvw)),
                  const((kvw, LANES)), const((LANES, kvw))],
        out_specs=out_specs,
        out_shape=out_shape,
        compiler_params=_cparams(("parallel", "parallel"), 48),
        name="shared_kv",
    )(h, nw, sc, sh, wk, wv, knw, r, rt)


def _q_kernel(h_ref, nw_ref, sc_ref, sh_ref, wq_ref, qnw_ref, r_ref, rt_ref, q_ref, *, out_scale):
    u = _norm_mod(h_ref[0], nw_ref[...], sc_ref[0], sh_ref[0]).astype(BF16)
    q = jnp.dot(u, wq_ref[...], preferred_element_type=F32)
    q = _head_rmsnorm(q, r_ref[...], rt_ref[...]) * qnw_ref[...]
    if out_scale != 1.0:
        q = q * out_scale
    q_ref[0] = q.astype(q_ref.dtype)


def _q_proj(h, nw, sc, sh, wq, qnw, r, rt, tm, out_dtype, out_scale=1.0):
    bsz, t, d = h.shape
    const = lambda shape: pl.BlockSpec(shape, lambda b, i: (0,) * len(shape))
    return pl.pallas_call(
        functools.partial(_q_kernel, out_scale=out_scale),
        grid=(bsz, t // tm),
        in_specs=[pl.BlockSpec((1, tm, d), lambda b, i: (b, i, 0)),
                  const((1, d)), _mod_spec(sc, tm), _mod_spec(sh, tm),
                  const((d, d)), const((1, d)), const((d, LANES)), const((LANES, d))],
        out_specs=pl.BlockSpec((1, tm, d), lambda b, i: (b, i, 0)),
        out_shape=jax.ShapeDtypeStruct((bsz, t, d), out_dtype),
        compiler_params=_cparams(("parallel", "parallel"), 48),
        name="q_proj",
    )(h, nw, sc, sh, wq, qnw, r, rt)


def _top_blocks(gate, n_valid, axis):
    pos = lax.broadcasted_iota(jnp.int32, gate.shape, axis)
    gm = jnp.where(pos < n_valid, gate, -jnp.inf)
    sel = jnp.zeros(gate.shape, F32)
    for _ in range(MOBA_TOPK):
        mx = jnp.max(gm, axis=axis, keepdims=True)
        cand = (gm == mx) & (mx > -jnp.inf)
        idx = jnp.min(jnp.where(cand, pos, jnp.int32(2 * LANES)), axis=axis, keepdims=True)
        pick = pos == idx
        sel = jnp.where(pick, 1.0, sel)
        gm = jnp.where(pick, -jnp.inf, gm)
    return sel


def _alibi_slope(head):
    return jnp.exp2(-8.0 * (head + 1).astype(F32) / N_HEADS)


ATT_GROUP = 2
ATT_TILES = 2
ACC_ROWS = HEAD_DIM + BF16_ROWS
LOG2_E = 1.4426950408889634


def _attn_prompt_kernel(q_ref, k_ref, vt_ref, km_ref, o_ref, *scratch):
    first_block = pl.program_id(2) * ATT_TILES
    for tile in range(ATT_TILES):
        rows = slice(tile * MOBA_BLOCK, (tile + 1) * MOBA_BLOCK)
        _attn_tile(first_block + tile, tile == 0, q_ref.at[0, rows, :], k_ref, vt_ref, km_ref,
                   o_ref.at[0, rows, :], *scratch)


def _attn_tile(i, may_be_first, q_ref, k_ref, vt_ref, km_ref, o_ref,
               qt_ref, fq_ref, sa_ref, sb_ref, sel_ref, m_ref, acc_ref):
    g = pl.program_id(1)
    blk = MOBA_BLOCK
    cols = 2 * blk
    grp = ATT_GROUP

    l2 = lax.broadcasted_iota(jnp.int32, (cols, LANES), 1)

    def _features():
        r2 = lax.broadcasted_iota(jnp.int32, (cols, LANES), 0)
        second = r2 >= blk
        slope = _alibi_slope(2 * g + second.astype(jnp.int32)) * LOG2_E
        offs_q = jnp.where(second, r2 - blk, r2).astype(F32)
        slope_parts = _split3(slope)
        feat = jnp.zeros((cols, LANES), F32)
        feat = _feature_lanes(l2, FEAT_POS_LO, slope_parts, feat)
        feat = _feature_lanes(l2, FEAT_POS_HI, slope_parts, feat)
        fq_ref[...] = _feature_lanes(l2, FEAT_ONE, _split3(-slope * offs_q), feat)

    if may_be_first:
        pl.when(i == 0)(_features)

    c1 = lax.broadcasted_iota(jnp.int32, (1, cols), 1)
    col_const = -(_alibi_slope(2 * g + (c1 >= blk).astype(jnp.int32)) * LOG2_E) * (i * blk).astype(F32)

    qt = q_ref[...].astype(F32)
    q2 = jnp.concatenate([qt, pltpu.roll(qt, HEAD_DIM, 1)], axis=0)
    q2t = jnp.where(l2 < HEAD_DIM, q2, fq_ref[...]).T
    is_q = lax.broadcasted_iota(jnp.int32, (LANES, 1), 0) < HEAD_DIM
    qt_ref[...] = (q2t * jnp.where(is_q, HEAD_DIM ** -0.5, 1.0)).astype(BF16)
    gate3 = jnp.dot(km_ref[0], qt_ref[...], preferred_element_type=F32)
    nbp = sel_ref.shape[0]
    gate = gate3[0:nbp] + gate3[nbp:2 * nbp] + gate3[2 * nbp:3 * nbp]
    sel_ref[...] = _top_blocks(gate, i, 0)

    def scores(n):
        start = pl.multiple_of(n * blk, blk)
        return jnp.dot(k_ref[0, pl.ds(start, blk), :], qt_ref[...], preferred_element_type=F32)

    ones_rows = jnp.ones((ACC_ROWS - HEAD_DIM, blk), BF16)

    def block_stats(s, n):
        mb = jnp.max(s, axis=0, keepdims=True)
        p = jnp.exp2((s - mb).astype(BF16))
        vt_aug = jnp.concatenate([vt_ref[0, n], ones_rows], axis=0)
        return mb, jnp.dot(vt_aug, p, preferred_element_type=F32)

    def merge(parts):
        m_old = m_ref[...]
        m_new = m_old
        for mb, _ in parts:
            m_new = jnp.maximum(m_new, mb)
        acc_new = jnp.exp2(m_old - m_new) * acc_ref[...]
        for mb, ab in parts:
            acc_new = acc_new + jnp.exp2(mb - m_new) * ab
        m_ref[...] = m_new
        acc_ref[...] = acc_new

    rr = lax.broadcasted_iota(jnp.int32, (blk, cols), 0)
    cc = lax.broadcasted_iota(jnp.int32, (blk, cols), 1)
    visible = jnp.where(cc >= blk, cc - blk, cc) >= rr
    m0, a0 = block_stats(jnp.where(visible, scores(i), -jnp.inf), i)
    m_ref[...] = m0 + col_const
    acc_ref[...] = a0

    n_groups = (i + grp - 1) // grp

    def clamp(n_raw):
        return jnp.maximum(jnp.minimum(n_raw, i - 1), 0)

    def score_group(kk, s_ref):
        for b in range(grp):
            s_ref[b * blk:(b + 1) * blk, :] = scores(clamp(kk * grp + b))

    def softmax_group(kk, s_ref):
        parts = []
        for b in range(grp):
            n_raw = kk * grp + b
            n = clamp(n_raw)
            mb, ab = block_stats(s_ref[b * blk:(b + 1) * blk, :], n)
            counted = sel_ref[pl.ds(n, 1), :] * (n_raw < i).astype(F32) > 0.0
            parts.append((jnp.where(counted, mb + col_const, NEG_BIG), ab))
        merge(parts)

    score_group(0, sa_ref)

    def step(kk, carry):
        score_group(2 * kk + 1, sb_ref)
        softmax_group(2 * kk, sa_ref)
        score_group(2 * kk + 2, sa_ref)
        softmax_group(2 * kk + 1, sb_ref)
        return carry

    lax.fori_loop(0, n_groups // 2, step, 0)

    @pl.when(n_groups % 2 == 1)
    def _last():
        softmax_group(n_groups - 1, sa_ref)

    o = acc_ref[0:HEAD_DIM, :] / acc_ref[HEAD_DIM:HEAD_DIM + 1, :]
    o_ref[...] = jnp.concatenate([o[:, 0:blk], o[:, blk:cols]], axis=0).T.astype(o_ref.dtype)


def _attn_prompt(q, kd, vt, kmd):
    bsz, t, d = q.shape
    blk = MOBA_BLOCK
    nb = t // blk
    nb_pad = kmd.shape[1] // N_SPLIT
    cols = 2 * blk
    return pl.pallas_call(
        _attn_prompt_kernel,
        grid=(bsz, N_KV_HEADS, nb // ATT_TILES),
        in_specs=[pl.BlockSpec((1, ATT_TILES * blk, LANES), lambda b, g, i: (b, i, g)),
                  pl.BlockSpec((1, t, LANES), lambda b, g, i: (b, 0, g)),
                  pl.BlockSpec((1, nb, HEAD_DIM, blk), lambda b, g, i: (b, 0, g, 0)),
                  pl.BlockSpec((1, N_SPLIT * nb_pad, LANES), lambda b, g, i: (b, 0, g))],
        out_specs=pl.BlockSpec((1, ATT_TILES * blk, LANES), lambda b, g, i: (b, i, g)),
        out_shape=jax.ShapeDtypeStruct((bsz, t, d), BF16),
        scratch_shapes=[pltpu.VMEM((LANES, cols), BF16), pltpu.VMEM((cols, LANES), F32),
                        pltpu.VMEM((ATT_GROUP * blk, cols), F32), pltpu.VMEM((ATT_GROUP * blk, cols), F32),
                        pltpu.VMEM((nb_pad, cols), F32), pltpu.VMEM((1, cols), F32),
                        pltpu.VMEM((ACC_ROWS, cols), F32)],
        compiler_params=_cparams(("parallel", "parallel", "arbitrary"), 48),
        name="moba_prompt",
    )(q, kd, vt, kmd)


def _attn_sample_kernel(pt_ref, q_ref, kn_ref, vn_ref, *rest, n_pages):
    del pt_ref
    k_refs = rest[:n_pages]
    v_refs = rest[n_pages:2 * n_pages]
    o_ref = rest[2 * n_pages]
    s_ref = rest[2 * n_pages + 1]
    t_new = q_ref.shape[1]
    past_len = n_pages * PAGE_SIZE
    kvw = N_KV_HEADS * HEAD_DIM
    rows = N_HEADS * t_new
    n_past_blk = past_len // MOBA_BLOCK
    pages_per_blk = MOBA_BLOCK // PAGE_SIZE
    q = q_ref[0]
    lane8 = lax.broadcasted_iota(jnp.int32, (t_new, LANES), 1)
    zeros8 = jnp.zeros((t_new, LANES), F32)
    row_blocks = []
    for h in range(N_HEADS):
        g = h // 2
        tile = q[:, g * LANES:(g + 1) * LANES]
        if h % 2 != g % 2:
            tile = pltpu.roll(tile, HEAD_DIM, 1)
        keep = (lane8 < HEAD_DIM) if g % 2 == 0 else (lane8 >= HEAD_DIM)
        tile = jnp.where(keep, tile, 0.0)
        row_blocks.append(jnp.concatenate([tile if cidx == g // 2 else zeros8 for cidx in range(kvw // LANES)], axis=1))
    qm = jnp.concatenate(row_blocks, axis=0)
    qmb = (qm * (HEAD_DIM ** -0.5)).astype(BF16)

    r1 = lax.broadcasted_iota(jnp.int32, (rows, 1), 0)
    slope = _alibi_slope(r1 // t_new)
    rr = lax.broadcasted_iota(jnp.int32, (rows, PAGE_SIZE), 0)
    cc = lax.broadcasted_iota(jnp.int32, (rows, PAGE_SIZE), 1)
    d_page0 = (past_len + rr % t_new - cc).astype(F32)

    lane_b = lax.broadcasted_iota(jnp.int32, (kvw, LANES), 1)
    kmt = jnp.zeros((kvw, LANES), F32)
    for n in range(n_past_blk):
        acc = k_refs[n * pages_per_blk][0].reshape(kvw, PAGE_SIZE)
        for pp in range(1, pages_per_blk):
            acc = acc + k_refs[n * pages_per_blk + pp][0].reshape(kvw, PAGE_SIZE)
        mean = jnp.sum(acc, axis=1, keepdims=True) * (1.0 / MOBA_BLOCK)
        kmt = jnp.where(lane_b == n, mean, kmt)
    gate = jnp.dot(qm, kmt, precision=HIGHEST, preferred_element_type=F32)
    sel = _top_blocks(gate, n_past_blk, 1)
    lane_s = lax.broadcasted_iota(jnp.int32, sel.shape, 1)

    for p in range(n_pages):
        kp = k_refs[p][0].reshape(kvw, PAGE_SIZE).astype(BF16)
        s = jnp.dot(qmb, kp, preferred_element_type=F32)
        s = s - slope * (d_page0 - float(p * PAGE_SIZE))
        picked = jnp.sum(jnp.where(lane_s == p // pages_per_blk, sel, 0.0), axis=-1, keepdims=True) > 0.0
        s_ref[:, p * PAGE_SIZE:(p + 1) * PAGE_SIZE] = jnp.where(picked, s, -jnp.inf)
    pad = jnp.zeros((LANES - t_new, kvw), F32)
    k_new = jnp.concatenate([kn_ref[0], pad], axis=0).astype(BF16)
    v_new = jnp.concatenate([vn_ref[0], pad], axis=0).astype(BF16)
    rr_own = lax.broadcasted_iota(jnp.int32, (rows, LANES), 0)
    cc_own = lax.broadcasted_iota(jnp.int32, (rows, LANES), 1)
    d_own = (rr_own % t_new - cc_own).astype(F32)
    s = lax.dot_general(qmb, k_new, NT_DIMS, preferred_element_type=F32) - slope * d_own
    s_ref[:, past_len:] = jnp.where((d_own >= 0.0) & (cc_own < t_new), s, -jnp.inf)

    s_all = s_ref[...]
    m = jnp.max(s_all, axis=-1, keepdims=True)
    p_all = jnp.exp(s_all - m)
    l = jnp.sum(p_all, axis=-1, keepdims=True)
    pb = p_all.astype(BF16)
    out = jnp.dot(pb[:, past_len:], v_new, preferred_element_type=F32)
    for p in range(n_pages):
        vp = v_refs[p][0].reshape(kvw, PAGE_SIZE).astype(BF16)
        out = out + lax.dot_general(pb[:, p * PAGE_SIZE:(p + 1) * PAGE_SIZE], vp, NT_DIMS,
                                    preferred_element_type=F32)
    out = out / l

    lo8 = lane8 < HEAD_DIM
    for cidx in range(N_HEADS // 2):
        src = slice((cidx // 2) * LANES, (cidx // 2 + 1) * LANES)
        a = out[(2 * cidx) * t_new:(2 * cidx + 1) * t_new, src]
        b = out[(2 * cidx + 1) * t_new:(2 * cidx + 2) * t_new, src]
        if cidx % 2 == 1:
            a = pltpu.roll(a, HEAD_DIM, 1)
        else:
            b = pltpu.roll(b, HEAD_DIM, 1)
        o_ref[0, :, cidx * LANES:(cidx + 1) * LANES] = jnp.where(lo8, a, b)


def _attn_sample(q, k_new, v_new, cache_kt, cache_vt, page_table, t_new):
    n_seq, n_pages = page_table.shape
    d = q.shape[-1]
    kvw = k_new.shape[-1]
    rows = N_HEADS * t_new
    page_spec = lambda p: pl.BlockSpec((1, N_KV_HEADS, HEAD_DIM, PAGE_SIZE), lambda s, pt: (pt[s, p], 0, 0, 0))
    grid_spec = pltpu.PrefetchScalarGridSpec(
        num_scalar_prefetch=1,
        grid=(n_seq,),
        in_specs=[pl.BlockSpec((1, t_new, d), lambda s, pt: (0, s, 0)),
                  pl.BlockSpec((1, t_new, kvw), lambda s, pt: (0, s, 0)),
                  pl.BlockSpec((1, t_new, kvw), lambda s, pt: (0, s, 0))]
        + [page_spec(p) for p in range(n_pages)] + [page_spec(p) for p in range(n_pages)],
        out_specs=pl.BlockSpec((1, t_new, d), lambda s, pt: (0, s, 0)),
        scratch_shapes=[pltpu.VMEM((rows, n_pages * PAGE_SIZE + LANES), F32)],
    )
    return pl.pallas_call(
        functools.partial(_attn_sample_kernel, n_pages=n_pages),
        grid_spec=grid_spec,
        out_shape=jax.ShapeDtypeStruct(q.shape, F32),
        compiler_params=_cparams(("arbitrary",), 48),
        name="moba_sample",
    )(page_table, q, k_new, v_new, *([cache_kt] * n_pages), *([cache_vt] * n_pages))


def _head_sum_matrices(n_heads):
    w = n_heads * HEAD_DIM
    head_of_lane = jnp.arange(w) // HEAD_DIM
    r = (head_of_lane[:, None] == jnp.arange(LANES)[None, :]).astype(BF16)
    return r, r.T


def _prep_weights(p):
    bf = lambda x: x.astype(BF16)
    w_in = p['ssm_w_in']
    wdt = jnp.pad(w_in[:, :, D_INNER + CONV_DIM:], ((0, 0), (0, 0), (0, LANES - SSM_HEADS)))
    pad_heads = lambda x: jnp.pad(x, ((0, 0), (0, LANES - SSM_HEADS)))
    head_params = jnp.stack([pad_heads(p['ssm_dt_bias']), pad_heads(p['ssm_a_log'])], axis=1)
    head_params = jnp.pad(head_params, ((0, 0), (0, SUBLANES - 2), (0, 0)))
    router_w = jnp.concatenate([p['moe_w_grp'], p['moe_w_exp']], axis=-1)
    router_w = jnp.pad(router_w, ((0, 0), (0, 0), (0, LANES - router_w.shape[-1])))
    router_b = jnp.concatenate([p['moe_b_grp'], p['moe_b_exp']], axis=-1)
    router_b = jnp.pad(router_b, ((0, 0), (0, LANES - router_b.shape[-1])))[:, None, :]
    rq, rqt = _head_sum_matrices(N_HEADS)
    rk, rkt = _head_sum_matrices(N_KV_HEADS)
    return dict(
        wz=bf(w_in[:, :, :D_INNER]), wxbc=bf(w_in[:, :, D_INNER:D_INNER + CONV_DIM]), wdt=bf(wdt),
        head_params=head_params, d_skip=jnp.repeat(p['ssm_d'], SSM_HEAD_DIM, axis=-1)[:, None, :],
        conv_w=p['ssm_conv_w'], conv_b=p['ssm_conv_b'][:, None, :], ssm_norm_w=p['ssm_norm_w'][:, None, :],
        w_out=bf(p['ssm_w_out']),
        norm_mix_w=p['norm_mix_w'][:, None, :], norm_ffn_w=p['norm_ffn_w'][:, None, :],
        kv_norm_w=p['kv_norm_w'][None, :], wk=bf(p['w_k']), wv=bf(p['w_v']),
        k_norm_w=jnp.tile(p['k_norm_w'], N_KV_HEADS)[None, :],
        wq=bf(p['attn_w_q']), q_norm_w=jnp.tile(p['q_norm_w'], (1, N_HEADS))[:, None, :], wo=bf(p['attn_w_o']),
        router_w=router_w, router_b=router_b,
        w1=bf(p['moe_w1']), w3=bf(p['moe_w3']), w2=bf(p['moe_w2']),
        rq=rq, rqt=rqt, rk=rk, rkt=rkt,
    )


def _trunk(h, mods, kv_mod, w, *, sample, conv0=None, ssm0=None, cache=None):
    bsz, t, _ = h.shape
    tm = min(t, 1024)
    conv_out = []
    st_all = None
    k_new = v_new = kd = vt = kmd = None
    for layer in range(DEPTH):
        sh_m, sc_m, g_m, sh_f, sc_f, g_f = mods[layer]
        nmw = w['norm_mix_w'][layer]
        if layer < N_A_LAYERS:
            z, xbc, dtr = _in_proj(h, nmw, sc_m, sh_m, w['wz'][layer], w['wxbc'][layer], w['wdt'][layer],
                                   min(t, 512))
            ssd_args = (w['conv_w'][layer], w['conv_b'][layer], w['head_params'][layer], w['d_skip'][layer],
                        w['ssm_norm_w'][layer])
            if sample:
                n_seq, t_new = conv0.shape[1], cache[3]
                seqs = lambda x: x.reshape(n_seq, t_new, x.shape[-1])
                conv_in = jnp.pad(conv0[layer], ((0, 0), (HALO - (CONV_WIDTH - 1), 0), (0, 0)))
                h0_all = ssm0.reshape(N_A_LAYERS, n_seq, D_INNER, D_STATE)
                g, conv8, st_all = _ssd(seqs(z), seqs(xbc), seqs(dtr), conv_in, h0_all, st_all, layer,
                                        *ssd_args, g_dtype=F32)
                g = g.reshape(1, t, D_INNER)
            else:
                conv_in = jnp.zeros((bsz, HALO, CONV_DIM), F32)
                g, conv8, st_all = _ssd(z, xbc, dtr, conv_in, None, st_all, layer, *ssd_args, g_dtype=BF16)
            conv_out.append(conv8[:, HALO - (CONV_WIDTH - 1):])
            h = _mm_res(g, w['w_out'][layer], h, g_m, tm)
        else:
            j = layer - N_A_LAYERS
            if sample:
                q = _q_proj(h, nmw, sc_m, sh_m, w['wq'][j], w['q_norm_w'][j], w['rq'], w['rqt'], tm, F32)
                o = _attn_sample(q, k_new, v_new, cache[0], cache[1], cache[2], cache[3])
            else:
                q = _q_proj(h, nmw, sc_m, sh_m, w['wq'][j], w['q_norm_w'][j], w['rq'], w['rqt'], tm, BF16,
                            out_scale=LOG2_E)
                o = _attn_prompt(q, kd, vt, kmd)
            h = _mm_res(o, w['wo'][j], h, g_m, tm)
        h = _moe(h, w['norm_ffn_w'][layer], sc_f, sh_f, g_f, w['router_w'][layer], w['router_b'][layer],
                 w['w1'][layer], w['w3'][layer], w['w2'][layer], tm)
        if layer == N_A_LAYERS - 1:
            kv_tm = min(t, 512)
            res = _kv_proj(h, w['kv_norm_w'], kv_mod[1], kv_mod[0], w['wk'], w['wv'], w['k_norm_w'],
                           w['rk'], w['rkt'], kv_tm, with_dup=not sample)
            k_new, v_new = res[0], res[1]
            if not sample:
                kd, vt = res[2], res[3]
                kmd = res[4].reshape(bsz, N_SPLIT, t // MOBA_BLOCK, N_KV_HEADS * LANES)
                kmd = jnp.pad(kmd, ((0, 0), (0, 0), (0, -kmd.shape[2] % BF16_ROWS), (0, 0)))
                kmd = kmd.reshape(bsz, -1, N_KV_HEADS * LANES).astype(BF16)
    ssm_out = st_all.reshape(N_A_LAYERS, st_all.shape[1], SSM_HEADS, SSM_HEAD_DIM, D_STATE)
    return h, jnp.stack(conv_out), ssm_out, k_new, v_new


def kernel(x_prompt, x_sample, state_conv, state_ssm, cache_k, cache_v, page_table, c_prompt, c_sample, w_mod, b_mod, norm_mix_w, norm_ffn_w, ssm_w_in, ssm_conv_w, ssm_conv_b, ssm_dt_bias, ssm_a_log, ssm_d, ssm_norm_w, ssm_w_out, kv_w_mod, kv_b_mod, kv_norm_w, w_k, w_v, k_norm_w, attn_w_q, q_norm_w, attn_w_o, moe_w_grp, moe_b_grp, moe_w_exp, moe_b_exp, moe_w1, moe_w3, moe_w2):
    params = dict(ssm_w_in=ssm_w_in, ssm_conv_w=ssm_conv_w, ssm_conv_b=ssm_conv_b, ssm_dt_bias=ssm_dt_bias,
                  ssm_a_log=ssm_a_log, ssm_d=ssm_d, ssm_norm_w=ssm_norm_w, ssm_w_out=ssm_w_out,
                  norm_mix_w=norm_mix_w, norm_ffn_w=norm_ffn_w, kv_norm_w=kv_norm_w, w_k=w_k, w_v=w_v,
                  k_norm_w=k_norm_w, attn_w_q=attn_w_q, q_norm_w=q_norm_w, attn_w_o=attn_w_o,
                  moe_w_grp=moe_w_grp, moe_b_grp=moe_b_grp, moe_w_exp=moe_w_exp, moe_b_exp=moe_b_exp,
                  moe_w1=moe_w1, moe_w3=moe_w3, moe_w2=moe_w2)
    w = _prep_weights(params)

    bp, seq, d = x_prompt.shape
    n_seq, t_new, _ = x_sample.shape
    n_pages = page_table.shape[1]
    past_len = n_pages * PAGE_SIZE
    assert seq % (ATT_TILES * MOBA_BLOCK) == 0
    assert past_len % MOBA_BLOCK == 0 and t_new <= MOBA_BLOCK and t_new % SUBLANES == 0

    n_c = bp + n_seq
    n_c_pad = -(-n_c // SUBLANES) * SUBLANES
    c_all = jnp.pad(jnp.concatenate([c_prompt, c_sample], axis=0), ((0, n_c_pad - n_c), (0, 0)))
    mod_all = _mod_vectors(c_all, w_mod, b_mod[:, None, :], 1536)
    kv_all = _mod_vectors(c_all, kv_w_mod[None], kv_b_mod[None, None, :], 1024)[0]

    def group_mods(lo, hi, per_token_repeat):
        def shape(x):
            if per_token_repeat:
                return jnp.repeat(x, per_token_repeat, axis=0)[None]
            return x[:, None, :]
        mods = [[shape(mod_all[l, lo:hi, k * d:(k + 1) * d]) for k in range(6)] for l in range(DEPTH)]
        kvm = [shape(kv_all[lo:hi, k * d:(k + 1) * d]) for k in range(2)]
        return mods, kvm

    mods_p, kv_p = group_mods(0, bp, 0)
    y_p, conv_p, ssm_p, k_p, v_p = _trunk(x_prompt, mods_p, kv_p, w, sample=False)

    mods_s, kv_s = group_mods(bp, n_c, t_new)
    kvw = N_KV_HEADS * HEAD_DIM
    cache = (jnp.transpose(cache_k, (0, 2, 3, 1)), jnp.transpose(cache_v, (0, 2, 3, 1)), page_table, t_new)
    y_s, conv_s, ssm_s, k_s, v_s = _trunk(x_sample.reshape(1, n_seq * t_new, d), mods_s, kv_s, w,
                                          sample=True, conv0=state_conv, ssm0=state_ssm, cache=cache)

    heads = lambda x, b, t: x.reshape(b, t, N_KV_HEADS, HEAD_DIM)
    return (y_p, y_s.reshape(n_seq, t_new, d), conv_p, ssm_p, heads(k_p, bp, seq), heads(v_p, bp, seq),
            conv_s, ssm_s, heads(k_s, n_seq, t_new), heads(v_s, n_seq, t_new))
```

```python
import functools

import jax
import jax.numpy as jnp
from jax import lax
from jax.experimental import pallas as pl
from jax.experimental.pallas import tpu as pltpu

F32 = jnp.float32
BF16 = jnp.bfloat16
HIGHEST = lax.Precision.HIGHEST

D_MODEL = 1024
DEPTH = 4
N_A_LAYERS = 2
D_INNER = 2048
SSM_HEADS = 32
SSM_HEAD_DIM = 64
SSM_GROUPS = 4
D_STATE = 128
CONV_WIDTH = 4
CONV_DIM = D_INNER + 2 * SSM_GROUPS * D_STATE
SSD_CHUNK = 128
N_HEADS = 16
HEAD_DIM = 64
N_KV_HEADS = 8
MOBA_BLOCK = 256
MOBA_TOPK = 3
N_EXPERT_GROUPS = 4
EXPERTS_PER_GROUP = 4
N_EXPERTS = 16
D_EXPERT = 256
PAGE_SIZE = 128
EPS = 1e-6

LANES = 128
SUBLANES = 8
BF16_ROWS = 16
MIB = 1024 * 1024
NEG_BIG = -1e30

VMEM_MIB = 48
VMEM_BIG_MIB = 56
TOKEN_TILE = 1024
IN_PROJ_TILE = 512
KV_TILE = 512
MOD_COLS = 1024
MOE_EXPERTS_PER_STEP = 8

NT_DIMS = (((1,), (1,)), ((), ()))


def _cparams(sem, vmem_mib=VMEM_MIB):
    return pltpu.CompilerParams(dimension_semantics=sem, vmem_limit_bytes=vmem_mib * MIB)


def _sigmoid(x):
    return 1.0 / (1.0 + jnp.exp(-x))


def _silu(x):
    return x * _sigmoid(x)


def _norm_mod(h, nw, sc, sh):
    ms = jnp.mean(h * h, axis=-1, keepdims=True)
    return (h * lax.rsqrt(ms + EPS)) * nw * (1.0 + sc) + sh


def _split_bf16(x):
    hi = x.astype(BF16)
    lo = (x - hi.astype(F32)).astype(BF16)
    return hi, lo


def _head_rmsnorm(x, r, rt):
    sq = x * x
    hi, lo = _split_bf16(sq)
    ss = jnp.dot(hi, r, preferred_element_type=F32) + jnp.dot(lo, r, preferred_element_type=F32)
    inv = lax.rsqrt(ss * (1.0 / HEAD_DIM) + EPS)
    ih, il = _split_bf16(inv)
    invx = jnp.dot(ih, rt, preferred_element_type=F32) + jnp.dot(il, rt, preferred_element_type=F32)
    return x * invx


def _mod_kernel(c_ref, w_ref, b_ref, o_ref):
    ca = _silu(c_ref[...])
    o_ref[0] = jnp.dot(ca, w_ref[0], precision=HIGHEST, preferred_element_type=F32) + b_ref[0]


def _mod_vectors(c_all, w, b, tn):
    n_layers, d, n = w.shape
    m = c_all.shape[0]
    return pl.pallas_call(
        _mod_kernel,
        grid=(n_layers, n // tn),
        in_specs=[pl.BlockSpec((m, d), lambda l, j: (0, 0)),
                  pl.BlockSpec((1, d, tn), lambda l, j: (l, 0, j)),
                  pl.BlockSpec((1, 1, tn), lambda l, j: (l, 0, j))],
        out_specs=pl.BlockSpec((1, m, tn), lambda l, j: (l, 0, j)),
        out_shape=jax.ShapeDtypeStruct((n_layers, m, n), F32),
        compiler_params=_cparams(("arbitrary", "arbitrary")),
        name="mod_vectors",
    )(c_all, w, b)


def _mod_spec(mod, tm):
    d = mod.shape[-1]
    if mod.shape[1] == 1:
        return pl.BlockSpec((1, 1, d), lambda b, i, *_: (b, 0, 0))
    return pl.BlockSpec((1, tm, d), lambda b, i, *_: (b, i, 0))


def _in_proj_kernel(h_ref, nw_ref, sc_ref, sh_ref, wz_ref, wx_ref, wd_ref, z_ref, xbc_ref, dt_ref):
    u = _norm_mod(h_ref[0], nw_ref[...], sc_ref[0], sh_ref[0]).astype(BF16)
    z_ref[0] = jnp.dot(u, wz_ref[...], preferred_element_type=F32)
    xbc_ref[0] = jnp.dot(u, wx_ref[...], preferred_element_type=F32)
    dt_ref[0] = jnp.dot(u, wd_ref[...], preferred_element_type=F32)


def _in_proj(h, nw, sc, sh, wz, wxbc, wdt, tm):
    bsz, t, d = h.shape
    const = lambda w: pl.BlockSpec(w.shape, lambda b, i: (0, 0))
    out = lambda w: pl.BlockSpec((1, tm, w.shape[1]), lambda b, i: (b, i, 0))
    return pl.pallas_call(
        _in_proj_kernel,
        grid=(bsz, t // tm),
        in_specs=[pl.BlockSpec((1, tm, d), lambda b, i: (b, i, 0)), pl.BlockSpec((1, d), lambda b, i: (0, 0)),
                  _mod_spec(sc, tm), _mod_spec(sh, tm), const(wz), const(wxbc), const(wdt)],
        out_specs=[out(wz), out(wxbc), out(wdt)],
        out_shape=[jax.ShapeDtypeStruct((bsz, t, w.shape[1]), F32) for w in (wz, wxbc, wdt)],
        compiler_params=_cparams(("parallel", "parallel"), VMEM_BIG_MIB),
        name="ssm_in_proj",
    )(h, nw, sc, sh, wz, wxbc, wdt)


def _mm_res_kernel(a_ref, w_ref, h_ref, g_ref, o_ref):
    acc = jnp.dot(a_ref[0].astype(BF16), w_ref[...], preferred_element_type=F32)
    o_ref[0] = h_ref[0] + g_ref[0] * acc


def _mm_res(a, w, h, gate, tm):
    bsz, t, k = a.shape
    d = w.shape[1]
    return pl.pallas_call(
        _mm_res_kernel,
        grid=(bsz, t // tm),
        in_specs=[pl.BlockSpec((1, tm, k), lambda b, i: (b, i, 0)),
                  pl.BlockSpec((k, d), lambda b, i: (0, 0)),
                  pl.BlockSpec((1, tm, d), lambda b, i: (b, i, 0)),
                  _mod_spec(gate, tm)],
        out_specs=pl.BlockSpec((1, tm, d), lambda b, i: (b, i, 0)),
        out_shape=jax.ShapeDtypeStruct((bsz, t, d), F32),
        compiler_params=_cparams(("parallel", "parallel")),
        name="matmul_residual",
    )(a, w, h, gate)


HALO = SUBLANES
XC_TILE = 512


def _ssd_kernel(z_ref, xbc_ref, dt_ref, cs_ref, h0_ref, cw_ref, cb_ref, hp_ref, dsk_ref, nw_ref,
                *rest, valid, has_init, has_prev):
    g_ref, cso_ref, st_ref, xs_ref, xc_ref, xt_ref, y_ref = rest[1:] if has_prev else rest
    L = SSD_CHUNK
    c = pl.program_id(1)

    @pl.when(c == 0)
    def _init():
        xs_ref[0:HALO, :] = cs_ref[0]
        if valid < L:
            xc_ref[...] = jnp.zeros_like(xc_ref)
        if has_init:
            st_ref[0] = h0_ref[0]
        else:
            st_ref[0] = jnp.zeros(st_ref.shape[1:], F32)

    xs_ref[HALO:HALO + valid, :] = xbc_ref[0]

    for j in range(CONV_DIM // XC_TILE):
        sl = slice(j * XC_TILE, (j + 1) * XC_TILE)
        x_rows = xs_ref[0:HALO + valid, sl]
        acc = x_rows * cw_ref[0:1, sl]
        for k in range(1, CONV_WIDTH):
            acc = pltpu.roll(acc, 1, 0) + x_rows * cw_ref[k:k + 1, sl]
        xc_ref[0:valid, sl] = _silu(acc[HALO:HALO + valid] + cb_ref[:, sl])

    cso_ref[0] = xs_ref[valid:valid + HALO, :]
    xs_ref[0:HALO, :] = xs_ref[valid:valid + HALO, :]

    dtr = dt_ref[0]
    if valid < L:
        dtr = jnp.concatenate([dtr, jnp.zeros((L - valid, LANES), F32)], axis=0)
    xx = dtr + hp_ref[0:1, :]
    dt = jnp.maximum(xx, 0.0) + jnp.log(1.0 + jnp.exp(-jnp.abs(xx)))
    row = lax.broadcasted_iota(jnp.int32, (L, L), 0)
    col = lax.broadcasted_iota(jnp.int32, (L, L), 1)
    if valid < L:
        dt = jnp.where(row < valid, dt, 0.0)
    a = dt * (-jnp.exp(hp_ref[1:2, :]))
    causal = row >= col
    a_cum = jnp.dot(causal.astype(F32), a, precision=HIGHEST, preferred_element_type=F32)
    a_cum_t = a_cum.T
    dt_t = dt.T
    sub_lo = row < HEAD_DIM
    R = L if valid == L else -(-valid // BF16_ROWS) * BF16_ROWS
    row_r = lax.broadcasted_iota(jnp.int32, (R, L), 0)
    col_r = lax.broadcasted_iota(jnp.int32, (R, L), 1)
    causal_r = row_r >= col_r
    lane_lo = col_r < HEAD_DIM

    for j in range(D_INNER // LANES):
        xt_ref[j * LANES:(j + 1) * LANES, :] = xc_ref[:, j * LANES:(j + 1) * LANES].T

    pairs_per_group = SSM_HEADS // SSM_GROUPS // 2
    grp_w = D_INNER // SSM_GROUPS
    for g in range(SSM_GROUPS):
        b_g = xc_ref[:, D_INNER + g * D_STATE:D_INNER + (g + 1) * D_STATE].astype(BF16)
        c_g = xc_ref[:, D_INNER + (SSM_GROUPS + g) * D_STATE:
                     D_INNER + (SSM_GROUPS + g + 1) * D_STATE].astype(BF16)
        cb = lax.dot_general(c_g[0:R], b_g, NT_DIMS, preferred_element_type=F32)
        st_g = st_ref[0, g * grp_w:(g + 1) * grp_w, :]
        y_off = lax.dot_general(c_g[0:R], st_g.astype(BF16), NT_DIMS, preferred_element_type=F32)
        xw_parts, dec_parts = [], []
        for jj in range(pairs_per_group):
            pair = g * pairs_per_group + jj
            s_mats, e_cols, w_rows, d_end = [], [], [], []
            for h in (2 * pair, 2 * pair + 1):
                colb = jnp.broadcast_to(a_cum[0:R, h:h + 1], (R, L))
                rowb = a_cum_t[h:h + 1, :]
                dec = jnp.where(causal_r, jnp.exp(colb - rowb), 0.0)
                s_mats.append((cb * dec * dt_t[h:h + 1, :]).astype(BF16))
                e_cols.append(jnp.exp(colb))
                a_last = a_cum_t[h:h + 1, L - 1:L]
                w_rows.append(dt_t[h:h + 1, :] * jnp.exp(a_last - rowb))
                d_end.append(jnp.broadcast_to(jnp.exp(a_last), (LANES, D_STATE)))
            psl = slice(pair * LANES, (pair + 1) * LANES)
            xp = xc_ref[:, psl]
            xpb = xp.astype(BF16)
            y_d = jnp.where(lane_lo,
                            jnp.dot(s_mats[0], xpb, preferred_element_type=F32),
                            jnp.dot(s_mats[1], xpb, preferred_element_type=F32))
            y_o = y_off[:, jj * LANES:(jj + 1) * LANES] * jnp.where(lane_lo, e_cols[0], e_cols[1])
            y_ref[0:R, psl] = y_d + y_o + dsk_ref[:, psl] * xp[0:R]
            xw_parts.append((xt_ref[psl, :] * jnp.where(sub_lo, w_rows[0], w_rows[1])).astype(BF16))
            dec_parts.append(jnp.where(sub_lo, d_end[0], d_end[1]))
        xw_g = jnp.concatenate(xw_parts, axis=0)
        new = jnp.dot(xw_g, b_g, preferred_element_type=F32)
        st_ref[0, g * grp_w:(g + 1) * grp_w, :] = st_g * jnp.concatenate(dec_parts, axis=0) + new

    zz = z_ref[0]
    gg = y_ref[0:valid, :] * _silu(zz)
    for g in range(SSM_GROUPS):
        sl = slice(g * grp_w, (g + 1) * grp_w)
        seg = gg[:, sl]
        ms = jnp.mean(seg * seg, axis=-1, keepdims=True)
        g_ref[0, :, sl] = (seg * lax.rsqrt(ms + EPS) * nw_ref[:, sl]).astype(g_ref.dtype)


def _ssd(z, xbc, dtr, conv_in, h0_all, st_all, layer, conv_w, conv_b, head_params, d_skip, norm_w, g_dtype):
    bsz, t, _ = z.shape
    L = SSD_CHUNK
    valid = min(L, t)
    assert t % valid == 0 and valid % SUBLANES == 0
    nc = t // valid
    has_init = h0_all is not None
    state_spec = pl.BlockSpec((None, 1, D_INNER, D_STATE), lambda b, c: (layer, b, 0, 0))
    if h0_all is None:
        h0_all = jnp.zeros((1, SUBLANES, D_STATE), F32)
        h0_spec = pl.BlockSpec((1, SUBLANES, D_STATE), lambda b, c: (0, 0, 0))
    else:
        h0_spec = state_spec
    full = lambda shape: pl.BlockSpec(shape, lambda b, c: (0,) * len(shape))
    operands = [z, xbc, dtr, conv_in, h0_all, conv_w, conv_b, head_params, d_skip, norm_w]
    in_specs = [pl.BlockSpec((1, valid, D_INNER), lambda b, c: (b, c, 0)),
                pl.BlockSpec((1, valid, CONV_DIM), lambda b, c: (b, c, 0)),
                pl.BlockSpec((1, valid, LANES), lambda b, c: (b, c, 0)),
                pl.BlockSpec((1, HALO, CONV_DIM), lambda b, c: (b, 0, 0)),
                h0_spec,
                full((CONV_WIDTH, CONV_DIM)), full((1, CONV_DIM)), full((SUBLANES, LANES)),
                full((1, D_INNER)), full((1, D_INNER))]
    aliases = {}
    if st_all is not None:
        aliases = {len(operands): 2}
        operands.append(st_all)
        in_specs.append(pl.BlockSpec(memory_space=pl.ANY))
    return pl.pallas_call(
        functools.partial(_ssd_kernel, valid=valid, has_init=has_init, has_prev=st_all is not None),
        grid=(bsz, nc),
        in_specs=in_specs,
        out_specs=[pl.BlockSpec((1, valid, D_INNER), lambda b, c: (b, c, 0)),
                   pl.BlockSpec((1, HALO, CONV_DIM), lambda b, c: (b, 0, 0)),
                   state_spec],
        out_shape=[jax.ShapeDtypeStruct((bsz, t, D_INNER), g_dtype),
                   jax.ShapeDtypeStruct((bsz, HALO, CONV_DIM), F32),
                   jax.ShapeDtypeStruct((N_A_LAYERS, bsz, D_INNER, D_STATE), F32)],
        input_output_aliases=aliases,
        scratch_shapes=[pltpu.VMEM((L + HALO, CONV_DIM), F32),
                        pltpu.VMEM((L, CONV_DIM), F32),
                        pltpu.VMEM((D_INNER, L), F32),
                        pltpu.VMEM((L, D_INNER), F32)],
        compiler_params=_cparams(("parallel", "arbitrary")),
        name="ssd_chunk_scan",
    )(*operands)


ROUTER_GRP0 = 0
ROUTER_EXP0 = N_EXPERT_GROUPS


def _route(logits):
    lane = lax.broadcasted_iota(jnp.int32, logits.shape, 1)
    big = jnp.int32(2 * LANES)
    is_grp = lane < N_EXPERT_GROUPS
    gl = jnp.where(is_grp, logits, -jnp.inf)
    gmax = jnp.max(gl, axis=-1, keepdims=True)
    gidx = jnp.min(jnp.where(gl == gmax, lane, big), axis=-1, keepdims=True)
    p_grp = 1.0 / jnp.sum(jnp.where(is_grp, jnp.exp(gl - gmax), 0.0), axis=-1, keepdims=True)
    e_rel = lane - ROUTER_EXP0
    in_grp = (e_rel >= gidx * EXPERTS_PER_GROUP) & (e_rel < (gidx + 1) * EXPERTS_PER_GROUP)
    el = jnp.where(in_grp, logits, -jnp.inf)
    m1 = jnp.max(el, axis=-1, keepdims=True)
    i1 = jnp.min(jnp.where(el == m1, lane, big), axis=-1, keepdims=True)
    el2 = jnp.where(lane == i1, -jnp.inf, el)
    m2 = jnp.max(el2, axis=-1, keepdims=True)
    i2 = jnp.min(jnp.where(el2 == m2, lane, big), axis=-1, keepdims=True)
    e2 = jnp.exp(m2 - m1)
    den = 1.0 + e2
    w1 = (1.0 / den) * p_grp
    w2 = (e2 / den) * p_grp
    return jnp.where(lane == i1, w1, jnp.where(lane == i2, w2, 0.0))


def _moe_kernel(h_ref, nw_ref, sc_ref, sh_ref, gf_ref, wr_ref, br_ref, w1_ref, w3_ref, w2_ref,
                o_ref, u_ref, gates_ref, acc_ref):
    step = pl.program_id(2)
    per_step = w1_ref.shape[0]

    @pl.when(step == 0)
    def _():
        u = _norm_mod(h_ref[0], nw_ref[...], sc_ref[0], sh_ref[0])
        u_hi, u_lo = _split_bf16(u)
        w_hi, w_lo = _split_bf16(wr_ref[...])
        logits = (jnp.dot(u_hi, w_hi, preferred_element_type=F32) + jnp.dot(u_lo, w_hi, preferred_element_type=F32)
                  + jnp.dot(u_hi, w_lo, preferred_element_type=F32)) + br_ref[...]
        gates_ref[...] = _route(logits)
        u_ref[...] = u_hi
        acc_ref[...] = jnp.zeros_like(acc_ref)

    u = u_ref[...]
    gates = gates_ref[...]
    lane = lax.broadcasted_iota(jnp.int32, gates.shape, 1)
    acts = []
    for j in range(per_step):
        a = (_silu(jnp.dot(u, w1_ref[j], preferred_element_type=F32))
             * jnp.dot(u, w3_ref[j], preferred_element_type=F32))
        e = step * per_step + j
        gcol = jnp.sum(jnp.where(lane == e + ROUTER_EXP0, gates, 0.0), axis=-1, keepdims=True)
        acts.append((a * gcol).astype(BF16))
    w2 = w2_ref[...].reshape(per_step * w2_ref.shape[1], w2_ref.shape[2])
    acc_ref[...] += jnp.dot(jnp.concatenate(acts, axis=1), w2, preferred_element_type=F32)

    @pl.when(step == pl.num_programs(2) - 1)
    def _():
        o_ref[0] = h_ref[0] + gf_ref[0] * acc_ref[...]


def _moe(h, nw, sc, sh, gf, wr, br, w1, w3, w2, tm):
    bsz, t, d = h.shape
    n_e, _, f = w1.shape
    per_step = MOE_EXPERTS_PER_STEP if sc.shape[1] == 1 else EXPERTS_PER_GROUP
    return pl.pallas_call(
        _moe_kernel,
        grid=(bsz, t // tm, n_e // per_step),
        in_specs=[pl.BlockSpec((1, tm, d), lambda b, i, e: (b, i, 0)),
                  pl.BlockSpec((1, d), lambda b, i, e: (0, 0)),
                  _mod_spec(sc, tm), _mod_spec(sh, tm), _mod_spec(gf, tm),
                  pl.BlockSpec((d, LANES), lambda b, i, e: (0, 0)),
                  pl.BlockSpec((1, LANES), lambda b, i, e: (0, 0)),
                  pl.BlockSpec((per_step, d, f), lambda b, i, e: (e, 0, 0)),
                  pl.BlockSpec((per_step, d, f), lambda b, i, e: (e, 0, 0)),
                  pl.BlockSpec((per_step, f, d), lambda b, i, e: (e, 0, 0))],
        out_specs=pl.BlockSpec((1, tm, d), lambda b, i, e: (b, i, 0)),
        out_shape=jax.ShapeDtypeStruct((bsz, t, d), F32),
        scratch_shapes=[pltpu.VMEM((tm, d), BF16), pltpu.VMEM((tm, LANES), F32), pltpu.VMEM((tm, d), F32)],
        compiler_params=_cparams(("parallel", "parallel", "arbitrary"), VMEM_BIG_MIB),
        name="hmoe",
    )(h, nw, sc, sh, gf, wr, br, w1, w3, w2)


N_SPLIT = 3
FEAT_POS_LO = HEAD_DIM
FEAT_POS_HI = FEAT_POS_LO + N_SPLIT
FEAT_ONE = FEAT_POS_HI + N_SPLIT


def _split3(x):
    h1 = x.astype(BF16).astype(F32)
    r = x - h1
    h2 = r.astype(BF16).astype(F32)
    h3 = (r - h2).astype(BF16).astype(F32)
    return h1, h2, h3


def _feature_lanes(lane, base, parts, other):
    out = other
    for k, part in enumerate(parts):
        out = jnp.where(lane == base + k, part, out)
    return out


def _key_tiles(x, feat):
    lane = lax.broadcasted_iota(jnp.int32, x.shape, 1)
    rolled = pltpu.roll(x, HEAD_DIM, 1)
    lo = lane < HEAD_DIM
    return jnp.where(lo, x, feat), jnp.where(lo, rolled, feat)


def _kv_kernel(h_ref, nw_ref, sc_ref, sh_ref, wk_ref, wv_ref, knw_ref, r_ref, rt_ref,
               kf_ref, vf_ref, *dup_refs, tm, with_dup):
    u = _norm_mod(h_ref[0], nw_ref[...], sc_ref[0], sh_ref[0]).astype(BF16)
    k = jnp.dot(u, wk_ref[...], preferred_element_type=F32)
    v = jnp.dot(u, wv_ref[...], preferred_element_type=F32)
    kn = _head_rmsnorm(k, r_ref[...], rt_ref[...]) * knw_ref[...]
    if not with_dup:
        kf_ref[0] = kn
        vf_ref[0] = v
    else:
        for hh in range(N_KV_HEADS):
            rows = pl.ds(hh, tm, stride=N_KV_HEADS)
            for src, dst in ((kn, kf_ref), (v, vf_ref)):
                tile = src[:, (hh // 2) * LANES:(hh // 2 + 1) * LANES]
                if hh % 2 == 1:
                    tile = pltpu.roll(tile, HEAD_DIM, 1)
                dst[0, rows, :] = tile[:, :HEAD_DIM]
        kd_ref, vt_ref, km_ref = dup_refs
        lane = lax.broadcasted_iota(jnp.int32, (tm, LANES), 1)
        pos = pl.program_id(1) * tm + lax.broadcasted_iota(jnp.int32, (tm, LANES), 0)
        pos_lo = pos % MOBA_BLOCK
        feat = jnp.zeros((tm, LANES), F32)
        feat = _feature_lanes(lane, FEAT_POS_LO, [pos_lo.astype(F32)] * N_SPLIT, feat)
        feat = _feature_lanes(lane, FEAT_POS_HI, [(pos - pos_lo).astype(F32)] * N_SPLIT, feat)
        feat = _feature_lanes(lane, FEAT_ONE, [jnp.ones((tm, LANES), F32)] * N_SPLIT, feat)
        lane_row = lax.broadcasted_iota(jnp.int32, (1, LANES), 1)
        for cidx in range(N_KV_HEADS // 2):
            d0, d1 = _key_tiles(kn[:, cidx * LANES:(cidx + 1) * LANES], feat)
            for hh, dd in ((2 * cidx, d0), (2 * cidx + 1, d1)):
                hsl = slice(hh * LANES, (hh + 1) * LANES)
                kd_ref[0, :, hsl] = dd.astype(BF16)
                for blk in range(tm // MOBA_BLOCK):
                    mean = jnp.mean(dd[blk * MOBA_BLOCK:(blk + 1) * MOBA_BLOCK], axis=0, keepdims=True)
                    for part, piece in enumerate(_split3(jnp.where(lane_row < HEAD_DIM, mean, 0.0))):
                        km_ref[0, part, blk, :, hsl] = piece
        for blk in range(tm // MOBA_BLOCK):
            vt_ref[0, blk] = v[blk * MOBA_BLOCK:(blk + 1) * MOBA_BLOCK].T.astype(BF16)


def _kv_proj(h, nw, sc, sh, wk, wv, knw, r, rt, tm, with_dup):
    bsz, t, d = h.shape
    kvw = wk.shape[1]
    const = lambda shape: pl.BlockSpec(shape, lambda b, i: (0,) * len(shape))
    if not with_dup:
        out_specs = [pl.BlockSpec((1, tm, kvw), lambda b, i: (b, i, 0))] * 2
        out_shape = [jax.ShapeDtypeStruct((bsz, t, kvw), F32)] * 2
    else:
        out_specs = [pl.BlockSpec((1, tm * N_KV_HEADS, HEAD_DIM), lambda b, i: (b, i, 0))] * 2
        out_shape = [jax.ShapeDtypeStruct((bsz, t * N_KV_HEADS, HEAD_DIM), F32)] * 2
        dupw = N_KV_HEADS * LANES
        nblk = tm // MOBA_BLOCK
        out_specs += [pl.BlockSpec((1, tm, dupw), lambda b, i: (b, i, 0)),
                      pl.BlockSpec((1, nblk, kvw, MOBA_BLOCK), lambda b, i: (b, i, 0, 0)),
                      pl.BlockSpec((1, N_SPLIT, nblk, 1, dupw), lambda b, i: (b, 0, i, 0, 0))]
        out_shape += [jax.ShapeDtypeStruct((bsz, t, dupw), BF16),
                      jax.ShapeDtypeStruct((bsz, t // MOBA_BLOCK, kvw, MOBA_BLOCK), BF16),
                      jax.ShapeDtypeStruct((bsz, N_SPLIT, t // MOBA_BLOCK, 1, dupw), F32)]
    return pl.pallas_call(
        functools.partial(_kv_kernel, tm=tm, with_dup=with_dup),
        grid=(bsz, t // tm),
        in_specs=[pl.BlockSpec((1, tm, d), lambda b, i: (b, i, 0)),
                  const((1, d)), _mod_spec(sc, tm), _mod_spec(sh, tm),
                  const((d, kvw)), const((d, kvw)), const((1, kvw)),
                  const((kvw, LANES)), const((LANES, kvw))],
        out_specs=out_specs,
        out_shape=out_shape,
        compiler_params=_cparams(("parallel", "parallel")),
        name="shared_kv",
    )(h, nw, sc, sh, wk, wv, knw, r, rt)


def _q_kernel(h_ref, nw_ref, sc_ref, sh_ref, wq_ref, qnw_ref, r_ref, rt_ref, q_ref, *, out_scale):
    u = _norm_mod(h_ref[0], nw_ref[...], sc_ref[0], sh_ref[0]).astype(BF16)
    q = jnp.dot(u, wq_ref[...], preferred_element_type=F32)
    q = _head_rmsnorm(q, r_ref[...], rt_ref[...]) * qnw_ref[...]
    if out_scale != 1.0:
        q = q * out_scale
    q_ref[0] = q.astype(q_ref.dtype)


def _q_proj(h, nw, sc, sh, wq, qnw, r, rt, tm, out_dtype, out_scale=1.0):
    bsz, t, d = h.shape
    const = lambda shape: pl.BlockSpec(shape, lambda b, i: (0,) * len(shape))
    return pl.pallas_call(
        functools.partial(_q_kernel, out_scale=out_scale),
        grid=(bsz, t // tm),
        in_specs=[pl.BlockSpec((1, tm, d), lambda b, i: (b, i, 0)),
                  const((1, d)), _mod_spec(sc, tm), _mod_spec(sh, tm),
                  const((d, d)), const((1, d)), const((d, LANES)), const((LANES, d))],
        out_specs=pl.BlockSpec((1, tm, d), lambda b, i: (b, i, 0)),
        out_shape=jax.ShapeDtypeStruct((bsz, t, d), out_dtype),
        compiler_params=_cparams(("parallel", "parallel")),
        name="q_proj",
    )(h, nw, sc, sh, wq, qnw, r, rt)


def _top_blocks(gate, n_valid, axis):
    pos = lax.broadcasted_iota(jnp.int32, gate.shape, axis)
    gm = jnp.where(pos < n_valid, gate, -jnp.inf)
    sel = jnp.zeros(gate.shape, F32)
    for _ in range(MOBA_TOPK):
        mx = jnp.max(gm, axis=axis, keepdims=True)
        cand = (gm == mx) & (mx > -jnp.inf)
        idx = jnp.min(jnp.where(cand, pos, jnp.int32(2 * LANES)), axis=axis, keepdims=True)
        pick = pos == idx
        sel = jnp.where(pick, 1.0, sel)
        gm = jnp.where(pick, -jnp.inf, gm)
    return sel


def _alibi_slope(head):
    return jnp.exp2(-8.0 * (head + 1).astype(F32) / N_HEADS)


ATT_GROUP = 2
ATT_TILES = 1
ACC_ROWS = HEAD_DIM + BF16_ROWS
LOG2_E = 1.4426950408889634


def _attn_prompt_kernel(q_ref, k_ref, vt_ref, km_ref, o_ref, *scratch):
    first_block = pl.program_id(2) * ATT_TILES
    for tile in range(ATT_TILES):
        rows = slice(tile * MOBA_BLOCK, (tile + 1) * MOBA_BLOCK)
        _attn_tile(first_block + tile, tile == 0, q_ref.at[0, rows, :], k_ref, vt_ref, km_ref,
                   o_ref.at[0, rows, :], *scratch)


def _attn_tile(i, may_be_first, q_ref, k_ref, vt_ref, km_ref, o_ref,
               qt_ref, fq_ref, sa_ref, sb_ref, sel_ref, m_ref, acc_ref):
    g = pl.program_id(1)
    blk = MOBA_BLOCK
    cols = 2 * blk
    grp = ATT_GROUP

    l2 = lax.broadcasted_iota(jnp.int32, (cols, LANES), 1)

    def _features():
        r2 = lax.broadcasted_iota(jnp.int32, (cols, LANES), 0)
        second = r2 >= blk
        slope = _alibi_slope(2 * g + second.astype(jnp.int32)) * LOG2_E
        offs_q = jnp.where(second, r2 - blk, r2).astype(F32)
        slope_parts = _split3(slope)
        feat = jnp.zeros((cols, LANES), F32)
        feat = _feature_lanes(l2, FEAT_POS_LO, slope_parts, feat)
        feat = _feature_lanes(l2, FEAT_POS_HI, slope_parts, feat)
        fq_ref[...] = _feature_lanes(l2, FEAT_ONE, _split3(-slope * offs_q), feat)

    if may_be_first:
        pl.when(i == 0)(_features)

    c1 = lax.broadcasted_iota(jnp.int32, (1, cols), 1)
    col_const = -(_alibi_slope(2 * g + (c1 >= blk).astype(jnp.int32)) * LOG2_E) * (i * blk).astype(F32)

    qt = q_ref[...].astype(F32)
    q2 = jnp.concatenate([qt, pltpu.roll(qt, HEAD_DIM, 1)], axis=0)
    q2t = jnp.where(l2 < HEAD_DIM, q2, fq_ref[...]).T
    is_q = lax.broadcasted_iota(jnp.int32, (LANES, 1), 0) < HEAD_DIM
    qt_ref[...] = (q2t * jnp.where(is_q, HEAD_DIM ** -0.5, 1.0)).astype(BF16)
    gate3 = jnp.dot(km_ref[0], qt_ref[...], preferred_element_type=F32)
    nbp = sel_ref.shape[0]
    gate = gate3[0:nbp] + gate3[nbp:2 * nbp] + gate3[2 * nbp:3 * nbp]
    sel_ref[...] = _top_blocks(gate, i, 0)

    def scores(n):
        start = pl.multiple_of(n * blk, blk)
        return jnp.dot(k_ref[0, pl.ds(start, blk), :], qt_ref[...], preferred_element_type=F32)

    ones_rows = jnp.ones((ACC_ROWS - HEAD_DIM, blk), BF16)

    def block_stats(s, n):
        mb = jnp.max(s, axis=0, keepdims=True)
        p = jnp.exp2((s - mb).astype(BF16))
        vt_aug = jnp.concatenate([vt_ref[0, n], ones_rows], axis=0)
        return mb, jnp.dot(vt_aug, p, preferred_element_type=F32)

    def merge(parts):
        m_old = m_ref[...]
        m_new = m_old
        for mb, _ in parts:
            m_new = jnp.maximum(m_new, mb)
        acc_new = jnp.exp2(m_old - m_new) * acc_ref[...]
        for mb, ab in parts:
            acc_new = acc_new + jnp.exp2(mb - m_new) * ab
        m_ref[...] = m_new
        acc_ref[...] = acc_new

    rr = lax.broadcasted_iota(jnp.int32, (blk, cols), 0)
    cc = lax.broadcasted_iota(jnp.int32, (blk, cols), 1)
    visible = jnp.where(cc >= blk, cc - blk, cc) >= rr
    m0, a0 = block_stats(jnp.where(visible, scores(i), -jnp.inf), i)
    m_ref[...] = m0 + col_const
    acc_ref[...] = a0

    n_groups = (i + grp - 1) // grp

    def clamp(n_raw):
        return jnp.maximum(jnp.minimum(n_raw, i - 1), 0)

    def score_group(kk, s_ref):
        for b in range(grp):
            s_ref[b * blk:(b + 1) * blk, :] = scores(clamp(kk * grp + b))

    def softmax_group(kk, s_ref):
        parts = []
        for b in range(grp):
            n_raw = kk * grp + b
            n = clamp(n_raw)
            mb, ab = block_stats(s_ref[b * blk:(b + 1) * blk, :], n)
            counted = sel_ref[pl.ds(n, 1), :] * (n_raw < i).astype(F32) > 0.0
            parts.append((jnp.where(counted, mb + col_const, NEG_BIG), ab))
        merge(parts)

    score_group(0, sa_ref)

    def step(kk, carry):
        score_group(2 * kk + 1, sb_ref)
        softmax_group(2 * kk, sa_ref)
        score_group(2 * kk + 2, sa_ref)
        softmax_group(2 * kk + 1, sb_ref)
        return carry

    lax.fori_loop(0, n_groups // 2, step, 0)

    @pl.when(n_groups % 2 == 1)
    def _last():
        softmax_group(n_groups - 1, sa_ref)

    o = acc_ref[0:HEAD_DIM, :] / acc_ref[HEAD_DIM:HEAD_DIM + 1, :]
    o_ref[...] = jnp.concatenate([o[:, 0:blk], o[:, blk:cols]], axis=0).T.astype(o_ref.dtype)


def _attn_prompt(q, kd, vt, kmd):
    bsz, t, d = q.shape
    blk = MOBA_BLOCK
    nb = t // blk
    nb_pad = kmd.shape[1] // N_SPLIT
    cols = 2 * blk
    return pl.pallas_call(
        _attn_prompt_kernel,
        grid=(bsz, N_KV_HEADS, nb // ATT_TILES),
        in_specs=[pl.BlockSpec((1, ATT_TILES * blk, LANES), lambda b, g, i: (b, i, g)),
                  pl.BlockSpec((1, t, LANES), lambda b, g, i: (b, 0, g)),
                  pl.BlockSpec((1, nb, HEAD_DIM, blk), lambda b, g, i: (b, 0, g, 0)),
                  pl.BlockSpec((1, N_SPLIT * nb_pad, LANES), lambda b, g, i: (b, 0, g))],
        out_specs=pl.BlockSpec((1, ATT_TILES * blk, LANES), lambda b, g, i: (b, i, g)),
        out_shape=jax.ShapeDtypeStruct((bsz, t, d), BF16),
        scratch_shapes=[pltpu.VMEM((LANES, cols), BF16), pltpu.VMEM((cols, LANES), F32),
                        pltpu.VMEM((ATT_GROUP * blk, cols), F32), pltpu.VMEM((ATT_GROUP * blk, cols), F32),
                        pltpu.VMEM((nb_pad, cols), F32), pltpu.VMEM((1, cols), F32),
                        pltpu.VMEM((ACC_ROWS, cols), F32)],
        compiler_params=_cparams(("parallel", "parallel", "arbitrary")),
        name="moba_prompt",
    )(q, kd, vt, kmd)


def _attn_sample_kernel(pt_ref, q_ref, kn_ref, vn_ref, *rest, n_pages, n_seqs):
    del pt_ref
    t_new = q_ref.shape[1] // n_seqs
    k_refs = rest[:n_seqs * n_pages]
    v_refs = rest[n_seqs * n_pages:2 * n_seqs * n_pages]
    o_ref, s_ref = rest[2 * n_seqs * n_pages:]
    for j in range(n_seqs):
        rows = slice(j * t_new, (j + 1) * t_new)
        pages = slice(j * n_pages, (j + 1) * n_pages)
        _attn_sample_one(q_ref.at[0, rows, :], kn_ref.at[0, rows, :], vn_ref.at[0, rows, :],
                         k_refs[pages], v_refs[pages], o_ref.at[0, rows, :], s_ref.at[j])


def _attn_sample_one(q_ref, kn_ref, vn_ref, k_refs, v_refs, o_ref, s_ref):
    n_pages = len(k_refs)
    t_new = q_ref.shape[0]
    past_len = n_pages * PAGE_SIZE
    kvw = N_KV_HEADS * HEAD_DIM
    rows = N_HEADS * t_new
    n_past_blk = past_len // MOBA_BLOCK
    pages_per_blk = MOBA_BLOCK // PAGE_SIZE
    q = q_ref[...]
    lane8 = lax.broadcasted_iota(jnp.int32, (t_new, LANES), 1)
    zeros8 = jnp.zeros((t_new, LANES), F32)
    row_blocks = []
    for h in range(N_HEADS):
        g = h // 2
        tile = q[:, g * LANES:(g + 1) * LANES]
        if h % 2 != g % 2:
            tile = pltpu.roll(tile, HEAD_DIM, 1)
        keep = (lane8 < HEAD_DIM) if g % 2 == 0 else (lane8 >= HEAD_DIM)
        tile = jnp.where(keep, tile, 0.0)
        row_blocks.append(jnp.concatenate([tile if cidx == g // 2 else zeros8 for cidx in range(kvw // LANES)], axis=1))
    qm = jnp.concatenate(row_blocks, axis=0)
    qmb = (qm * (HEAD_DIM ** -0.5)).astype(BF16)

    r1 = lax.broadcasted_iota(jnp.int32, (rows, 1), 0)
    slope = _alibi_slope(r1 // t_new)
    rr = lax.broadcasted_iota(jnp.int32, (rows, PAGE_SIZE), 0)
    cc = lax.broadcasted_iota(jnp.int32, (rows, PAGE_SIZE), 1)
    d_page0 = (past_len + rr % t_new - cc).astype(F32)

    lane_b = lax.broadcasted_iota(jnp.int32, (kvw, LANES), 1)
    kmt = jnp.zeros((kvw, LANES), F32)
    for n in range(n_past_blk):
        acc = k_refs[n * pages_per_blk][0].reshape(kvw, PAGE_SIZE)
        for pp in range(1, pages_per_blk):
            acc = acc + k_refs[n * pages_per_blk + pp][0].reshape(kvw, PAGE_SIZE)
        mean = jnp.sum(acc, axis=1, keepdims=True) * (1.0 / MOBA_BLOCK)
        kmt = jnp.where(lane_b == n, mean, kmt)
    gate = jnp.dot(qm, kmt, precision=HIGHEST, preferred_element_type=F32)
    sel = _top_blocks(gate, n_past_blk, 1)
    lane_s = lax.broadcasted_iota(jnp.int32, sel.shape, 1)

    for p in range(n_pages):
        kp = k_refs[p][0].reshape(kvw, PAGE_SIZE).astype(BF16)
        s = jnp.dot(qmb, kp, preferred_element_type=F32)
        s = s - slope * (d_page0 - float(p * PAGE_SIZE))
        picked = jnp.sum(jnp.where(lane_s == p // pages_per_blk, sel, 0.0), axis=-1, keepdims=True) > 0.0
        s_ref[:, p * PAGE_SIZE:(p + 1) * PAGE_SIZE] = jnp.where(picked, s, -jnp.inf)
    pad = jnp.zeros((LANES - t_new, kvw), F32)
    k_new = jnp.concatenate([kn_ref[...], pad], axis=0).astype(BF16)
    v_new = jnp.concatenate([vn_ref[...], pad], axis=0).astype(BF16)
    rr_own = lax.broadcasted_iota(jnp.int32, (rows, LANES), 0)
    cc_own = lax.broadcasted_iota(jnp.int32, (rows, LANES), 1)
    d_own = (rr_own % t_new - cc_own).astype(F32)
    s = lax.dot_general(qmb, k_new, NT_DIMS, preferred_element_type=F32) - slope * d_own
    s_ref[:, past_len:] = jnp.where((d_own >= 0.0) & (cc_own < t_new), s, -jnp.inf)

    s_all = s_ref[...]
    m = jnp.max(s_all, axis=-1, keepdims=True)
    p_all = jnp.exp(s_all - m)
    l = jnp.sum(p_all, axis=-1, keepdims=True)
    pb = p_all.astype(BF16)
    out = jnp.dot(pb[:, past_len:], v_new, preferred_element_type=F32)
    for p in range(n_pages):
        vp = v_refs[p][0].reshape(kvw, PAGE_SIZE).astype(BF16)
        out = out + lax.dot_general(pb[:, p * PAGE_SIZE:(p + 1) * PAGE_SIZE], vp, NT_DIMS,
                                    preferred_element_type=F32)
    out = out / l

    lo8 = lane8 < HEAD_DIM
    for cidx in range(N_HEADS // 2):
        src = slice((cidx // 2) * LANES, (cidx // 2 + 1) * LANES)
        a = out[(2 * cidx) * t_new:(2 * cidx + 1) * t_new, src]
        b = out[(2 * cidx + 1) * t_new:(2 * cidx + 2) * t_new, src]
        if cidx % 2 == 1:
            a = pltpu.roll(a, HEAD_DIM, 1)
        else:
            b = pltpu.roll(b, HEAD_DIM, 1)
        o_ref[:, cidx * LANES:(cidx + 1) * LANES] = jnp.where(lo8, a, b)


def _attn_sample(q, k_new, v_new, cache_kt, cache_vt, page_table, t_new):
    n_seq, n_pages = page_table.shape
    d = q.shape[-1]
    kvw = k_new.shape[-1]
    rows = N_HEADS * t_new
    n_seqs = 2 if n_seq % 2 == 0 else 1

    def page_spec(j, p):
        return pl.BlockSpec((1, N_KV_HEADS, HEAD_DIM, PAGE_SIZE), lambda s, pt: (pt[n_seqs * s + j, p], 0, 0, 0))

    page_specs = [page_spec(j, p) for j in range(n_seqs) for p in range(n_pages)]
    grid_spec = pltpu.PrefetchScalarGridSpec(
        num_scalar_prefetch=1,
        grid=(n_seq // n_seqs,),
        in_specs=[pl.BlockSpec((1, n_seqs * t_new, d), lambda s, pt: (0, s, 0)),
                  pl.BlockSpec((1, n_seqs * t_new, kvw), lambda s, pt: (0, s, 0)),
                  pl.BlockSpec((1, n_seqs * t_new, kvw), lambda s, pt: (0, s, 0))] + page_specs + page_specs,
        out_specs=pl.BlockSpec((1, n_seqs * t_new, d), lambda s, pt: (0, s, 0)),
        scratch_shapes=[pltpu.VMEM((n_seqs, rows, n_pages * PAGE_SIZE + LANES), F32)],
    )
    n_page_args = n_seqs * n_pages
    return pl.pallas_call(
        functools.partial(_attn_sample_kernel, n_pages=n_pages, n_seqs=n_seqs),
        grid_spec=grid_spec,
        out_shape=jax.ShapeDtypeStruct(q.shape, F32),
        compiler_params=_cparams(("arbitrary",), VMEM_BIG_MIB),
        name="moba_sample",
    )(page_table, q, k_new, v_new, *([cache_kt] * n_page_args), *([cache_vt] * n_page_args))


def _head_sum_matrices(n_heads):
    w = n_heads * HEAD_DIM
    head_of_lane = jnp.arange(w) // HEAD_DIM
    r = (head_of_lane[:, None] == jnp.arange(LANES)[None, :]).astype(BF16)
    return r, r.T


def _prep_weights(p):
    bf = lambda x: x.astype(BF16)
    w_in = p['ssm_w_in']
    wdt = jnp.pad(w_in[:, :, D_INNER + CONV_DIM:], ((0, 0), (0, 0), (0, LANES - SSM_HEADS)))
    pad_heads = lambda x: jnp.pad(x, ((0, 0), (0, LANES - SSM_HEADS)))
    head_params = jnp.stack([pad_heads(p['ssm_dt_bias']), pad_heads(p['ssm_a_log'])], axis=1)
    head_params = jnp.pad(head_params, ((0, 0), (0, SUBLANES - 2), (0, 0)))
    router_w = jnp.concatenate([p['moe_w_grp'], p['moe_w_exp']], axis=-1)
    router_w = jnp.pad(router_w, ((0, 0), (0, 0), (0, LANES - router_w.shape[-1])))
    router_b = jnp.concatenate([p['moe_b_grp'], p['moe_b_exp']], axis=-1)
    router_b = jnp.pad(router_b, ((0, 0), (0, LANES - router_b.shape[-1])))[:, None, :]
    rq, rqt = _head_sum_matrices(N_HEADS)
    rk, rkt = _head_sum_matrices(N_KV_HEADS)
    return dict(
        wz=bf(w_in[:, :, :D_INNER]), wxbc=bf(w_in[:, :, D_INNER:D_INNER + CONV_DIM]), wdt=bf(wdt),
        head_params=head_params, d_skip=jnp.repeat(p['ssm_d'], SSM_HEAD_DIM, axis=-1)[:, None, :],
        conv_w=p['ssm_conv_w'], conv_b=p['ssm_conv_b'][:, None, :], ssm_norm_w=p['ssm_norm_w'][:, None, :],
        w_out=bf(p['ssm_w_out']),
        norm_mix_w=p['norm_mix_w'][:, None, :], norm_ffn_w=p['norm_ffn_w'][:, None, :],
        kv_norm_w=p['kv_norm_w'][None, :], wk=bf(p['w_k']), wv=bf(p['w_v']),
        k_norm_w=jnp.tile(p['k_norm_w'], N_KV_HEADS)[None, :],
        wq=bf(p['attn_w_q']), q_norm_w=jnp.tile(p['q_norm_w'], (1, N_HEADS))[:, None, :], wo=bf(p['attn_w_o']),
        router_w=router_w, router_b=router_b,
        w1=bf(p['moe_w1']), w3=bf(p['moe_w3']), w2=bf(p['moe_w2']),
        rq=rq, rqt=rqt, rk=rk, rkt=rkt,
    )


def _trunk(h, mods, kv_mod, w, *, sample, conv0=None, ssm0=None, cache=None):
    bsz, t, _ = h.shape
    tm = min(t, TOKEN_TILE)
    conv_out = []
    st_all = None
    k_new = v_new = kd = vt = kmd = None
    for layer in range(DEPTH):
        sh_m, sc_m, g_m, sh_f, sc_f, g_f = mods[layer]
        nmw = w['norm_mix_w'][layer]
        if layer < N_A_LAYERS:
            z, xbc, dtr = _in_proj(h, nmw, sc_m, sh_m, w['wz'][layer], w['wxbc'][layer], w['wdt'][layer],
                                   min(t, IN_PROJ_TILE))
            ssd_args = (w['conv_w'][layer], w['conv_b'][layer], w['head_params'][layer], w['d_skip'][layer],
                        w['ssm_norm_w'][layer])
            if sample:
                n_seq, t_new = conv0.shape[1], cache[3]
                seqs = lambda x: x.reshape(n_seq, t_new, x.shape[-1])
                conv_in = jnp.pad(conv0[layer], ((0, 0), (HALO - (CONV_WIDTH - 1), 0), (0, 0)))
                h0_all = ssm0.reshape(N_A_LAYERS, n_seq, D_INNER, D_STATE)
                g, conv8, st_all = _ssd(seqs(z), seqs(xbc), seqs(dtr), conv_in, h0_all, st_all, layer,
                                        *ssd_args, g_dtype=F32)
                g = g.reshape(1, t, D_INNER)
            else:
                conv_in = jnp.zeros((bsz, HALO, CONV_DIM), F32)
                g, conv8, st_all = _ssd(z, xbc, dtr, conv_in, None, st_all, layer, *ssd_args, g_dtype=BF16)
            conv_out.append(conv8[:, HALO - (CONV_WIDTH - 1):])
            h = _mm_res(g, w['w_out'][layer], h, g_m, tm)
        else:
            j = layer - N_A_LAYERS
            if sample:
                q = _q_proj(h, nmw, sc_m, sh_m, w['wq'][j], w['q_norm_w'][j], w['rq'], w['rqt'], tm, F32)
                o = _attn_sample(q, k_new, v_new, cache[0], cache[1], cache[2], cache[3])
            else:
                q = _q_proj(h, nmw, sc_m, sh_m, w['wq'][j], w['q_norm_w'][j], w['rq'], w['rqt'], tm, BF16,
                            out_scale=LOG2_E)
                o = _attn_prompt(q, kd, vt, kmd)
            h = _mm_res(o, w['wo'][j], h, g_m, tm)
        h = _moe(h, w['norm_ffn_w'][layer], sc_f, sh_f, g_f, w['router_w'][layer], w['router_b'][layer],
                 w['w1'][layer], w['w3'][layer], w['w2'][layer], tm)
        if layer == N_A_LAYERS - 1:
            kv_tm = min(t, KV_TILE)
            res = _kv_proj(h, w['kv_norm_w'], kv_mod[1], kv_mod[0], w['wk'], w['wv'], w['k_norm_w'],
                           w['rk'], w['rkt'], kv_tm, with_dup=not sample)
            k_new, v_new = res[0], res[1]
            if not sample:
                kd, vt = res[2], res[3]
                kmd = res[4].reshape(bsz, N_SPLIT, t // MOBA_BLOCK, N_KV_HEADS * LANES)
                kmd = jnp.pad(kmd, ((0, 0), (0, 0), (0, -kmd.shape[2] % BF16_ROWS), (0, 0)))
                kmd = kmd.reshape(bsz, -1, N_KV_HEADS * LANES).astype(BF16)
    ssm_out = st_all.reshape(N_A_LAYERS, st_all.shape[1], SSM_HEADS, SSM_HEAD_DIM, D_STATE)
    return h, jnp.stack(conv_out), ssm_out, k_new, v_new


def kernel(x_prompt, x_sample, state_conv, state_ssm, cache_k, cache_v, page_table, c_prompt, c_sample, w_mod, b_mod, norm_mix_w, norm_ffn_w, ssm_w_in, ssm_conv_w, ssm_conv_b, ssm_dt_bias, ssm_a_log, ssm_d, ssm_norm_w, ssm_w_out, kv_w_mod, kv_b_mod, kv_norm_w, w_k, w_v, k_norm_w, attn_w_q, q_norm_w, attn_w_o, moe_w_grp, moe_b_grp, moe_w_exp, moe_b_exp, moe_w1, moe_w3, moe_w2):
    params = dict(ssm_w_in=ssm_w_in, ssm_conv_w=ssm_conv_w, ssm_conv_b=ssm_conv_b, ssm_dt_bias=ssm_dt_bias,
                  ssm_a_log=ssm_a_log, ssm_d=ssm_d, ssm_norm_w=ssm_norm_w, ssm_w_out=ssm_w_out,
                  norm_mix_w=norm_mix_w, norm_ffn_w=norm_ffn_w, kv_norm_w=kv_norm_w, w_k=w_k, w_v=w_v,
                  k_norm_w=k_norm_w, attn_w_q=attn_w_q, q_norm_w=q_norm_w, attn_w_o=attn_w_o,
                  moe_w_grp=moe_w_grp, moe_b_grp=moe_b_grp, moe_w_exp=moe_w_exp, moe_b_exp=moe_b_exp,
                  moe_w1=moe_w1, moe_w3=moe_w3, moe_w2=moe_w2)
    w = _prep_weights(params)

    bp, seq, d = x_prompt.shape
    n_seq, t_new, _ = x_sample.shape
    n_pages = page_table.shape[1]
    past_len = n_pages * PAGE_SIZE
    assert seq % (ATT_TILES * MOBA_BLOCK) == 0
    assert past_len % MOBA_BLOCK == 0 and t_new <= MOBA_BLOCK and t_new % SUBLANES == 0

    n_c = bp + n_seq
    n_c_pad = -(-n_c // SUBLANES) * SUBLANES
    c_all = jnp.pad(jnp.concatenate([c_prompt, c_sample], axis=0), ((0, n_c_pad - n_c), (0, 0)))
    mod_all = _mod_vectors(c_all, w_mod, b_mod[:, None, :], MOD_COLS)
    kv_all = _mod_vectors(c_all, kv_w_mod[None], kv_b_mod[None, None, :], MOD_COLS)[0]

    def group_mods(lo, hi, per_token_repeat):
        def shape(x):
            if per_token_repeat:
                return jnp.repeat(x, per_token_repeat, axis=0)[None]
            return x[:, None, :]
        mods = [[shape(mod_all[l, lo:hi, k * d:(k + 1) * d]) for k in range(6)] for l in range(DEPTH)]
        kvm = [shape(kv_all[lo:hi, k * d:(k + 1) * d]) for k in range(2)]
        return mods, kvm

    mods_p, kv_p = group_mods(0, bp, 0)
    y_p, conv_p, ssm_p, k_p, v_p = _trunk(x_prompt, mods_p, kv_p, w, sample=False)

    mods_s, kv_s = group_mods(bp, n_c, t_new)
    kvw = N_KV_HEADS * HEAD_DIM
    cache = (jnp.transpose(cache_k, (0, 2, 3, 1)), jnp.transpose(cache_v, (0, 2, 3, 1)), page_table, t_new)
    y_s, conv_s, ssm_s, k_s, v_s = _trunk(x_sample.reshape(1, n_seq * t_new, d), mods_s, kv_s, w,
                                          sample=True, conv0=state_conv, ssm0=state_ssm, cache=cache)

    heads = lambda x, b, t: x.reshape(b, t, N_KV_HEADS, HEAD_DIM)
    return (y_p, y_s.reshape(n_seq, t_new, d), conv_p, ssm_p, heads(k_p, bp, seq), heads(v_p, bp, seq),
            conv_s, ssm_s, heads(k_s, n_seq, t_new), heads(v_s, n_seq, t_new))
```

```python
import functools

import jax
import jax.numpy as jnp
from jax import lax
from jax.experimental import pallas as pl
from jax.experimental.pallas import tpu as pltpu

F32 = jnp.float32
BF16 = jnp.bfloat16
HIGHEST = lax.Precision.HIGHEST

DEPTH = 4
N_A_LAYERS = 2
D_INNER = 2048
SSM_HEADS = 32
SSM_HEAD_DIM = 64
SSM_GROUPS = 4
D_STATE = 128
CONV_WIDTH = 4
CONV_DIM = D_INNER + 2 * SSM_GROUPS * D_STATE
SSD_CHUNK = 128
N_HEADS = 16
HEAD_DIM = 64
N_KV_HEADS = 8
MOBA_BLOCK = 256
MOBA_TOPK = 3
N_EXPERT_GROUPS = 4
EXPERTS_PER_GROUP = 4
PAGE_SIZE = 128
EPS = 1e-6

LANES = 128
SUBLANES = 8
BF16_ROWS = 16
MIB = 1024 * 1024
NEG_BIG = -1e30

VMEM_MIB = 48
VMEM_BIG_MIB = 56
TOKEN_TILE = 1024
IN_PROJ_TILE = 512
KV_TILE = 512
MOD_COLS = 1024
MOE_EXPERTS_PER_STEP = 8

NT_DIMS = (((1,), (1,)), ((), ()))


def _cparams(sem, vmem_mib=VMEM_MIB):
    return pltpu.CompilerParams(dimension_semantics=sem, vmem_limit_bytes=vmem_mib * MIB)


def _sigmoid(x):
    return 1.0 / (1.0 + jnp.exp(-x))


def _silu(x):
    return x * _sigmoid(x)


def _norm_mod(h, nw, sc, sh):
    ms = jnp.mean(h * h, axis=-1, keepdims=True)
    return (h * lax.rsqrt(ms + EPS)) * nw * (1.0 + sc) + sh


def _split_bf16(x):
    hi = x.astype(BF16)
    lo = (x - hi.astype(F32)).astype(BF16)
    return hi, lo


def _head_rmsnorm(x, r, rt):
    sq = x * x
    hi, lo = _split_bf16(sq)
    ss = jnp.dot(hi, r, preferred_element_type=F32) + jnp.dot(lo, r, preferred_element_type=F32)
    inv = lax.rsqrt(ss * (1.0 / HEAD_DIM) + EPS)
    ih, il = _split_bf16(inv)
    invx = jnp.dot(ih, rt, preferred_element_type=F32) + jnp.dot(il, rt, preferred_element_type=F32)
    return x * invx


def _mod_kernel(c_ref, w_ref, b_ref, o_ref):
    ca = _silu(c_ref[...])
    o_ref[0] = jnp.dot(ca, w_ref[0], precision=HIGHEST, preferred_element_type=F32) + b_ref[0]


def _mod_vectors(c_all, w, b, tn):
    n_layers, d, n = w.shape
    m = c_all.shape[0]
    return pl.pallas_call(
        _mod_kernel,
        grid=(n_layers, n // tn),
        in_specs=[pl.BlockSpec((m, d), lambda l, j: (0, 0)),
                  pl.BlockSpec((1, d, tn), lambda l, j: (l, 0, j)),
                  pl.BlockSpec((1, 1, tn), lambda l, j: (l, 0, j))],
        out_specs=pl.BlockSpec((1, m, tn), lambda l, j: (l, 0, j)),
        out_shape=jax.ShapeDtypeStruct((n_layers, m, n), F32),
        compiler_params=_cparams(("arbitrary", "arbitrary")),
        name="mod_vectors",
    )(c_all, w, b)


def _mod_spec(mod, tm):
    d = mod.shape[-1]
    if mod.shape[1] == 1:
        return pl.BlockSpec((1, 1, d), lambda b, i, *_: (b, 0, 0))
    return pl.BlockSpec((1, tm, d), lambda b, i, *_: (b, i, 0))


def _in_proj_kernel(h_ref, nw_ref, sc_ref, sh_ref, wz_ref, wx_ref, wd_ref, z_ref, xbc_ref, dt_ref):
    u = _norm_mod(h_ref[0], nw_ref[...], sc_ref[0], sh_ref[0]).astype(BF16)
    z_ref[0] = jnp.dot(u, wz_ref[...], preferred_element_type=F32)
    xbc_ref[0] = jnp.dot(u, wx_ref[...], preferred_element_type=F32)
    dt_ref[0] = jnp.dot(u, wd_ref[...], preferred_element_type=F32)


def _in_proj(h, nw, sc, sh, wz, wxbc, wdt, tm):
    bsz, t, d = h.shape
    const = lambda w: pl.BlockSpec(w.shape, lambda b, i: (0, 0))
    out = lambda w: pl.BlockSpec((1, tm, w.shape[1]), lambda b, i: (b, i, 0))
    return pl.pallas_call(
        _in_proj_kernel,
        grid=(bsz, t // tm),
        in_specs=[pl.BlockSpec((1, tm, d), lambda b, i: (b, i, 0)), pl.BlockSpec((1, d), lambda b, i: (0, 0)),
                  _mod_spec(sc, tm), _mod_spec(sh, tm), const(wz), const(wxbc), const(wdt)],
        out_specs=[out(wz), out(wxbc), out(wdt)],
        out_shape=[jax.ShapeDtypeStruct((bsz, t, w.shape[1]), F32) for w in (wz, wxbc, wdt)],
        compiler_params=_cparams(("parallel", "parallel"), VMEM_BIG_MIB),
        name="ssm_in_proj",
    )(h, nw, sc, sh, wz, wxbc, wdt)


def _mm_res_kernel(a_ref, w_ref, h_ref, g_ref, o_ref):
    acc = jnp.dot(a_ref[0].astype(BF16), w_ref[...], preferred_element_type=F32)
    o_ref[0] = h_ref[0] + g_ref[0] * acc


def _mm_res(a, w, h, gate, tm):
    bsz, t, k = a.shape
    d = w.shape[1]
    return pl.pallas_call(
        _mm_res_kernel,
        grid=(bsz, t // tm),
        in_specs=[pl.BlockSpec((1, tm, k), lambda b, i: (b, i, 0)),
                  pl.BlockSpec((k, d), lambda b, i: (0, 0)),
                  pl.BlockSpec((1, tm, d), lambda b, i: (b, i, 0)),
                  _mod_spec(gate, tm)],
        out_specs=pl.BlockSpec((1, tm, d), lambda b, i: (b, i, 0)),
        out_shape=jax.ShapeDtypeStruct((bsz, t, d), F32),
        compiler_params=_cparams(("parallel", "parallel")),
        name="matmul_residual",
    )(a, w, h, gate)


HALO = SUBLANES
XC_TILE = 512


def _ssd_kernel(z_ref, xbc_ref, dt_ref, cs_ref, h0_ref, cw_ref, cb_ref, hp_ref, dsk_ref, nw_ref,
                *rest, valid, has_init, has_prev):
    g_ref, cso_ref, st_ref, xs_ref, xc_ref, xt_ref, y_ref = rest[1:] if has_prev else rest
    L = SSD_CHUNK
    c = pl.program_id(1)

    @pl.when(c == 0)
    def _init():
        xs_ref[0:HALO, :] = cs_ref[0]
        if valid < L:
            xc_ref[...] = jnp.zeros_like(xc_ref)
        if has_init:
            st_ref[0] = h0_ref[0]
        else:
            st_ref[0] = jnp.zeros(st_ref.shape[1:], F32)

    xs_ref[HALO:HALO + valid, :] = xbc_ref[0]

    for j in range(CONV_DIM // XC_TILE):
        sl = slice(j * XC_TILE, (j + 1) * XC_TILE)
        x_rows = xs_ref[0:HALO + valid, sl]
        acc = x_rows * cw_ref[0:1, sl]
        for k in range(1, CONV_WIDTH):
            acc = pltpu.roll(acc, 1, 0) + x_rows * cw_ref[k:k + 1, sl]
        xc_ref[0:valid, sl] = _silu(acc[HALO:HALO + valid] + cb_ref[:, sl])

    cso_ref[0] = xs_ref[valid:valid + HALO, :]
    xs_ref[0:HALO, :] = xs_ref[valid:valid + HALO, :]

    dtr = dt_ref[0]
    if valid < L:
        dtr = jnp.concatenate([dtr, jnp.zeros((L - valid, LANES), F32)], axis=0)
    xx = dtr + hp_ref[0:1, :]
    dt = jnp.maximum(xx, 0.0) + jnp.log(1.0 + jnp.exp(-jnp.abs(xx)))
    row = lax.broadcasted_iota(jnp.int32, (L, L), 0)
    col = lax.broadcasted_iota(jnp.int32, (L, L), 1)
    if valid < L:
        dt = jnp.where(row < valid, dt, 0.0)
    a = dt * (-jnp.exp(hp_ref[1:2, :]))
    causal = row >= col
    a_cum = jnp.dot(causal.astype(F32), a, precision=HIGHEST, preferred_element_type=F32)
    a_cum_t = a_cum.T
    dt_t = dt.T
    sub_lo = row < HEAD_DIM
    R = L if valid == L else -(-valid // BF16_ROWS) * BF16_ROWS
    row_r = lax.broadcasted_iota(jnp.int32, (R, L), 0)
    col_r = lax.broadcasted_iota(jnp.int32, (R, L), 1)
    causal_r = row_r >= col_r
    lane_lo = col_r < HEAD_DIM

    for j in range(D_INNER // LANES):
        xt_ref[j * LANES:(j + 1) * LANES, :] = xc_ref[:, j * LANES:(j + 1) * LANES].T

    pairs_per_group = SSM_HEADS // SSM_GROUPS // 2
    grp_w = D_INNER // SSM_GROUPS
    for g in range(SSM_GROUPS):
        b_g = xc_ref[:, D_INNER + g * D_STATE:D_INNER + (g + 1) * D_STATE].astype(BF16)
        c_g = xc_ref[:, D_INNER + (SSM_GROUPS + g) * D_STATE:
                     D_INNER + (SSM_GROUPS + g + 1) * D_STATE].astype(BF16)
        cb = lax.dot_general(c_g[0:R], b_g, NT_DIMS, preferred_element_type=F32)
        st_g = st_ref[0, g * grp_w:(g + 1) * grp_w, :]
        y_off = lax.dot_general(c_g[0:R], st_g.astype(BF16), NT_DIMS, preferred_element_type=F32)
        xw_parts, dec_parts = [], []
        for jj in range(pairs_per_group):
            pair = g * pairs_per_group + jj
            s_mats, e_cols, w_rows, d_end = [], [], [], []
            for h in (2 * pair, 2 * pair + 1):
                colb = jnp.broadcast_to(a_cum[0:R, h:h + 1], (R, L))
                rowb = a_cum_t[h:h + 1, :]
                dec = jnp.where(causal_r, jnp.exp(colb - rowb), 0.0)
                s_mats.append((cb * dec * dt_t[h:h + 1, :]).astype(BF16))
                e_cols.append(jnp.exp(colb))
                a_last = a_cum_t[h:h + 1, L - 1:L]
                w_rows.append(dt_t[h:h + 1, :] * jnp.exp(a_last - rowb))
                d_end.append(jnp.broadcast_to(jnp.exp(a_last), (LANES, D_STATE)))
            psl = slice(pair * LANES, (pair + 1) * LANES)
            xp = xc_ref[:, psl]
            xpb = xp.astype(BF16)
            y_d = jnp.where(lane_lo,
                            jnp.dot(s_mats[0], xpb, preferred_element_type=F32),
                            jnp.dot(s_mats[1], xpb, preferred_element_type=F32))
            y_o = y_off[:, jj * LANES:(jj + 1) * LANES] * jnp.where(lane_lo, e_cols[0], e_cols[1])
            y_ref[0:R, psl] = y_d + y_o + dsk_ref[:, psl] * xp[0:R]
            xw_parts.append((xt_ref[psl, :] * jnp.where(sub_lo, w_rows[0], w_rows[1])).astype(BF16))
            dec_parts.append(jnp.where(sub_lo, d_end[0], d_end[1]))
        xw_g = jnp.concatenate(xw_parts, axis=0)
        new = jnp.dot(xw_g, b_g, preferred_element_type=F32)
        st_ref[0, g * grp_w:(g + 1) * grp_w, :] = st_g * jnp.concatenate(dec_parts, axis=0) + new

    zz = z_ref[0]
    gg = y_ref[0:valid, :] * _silu(zz)
    for g in range(SSM_GROUPS):
        sl = slice(g * grp_w, (g + 1) * grp_w)
        seg = gg[:, sl]
        ms = jnp.mean(seg * seg, axis=-1, keepdims=True)
        g_ref[0, :, sl] = (seg * lax.rsqrt(ms + EPS) * nw_ref[:, sl]).astype(g_ref.dtype)


def _ssd(z, xbc, dtr, conv_in, h0_all, st_all, layer, conv_w, conv_b, head_params, d_skip, norm_w, g_dtype):
    bsz, t, _ = z.shape
    L = SSD_CHUNK
    valid = min(L, t)
    assert t % valid == 0 and valid % SUBLANES == 0
    nc = t // valid
    has_init = h0_all is not None
    state_spec = pl.BlockSpec((None, 1, D_INNER, D_STATE), lambda b, c: (layer, b, 0, 0))
    if h0_all is None:
        h0_all = jnp.zeros((1, SUBLANES, D_STATE), F32)
        h0_spec = pl.BlockSpec((1, SUBLANES, D_STATE), lambda b, c: (0, 0, 0))
    else:
        h0_spec = state_spec
    full = lambda shape: pl.BlockSpec(shape, lambda b, c: (0,) * len(shape))
    operands = [z, xbc, dtr, conv_in, h0_all, conv_w, conv_b, head_params, d_skip, norm_w]
    in_specs = [pl.BlockSpec((1, valid, D_INNER), lambda b, c: (b, c, 0)),
                pl.BlockSpec((1, valid, CONV_DIM), lambda b, c: (b, c, 0)),
                pl.BlockSpec((1, valid, LANES), lambda b, c: (b, c, 0)),
                pl.BlockSpec((1, HALO, CONV_DIM), lambda b, c: (b, 0, 0)),
                h0_spec,
                full((CONV_WIDTH, CONV_DIM)), full((1, CONV_DIM)), full((SUBLANES, LANES)),
                full((1, D_INNER)), full((1, D_INNER))]
    aliases = {}
    if st_all is not None:
        aliases = {len(operands): 2}
        operands.append(st_all)
        in_specs.append(pl.BlockSpec(memory_space=pl.ANY))
    return pl.pallas_call(
        functools.partial(_ssd_kernel, valid=valid, has_init=has_init, has_prev=st_all is not None),
        grid=(bsz, nc),
        in_specs=in_specs,
        out_specs=[pl.BlockSpec((1, valid, D_INNER), lambda b, c: (b, c, 0)),
                   pl.BlockSpec((1, HALO, CONV_DIM), lambda b, c: (b, 0, 0)),
                   state_spec],
        out_shape=[jax.ShapeDtypeStruct((bsz, t, D_INNER), g_dtype),
                   jax.ShapeDtypeStruct((bsz, HALO, CONV_DIM), F32),
                   jax.ShapeDtypeStruct((N_A_LAYERS, bsz, D_INNER, D_STATE), F32)],
        input_output_aliases=aliases,
        scratch_shapes=[pltpu.VMEM((L + HALO, CONV_DIM), F32),
                        pltpu.VMEM((L, CONV_DIM), F32),
                        pltpu.VMEM((D_INNER, L), F32),
                        pltpu.VMEM((L, D_INNER), F32)],
        compiler_params=_cparams(("parallel", "arbitrary")),
        name="ssd_chunk_scan",
    )(*operands)


ROUTER_EXP0 = N_EXPERT_GROUPS


def _route(logits):
    lane = lax.broadcasted_iota(jnp.int32, logits.shape, 1)
    big = jnp.int32(2 * LANES)
    is_grp = lane < N_EXPERT_GROUPS
    gl = jnp.where(is_grp, logits, -jnp.inf)
    gmax = jnp.max(gl, axis=-1, keepdims=True)
    gidx = jnp.min(jnp.where(gl == gmax, lane, big), axis=-1, keepdims=True)
    p_grp = 1.0 / jnp.sum(jnp.where(is_grp, jnp.exp(gl - gmax), 0.0), axis=-1, keepdims=True)
    e_rel = lane - ROUTER_EXP0
    in_grp = (e_rel >= gidx * EXPERTS_PER_GROUP) & (e_rel < (gidx + 1) * EXPERTS_PER_GROUP)
    el = jnp.where(in_grp, logits, -jnp.inf)
    m1 = jnp.max(el, axis=-1, keepdims=True)
    i1 = jnp.min(jnp.where(el == m1, lane, big), axis=-1, keepdims=True)
    el2 = jnp.where(lane == i1, -jnp.inf, el)
    m2 = jnp.max(el2, axis=-1, keepdims=True)
    i2 = jnp.min(jnp.where(el2 == m2, lane, big), axis=-1, keepdims=True)
    e2 = jnp.exp(m2 - m1)
    den = 1.0 + e2
    w1 = (1.0 / den) * p_grp
    w2 = (e2 / den) * p_grp
    return jnp.where(lane == i1, w1, jnp.where(lane == i2, w2, 0.0))


def _moe_kernel(h_ref, nw_ref, sc_ref, sh_ref, gf_ref, wr_ref, br_ref, w1_ref, w3_ref, w2_ref,
                o_ref, u_ref, gates_ref, acc_ref):
    step = pl.program_id(2)
    per_step = w1_ref.shape[0]

    @pl.when(step == 0)
    def _():
        u = _norm_mod(h_ref[0], nw_ref[...], sc_ref[0], sh_ref[0])
        u_hi, u_lo = _split_bf16(u)
        w_hi, w_lo = _split_bf16(wr_ref[...])
        logits = (jnp.dot(u_hi, w_hi, preferred_element_type=F32) + jnp.dot(u_lo, w_hi, preferred_element_type=F32)
                  + jnp.dot(u_hi, w_lo, preferred_element_type=F32)) + br_ref[...]
        gates_ref[...] = _route(logits)
        u_ref[...] = u_hi
        acc_ref[...] = jnp.zeros_like(acc_ref)

    u = u_ref[...]
    gates = gates_ref[...]
    lane = lax.broadcasted_iota(jnp.int32, gates.shape, 1)
    acts = []
    for j in range(per_step):
        a = (_silu(jnp.dot(u, w1_ref[j], preferred_element_type=F32))
             * jnp.dot(u, w3_ref[j], preferred_element_type=F32))
        e = step * per_step + j
        gcol = jnp.sum(jnp.where(lane == e + ROUTER_EXP0, gates, 0.0), axis=-1, keepdims=True)
        acts.append((a * gcol).astype(BF16))
    w2 = w2_ref[...].reshape(per_step * w2_ref.shape[1], w2_ref.shape[2])
    acc_ref[...] += jnp.dot(jnp.concatenate(acts, axis=1), w2, preferred_element_type=F32)

    @pl.when(step == pl.num_programs(2) - 1)
    def _():
        o_ref[0] = h_ref[0] + gf_ref[0] * acc_ref[...]


def _moe(h, nw, sc, sh, gf, wr, br, w1, w3, w2, tm):
    bsz, t, d = h.shape
    n_e, _, f = w1.shape
    per_step = MOE_EXPERTS_PER_STEP if sc.shape[1] == 1 else EXPERTS_PER_GROUP
    return pl.pallas_call(
        _moe_kernel,
        grid=(bsz, t // tm, n_e // per_step),
        in_specs=[pl.BlockSpec((1, tm, d), lambda b, i, e: (b, i, 0)),
                  pl.BlockSpec((1, d), lambda b, i, e: (0, 0)),
                  _mod_spec(sc, tm), _mod_spec(sh, tm), _mod_spec(gf, tm),
                  pl.BlockSpec((d, LANES), lambda b, i, e: (0, 0)),
                  pl.BlockSpec((1, LANES), lambda b, i, e: (0, 0)),
                  pl.BlockSpec((per_step, d, f), lambda b, i, e: (e, 0, 0)),
                  pl.BlockSpec((per_step, d, f), lambda b, i, e: (e, 0, 0)),
                  pl.BlockSpec((per_step, f, d), lambda b, i, e: (e, 0, 0))],
        out_specs=pl.BlockSpec((1, tm, d), lambda b, i, e: (b, i, 0)),
        out_shape=jax.ShapeDtypeStruct((bsz, t, d), F32),
        scratch_shapes=[pltpu.VMEM((tm, d), BF16), pltpu.VMEM((tm, LANES), F32), pltpu.VMEM((tm, d), F32)],
        compiler_params=_cparams(("parallel", "parallel", "arbitrary"), VMEM_BIG_MIB),
        name="hmoe",
    )(h, nw, sc, sh, gf, wr, br, w1, w3, w2)


N_SPLIT = 3
FEAT_POS_LO = HEAD_DIM
FEAT_POS_HI = FEAT_POS_LO + N_SPLIT
FEAT_ONE = FEAT_POS_HI + N_SPLIT


def _split3(x):
    h1 = x.astype(BF16).astype(F32)
    r = x - h1
    h2 = r.astype(BF16).astype(F32)
    h3 = (r - h2).astype(BF16).astype(F32)
    return h1, h2, h3


def _feature_lanes(lane, base, parts, other):
    out = other
    for k, part in enumerate(parts):
        out = jnp.where(lane == base + k, part, out)
    return out


def _key_tiles(x, feat):
    lane = lax.broadcasted_iota(jnp.int32, x.shape, 1)
    rolled = pltpu.roll(x, HEAD_DIM, 1)
    lo = lane < HEAD_DIM
    return jnp.where(lo, x, feat), jnp.where(lo, rolled, feat)


def _kv_kernel(h_ref, nw_ref, sc_ref, sh_ref, wk_ref, wv_ref, knw_ref, r_ref, rt_ref,
               kf_ref, vf_ref, *dup_refs, tm, with_dup):
    u = _norm_mod(h_ref[0], nw_ref[...], sc_ref[0], sh_ref[0]).astype(BF16)
    k = jnp.dot(u, wk_ref[...], preferred_element_type=F32)
    v = jnp.dot(u, wv_ref[...], preferred_element_type=F32)
    kn = _head_rmsnorm(k, r_ref[...], rt_ref[...]) * knw_ref[...]
    if not with_dup:
        kf_ref[0] = kn
        vf_ref[0] = v
    else:
        for hh in range(N_KV_HEADS):
            rows = pl.ds(hh, tm, stride=N_KV_HEADS)
            for src, dst in ((kn, kf_ref), (v, vf_ref)):
                tile = src[:, (hh // 2) * LANES:(hh // 2 + 1) * LANES]
                if hh % 2 == 1:
                    tile = pltpu.roll(tile, HEAD_DIM, 1)
                dst[0, rows, :] = tile[:, :HEAD_DIM]
        kd_ref, vt_ref, km_ref = dup_refs
        lane = lax.broadcasted_iota(jnp.int32, (tm, LANES), 1)
        pos = pl.program_id(1) * tm + lax.broadcasted_iota(jnp.int32, (tm, LANES), 0)
        pos_lo = pos % MOBA_BLOCK
        feat = jnp.zeros((tm, LANES), F32)
        feat = _feature_lanes(lane, FEAT_POS_LO, [pos_lo.astype(F32)] * N_SPLIT, feat)
        feat = _feature_lanes(lane, FEAT_POS_HI, [(pos - pos_lo).astype(F32)] * N_SPLIT, feat)
        feat = _feature_lanes(lane, FEAT_ONE, [jnp.ones((tm, LANES), F32)] * N_SPLIT, feat)
        lane_row = lax.broadcasted_iota(jnp.int32, (1, LANES), 1)
        for cidx in range(N_KV_HEADS // 2):
            d0, d1 = _key_tiles(kn[:, cidx * LANES:(cidx + 1) * LANES], feat)
            for hh, dd in ((2 * cidx, d0), (2 * cidx + 1, d1)):
                hsl = slice(hh * LANES, (hh + 1) * LANES)
                kd_ref[0, :, hsl] = dd.astype(BF16)
                for blk in range(tm // MOBA_BLOCK):
                    mean = jnp.mean(dd[blk * MOBA_BLOCK:(blk + 1) * MOBA_BLOCK], axis=0, keepdims=True)
                    for part, piece in enumerate(_split3(jnp.where(lane_row < HEAD_DIM, mean, 0.0))):
                        km_ref[0, part, blk, :, hsl] = piece
        for blk in range(tm // MOBA_BLOCK):
            vt_ref[0, blk] = v[blk * MOBA_BLOCK:(blk + 1) * MOBA_BLOCK].T.astype(BF16)


def _kv_proj(h, nw, sc, sh, wk, wv, knw, r, rt, tm, with_dup):
    bsz, t, d = h.shape
    kvw = wk.shape[1]
    const = lambda shape: pl.BlockSpec(shape, lambda b, i: (0,) * len(shape))
    if not with_dup:
        out_specs = [pl.BlockSpec((1, tm, kvw), lambda b, i: (b, i, 0))] * 2
        out_shape = [jax.ShapeDtypeStruct((bsz, t, kvw), F32)] * 2
    else:
        out_specs = [pl.BlockSpec((1, tm * N_KV_HEADS, HEAD_DIM), lambda b, i: (b, i, 0))] * 2
        out_shape = [jax.ShapeDtypeStruct((bsz, t * N_KV_HEADS, HEAD_DIM), F32)] * 2
        dupw = N_KV_HEADS * LANES
        nblk = tm // MOBA_BLOCK
        out_specs += [pl.BlockSpec((1, tm, dupw), lambda b, i: (b, i, 0)),
                      pl.BlockSpec((1, nblk, kvw, MOBA_BLOCK), lambda b, i: (b, i, 0, 0)),
                      pl.BlockSpec((1, N_SPLIT, nblk, 1, dupw), lambda b, i: (b, 0, i, 0, 0))]
        out_shape += [jax.ShapeDtypeStruct((bsz, t, dupw), BF16),
                      jax.ShapeDtypeStruct((bsz, t // MOBA_BLOCK, kvw, MOBA_BLOCK), BF16),
                      jax.ShapeDtypeStruct((bsz, N_SPLIT, t // MOBA_BLOCK, 1, dupw), F32)]
    return pl.pallas_call(
        functools.partial(_kv_kernel, tm=tm, with_dup=with_dup),
        grid=(bsz, t // tm),
        in_specs=[pl.BlockSpec((1, tm, d), lambda b, i: (b, i, 0)),
                  const((1, d)), _mod_spec(sc, tm), _mod_spec(sh, tm),
                  const((d, kvw)), const((d, kvw)), const((1, kvw)),
                  const((kvw, LANES)), const((LANES, kvw))],
        out_specs=out_specs,
        out_shape=out_shape,
        compiler_params=_cparams(("parallel", "parallel")),
        name="shared_kv",
    )(h, nw, sc, sh, wk, wv, knw, r, rt)


def _q_kernel(h_ref, nw_ref, sc_ref, sh_ref, wq_ref, qnw_ref, r_ref, rt_ref, q_ref, *, out_scale):
    u = _norm_mod(h_ref[0], nw_ref[...], sc_ref[0], sh_ref[0]).astype(BF16)
    q = jnp.dot(u, wq_ref[...], preferred_element_type=F32)
    q = _head_rmsnorm(q, r_ref[...], rt_ref[...]) * qnw_ref[...]
    if out_scale != 1.0:
        q = q * out_scale
    q_ref[0] = q.astype(q_ref.dtype)


def _q_proj(h, nw, sc, sh, wq, qnw, r, rt, tm, out_dtype, out_scale=1.0):
    bsz, t, d = h.shape
    const = lambda shape: pl.BlockSpec(shape, lambda b, i: (0,) * len(shape))
    return pl.pallas_call(
        functools.partial(_q_kernel, out_scale=out_scale),
        grid=(bsz, t // tm),
        in_specs=[pl.BlockSpec((1, tm, d), lambda b, i: (b, i, 0)),
                  const((1, d)), _mod_spec(sc, tm), _mod_spec(sh, tm),
                  const((d, d)), const((1, d)), const((d, LANES)), const((LANES, d))],
        out_specs=pl.BlockSpec((1, tm, d), lambda b, i: (b, i, 0)),
        out_shape=jax.ShapeDtypeStruct((bsz, t, d), out_dtype),
        compiler_params=_cparams(("parallel", "parallel")),
        name="q_proj",
    )(h, nw, sc, sh, wq, qnw, r, rt)


def _top_blocks(gate, n_valid, axis):
    pos = lax.broadcasted_iota(jnp.int32, gate.shape, axis)
    gm = jnp.where(pos < n_valid, gate, -jnp.inf)
    sel = jnp.zeros(gate.shape, F32)
    for _ in range(MOBA_TOPK):
        mx = jnp.max(gm, axis=axis, keepdims=True)
        cand = (gm == mx) & (mx > -jnp.inf)
        idx = jnp.min(jnp.where(cand, pos, jnp.int32(2 * LANES)), axis=axis, keepdims=True)
        pick = pos == idx
        sel = jnp.where(pick, 1.0, sel)
        gm = jnp.where(pick, -jnp.inf, gm)
    return sel


def _alibi_slope(head):
    return jnp.exp2(-8.0 * (head + 1).astype(F32) / N_HEADS)


ATT_GROUP = 2
ACC_ROWS = HEAD_DIM + BF16_ROWS
LOG2_E = 1.4426950408889634


def _attn_prompt_kernel(q_ref, k_ref, vt_ref, km_ref, o_ref,
                        qt_ref, fq_ref, sa_ref, sb_ref, sel_ref, m_ref, acc_ref):
    g = pl.program_id(1)
    i = pl.program_id(2)
    blk = MOBA_BLOCK
    cols = 2 * blk
    grp = ATT_GROUP

    l2 = lax.broadcasted_iota(jnp.int32, (cols, LANES), 1)

    @pl.when(i == 0)
    def _features():
        r2 = lax.broadcasted_iota(jnp.int32, (cols, LANES), 0)
        second = r2 >= blk
        slope = _alibi_slope(2 * g + second.astype(jnp.int32)) * LOG2_E
        offs_q = jnp.where(second, r2 - blk, r2).astype(F32)
        slope_parts = _split3(slope)
        feat = jnp.zeros((cols, LANES), F32)
        feat = _feature_lanes(l2, FEAT_POS_LO, slope_parts, feat)
        feat = _feature_lanes(l2, FEAT_POS_HI, slope_parts, feat)
        fq_ref[...] = _feature_lanes(l2, FEAT_ONE, _split3(-slope * offs_q), feat)

    c1 = lax.broadcasted_iota(jnp.int32, (1, cols), 1)
    col_const = -(_alibi_slope(2 * g + (c1 >= blk).astype(jnp.int32)) * LOG2_E) * (i * blk).astype(F32)

    qt = q_ref[0].astype(F32)
    q2 = jnp.concatenate([qt, pltpu.roll(qt, HEAD_DIM, 1)], axis=0)
    q2t = jnp.where(l2 < HEAD_DIM, q2, fq_ref[...]).T
    is_q = lax.broadcasted_iota(jnp.int32, (LANES, 1), 0) < HEAD_DIM
    qt_ref[...] = (q2t * jnp.where(is_q, HEAD_DIM ** -0.5, 1.0)).astype(BF16)
    gate3 = jnp.dot(km_ref[0], qt_ref[...], preferred_element_type=F32)
    nbp = sel_ref.shape[0]
    gate = gate3[0:nbp] + gate3[nbp:2 * nbp] + gate3[2 * nbp:3 * nbp]
    sel_ref[...] = _top_blocks(gate, i, 0)

    def scores(n):
        start = pl.multiple_of(n * blk, blk)
        return jnp.dot(k_ref[0, pl.ds(start, blk), :], qt_ref[...], preferred_element_type=F32)

    ones_rows = jnp.ones((ACC_ROWS - HEAD_DIM, blk), BF16)

    def block_stats(s, n):
        mb = jnp.max(s, axis=0, keepdims=True)
        p = jnp.exp2((s - mb).astype(BF16))
        vt_aug = jnp.concatenate([vt_ref[0, n], ones_rows], axis=0)
        return mb, jnp.dot(vt_aug, p, preferred_element_type=F32)

    def merge(parts):
        m_old = m_ref[...]
        m_new = m_old
        for mb, _ in parts:
            m_new = jnp.maximum(m_new, mb)
        acc_new = jnp.exp2(m_old - m_new) * acc_ref[...]
        for mb, ab in parts:
            acc_new = acc_new + jnp.exp2(mb - m_new) * ab
        m_ref[...] = m_new
        acc_ref[...] = acc_new

    rr = lax.broadcasted_iota(jnp.int32, (blk, cols), 0)
    cc = lax.broadcasted_iota(jnp.int32, (blk, cols), 1)
    visible = jnp.where(cc >= blk, cc - blk, cc) >= rr
    m0, a0 = block_stats(jnp.where(visible, scores(i), -jnp.inf), i)
    m_ref[...] = m0 + col_const
    acc_ref[...] = a0

    n_groups = (i + grp - 1) // grp

    def clamp(n_raw):
        return jnp.maximum(jnp.minimum(n_raw, i - 1), 0)

    def score_group(kk, s_ref):
        for b in range(grp):
            s_ref[b * blk:(b + 1) * blk, :] = scores(clamp(kk * grp + b))

    def softmax_group(kk, s_ref):
        parts = []
        for b in range(grp):
            n_raw = kk * grp + b
            n = clamp(n_raw)
            mb, ab = block_stats(s_ref[b * blk:(b + 1) * blk, :], n)
            counted = sel_ref[pl.ds(n, 1), :] * (n_raw < i).astype(F32) > 0.0
            parts.append((jnp.where(counted, mb + col_const, NEG_BIG), ab))
        merge(parts)

    score_group(0, sa_ref)

    def step(kk, carry):
        score_group(2 * kk + 1, sb_ref)
        softmax_group(2 * kk, sa_ref)
        score_group(2 * kk + 2, sa_ref)
        softmax_group(2 * kk + 1, sb_ref)
        return carry

    lax.fori_loop(0, n_groups // 2, step, 0)

    @pl.when(n_groups % 2 == 1)
    def _last():
        softmax_group(n_groups - 1, sa_ref)

    o = acc_ref[0:HEAD_DIM, :] / acc_ref[HEAD_DIM:HEAD_DIM + 1, :]
    o_ref[0] = jnp.concatenate([o[:, 0:blk], o[:, blk:cols]], axis=0).T.astype(o_ref.dtype)


def _attn_prompt(q, kd, vt, kmd):
    bsz, t, d = q.shape
    blk = MOBA_BLOCK
    nb = t // blk
    nb_pad = kmd.shape[1] // N_SPLIT
    cols = 2 * blk
    return pl.pallas_call(
        _attn_prompt_kernel,
        grid=(bsz, N_KV_HEADS, nb),
        in_specs=[pl.BlockSpec((1, blk, LANES), lambda b, g, i: (b, i, g)),
                  pl.BlockSpec((1, t, LANES), lambda b, g, i: (b, 0, g)),
                  pl.BlockSpec((1, nb, HEAD_DIM, blk), lambda b, g, i: (b, 0, g, 0)),
                  pl.BlockSpec((1, N_SPLIT * nb_pad, LANES), lambda b, g, i: (b, 0, g))],
        out_specs=pl.BlockSpec((1, blk, LANES), lambda b, g, i: (b, i, g)),
        out_shape=jax.ShapeDtypeStruct((bsz, t, d), BF16),
        scratch_shapes=[pltpu.VMEM((LANES, cols), BF16), pltpu.VMEM((cols, LANES), F32),
                        pltpu.VMEM((ATT_GROUP * blk, cols), F32), pltpu.VMEM((ATT_GROUP * blk, cols), F32),
                        pltpu.VMEM((nb_pad, cols), F32), pltpu.VMEM((1, cols), F32),
                        pltpu.VMEM((ACC_ROWS, cols), F32)],
        compiler_params=_cparams(("parallel", "parallel", "arbitrary")),
        name="moba_prompt",
    )(q, kd, vt, kmd)


def _attn_sample_kernel(pt_ref, q_ref, kn_ref, vn_ref, *rest, n_pages, n_seqs):
    del pt_ref
    t_new = q_ref.shape[1] // n_seqs
    k_refs = rest[:n_seqs * n_pages]
    v_refs = rest[n_seqs * n_pages:2 * n_seqs * n_pages]
    o_ref, s_ref = rest[2 * n_seqs * n_pages:]
    for j in range(n_seqs):
        rows = slice(j * t_new, (j + 1) * t_new)
        pages = slice(j * n_pages, (j + 1) * n_pages)
        _attn_sample_one(q_ref.at[0, rows, :], kn_ref.at[0, rows, :], vn_ref.at[0, rows, :],
                         k_refs[pages], v_refs[pages], o_ref.at[0, rows, :], s_ref.at[j])


def _attn_sample_one(q_ref, kn_ref, vn_ref, k_refs, v_refs, o_ref, s_ref):
    n_pages = len(k_refs)
    t_new = q_ref.shape[0]
    past_len = n_pages * PAGE_SIZE
    kvw = N_KV_HEADS * HEAD_DIM
    rows = N_HEADS * t_new
    n_past_blk = past_len // MOBA_BLOCK
    pages_per_blk = MOBA_BLOCK // PAGE_SIZE
    q = q_ref[...]
    lane8 = lax.broadcasted_iota(jnp.int32, (t_new, LANES), 1)
    zeros8 = jnp.zeros((t_new, LANES), F32)
    row_blocks = []
    for h in range(N_HEADS):
        g = h // 2
        tile = q[:, g * LANES:(g + 1) * LANES]
        if h % 2 != g % 2:
            tile = pltpu.roll(tile, HEAD_DIM, 1)
        keep = (lane8 < HEAD_DIM) if g % 2 == 0 else (lane8 >= HEAD_DIM)
        tile = jnp.where(keep, tile, 0.0)
        row_blocks.append(jnp.concatenate([tile if cidx == g // 2 else zeros8 for cidx in range(kvw // LANES)], axis=1))
    qm = jnp.concatenate(row_blocks, axis=0)
    qmb = (qm * (HEAD_DIM ** -0.5)).astype(BF16)

    r1 = lax.broadcasted_iota(jnp.int32, (rows, 1), 0)
    slope = _alibi_slope(r1 // t_new)
    rr = lax.broadcasted_iota(jnp.int32, (rows, PAGE_SIZE), 0)
    cc = lax.broadcasted_iota(jnp.int32, (rows, PAGE_SIZE), 1)
    d_page0 = (past_len + rr % t_new - cc).astype(F32)

    lane_b = lax.broadcasted_iota(jnp.int32, (kvw, LANES), 1)
    kmt = jnp.zeros((kvw, LANES), F32)
    for n in range(n_past_blk):
        acc = k_refs[n * pages_per_blk][0].reshape(kvw, PAGE_SIZE)
        for pp in range(1, pages_per_blk):
            acc = acc + k_refs[n * pages_per_blk + pp][0].reshape(kvw, PAGE_SIZE)
        mean = jnp.sum(acc, axis=1, keepdims=True) * (1.0 / MOBA_BLOCK)
        kmt = jnp.where(lane_b == n, mean, kmt)
    gate = jnp.dot(qm, kmt, precision=HIGHEST, preferred_element_type=F32)
    sel = _top_blocks(gate, n_past_blk, 1)
    lane_s = lax.broadcasted_iota(jnp.int32, sel.shape, 1)

    for p in range(n_pages):
        kp = k_refs[p][0].reshape(kvw, PAGE_SIZE).astype(BF16)
        s = jnp.dot(qmb, kp, preferred_element_type=F32)
        s = s - slope * (d_page0 - float(p * PAGE_SIZE))
        picked = jnp.sum(jnp.where(lane_s == p // pages_per_blk, sel, 0.0), axis=-1, keepdims=True) > 0.0
        s_ref[:, p * PAGE_SIZE:(p + 1) * PAGE_SIZE] = jnp.where(picked, s, -jnp.inf)
    pad = jnp.zeros((LANES - t_new, kvw), F32)
    k_new = jnp.concatenate([kn_ref[...], pad], axis=0).astype(BF16)
    v_new = jnp.concatenate([vn_ref[...], pad], axis=0).astype(BF16)
    rr_own = lax.broadcasted_iota(jnp.int32, (rows, LANES), 0)
    cc_own = lax.broadcasted_iota(jnp.int32, (rows, LANES), 1)
    d_own = (rr_own % t_new - cc_own).astype(F32)
    s = lax.dot_general(qmb, k_new, NT_DIMS, preferred_element_type=F32) - slope * d_own
    s_ref[:, past_len:] = jnp.where((d_own >= 0.0) & (cc_own < t_new), s, -jnp.inf)

    s_all = s_ref[...]
    m = jnp.max(s_all, axis=-1, keepdims=True)
    p_all = jnp.exp(s_all - m)
    l = jnp.sum(p_all, axis=-1, keepdims=True)
    pb = p_all.astype(BF16)
    out = jnp.dot(pb[:, past_len:], v_new, preferred_element_type=F32)
    for p in range(n_pages):
        vp = v_refs[p][0].reshape(kvw, PAGE_SIZE).astype(BF16)
        out = out + lax.dot_general(pb[:, p * PAGE_SIZE:(p + 1) * PAGE_SIZE], vp, NT_DIMS,
                                    preferred_element_type=F32)
    out = out / l

    lo8 = lane8 < HEAD_DIM
    for cidx in range(N_HEADS // 2):
        src = slice((cidx // 2) * LANES, (cidx // 2 + 1) * LANES)
        a = out[(2 * cidx) * t_new:(2 * cidx + 1) * t_new, src]
        b = out[(2 * cidx + 1) * t_new:(2 * cidx + 2) * t_new, src]
        if cidx % 2 == 1:
            a = pltpu.roll(a, HEAD_DIM, 1)
        else:
            b = pltpu.roll(b, HEAD_DIM, 1)
        o_ref[:, cidx * LANES:(cidx + 1) * LANES] = jnp.where(lo8, a, b)


def _attn_sample(q, k_new, v_new, cache_kt, cache_vt, page_table, t_new):
    n_seq, n_pages = page_table.shape
    d = q.shape[-1]
    kvw = k_new.shape[-1]
    rows = N_HEADS * t_new
    n_seqs = 2 if n_seq % 2 == 0 else 1

    def page_spec(j, p):
        return pl.BlockSpec((1, N_KV_HEADS, HEAD_DIM, PAGE_SIZE), lambda s, pt: (pt[n_seqs * s + j, p], 0, 0, 0))

    page_specs = [page_spec(j, p) for j in range(n_seqs) for p in range(n_pages)]
    grid_spec = pltpu.PrefetchScalarGridSpec(
        num_scalar_prefetch=1,
        grid=(n_seq // n_seqs,),
        in_specs=[pl.BlockSpec((1, n_seqs * t_new, d), lambda s, pt: (0, s, 0)),
                  pl.BlockSpec((1, n_seqs * t_new, kvw), lambda s, pt: (0, s, 0)),
                  pl.BlockSpec((1, n_seqs * t_new, kvw), lambda s, pt: (0, s, 0))] + page_specs + page_specs,
        out_specs=pl.BlockSpec((1, n_seqs * t_new, d), lambda s, pt: (0, s, 0)),
        scratch_shapes=[pltpu.VMEM((n_seqs, rows, n_pages * PAGE_SIZE + LANES), F32)],
    )
    n_page_args = n_seqs * n_pages
    return pl.pallas_call(
        functools.partial(_attn_sample_kernel, n_pages=n_pages, n_seqs=n_seqs),
        grid_spec=grid_spec,
        out_shape=jax.ShapeDtypeStruct(q.shape, F32),
        compiler_params=_cparams(("arbitrary",), VMEM_BIG_MIB),
        name="moba_sample",
    )(page_table, q, k_new, v_new, *([cache_kt] * n_page_args), *([cache_vt] * n_page_args))


def _head_sum_matrices(n_heads):
    w = n_heads * HEAD_DIM
    head_of_lane = jnp.arange(w) // HEAD_DIM
    r = (head_of_lane[:, None] == jnp.arange(LANES)[None, :]).astype(BF16)
    return r, r.T


def _prep_weights(p):
    bf = lambda x: x.astype(BF16)
    w_in = p['ssm_w_in']
    wdt = jnp.pad(w_in[:, :, D_INNER + CONV_DIM:], ((0, 0), (0, 0), (0, LANES - SSM_HEADS)))
    pad_heads = lambda x: jnp.pad(x, ((0, 0), (0, LANES - SSM_HEADS)))
    head_params = jnp.stack([pad_heads(p['ssm_dt_bias']), pad_heads(p['ssm_a_log'])], axis=1)
    head_params = jnp.pad(head_params, ((0, 0), (0, SUBLANES - 2), (0, 0)))
    router_w = jnp.concatenate([p['moe_w_grp'], p['moe_w_exp']], axis=-1)
    router_w = jnp.pad(router_w, ((0, 0), (0, 0), (0, LANES - router_w.shape[-1])))
    router_b = jnp.concatenate([p['moe_b_grp'], p['moe_b_exp']], axis=-1)
    router_b = jnp.pad(router_b, ((0, 0), (0, LANES - router_b.shape[-1])))[:, None, :]
    rq, rqt = _head_sum_matrices(N_HEADS)
    rk, rkt = _head_sum_matrices(N_KV_HEADS)
    return dict(
        wz=bf(w_in[:, :, :D_INNER]), wxbc=bf(w_in[:, :, D_INNER:D_INNER + CONV_DIM]), wdt=bf(wdt),
        head_params=head_params, d_skip=jnp.repeat(p['ssm_d'], SSM_HEAD_DIM, axis=-1)[:, None, :],
        conv_w=p['ssm_conv_w'], conv_b=p['ssm_conv_b'][:, None, :], ssm_norm_w=p['ssm_norm_w'][:, None, :],
        w_out=bf(p['ssm_w_out']),
        norm_mix_w=p['norm_mix_w'][:, None, :], norm_ffn_w=p['norm_ffn_w'][:, None, :],
        kv_norm_w=p['kv_norm_w'][None, :], wk=bf(p['w_k']), wv=bf(p['w_v']),
        k_norm_w=jnp.tile(p['k_norm_w'], N_KV_HEADS)[None, :],
        wq=bf(p['attn_w_q']), q_norm_w=jnp.tile(p['q_norm_w'], (1, N_HEADS))[:, None, :], wo=bf(p['attn_w_o']),
        router_w=router_w, router_b=router_b,
        w1=bf(p['moe_w1']), w3=bf(p['moe_w3']), w2=bf(p['moe_w2']),
        rq=rq, rqt=rqt, rk=rk, rkt=rkt,
    )


def _trunk(h, mods, kv_mod, w, *, sample, conv0=None, ssm0=None, cache=None):
    bsz, t, _ = h.shape
    tm = min(t, TOKEN_TILE)
    conv_out = []
    st_all = None
    k_new = v_new = kd = vt = kmd = None
    for layer in range(DEPTH):
        sh_m, sc_m, g_m, sh_f, sc_f, g_f = mods[layer]
        nmw = w['norm_mix_w'][layer]
        if layer < N_A_LAYERS:
            z, xbc, dtr = _in_proj(h, nmw, sc_m, sh_m, w['wz'][layer], w['wxbc'][layer], w['wdt'][layer],
                                   min(t, IN_PROJ_TILE))
            ssd_args = (w['conv_w'][layer], w['conv_b'][layer], w['head_params'][layer], w['d_skip'][layer],
                        w['ssm_norm_w'][layer])
            if sample:
                n_seq, t_new = conv0.shape[1], cache[3]
                seqs = lambda x: x.reshape(n_seq, t_new, x.shape[-1])
                conv_in = jnp.pad(conv0[layer], ((0, 0), (HALO - (CONV_WIDTH - 1), 0), (0, 0)))
                h0_all = ssm0.reshape(N_A_LAYERS, n_seq, D_INNER, D_STATE)
                g, conv8, st_all = _ssd(seqs(z), seqs(xbc), seqs(dtr), conv_in, h0_all, st_all, layer,
                                        *ssd_args, g_dtype=F32)
                g = g.reshape(1, t, D_INNER)
            else:
                conv_in = jnp.zeros((bsz, HALO, CONV_DIM), F32)
                g, conv8, st_all = _ssd(z, xbc, dtr, conv_in, None, st_all, layer, *ssd_args, g_dtype=BF16)
            conv_out.append(conv8[:, HALO - (CONV_WIDTH - 1):])
            h = _mm_res(g, w['w_out'][layer], h, g_m, tm)
        else:
            j = layer - N_A_LAYERS
            if sample:
                q = _q_proj(h, nmw, sc_m, sh_m, w['wq'][j], w['q_norm_w'][j], w['rq'], w['rqt'], tm, F32)
                o = _attn_sample(q, k_new, v_new, cache[0], cache[1], cache[2], cache[3])
            else:
                q = _q_proj(h, nmw, sc_m, sh_m, w['wq'][j], w['q_norm_w'][j], w['rq'], w['rqt'], tm, BF16,
                            out_scale=LOG2_E)
                o = _attn_prompt(q, kd, vt, kmd)
            h = _mm_res(o, w['wo'][j], h, g_m, tm)
        h = _moe(h, w['norm_ffn_w'][layer], sc_f, sh_f, g_f, w['router_w'][layer], w['router_b'][layer],
                 w['w1'][layer], w['w3'][layer], w['w2'][layer], tm)
        if layer == N_A_LAYERS - 1:
            kv_tm = min(t, KV_TILE)
            res = _kv_proj(h, w['kv_norm_w'], kv_mod[1], kv_mod[0], w['wk'], w['wv'], w['k_norm_w'],
                           w['rk'], w['rkt'], kv_tm, with_dup=not sample)
            k_new, v_new = res[0], res[1]
            if not sample:
                kd, vt = res[2], res[3]
                kmd = res[4].reshape(bsz, N_SPLIT, t // MOBA_BLOCK, N_KV_HEADS * LANES)
                kmd = jnp.pad(kmd, ((0, 0), (0, 0), (0, -kmd.shape[2] % BF16_ROWS), (0, 0)))
                kmd = kmd.reshape(bsz, -1, N_KV_HEADS * LANES).astype(BF16)
    ssm_out = st_all.reshape(N_A_LAYERS, st_all.shape[1], SSM_HEADS, SSM_HEAD_DIM, D_STATE)
    return h, jnp.stack(conv_out), ssm_out, k_new, v_new


def kernel(x_prompt, x_sample, state_conv, state_ssm, cache_k, cache_v, page_table, c_prompt, c_sample, w_mod, b_mod, norm_mix_w, norm_ffn_w, ssm_w_in, ssm_conv_w, ssm_conv_b, ssm_dt_bias, ssm_a_log, ssm_d, ssm_norm_w, ssm_w_out, kv_w_mod, kv_b_mod, kv_norm_w, w_k, w_v, k_norm_w, attn_w_q, q_norm_w, attn_w_o, moe_w_grp, moe_b_grp, moe_w_exp, moe_b_exp, moe_w1, moe_w3, moe_w2):
    params = dict(ssm_w_in=ssm_w_in, ssm_conv_w=ssm_conv_w, ssm_conv_b=ssm_conv_b, ssm_dt_bias=ssm_dt_bias,
                  ssm_a_log=ssm_a_log, ssm_d=ssm_d, ssm_norm_w=ssm_norm_w, ssm_w_out=ssm_w_out,
                  norm_mix_w=norm_mix_w, norm_ffn_w=norm_ffn_w, kv_norm_w=kv_norm_w, w_k=w_k, w_v=w_v,
                  k_norm_w=k_norm_w, attn_w_q=attn_w_q, q_norm_w=q_norm_w, attn_w_o=attn_w_o,
                  moe_w_grp=moe_w_grp, moe_b_grp=moe_b_grp, moe_w_exp=moe_w_exp, moe_b_exp=moe_b_exp,
                  moe_w1=moe_w1, moe_w3=moe_w3, moe_w2=moe_w2)
    w = _prep_weights(params)

    bp, seq, d = x_prompt.shape
    n_seq, t_new, _ = x_sample.shape
    n_pages = page_table.shape[1]
    past_len = n_pages * PAGE_SIZE
    assert seq % MOBA_BLOCK == 0
    assert past_len % MOBA_BLOCK == 0 and t_new <= MOBA_BLOCK and t_new % SUBLANES == 0

    n_c = bp + n_seq
    n_c_pad = -(-n_c // SUBLANES) * SUBLANES
    c_all = jnp.pad(jnp.concatenate([c_prompt, c_sample], axis=0), ((0, n_c_pad - n_c), (0, 0)))
    mod_all = _mod_vectors(c_all, w_mod, b_mod[:, None, :], MOD_COLS)
    kv_all = _mod_vectors(c_all, kv_w_mod[None], kv_b_mod[None, None, :], MOD_COLS)[0]

    def group_mods(lo, hi, per_token_repeat):
        def shape(x):
            if per_token_repeat:
                return jnp.repeat(x, per_token_repeat, axis=0)[None]
            return x[:, None, :]
        mods = [[shape(mod_all[l, lo:hi, k * d:(k + 1) * d]) for k in range(6)] for l in range(DEPTH)]
        kvm = [shape(kv_all[lo:hi, k * d:(k + 1) * d]) for k in range(2)]
        return mods, kvm

    mods_p, kv_p = group_mods(0, bp, 0)
    y_p, conv_p, ssm_p, k_p, v_p = _trunk(x_prompt, mods_p, kv_p, w, sample=False)

    mods_s, kv_s = group_mods(bp, n_c, t_new)
    kvw = N_KV_HEADS * HEAD_DIM
    cache = (jnp.transpose(cache_k, (0, 2, 3, 1)), jnp.transpose(cache_v, (0, 2, 3, 1)), page_table, t_new)
    y_s, conv_s, ssm_s, k_s, v_s = _trunk(x_sample.reshape(1, n_seq * t_new, d), mods_s, kv_s, w,
                                          sample=True, conv0=state_conv, ssm0=state_ssm, cache=cache)

    heads = lambda x, b, t: x.reshape(b, t, N_KV_HEADS, HEAD_DIM)
    return (y_p, y_s.reshape(n_seq, t_new, d), conv_p, ssm_p, heads(k_p, bp, seq), heads(v_p, bp, seq),
            conv_s, ssm_s, heads(k_s, n_seq, t_new), heads(v_s, n_seq, t_new))
```

```python
import functools

import jax
import jax.numpy as jnp
from jax import lax
from jax.experimental import pallas as pl
from jax.experimental.pallas import tpu as pltpu

F32 = jnp.float32
BF16 = jnp.bfloat16
HIGHEST = lax.Precision.HIGHEST

DEPTH = 4
N_A_LAYERS = 2
D_INNER = 2048
SSM_HEADS = 32
SSM_HEAD_DIM = 64
SSM_GROUPS = 4
D_STATE = 128
CONV_WIDTH = 4
CONV_DIM = D_INNER + 2 * SSM_GROUPS * D_STATE
SSD_CHUNK = 128
N_HEADS = 16
HEAD_DIM = 64
N_KV_HEADS = 8
MOBA_BLOCK = 256
MOBA_TOPK = 3
N_EXPERT_GROUPS = 4
EXPERTS_PER_GROUP = 4
PAGE_SIZE = 128
EPS = 1e-6

LANES = 128
SUBLANES = 8
BF16_ROWS = 16
MIB = 1024 * 1024
NEG_BIG = -1e30

VMEM_MIB = 48
VMEM_BIG_MIB = 56
TOKEN_TILE = 1024
IN_PROJ_TILE = 512
KV_TILE = 512
MOD_COLS = 1024
MOE_EXPERTS_PER_STEP = 8

NT_DIMS = (((1,), (1,)), ((), ()))


def _cparams(sem, vmem_mib=VMEM_MIB):
    return pltpu.CompilerParams(dimension_semantics=sem, vmem_limit_bytes=vmem_mib * MIB)


def _sigmoid(x):
    return 1.0 / (1.0 + jnp.exp(-x))


def _silu(x):
    return x * _sigmoid(x)


def _norm_mod(h, nw, sc, sh):
    ms = jnp.mean(h * h, axis=-1, keepdims=True)
    return (h * lax.rsqrt(ms + EPS)) * nw * (1.0 + sc) + sh


def _split_bf16(x):
    hi = x.astype(BF16)
    lo = (x - hi.astype(F32)).astype(BF16)
    return hi, lo


def _head_rmsnorm(x, r, rt):
    sq = x * x
    hi, lo = _split_bf16(sq)
    ss = jnp.dot(hi, r, preferred_element_type=F32) + jnp.dot(lo, r, preferred_element_type=F32)
    inv = lax.rsqrt(ss * (1.0 / HEAD_DIM) + EPS)
    ih, il = _split_bf16(inv)
    invx = jnp.dot(ih, rt, preferred_element_type=F32) + jnp.dot(il, rt, preferred_element_type=F32)
    return x * invx


def _mod_kernel(c_ref, w_ref, b_ref, o_ref):
    ca = _silu(c_ref[...])
    o_ref[0] = jnp.dot(ca, w_ref[0], precision=HIGHEST, preferred_element_type=F32) + b_ref[0]


def _mod_vectors(c_all, w, b, tn):
    n_layers, d, n = w.shape
    m = c_all.shape[0]
    return pl.pallas_call(
        _mod_kernel,
        grid=(n_layers, n // tn),
        in_specs=[pl.BlockSpec((m, d), lambda l, j: (0, 0)),
                  pl.BlockSpec((1, d, tn), lambda l, j: (l, 0, j)),
                  pl.BlockSpec((1, 1, tn), lambda l, j: (l, 0, j))],
        out_specs=pl.BlockSpec((1, m, tn), lambda l, j: (l, 0, j)),
        out_shape=jax.ShapeDtypeStruct((n_layers, m, n), F32),
        compiler_params=_cparams(("arbitrary", "arbitrary")),
        name="mod_vectors",
    )(c_all, w, b)


def _mod_spec(mod, tm):
    d = mod.shape[-1]
    if mod.shape[1] == 1:
        return pl.BlockSpec((1, 1, d), lambda b, i, *_: (b, 0, 0))
    return pl.BlockSpec((1, tm, d), lambda b, i, *_: (b, i, 0))


def _in_proj_kernel(h_ref, nw_ref, sc_ref, sh_ref, wz_ref, wx_ref, wd_ref, z_ref, xbc_ref, dt_ref):
    u = _norm_mod(h_ref[0], nw_ref[...], sc_ref[0], sh_ref[0]).astype(BF16)
    z_ref[0] = jnp.dot(u, wz_ref[...], preferred_element_type=F32)
    xbc_ref[0] = jnp.dot(u, wx_ref[...], preferred_element_type=F32)
    dt_ref[0] = jnp.dot(u, wd_ref[...], preferred_element_type=F32)


def _in_proj(h, nw, sc, sh, wz, wxbc, wdt, tm):
    bsz, t, d = h.shape
    const = lambda w: pl.BlockSpec(w.shape, lambda b, i: (0, 0))
    out = lambda w: pl.BlockSpec((1, tm, w.shape[1]), lambda b, i: (b, i, 0))
    return pl.pallas_call(
        _in_proj_kernel,
        grid=(bsz, t // tm),
        in_specs=[pl.BlockSpec((1, tm, d), lambda b, i: (b, i, 0)), pl.BlockSpec((1, d), lambda b, i: (0, 0)),
                  _mod_spec(sc, tm), _mod_spec(sh, tm), const(wz), const(wxbc), const(wdt)],
        out_specs=[out(wz), out(wxbc), out(wdt)],
        out_shape=[jax.ShapeDtypeStruct((bsz, t, w.shape[1]), F32) for w in (wz, wxbc, wdt)],
        compiler_params=_cparams(("parallel", "parallel"), VMEM_BIG_MIB),
        name="ssm_in_proj",
    )(h, nw, sc, sh, wz, wxbc, wdt)


def _mm_res_kernel(a_ref, w_ref, h_ref, g_ref, o_ref):
    acc = jnp.dot(a_ref[0].astype(BF16), w_ref[...], preferred_element_type=F32)
    o_ref[0] = h_ref[0] + g_ref[0] * acc


def _mm_res(a, w, h, gate, tm):
    bsz, t, k = a.shape
    d = w.shape[1]
    return pl.pallas_call(
        _mm_res_kernel,
        grid=(bsz, t // tm),
        in_specs=[pl.BlockSpec((1, tm, k), lambda b, i: (b, i, 0)),
                  pl.BlockSpec((k, d), lambda b, i: (0, 0)),
                  pl.BlockSpec((1, tm, d), lambda b, i: (b, i, 0)),
                  _mod_spec(gate, tm)],
        out_specs=pl.BlockSpec((1, tm, d), lambda b, i: (b, i, 0)),
        out_shape=jax.ShapeDtypeStruct((bsz, t, d), F32),
        compiler_params=_cparams(("parallel", "parallel")),
        name="matmul_residual",
    )(a, w, h, gate)


HALO = SUBLANES
XC_TILE = 512


def _ssd_kernel(z_ref, xbc_ref, dt_ref, cs_ref, h0_ref, cw_ref, cb_ref, hp_ref, dsk_ref, nw_ref,
                *rest, seqs, valid, has_init, has_prev):
    g_ref, cso_ref, st_ref, *scratch = rest[1:] if has_prev else rest
    for j in range(seqs):
        one = lambda ref: ref.at[pl.ds(j, 1)]
        _ssd_one(one(z_ref), one(xbc_ref), one(dt_ref), one(cs_ref), one(h0_ref) if has_init else h0_ref,
                 cw_ref, cb_ref, hp_ref, dsk_ref, nw_ref, one(g_ref), one(cso_ref), one(st_ref),
                 *[ref.at[j] for ref in scratch], valid=valid, has_init=has_init)


def _ssd_one(z_ref, xbc_ref, dt_ref, cs_ref, h0_ref, cw_ref, cb_ref, hp_ref, dsk_ref, nw_ref,
             g_ref, cso_ref, st_ref, xs_ref, xc_ref, xt_ref, y_ref, *, valid, has_init):
    L = SSD_CHUNK
    c = pl.program_id(1)

    @pl.when(c == 0)
    def _init():
        xs_ref[0:HALO, :] = cs_ref[0]
        if valid < L:
            xc_ref[...] = jnp.zeros_like(xc_ref)
        if has_init:
            st_ref[0] = h0_ref[0]
        else:
            st_ref[0] = jnp.zeros(st_ref.shape[1:], F32)

    xs_ref[HALO:HALO + valid, :] = xbc_ref[0]

    for j in range(CONV_DIM // XC_TILE):
        sl = slice(j * XC_TILE, (j + 1) * XC_TILE)
        x_rows = xs_ref[0:HALO + valid, sl]
        acc = x_rows * cw_ref[0:1, sl]
        for k in range(1, CONV_WIDTH):
            acc = pltpu.roll(acc, 1, 0) + x_rows * cw_ref[k:k + 1, sl]
        xc_ref[0:valid, sl] = _silu(acc[HALO:HALO + valid] + cb_ref[:, sl])

    cso_ref[0] = xs_ref[valid:valid + HALO, :]
    xs_ref[0:HALO, :] = xs_ref[valid:valid + HALO, :]

    dtr = dt_ref[0]
    if valid < L:
        dtr = jnp.concatenate([dtr, jnp.zeros((L - valid, LANES), F32)], axis=0)
    xx = dtr + hp_ref[0:1, :]
    dt = jnp.maximum(xx, 0.0) + jnp.log(1.0 + jnp.exp(-jnp.abs(xx)))
    row = lax.broadcasted_iota(jnp.int32, (L, L), 0)
    col = lax.broadcasted_iota(jnp.int32, (L, L), 1)
    if valid < L:
        dt = jnp.where(row < valid, dt, 0.0)
    a = dt * (-jnp.exp(hp_ref[1:2, :]))
    causal = row >= col
    a_cum = jnp.dot(causal.astype(F32), a, precision=HIGHEST, preferred_element_type=F32)
    a_cum_t = a_cum.T
    dt_t = dt.T
    sub_lo = row < HEAD_DIM
    R = L if valid == L else -(-valid // BF16_ROWS) * BF16_ROWS
    row_r = lax.broadcasted_iota(jnp.int32, (R, L), 0)
    col_r = lax.broadcasted_iota(jnp.int32, (R, L), 1)
    causal_r = row_r >= col_r
    lane_lo = col_r < HEAD_DIM

    for j in range(D_INNER // LANES):
        xt_ref[j * LANES:(j + 1) * LANES, :] = xc_ref[:, j * LANES:(j + 1) * LANES].T

    pairs_per_group = SSM_HEADS // SSM_GROUPS // 2
    grp_w = D_INNER // SSM_GROUPS
    for g in range(SSM_GROUPS):
        b_g = xc_ref[:, D_INNER + g * D_STATE:D_INNER + (g + 1) * D_STATE].astype(BF16)
        c_g = xc_ref[:, D_INNER + (SSM_GROUPS + g) * D_STATE:
                     D_INNER + (SSM_GROUPS + g + 1) * D_STATE].astype(BF16)
        cb = lax.dot_general(c_g[0:R], b_g, NT_DIMS, preferred_element_type=F32)
        st_g = st_ref[0, g * grp_w:(g + 1) * grp_w, :]
        y_off = lax.dot_general(c_g[0:R], st_g.astype(BF16), NT_DIMS, preferred_element_type=F32)
        xw_parts, dec_parts = [], []
        for jj in range(pairs_per_group):
            pair = g * pairs_per_group + jj
            s_mats, e_cols, w_rows, d_end = [], [], [], []
            for h in (2 * pair, 2 * pair + 1):
                colb = jnp.broadcast_to(a_cum[0:R, h:h + 1], (R, L))
                rowb = a_cum_t[h:h + 1, :]
                dec = jnp.where(causal_r, jnp.exp(colb - rowb), 0.0)
                s_mats.append((cb * dec * dt_t[h:h + 1, :]).astype(BF16))
                e_cols.append(jnp.exp(colb))
                a_last = a_cum_t[h:h + 1, L - 1:L]
                w_rows.append(dt_t[h:h + 1, :] * jnp.exp(a_last - rowb))
                d_end.append(jnp.broadcast_to(jnp.exp(a_last), (LANES, D_STATE)))
            psl = slice(pair * LANES, (pair + 1) * LANES)
            xp = xc_ref[:, psl]
            xpb = xp.astype(BF16)
            y_d = jnp.where(lane_lo,
                            jnp.dot(s_mats[0], xpb, preferred_element_type=F32),
                            jnp.dot(s_mats[1], xpb, preferred_element_type=F32))
            y_o = y_off[:, jj * LANES:(jj + 1) * LANES] * jnp.where(lane_lo, e_cols[0], e_cols[1])
            y_ref[0:R, psl] = y_d + y_o + dsk_ref[:, psl] * xp[0:R]
            xw_parts.append((xt_ref[psl, :] * jnp.where(sub_lo, w_rows[0], w_rows[1])).astype(BF16))
            dec_parts.append(jnp.where(sub_lo, d_end[0], d_end[1]))
        xw_g = jnp.concatenate(xw_parts, axis=0)
        new = jnp.dot(xw_g, b_g, preferred_element_type=F32)
        st_ref[0, g * grp_w:(g + 1) * grp_w, :] = st_g * jnp.concatenate(dec_parts, axis=0) + new

    zz = z_ref[0]
    gg = y_ref[0:valid, :] * _silu(zz)
    for g in range(SSM_GROUPS):
        sl = slice(g * grp_w, (g + 1) * grp_w)
        seg = gg[:, sl]
        ms = jnp.mean(seg * seg, axis=-1, keepdims=True)
        g_ref[0, :, sl] = (seg * lax.rsqrt(ms + EPS) * nw_ref[:, sl]).astype(g_ref.dtype)


def _ssd(z, xbc, dtr, conv_in, h0_all, st_all, layer, conv_w, conv_b, head_params, d_skip, norm_w, g_dtype):
    bsz, t, _ = z.shape
    L = SSD_CHUNK
    valid = min(L, t)
    assert t % valid == 0 and valid % SUBLANES == 0
    nc = t // valid
    has_init = h0_all is not None
    seqs = 2 if bsz % 2 == 0 else 1
    state_spec = pl.BlockSpec((None, seqs, D_INNER, D_STATE), lambda b, c: (layer, b, 0, 0))
    if h0_all is None:
        h0_all = jnp.zeros((1, SUBLANES, D_STATE), F32)
        h0_spec = pl.BlockSpec((1, SUBLANES, D_STATE), lambda b, c: (0, 0, 0))
    else:
        h0_spec = state_spec
    full = lambda shape: pl.BlockSpec(shape, lambda b, c: (0,) * len(shape))
    operands = [z, xbc, dtr, conv_in, h0_all, conv_w, conv_b, head_params, d_skip, norm_w]
    in_specs = [pl.BlockSpec((seqs, valid, D_INNER), lambda b, c: (b, c, 0)),
                pl.BlockSpec((seqs, valid, CONV_DIM), lambda b, c: (b, c, 0)),
                pl.BlockSpec((seqs, valid, LANES), lambda b, c: (b, c, 0)),
                pl.BlockSpec((seqs, HALO, CONV_DIM), lambda b, c: (b, 0, 0)),
                h0_spec,
                full((CONV_WIDTH, CONV_DIM)), full((1, CONV_DIM)), full((SUBLANES, LANES)),
                full((1, D_INNER)), full((1, D_INNER))]
    aliases = {}
    if st_all is not None:
        aliases = {len(operands): 2}
        operands.append(st_all)
        in_specs.append(pl.BlockSpec(memory_space=pl.ANY))
    return pl.pallas_call(
        functools.partial(_ssd_kernel, seqs=seqs, valid=valid, has_init=has_init, has_prev=st_all is not None),
        grid=(bsz // seqs, nc),
        in_specs=in_specs,
        out_specs=[pl.BlockSpec((seqs, valid, D_INNER), lambda b, c: (b, c, 0)),
                   pl.BlockSpec((seqs, HALO, CONV_DIM), lambda b, c: (b, 0, 0)),
                   state_spec],
        out_shape=[jax.ShapeDtypeStruct((bsz, t, D_INNER), g_dtype),
                   jax.ShapeDtypeStruct((bsz, HALO, CONV_DIM), F32),
                   jax.ShapeDtypeStruct((N_A_LAYERS, bsz, D_INNER, D_STATE), F32)],
        input_output_aliases=aliases,
        scratch_shapes=[pltpu.VMEM((seqs, L + HALO, CONV_DIM), F32),
                        pltpu.VMEM((seqs, L, CONV_DIM), F32),
                        pltpu.VMEM((seqs, D_INNER, L), F32),
                        pltpu.VMEM((seqs, L, D_INNER), F32)],
        compiler_params=_cparams(("parallel", "arbitrary")),
        name="ssd_chunk_scan",
    )(*operands)


ROUTER_EXP0 = N_EXPERT_GROUPS


def _route(logits):
    lane = lax.broadcasted_iota(jnp.int32, logits.shape, 1)
    big = jnp.int32(2 * LANES)
    is_grp = lane < N_EXPERT_GROUPS
    gl = jnp.where(is_grp, logits, -jnp.inf)
    gmax = jnp.max(gl, axis=-1, keepdims=True)
    gidx = jnp.min(jnp.where(gl == gmax, lane, big), axis=-1, keepdims=True)
    p_grp = 1.0 / jnp.sum(jnp.where(is_grp, jnp.exp(gl - gmax), 0.0), axis=-1, keepdims=True)
    e_rel = lane - ROUTER_EXP0
    in_grp = (e_rel >= gidx * EXPERTS_PER_GROUP) & (e_rel < (gidx + 1) * EXPERTS_PER_GROUP)
    el = jnp.where(in_grp, logits, -jnp.inf)
    m1 = jnp.max(el, axis=-1, keepdims=True)
    i1 = jnp.min(jnp.where(el == m1, lane, big), axis=-1, keepdims=True)
    el2 = jnp.where(lane == i1, -jnp.inf, el)
    m2 = jnp.max(el2, axis=-1, keepdims=True)
    i2 = jnp.min(jnp.where(el2 == m2, lane, big), axis=-1, keepdims=True)
    e2 = jnp.exp(m2 - m1)
    den = 1.0 + e2
    w1 = (1.0 / den) * p_grp
    w2 = (e2 / den) * p_grp
    return jnp.where(lane == i1, w1, jnp.where(lane == i2, w2, 0.0))


def _moe_kernel(h_ref, nw_ref, sc_ref, sh_ref, gf_ref, wr_ref, br_ref, w1_ref, w3_ref, w2_ref,
                o_ref, u_ref, gates_ref, acc_ref):
    step = pl.program_id(2)
    per_step = w1_ref.shape[0]

    @pl.when(step == 0)
    def _():
        u = _norm_mod(h_ref[0], nw_ref[...], sc_ref[0], sh_ref[0])
        u_hi, u_lo = _split_bf16(u)
        w_hi, w_lo = _split_bf16(wr_ref[...])
        logits = (jnp.dot(u_hi, w_hi, preferred_element_type=F32) + jnp.dot(u_lo, w_hi, preferred_element_type=F32)
                  + jnp.dot(u_hi, w_lo, preferred_element_type=F32)) + br_ref[...]
        gates_ref[...] = _route(logits)
        u_ref[...] = u_hi
        acc_ref[...] = jnp.zeros_like(acc_ref)

    u = u_ref[...]
    gates = gates_ref[...]
    lane = lax.broadcasted_iota(jnp.int32, gates.shape, 1)
    acts = []
    for j in range(per_step):
        a = (_silu(jnp.dot(u, w1_ref[j], preferred_element_type=F32))
             * jnp.dot(u, w3_ref[j], preferred_element_type=F32))
        e = step * per_step + j
        gcol = jnp.sum(jnp.where(lane == e + ROUTER_EXP0, gates, 0.0), axis=-1, keepdims=True)
        acts.append((a * gcol).astype(BF16))
    w2 = w2_ref[...].reshape(per_step * w2_ref.shape[1], w2_ref.shape[2])
    acc_ref[...] += jnp.dot(jnp.concatenate(acts, axis=1), w2, preferred_element_type=F32)

    @pl.when(step == pl.num_programs(2) - 1)
    def _():
        o_ref[0] = h_ref[0] + gf_ref[0] * acc_ref[...]


def _moe(h, nw, sc, sh, gf, wr, br, w1, w3, w2, tm):
    bsz, t, d = h.shape
    n_e, _, f = w1.shape
    per_step = MOE_EXPERTS_PER_STEP if sc.shape[1] == 1 else EXPERTS_PER_GROUP
    return pl.pallas_call(
        _moe_kernel,
        grid=(bsz, t // tm, n_e // per_step),
        in_specs=[pl.BlockSpec((1, tm, d), lambda b, i, e: (b, i, 0)),
                  pl.BlockSpec((1, d), lambda b, i, e: (0, 0)),
                  _mod_spec(sc, tm), _mod_spec(sh, tm), _mod_spec(gf, tm),
                  pl.BlockSpec((d, LANES), lambda b, i, e: (0, 0)),
                  pl.BlockSpec((1, LANES), lambda b, i, e: (0, 0)),
                  pl.BlockSpec((per_step, d, f), lambda b, i, e: (e, 0, 0)),
                  pl.BlockSpec((per_step, d, f), lambda b, i, e: (e, 0, 0)),
                  pl.BlockSpec((per_step, f, d), lambda b, i, e: (e, 0, 0))],
        out_specs=pl.BlockSpec((1, tm, d), lambda b, i, e: (b, i, 0)),
        out_shape=jax.ShapeDtypeStruct((bsz, t, d), F32),
        scratch_shapes=[pltpu.VMEM((tm, d), BF16), pltpu.VMEM((tm, LANES), F32), pltpu.VMEM((tm, d), F32)],
        compiler_params=_cparams(("parallel", "parallel", "arbitrary"), VMEM_BIG_MIB),
        name="hmoe",
    )(h, nw, sc, sh, gf, wr, br, w1, w3, w2)


N_SPLIT = 3
FEAT_POS_LO = HEAD_DIM
FEAT_POS_HI = FEAT_POS_LO + N_SPLIT
FEAT_ONE = FEAT_POS_HI + N_SPLIT


def _split3(x):
    h1 = x.astype(BF16).astype(F32)
    r = x - h1
    h2 = r.astype(BF16).astype(F32)
    h3 = (r - h2).astype(BF16).astype(F32)
    return h1, h2, h3


def _feature_lanes(lane, base, parts, other):
    out = other
    for k, part in enumerate(parts):
        out = jnp.where(lane == base + k, part, out)
    return out


def _key_tiles(x, feat):
    lane = lax.broadcasted_iota(jnp.int32, x.shape, 1)
    rolled = pltpu.roll(x, HEAD_DIM, 1)
    lo = lane < HEAD_DIM
    return jnp.where(lo, x, feat), jnp.where(lo, rolled, feat)


def _kv_kernel(h_ref, nw_ref, sc_ref, sh_ref, wk_ref, wv_ref, knw_ref, r_ref, rt_ref,
               kf_ref, vf_ref, *dup_refs, tm, with_dup):
    u = _norm_mod(h_ref[0], nw_ref[...], sc_ref[0], sh_ref[0]).astype(BF16)
    k = jnp.dot(u, wk_ref[...], preferred_element_type=F32)
    v = jnp.dot(u, wv_ref[...], preferred_element_type=F32)
    kn = _head_rmsnorm(k, r_ref[...], rt_ref[...]) * knw_ref[...]
    if not with_dup:
        kf_ref[0] = kn
        vf_ref[0] = v
    else:
        for hh in range(N_KV_HEADS):
            rows = pl.ds(hh, tm, stride=N_KV_HEADS)
            for src, dst in ((kn, kf_ref), (v, vf_ref)):
                tile = src[:, (hh // 2) * LANES:(hh // 2 + 1) * LANES]
                if hh % 2 == 1:
                    tile = pltpu.roll(tile, HEAD_DIM, 1)
                dst[0, rows, :] = tile[:, :HEAD_DIM]
        kd_ref, vt_ref, km_ref = dup_refs
        lane = lax.broadcasted_iota(jnp.int32, (tm, LANES), 1)
        pos = pl.program_id(1) * tm + lax.broadcasted_iota(jnp.int32, (tm, LANES), 0)
        pos_lo = pos % MOBA_BLOCK
        feat = jnp.zeros((tm, LANES), F32)
        feat = _feature_lanes(lane, FEAT_POS_LO, [pos_lo.astype(F32)] * N_SPLIT, feat)
        feat = _feature_lanes(lane, FEAT_POS_HI, [(pos - pos_lo).astype(F32)] * N_SPLIT, feat)
        feat = _feature_lanes(lane, FEAT_ONE, [jnp.ones((tm, LANES), F32)] * N_SPLIT, feat)
        lane_row = lax.broadcasted_iota(jnp.int32, (1, LANES), 1)
        for cidx in range(N_KV_HEADS // 2):
            d0, d1 = _key_tiles(kn[:, cidx * LANES:(cidx + 1) * LANES], feat)
            for hh, dd in ((2 * cidx, d0), (2 * cidx + 1, d1)):
                hsl = slice(hh * LANES, (hh + 1) * LANES)
                kd_ref[0, :, hsl] = dd.astype(BF16)
                for blk in range(tm // MOBA_BLOCK):
                    mean = jnp.mean(dd[blk * MOBA_BLOCK:(blk + 1) * MOBA_BLOCK], axis=0, keepdims=True)
                    for part, piece in enumerate(_split3(jnp.where(lane_row < HEAD_DIM, mean, 0.0))):
                        km_ref[0, part, blk, :, hsl] = piece
        for blk in range(tm // MOBA_BLOCK):
            vt_ref[0, blk] = v[blk * MOBA_BLOCK:(blk + 1) * MOBA_BLOCK].T.astype(BF16)


def _kv_proj(h, nw, sc, sh, wk, wv, knw, r, rt, tm, with_dup):
    bsz, t, d = h.shape
    kvw = wk.shape[1]
    const = lambda shape: pl.BlockSpec(shape, lambda b, i: (0,) * len(shape))
    if not with_dup:
        out_specs = [pl.BlockSpec((1, tm, kvw), lambda b, i: (b, i, 0))] * 2
        out_shape = [jax.ShapeDtypeStruct((bsz, t, kvw), F32)] * 2
    else:
        out_specs = [pl.BlockSpec((1, tm * N_KV_HEADS, HEAD_DIM), lambda b, i: (b, i, 0))] * 2
        out_shape = [jax.ShapeDtypeStruct((bsz, t * N_KV_HEADS, HEAD_DIM), F32)] * 2
        dupw = N_KV_HEADS * LANES
        nblk = tm // MOBA_BLOCK
        out_specs += [pl.BlockSpec((1, tm, dupw), lambda b, i: (b, i, 0)),
                      pl.BlockSpec((1, nblk, kvw, MOBA_BLOCK), lambda b, i: (b, i, 0, 0)),
                      pl.BlockSpec((1, N_SPLIT, nblk, 1, dupw), lambda b, i: (b, 0, i, 0, 0))]
        out_shape += [jax.ShapeDtypeStruct((bsz, t, dupw), BF16),
                      jax.ShapeDtypeStruct((bsz, t // MOBA_BLOCK, kvw, MOBA_BLOCK), BF16),
                      jax.ShapeDtypeStruct((bsz, N_SPLIT, t // MOBA_BLOCK, 1, dupw), F32)]
    return pl.pallas_call(
        functools.partial(_kv_kernel, tm=tm, with_dup=with_dup),
        grid=(bsz, t // tm),
        in_specs=[pl.BlockSpec((1, tm, d), lambda b, i: (b, i, 0)),
                  const((1, d)), _mod_spec(sc, tm), _mod_spec(sh, tm),
                  const((d, kvw)), const((d, kvw)), const((1, kvw)),
                  const((kvw, LANES)), const((LANES, kvw))],
        out_specs=out_specs,
        out_shape=out_shape,
        compiler_params=_cparams(("parallel", "parallel")),
        name="shared_kv",
    )(h, nw, sc, sh, wk, wv, knw, r, rt)


def _q_kernel(h_ref, nw_ref, sc_ref, sh_ref, wq_ref, qnw_ref, r_ref, rt_ref, q_ref, *, out_scale):
    u = _norm_mod(h_ref[0], nw_ref[...], sc_ref[0], sh_ref[0]).astype(BF16)
    q = jnp.dot(u, wq_ref[...], preferred_element_type=F32)
    q = _head_rmsnorm(q, r_ref[...], rt_ref[...]) * qnw_ref[...]
    if out_scale != 1.0:
        q = q * out_scale
    q_ref[0] = q.astype(q_ref.dtype)


def _q_proj(h, nw, sc, sh, wq, qnw, r, rt, tm, out_dtype, out_scale=1.0):
    bsz, t, d = h.shape
    const = lambda shape: pl.BlockSpec(shape, lambda b, i: (0,) * len(shape))
    return pl.pallas_call(
        functools.partial(_q_kernel, out_scale=out_scale),
        grid=(bsz, t // tm),
        in_specs=[pl.BlockSpec((1, tm, d), lambda b, i: (b, i, 0)),
                  const((1, d)), _mod_spec(sc, tm), _mod_spec(sh, tm),
                  const((d, d)), const((1, d)), const((d, LANES)), const((LANES, d))],
        out_specs=pl.BlockSpec((1, tm, d), lambda b, i: (b, i, 0)),
        out_shape=jax.ShapeDtypeStruct((bsz, t, d), out_dtype),
        compiler_params=_cparams(("parallel", "parallel")),
        name="q_proj",
    )(h, nw, sc, sh, wq, qnw, r, rt)


def _top_blocks(gate, n_valid, axis):
    pos = lax.broadcasted_iota(jnp.int32, gate.shape, axis)
    gm = jnp.where(pos < n_valid, gate, -jnp.inf)
    sel = jnp.zeros(gate.shape, F32)
    for _ in range(MOBA_TOPK):
        mx = jnp.max(gm, axis=axis, keepdims=True)
        cand = (gm == mx) & (mx > -jnp.inf)
        idx = jnp.min(jnp.where(cand, pos, jnp.int32(2 * LANES)), axis=axis, keepdims=True)
        pick = pos == idx
        sel = jnp.where(pick, 1.0, sel)
        gm = jnp.where(pick, -jnp.inf, gm)
    return sel


def _alibi_slope(head):
    return jnp.exp2(-8.0 * (head + 1).astype(F32) / N_HEADS)


ATT_GROUP = 2
ACC_ROWS = HEAD_DIM + BF16_ROWS
LOG2_E = 1.4426950408889634


def _attn_prompt_kernel(q_ref, k_ref, vt_ref, km_ref, o_ref,
                        qt_ref, fq_ref, sa_ref, sb_ref, sel_ref, m_ref, acc_ref):
    g = pl.program_id(1)
    i = pl.program_id(2)
    blk = MOBA_BLOCK
    cols = 2 * blk
    grp = ATT_GROUP

    l2 = lax.broadcasted_iota(jnp.int32, (cols, LANES), 1)

    @pl.when(i == 0)
    def _features():
        r2 = lax.broadcasted_iota(jnp.int32, (cols, LANES), 0)
        second = r2 >= blk
        slope = _alibi_slope(2 * g + second.astype(jnp.int32)) * LOG2_E
        offs_q = jnp.where(second, r2 - blk, r2).astype(F32)
        slope_parts = _split3(slope)
        feat = jnp.zeros((cols, LANES), F32)
        feat = _feature_lanes(l2, FEAT_POS_LO, slope_parts, feat)
        feat = _feature_lanes(l2, FEAT_POS_HI, slope_parts, feat)
        fq_ref[...] = _feature_lanes(l2, FEAT_ONE, _split3(-slope * offs_q), feat)

    c1 = lax.broadcasted_iota(jnp.int32, (1, cols), 1)
    col_const = -(_alibi_slope(2 * g + (c1 >= blk).astype(jnp.int32)) * LOG2_E) * (i * blk).astype(F32)

    qt = q_ref[0].astype(F32)
    q2 = jnp.concatenate([qt, pltpu.roll(qt, HEAD_DIM, 1)], axis=0)
    q2t = jnp.where(l2 < HEAD_DIM, q2, fq_ref[...]).T
    is_q = lax.broadcasted_iota(jnp.int32, (LANES, 1), 0) < HEAD_DIM
    qt_ref[...] = (q2t * jnp.where(is_q, HEAD_DIM ** -0.5, 1.0)).astype(BF16)
    gate3 = jnp.dot(km_ref[0], qt_ref[...], preferred_element_type=F32)
    nbp = sel_ref.shape[0]
    gate = gate3[0:nbp] + gate3[nbp:2 * nbp] + gate3[2 * nbp:3 * nbp]
    sel_ref[...] = _top_blocks(gate, i, 0)

    def scores(n):
        start = pl.multiple_of(n * blk, blk)
        return jnp.dot(k_ref[0, pl.ds(start, blk), :], qt_ref[...], preferred_element_type=F32)

    ones_rows = jnp.ones((ACC_ROWS - HEAD_DIM, blk), BF16)

    def block_stats(s, n):
        mb = jnp.max(s, axis=0, keepdims=True)
        p = jnp.exp2((s - mb).astype(BF16))
        vt_aug = jnp.concatenate([vt_ref[0, n], ones_rows], axis=0)
        return mb, jnp.dot(vt_aug, p, preferred_element_type=F32)

    def merge(parts):
        m_old = m_ref[...]
        m_new = m_old
        for mb, _ in parts:
            m_new = jnp.maximum(m_new, mb)
        acc_new = jnp.exp2(m_old - m_new) * acc_ref[...]
        for mb, ab in parts:
            acc_new = acc_new + jnp.exp2(mb - m_new) * ab
        m_ref[...] = m_new
        acc_ref[...] = acc_new

    rr = lax.broadcasted_iota(jnp.int32, (blk, cols), 0)
    cc = lax.broadcasted_iota(jnp.int32, (blk, cols), 1)
    visible = jnp.where(cc >= blk, cc - blk, cc) >= rr
    m0, a0 = block_stats(jnp.where(visible, scores(i), -jnp.inf), i)
    m_ref[...] = m0 + col_const
    acc_ref[...] = a0

    n_groups = (i + grp - 1) // grp

    def clamp(n_raw):
        return jnp.maximum(jnp.minimum(n_raw, i - 1), 0)

    def score_group(kk, s_ref):
        for b in range(grp):
            s_ref[b * blk:(b + 1) * blk, :] = scores(clamp(kk * grp + b))

    def softmax_group(kk, s_ref):
        parts = []
        for b in range(grp):
            n_raw = kk * grp + b
            n = clamp(n_raw)
            mb, ab = block_stats(s_ref[b * blk:(b + 1) * blk, :], n)
            counted = sel_ref[pl.ds(n, 1), :] * (n_raw < i).astype(F32) > 0.0
            parts.append((jnp.where(counted, mb + col_const, NEG_BIG), ab))
        merge(parts)

    score_group(0, sa_ref)

    def step(kk, carry):
        score_group(2 * kk + 1, sb_ref)
        softmax_group(2 * kk, sa_ref)
        score_group(2 * kk + 2, sa_ref)
        softmax_group(2 * kk + 1, sb_ref)
        return carry

    lax.fori_loop(0, n_groups // 2, step, 0)

    @pl.when(n_groups % 2 == 1)
    def _last():
        softmax_group(n_groups - 1, sa_ref)

    o = acc_ref[0:HEAD_DIM, :] / acc_ref[HEAD_DIM:HEAD_DIM + 1, :]
    o_ref[0] = jnp.concatenate([o[:, 0:blk], o[:, blk:cols]], axis=0).T.astype(o_ref.dtype)


def _attn_prompt(q, kd, vt, kmd):
    bsz, t, d = q.shape
    blk = MOBA_BLOCK
    nb = t // blk
    nb_pad = kmd.shape[1] // N_SPLIT
    cols = 2 * blk
    return pl.pallas_call(
        _attn_prompt_kernel,
        grid=(bsz, N_KV_HEADS, nb),
        in_specs=[pl.BlockSpec((1, blk, LANES), lambda b, g, i: (b, i, g)),
                  pl.BlockSpec((1, t, LANES), lambda b, g, i: (b, 0, g)),
                  pl.BlockSpec((1, nb, HEAD_DIM, blk), lambda b, g, i: (b, 0, g, 0)),
                  pl.BlockSpec((1, N_SPLIT * nb_pad, LANES), lambda b, g, i: (b, 0, g))],
        out_specs=pl.BlockSpec((1, blk, LANES), lambda b, g, i: (b, i, g)),
        out_shape=jax.ShapeDtypeStruct((bsz, t, d), BF16),
        scratch_shapes=[pltpu.VMEM((LANES, cols), BF16), pltpu.VMEM((cols, LANES), F32),
                        pltpu.VMEM((ATT_GROUP * blk, cols), F32), pltpu.VMEM((ATT_GROUP * blk, cols), F32),
                        pltpu.VMEM((nb_pad, cols), F32), pltpu.VMEM((1, cols), F32),
                        pltpu.VMEM((ACC_ROWS, cols), F32)],
        compiler_params=_cparams(("parallel", "parallel", "arbitrary")),
        name="moba_prompt",
    )(q, kd, vt, kmd)


def _attn_sample_kernel(pt_ref, q_ref, kn_ref, vn_ref, *rest, n_pages, n_seqs):
    del pt_ref
    t_new = q_ref.shape[1] // n_seqs
    k_refs = rest[:n_seqs * n_pages]
    v_refs = rest[n_seqs * n_pages:2 * n_seqs * n_pages]
    o_ref, s_ref = rest[2 * n_seqs * n_pages:]
    for j in range(n_seqs):
        rows = slice(j * t_new, (j + 1) * t_new)
        pages = slice(j * n_pages, (j + 1) * n_pages)
        _attn_sample_one(q_ref.at[0, rows, :], kn_ref.at[0, rows, :], vn_ref.at[0, rows, :],
                         k_refs[pages], v_refs[pages], o_ref.at[0, rows, :], s_ref.at[j])


def _attn_sample_one(q_ref, kn_ref, vn_ref, k_refs, v_refs, o_ref, s_ref):
    n_pages = len(k_refs)
    t_new = q_ref.shape[0]
    past_len = n_pages * PAGE_SIZE
    kvw = N_KV_HEADS * HEAD_DIM
    rows = N_HEADS * t_new
    n_past_blk = past_len // MOBA_BLOCK
    pages_per_blk = MOBA_BLOCK // PAGE_SIZE
    q = q_ref[...]
    lane8 = lax.broadcasted_iota(jnp.int32, (t_new, LANES), 1)
    zeros8 = jnp.zeros((t_new, LANES), F32)
    row_blocks = []
    for h in range(N_HEADS):
        g = h // 2
        tile = q[:, g * LANES:(g + 1) * LANES]
        if h % 2 != g % 2:
            tile = pltpu.roll(tile, HEAD_DIM, 1)
        keep = (lane8 < HEAD_DIM) if g % 2 == 0 else (lane8 >= HEAD_DIM)
        tile = jnp.where(keep, tile, 0.0)
        row_blocks.append(jnp.concatenate([tile if cidx == g // 2 else zeros8 for cidx in range(kvw // LANES)], axis=1))
    qm = jnp.concatenate(row_blocks, axis=0)
    qmb = (qm * (HEAD_DIM ** -0.5)).astype(BF16)

    r1 = lax.broadcasted_iota(jnp.int32, (rows, 1), 0)
    slope = _alibi_slope(r1 // t_new)
    rr = lax.broadcasted_iota(jnp.int32, (rows, PAGE_SIZE), 0)
    cc = lax.broadcasted_iota(jnp.int32, (rows, PAGE_SIZE), 1)
    d_page0 = (past_len + rr % t_new - cc).astype(F32)

    lane_b = lax.broadcasted_iota(jnp.int32, (kvw, LANES), 1)
    kmt = jnp.zeros((kvw, LANES), F32)
    for n in range(n_past_blk):
        acc = k_refs[n * pages_per_blk][0].reshape(kvw, PAGE_SIZE)
        for pp in range(1, pages_per_blk):
            acc = acc + k_refs[n * pages_per_blk + pp][0].reshape(kvw, PAGE_SIZE)
        mean = jnp.sum(acc, axis=1, keepdims=True) * (1.0 / MOBA_BLOCK)
        kmt = jnp.where(lane_b == n, mean, kmt)
    gate = jnp.dot(qm, kmt, precision=HIGHEST, preferred_element_type=F32)
    sel = _top_blocks(gate, n_past_blk, 1)
    lane_s = lax.broadcasted_iota(jnp.int32, sel.shape, 1)

    for p in range(n_pages):
        kp = k_refs[p][0].reshape(kvw, PAGE_SIZE).astype(BF16)
        s = jnp.dot(qmb, kp, preferred_element_type=F32)
        s = s - slope * (d_page0 - float(p * PAGE_SIZE))
        picked = jnp.sum(jnp.where(lane_s == p // pages_per_blk, sel, 0.0), axis=-1, keepdims=True) > 0.0
        s_ref[:, p * PAGE_SIZE:(p + 1) * PAGE_SIZE] = jnp.where(picked, s, -jnp.inf)
    pad = jnp.zeros((LANES - t_new, kvw), F32)
    k_new = jnp.concatenate([kn_ref[...], pad], axis=0).astype(BF16)
    v_new = jnp.concatenate([vn_ref[...], pad], axis=0).astype(BF16)
    rr_own = lax.broadcasted_iota(jnp.int32, (rows, LANES), 0)
    cc_own = lax.broadcasted_iota(jnp.int32, (rows, LANES), 1)
    d_own = (rr_own % t_new - cc_own).astype(F32)
    s = lax.dot_general(qmb, k_new, NT_DIMS, preferred_element_type=F32) - slope * d_own
    s_ref[:, past_len:] = jnp.where((d_own >= 0.0) & (cc_own < t_new), s, -jnp.inf)

    s_all = s_ref[...]
    m = jnp.max(s_all, axis=-1, keepdims=True)
    p_all = jnp.exp(s_all - m)
    l = jnp.sum(p_all, axis=-1, keepdims=True)
    pb = p_all.astype(BF16)
    out = jnp.dot(pb[:, past_len:], v_new, preferred_element_type=F32)
    for p in range(n_pages):
        vp = v_refs[p][0].reshape(kvw, PAGE_SIZE).astype(BF16)
        out = out + lax.dot_general(pb[:, p * PAGE_SIZE:(p + 1) * PAGE_SIZE], vp, NT_DIMS,
                                    preferred_element_type=F32)
    out = out / l

    lo8 = lane8 < HEAD_DIM
    for cidx in range(N_HEADS // 2):
        src = slice((cidx // 2) * LANES, (cidx // 2 + 1) * LANES)
        a = out[(2 * cidx) * t_new:(2 * cidx + 1) * t_new, src]
        b = out[(2 * cidx + 1) * t_new:(2 * cidx + 2) * t_new, src]
        if cidx % 2 == 1:
            a = pltpu.roll(a, HEAD_DIM, 1)
        else:
            b = pltpu.roll(b, HEAD_DIM, 1)
        o_ref[:, cidx * LANES:(cidx + 1) * LANES] = jnp.where(lo8, a, b)


def _attn_sample(q, k_new, v_new, cache_kt, cache_vt, page_table, t_new):
    n_seq, n_pages = page_table.shape
    d = q.shape[-1]
    kvw = k_new.shape[-1]
    rows = N_HEADS * t_new
    n_seqs = 2 if n_seq % 2 == 0 else 1

    def page_spec(j, p):
        return pl.BlockSpec((1, N_KV_HEADS, HEAD_DIM, PAGE_SIZE), lambda s, pt: (pt[n_seqs * s + j, p], 0, 0, 0))

    page_specs = [page_spec(j, p) for j in range(n_seqs) for p in range(n_pages)]
    grid_spec = pltpu.PrefetchScalarGridSpec(
        num_scalar_prefetch=1,
        grid=(n_seq // n_seqs,),
        in_specs=[pl.BlockSpec((1, n_seqs * t_new, d), lambda s, pt: (0, s, 0)),
                  pl.BlockSpec((1, n_seqs * t_new, kvw), lambda s, pt: (0, s, 0)),
                  pl.BlockSpec((1, n_seqs * t_new, kvw), lambda s, pt: (0, s, 0))] + page_specs + page_specs,
        out_specs=pl.BlockSpec((1, n_seqs * t_new, d), lambda s, pt: (0, s, 0)),
        scratch_shapes=[pltpu.VMEM((n_seqs, rows, n_pages * PAGE_SIZE + LANES), F32)],
    )
    n_page_args = n_seqs * n_pages
    return pl.pallas_call(
        functools.partial(_attn_sample_kernel, n_pages=n_pages, n_seqs=n_seqs),
        grid_spec=grid_spec,
        out_shape=jax.ShapeDtypeStruct(q.shape, F32),
        compiler_params=_cparams(("arbitrary",), VMEM_BIG_MIB),
        name="moba_sample",
    )(page_table, q, k_new, v_new, *([cache_kt] * n_page_args), *([cache_vt] * n_page_args))


def _head_sum_matrices(n_heads):
    w = n_heads * HEAD_DIM
    head_of_lane = jnp.arange(w) // HEAD_DIM
    r = (head_of_lane[:, None] == jnp.arange(LANES)[None, :]).astype(BF16)
    return r, r.T


def _prep_weights(p):
    bf = lambda x: x.astype(BF16)
    w_in = p['ssm_w_in']
    wdt = jnp.pad(w_in[:, :, D_INNER + CONV_DIM:], ((0, 0), (0, 0), (0, LANES - SSM_HEADS)))
    pad_heads = lambda x: jnp.pad(x, ((0, 0), (0, LANES - SSM_HEADS)))
    head_params = jnp.stack([pad_heads(p['ssm_dt_bias']), pad_heads(p['ssm_a_log'])], axis=1)
    head_params = jnp.pad(head_params, ((0, 0), (0, SUBLANES - 2), (0, 0)))
    router_w = jnp.concatenate([p['moe_w_grp'], p['moe_w_exp']], axis=-1)
    router_w = jnp.pad(router_w, ((0, 0), (0, 0), (0, LANES - router_w.shape[-1])))
    router_b = jnp.concatenate([p['moe_b_grp'], p['moe_b_exp']], axis=-1)
    router_b = jnp.pad(router_b, ((0, 0), (0, LANES - router_b.shape[-1])))[:, None, :]
    rq, rqt = _head_sum_matrices(N_HEADS)
    rk, rkt = _head_sum_matrices(N_KV_HEADS)
    return dict(
        wz=bf(w_in[:, :, :D_INNER]), wxbc=bf(w_in[:, :, D_INNER:D_INNER + CONV_DIM]), wdt=bf(wdt),
        head_params=head_params, d_skip=jnp.repeat(p['ssm_d'], SSM_HEAD_DIM, axis=-1)[:, None, :],
        conv_w=p['ssm_conv_w'], conv_b=p['ssm_conv_b'][:, None, :], ssm_norm_w=p['ssm_norm_w'][:, None, :],
        w_out=bf(p['ssm_w_out']),
        norm_mix_w=p['norm_mix_w'][:, None, :], norm_ffn_w=p['norm_ffn_w'][:, None, :],
        kv_norm_w=p['kv_norm_w'][None, :], wk=bf(p['w_k']), wv=bf(p['w_v']),
        k_norm_w=jnp.tile(p['k_norm_w'], N_KV_HEADS)[None, :],
        wq=bf(p['attn_w_q']), q_norm_w=jnp.tile(p['q_norm_w'], (1, N_HEADS))[:, None, :], wo=bf(p['attn_w_o']),
        router_w=router_w, router_b=router_b,
        w1=bf(p['moe_w1']), w3=bf(p['moe_w3']), w2=bf(p['moe_w2']),
        rq=rq, rqt=rqt, rk=rk, rkt=rkt,
    )


def _trunk(h, mods, kv_mod, w, *, sample, conv0=None, ssm0=None, cache=None):
    bsz, t, _ = h.shape
    tm = min(t, TOKEN_TILE)
    conv_out = []
    st_all = None
    k_new = v_new = kd = vt = kmd = None
    for layer in range(DEPTH):
        sh_m, sc_m, g_m, sh_f, sc_f, g_f = mods[layer]
        nmw = w['norm_mix_w'][layer]
        if layer < N_A_LAYERS:
            z, xbc, dtr = _in_proj(h, nmw, sc_m, sh_m, w['wz'][layer], w['wxbc'][layer], w['wdt'][layer],
                                   min(t, IN_PROJ_TILE))
            ssd_args = (w['conv_w'][layer], w['conv_b'][layer], w['head_params'][layer], w['d_skip'][layer],
                        w['ssm_norm_w'][layer])
            if sample:
                n_seq, t_new = conv0.shape[1], cache[3]
                seqs = lambda x: x.reshape(n_seq, t_new, x.shape[-1])
                conv_in = jnp.pad(conv0[layer], ((0, 0), (HALO - (CONV_WIDTH - 1), 0), (0, 0)))
                h0_all = ssm0.reshape(N_A_LAYERS, n_seq, D_INNER, D_STATE)
                g, conv8, st_all = _ssd(seqs(z), seqs(xbc), seqs(dtr), conv_in, h0_all, st_all, layer,
                                        *ssd_args, g_dtype=F32)
                g = g.reshape(1, t, D_INNER)
            else:
                conv_in = jnp.zeros((bsz, HALO, CONV_DIM), F32)
                g, conv8, st_all = _ssd(z, xbc, dtr, conv_in, None, st_all, layer, *ssd_args, g_dtype=BF16)
            conv_out.append(conv8[:, HALO - (CONV_WIDTH - 1):])
            h = _mm_res(g, w['w_out'][layer], h, g_m, tm)
        else:
            j = layer - N_A_LAYERS
            if sample:
                q = _q_proj(h, nmw, sc_m, sh_m, w['wq'][j], w['q_norm_w'][j], w['rq'], w['rqt'], tm, F32)
                o = _attn_sample(q, k_new, v_new, cache[0], cache[1], cache[2], cache[3])
            else:
                q = _q_proj(h, nmw, sc_m, sh_m, w['wq'][j], w['q_norm_w'][j], w['rq'], w['rqt'], tm, BF16,
                            out_scale=LOG2_E)
                o = _attn_prompt(q, kd, vt, kmd)
            h = _mm_res(o, w['wo'][j], h, g_m, tm)
        h = _moe(h, w['norm_ffn_w'][layer], sc_f, sh_f, g_f, w['router_w'][layer], w['router_b'][layer],
                 w['w1'][layer], w['w3'][layer], w['w2'][layer], tm)
        if layer == N_A_LAYERS - 1:
            kv_tm = min(t, KV_TILE)
            res = _kv_proj(h, w['kv_norm_w'], kv_mod[1], kv_mod[0], w['wk'], w['wv'], w['k_norm_w'],
                           w['rk'], w['rkt'], kv_tm, with_dup=not sample)
            k_new, v_new = res[0], res[1]
            if not sample:
                kd, vt = res[2], res[3]
                kmd = res[4].reshape(bsz, N_SPLIT, t // MOBA_BLOCK, N_KV_HEADS * LANES)
                kmd = jnp.pad(kmd, ((0, 0), (0, 0), (0, -kmd.shape[2] % BF16_ROWS), (0, 0)))
                kmd = kmd.reshape(bsz, -1, N_KV_HEADS * LANES).astype(BF16)
    ssm_out = st_all.reshape(N_A_LAYERS, st_all.shape[1], SSM_HEADS, SSM_HEAD_DIM, D_STATE)
    return h, jnp.stack(conv_out), ssm_out, k_new, v_new


def kernel(x_prompt, x_sample, state_conv, state_ssm, cache_k, cache_v, page_table, c_prompt, c_sample, w_mod, b_mod, norm_mix_w, norm_ffn_w, ssm_w_in, ssm_conv_w, ssm_conv_b, ssm_dt_bias, ssm_a_log, ssm_d, ssm_norm_w, ssm_w_out, kv_w_mod, kv_b_mod, kv_norm_w, w_k, w_v, k_norm_w, attn_w_q, q_norm_w, attn_w_o, moe_w_grp, moe_b_grp, moe_w_exp, moe_b_exp, moe_w1, moe_w3, moe_w2):
    params = dict(ssm_w_in=ssm_w_in, ssm_conv_w=ssm_conv_w, ssm_conv_b=ssm_conv_b, ssm_dt_bias=ssm_dt_bias,
                  ssm_a_log=ssm_a_log, ssm_d=ssm_d, ssm_norm_w=ssm_norm_w, ssm_w_out=ssm_w_out,
                  norm_mix_w=norm_mix_w, norm_ffn_w=norm_ffn_w, kv_norm_w=kv_norm_w, w_k=w_k, w_v=w_v,
                  k_norm_w=k_norm_w, attn_w_q=attn_w_q, q_norm_w=q_norm_w, attn_w_o=attn_w_o,
                  moe_w_grp=moe_w_grp, moe_b_grp=moe_b_grp, moe_w_exp=moe_w_exp, moe_b_exp=moe_b_exp,
                  moe_w1=moe_w1, moe_w3=moe_w3, moe_w2=moe_w2)
    w = _prep_weights(params)

    bp, seq, d = x_prompt.shape
    n_seq, t_new, _ = x_sample.shape
    n_pages = page_table.shape[1]
    past_len = n_pages * PAGE_SIZE
    assert seq % MOBA_BLOCK == 0
    assert past_len % MOBA_BLOCK == 0 and t_new <= MOBA_BLOCK and t_new % SUBLANES == 0

    n_c = bp + n_seq
    n_c_pad = -(-n_c // SUBLANES) * SUBLANES
    c_all = jnp.pad(jnp.concatenate([c_prompt, c_sample], axis=0), ((0, n_c_pad - n_c), (0, 0)))
    mod_all = _mod_vectors(c_all, w_mod, b_mod[:, None, :], MOD_COLS)
    kv_all = _mod_vectors(c_all, kv_w_mod[None], kv_b_mod[None, None, :], MOD_COLS)[0]

    def group_mods(lo, hi, per_token_repeat):
        def shape(x):
            if per_token_repeat:
                return jnp.repeat(x, per_token_repeat, axis=0)[None]
            return x[:, None, :]
        mods = [[shape(mod_all[l, lo:hi, k * d:(k + 1) * d]) for k in range(6)] for l in range(DEPTH)]
        kvm = [shape(kv_all[lo:hi, k * d:(k + 1) * d]) for k in range(2)]
        return mods, kvm

    mods_p, kv_p = group_mods(0, bp, 0)
    y_p, conv_p, ssm_p, k_p, v_p = _trunk(x_prompt, mods_p, kv_p, w, sample=False)

    mods_s, kv_s = group_mods(bp, n_c, t_new)
    kvw = N_KV_HEADS * HEAD_DIM
    cache = (jnp.transpose(cache_k, (0, 2, 3, 1)), jnp.transpose(cache_v, (0, 2, 3, 1)), page_table, t_new)
    y_s, conv_s, ssm_s, k_s, v_s = _trunk(x_sample.reshape(1, n_seq * t_new, d), mods_s, kv_s, w,
                                          sample=True, conv0=state_conv, ssm0=state_ssm, cache=cache)

    heads = lambda x, b, t: x.reshape(b, t, N_KV_HEADS, HEAD_DIM)
    return (y_p, y_s.reshape(n_seq, t_new, d), conv_p, ssm_p, heads(k_p, bp, seq), heads(v_p, bp, seq),
            conv_s, ssm_s, heads(k_s, n_seq, t_new), heads(v_s, n_seq, t_new))
```

```python
import functools

import jax
import jax.numpy as jnp
from jax import lax
from jax.experimental import pallas as pl
from jax.experimental.pallas import tpu as pltpu

F32 = jnp.float32
BF16 = jnp.bfloat16
HIGHEST = lax.Precision.HIGHEST

DEPTH = 4
N_A_LAYERS = 2
D_INNER = 2048
SSM_HEADS = 32
SSM_HEAD_DIM = 64
SSM_GROUPS = 4
D_STATE = 128
CONV_WIDTH = 4
CONV_DIM = D_INNER + 2 * SSM_GROUPS * D_STATE
SSD_CHUNK = 128
N_HEADS = 16
HEAD_DIM = 64
N_KV_HEADS = 8
MOBA_BLOCK = 256
MOBA_TOPK = 3
N_EXPERT_GROUPS = 4
EXPERTS_PER_GROUP = 4
PAGE_SIZE = 128
EPS = 1e-6

LANES = 128
SUBLANES = 8
BF16_ROWS = 16
MIB = 1024 * 1024
NEG_BIG = -1e30

VMEM_MIB = 48
VMEM_BIG_MIB = 56
TOKEN_TILE = 1024
IN_PROJ_TILE = 512
KV_TILE = 512
MOD_COLS = 1024
MOE_EXPERTS_PER_STEP = 8

NT_DIMS = (((1,), (1,)), ((), ()))


def _cparams(sem, vmem_mib=VMEM_MIB):
    return pltpu.CompilerParams(dimension_semantics=sem, vmem_limit_bytes=vmem_mib * MIB)


def _sigmoid(x):
    return 1.0 / (1.0 + jnp.exp(-x))


def _silu(x):
    return x * _sigmoid(x)


def _norm_mod(h, nw, sc, sh):
    ms = jnp.mean(h * h, axis=-1, keepdims=True)
    return (h * lax.rsqrt(ms + EPS)) * nw * (1.0 + sc) + sh


def _split_bf16(x):
    hi = x.astype(BF16)
    lo = (x - hi.astype(F32)).astype(BF16)
    return hi, lo


def _head_rmsnorm(x, r, rt):
    sq = x * x
    hi, lo = _split_bf16(sq)
    ss = jnp.dot(hi, r, preferred_element_type=F32) + jnp.dot(lo, r, preferred_element_type=F32)
    inv = lax.rsqrt(ss * (1.0 / HEAD_DIM) + EPS)
    ih, il = _split_bf16(inv)
    invx = jnp.dot(ih, rt, preferred_element_type=F32) + jnp.dot(il, rt, preferred_element_type=F32)
    return x * invx


def _mod_kernel(c_ref, w_ref, b_ref, o_ref):
    ca = _silu(c_ref[...])
    o_ref[0] = jnp.dot(ca, w_ref[0], precision=HIGHEST, preferred_element_type=F32) + b_ref[0]


def _mod_vectors(c_all, w, b, tn):
    n_layers, d, n = w.shape
    m = c_all.shape[0]
    return pl.pallas_call(
        _mod_kernel,
        grid=(n_layers, n // tn),
        in_specs=[pl.BlockSpec((m, d), lambda l, j: (0, 0)),
                  pl.BlockSpec((1, d, tn), lambda l, j: (l, 0, j)),
                  pl.BlockSpec((1, 1, tn), lambda l, j: (l, 0, j))],
        out_specs=pl.BlockSpec((1, m, tn), lambda l, j: (l, 0, j)),
        out_shape=jax.ShapeDtypeStruct((n_layers, m, n), F32),
        compiler_params=_cparams(("arbitrary", "arbitrary")),
        name="mod_vectors",
    )(c_all, w, b)


def _mod_spec(mod, tm):
    d = mod.shape[-1]
    if mod.shape[1] == 1:
        return pl.BlockSpec((1, 1, d), lambda b, i, *_: (b, 0, 0))
    return pl.BlockSpec((1, tm, d), lambda b, i, *_: (b, i, 0))


def _in_proj_kernel(h_ref, nw_ref, sc_ref, sh_ref, wz_ref, wx_ref, wd_ref, z_ref, xbc_ref, dt_ref):
    u = _norm_mod(h_ref[0], nw_ref[...], sc_ref[0], sh_ref[0]).astype(BF16)
    z_ref[0] = jnp.dot(u, wz_ref[...], preferred_element_type=F32)
    xbc_ref[0] = jnp.dot(u, wx_ref[...], preferred_element_type=F32)
    dt_ref[0] = jnp.dot(u, wd_ref[...], preferred_element_type=F32)


def _in_proj(h, nw, sc, sh, wz, wxbc, wdt, tm):
    bsz, t, d = h.shape
    const = lambda w: pl.BlockSpec(w.shape, lambda b, i: (0, 0))
    out = lambda w: pl.BlockSpec((1, tm, w.shape[1]), lambda b, i: (b, i, 0))
    return pl.pallas_call(
        _in_proj_kernel,
        grid=(bsz, t // tm),
        in_specs=[pl.BlockSpec((1, tm, d), lambda b, i: (b, i, 0)), pl.BlockSpec((1, d), lambda b, i: (0, 0)),
                  _mod_spec(sc, tm), _mod_spec(sh, tm), const(wz), const(wxbc), const(wdt)],
        out_specs=[out(wz), out(wxbc), out(wdt)],
        out_shape=[jax.ShapeDtypeStruct((bsz, t, w.shape[1]), F32) for w in (wz, wxbc, wdt)],
        compiler_params=_cparams(("parallel", "parallel"), VMEM_BIG_MIB),
        name="ssm_in_proj",
    )(h, nw, sc, sh, wz, wxbc, wdt)


def _mm_res_kernel(a_ref, w_ref, h_ref, g_ref, o_ref):
    acc = jnp.dot(a_ref[0].astype(BF16), w_ref[...], preferred_element_type=F32)
    o_ref[0] = h_ref[0] + g_ref[0] * acc


def _mm_res(a, w, h, gate, tm):
    bsz, t, k = a.shape
    d = w.shape[1]
    return pl.pallas_call(
        _mm_res_kernel,
        grid=(bsz, t // tm),
        in_specs=[pl.BlockSpec((1, tm, k), lambda b, i: (b, i, 0)),
                  pl.BlockSpec((k, d), lambda b, i: (0, 0)),
                  pl.BlockSpec((1, tm, d), lambda b, i: (b, i, 0)),
                  _mod_spec(gate, tm)],
        out_specs=pl.BlockSpec((1, tm, d), lambda b, i: (b, i, 0)),
        out_shape=jax.ShapeDtypeStruct((bsz, t, d), F32),
        compiler_params=_cparams(("parallel", "parallel")),
        name="matmul_residual",
    )(a, w, h, gate)


HALO = SUBLANES
XC_TILE = 512


def _ssd_kernel(z_ref, xbc_ref, dt_ref, cs_ref, h0_ref, cw_ref, cb_ref, hp_ref, dsk_ref, nw_ref,
                *rest, seqs, valid, has_init, has_prev):
    g_ref, cso_ref, st_ref, *scratch = rest[1:] if has_prev else rest
    for j in range(seqs):
        one = lambda ref: ref.at[pl.ds(j, 1)]
        _ssd_one(one(z_ref), one(xbc_ref), one(dt_ref), one(cs_ref), one(h0_ref) if has_init else h0_ref,
                 cw_ref, cb_ref, hp_ref, dsk_ref, nw_ref, one(g_ref), one(cso_ref), one(st_ref),
                 *[ref.at[j] for ref in scratch], valid=valid, has_init=has_init)


def _ssd_one(z_ref, xbc_ref, dt_ref, cs_ref, h0_ref, cw_ref, cb_ref, hp_ref, dsk_ref, nw_ref,
             g_ref, cso_ref, st_ref, xs_ref, xc_ref, xt_ref, y_ref, *, valid, has_init):
    L = SSD_CHUNK
    c = pl.program_id(1)

    @pl.when(c == 0)
    def _init():
        xs_ref[0:HALO, :] = cs_ref[0]
        if valid < L:
            xc_ref[...] = jnp.zeros_like(xc_ref)
        if has_init:
            st_ref[0] = h0_ref[0]
        else:
            st_ref[0] = jnp.zeros(st_ref.shape[1:], F32)

    xs_ref[HALO:HALO + valid, :] = xbc_ref[0]

    for j in range(CONV_DIM // XC_TILE):
        sl = slice(j * XC_TILE, (j + 1) * XC_TILE)
        x_rows = xs_ref[0:HALO + valid, sl]
        acc = x_rows * cw_ref[0:1, sl]
        for k in range(1, CONV_WIDTH):
            acc = pltpu.roll(acc, 1, 0) + x_rows * cw_ref[k:k + 1, sl]
        xc_ref[0:valid, sl] = _silu(acc[HALO:HALO + valid] + cb_ref[:, sl])

    cso_ref[0] = xs_ref[valid:valid + HALO, :]
    xs_ref[0:HALO, :] = xs_ref[valid:valid + HALO, :]

    dtr = dt_ref[0]
    if valid < L:
        dtr = jnp.concatenate([dtr, jnp.zeros((L - valid, LANES), F32)], axis=0)
    xx = dtr + hp_ref[0:1, :]
    dt = jnp.maximum(xx, 0.0) + jnp.log(1.0 + jnp.exp(-jnp.abs(xx)))
    row = lax.broadcasted_iota(jnp.int32, (L, L), 0)
    col = lax.broadcasted_iota(jnp.int32, (L, L), 1)
    if valid < L:
        dt = jnp.where(row < valid, dt, 0.0)
    a = dt * (-jnp.exp(hp_ref[1:2, :]))
    causal = row >= col
    a_cum = jnp.dot(causal.astype(F32), a, precision=HIGHEST, preferred_element_type=F32)
    a_cum_t = a_cum.T
    dt_t = dt.T
    sub_lo = row < HEAD_DIM
    R = L if valid == L else -(-valid // BF16_ROWS) * BF16_ROWS
    row_r = lax.broadcasted_iota(jnp.int32, (R, L), 0)
    col_r = lax.broadcasted_iota(jnp.int32, (R, L), 1)
    causal_r = row_r >= col_r
    lane_lo = col_r < HEAD_DIM

    for j in range(D_INNER // LANES):
        xt_ref[j * LANES:(j + 1) * LANES, :] = xc_ref[:, j * LANES:(j + 1) * LANES].T

    pairs_per_group = SSM_HEADS // SSM_GROUPS // 2
    grp_w = D_INNER // SSM_GROUPS
    for g in range(SSM_GROUPS):
        b_g = xc_ref[:, D_INNER + g * D_STATE:D_INNER + (g + 1) * D_STATE].astype(BF16)
        c_g = xc_ref[:, D_INNER + (SSM_GROUPS + g) * D_STATE:
                     D_INNER + (SSM_GROUPS + g + 1) * D_STATE].astype(BF16)
        cb = lax.dot_general(c_g[0:R], b_g, NT_DIMS, preferred_element_type=F32)
        st_g = st_ref[0, g * grp_w:(g + 1) * grp_w, :]
        y_off = lax.dot_general(c_g[0:R], st_g.astype(BF16), NT_DIMS, preferred_element_type=F32)
        xw_parts, dec_parts = [], []
        for jj in range(pairs_per_group):
            pair = g * pairs_per_group + jj
            s_mats, e_cols, w_rows, d_end = [], [], [], []
            for h in (2 * pair, 2 * pair + 1):
                colb = jnp.broadcast_to(a_cum[0:R, h:h + 1], (R, L))
                rowb = a_cum_t[h:h + 1, :]
                dec = jnp.where(causal_r, jnp.exp(colb - rowb), 0.0)
                s_mats.append((cb * dec * dt_t[h:h + 1, :]).astype(BF16))
                e_cols.append(jnp.exp(colb))
                a_last = a_cum_t[h:h + 1, L - 1:L]
                w_rows.append(dt_t[h:h + 1, :] * jnp.exp(a_last - rowb))
                d_end.append(jnp.broadcast_to(jnp.exp(a_last), (LANES, D_STATE)))
            psl = slice(pair * LANES, (pair + 1) * LANES)
            xp = xc_ref[:, psl]
            xpb = xp.astype(BF16)
            y_d = jnp.where(lane_lo,
                            jnp.dot(s_mats[0], xpb, preferred_element_type=F32),
                            jnp.dot(s_mats[1], xpb, preferred_element_type=F32))
            y_o = y_off[:, jj * LANES:(jj + 1) * LANES] * jnp.where(lane_lo, e_cols[0], e_cols[1])
            y_ref[0:R, psl] = y_d + y_o + dsk_ref[:, psl] * xp[0:R]
            xw_parts.append((xt_ref[psl, :] * jnp.where(sub_lo, w_rows[0], w_rows[1])).astype(BF16))
            dec_parts.append(jnp.where(sub_lo, d_end[0], d_end[1]))
        xw_g = jnp.concatenate(xw_parts, axis=0)
        new = jnp.dot(xw_g, b_g, preferred_element_type=F32)
        st_ref[0, g * grp_w:(g + 1) * grp_w, :] = st_g * jnp.concatenate(dec_parts, axis=0) + new

    zz = z_ref[0]
    gg = y_ref[0:valid, :] * _silu(zz)
    for g in range(SSM_GROUPS):
        sl = slice(g * grp_w, (g + 1) * grp_w)
        seg = gg[:, sl]
        ms = jnp.mean(seg * seg, axis=-1, keepdims=True)
        g_ref[0, :, sl] = (seg * lax.rsqrt(ms + EPS) * nw_ref[:, sl]).astype(g_ref.dtype)


def _ssd(z, xbc, dtr, conv_in, h0_all, st_all, layer, conv_w, conv_b, head_params, d_skip, norm_w, g_dtype):
    bsz, t, _ = z.shape
    L = SSD_CHUNK
    valid = min(L, t)
    assert t % valid == 0 and valid % SUBLANES == 0
    nc = t // valid
    has_init = h0_all is not None
    seqs = 2 if bsz % 2 == 0 else 1
    state_spec = pl.BlockSpec((None, seqs, D_INNER, D_STATE), lambda b, c: (layer, b, 0, 0))
    if h0_all is None:
        h0_all = jnp.zeros((1, SUBLANES, D_STATE), F32)
        h0_spec = pl.BlockSpec((1, SUBLANES, D_STATE), lambda b, c: (0, 0, 0))
    else:
        h0_spec = state_spec
    full = lambda shape: pl.BlockSpec(shape, lambda b, c: (0,) * len(shape))
    operands = [z, xbc, dtr, conv_in, h0_all, conv_w, conv_b, head_params, d_skip, norm_w]
    in_specs = [pl.BlockSpec((seqs, valid, D_INNER), lambda b, c: (b, c, 0)),
                pl.BlockSpec((seqs, valid, CONV_DIM), lambda b, c: (b, c, 0)),
                pl.BlockSpec((seqs, valid, LANES), lambda b, c: (b, c, 0)),
                pl.BlockSpec((seqs, HALO, CONV_DIM), lambda b, c: (b, 0, 0)),
                h0_spec,
                full((CONV_WIDTH, CONV_DIM)), full((1, CONV_DIM)), full((SUBLANES, LANES)),
                full((1, D_INNER)), full((1, D_INNER))]
    aliases = {}
    if st_all is not None:
        aliases = {len(operands): 2}
        operands.append(st_all)
        in_specs.append(pl.BlockSpec(memory_space=pl.ANY))
    return pl.pallas_call(
        functools.partial(_ssd_kernel, seqs=seqs, valid=valid, has_init=has_init, has_prev=st_all is not None),
        grid=(bsz // seqs, nc),
        in_specs=in_specs,
        out_specs=[pl.BlockSpec((seqs, valid, D_INNER), lambda b, c: (b, c, 0)),
                   pl.BlockSpec((seqs, HALO, CONV_DIM), lambda b, c: (b, 0, 0)),
                   state_spec],
        out_shape=[jax.ShapeDtypeStruct((bsz, t, D_INNER), g_dtype),
                   jax.ShapeDtypeStruct((bsz, HALO, CONV_DIM), F32),
                   jax.ShapeDtypeStruct((N_A_LAYERS, bsz, D_INNER, D_STATE), F32)],
        input_output_aliases=aliases,
        scratch_shapes=[pltpu.VMEM((seqs, L + HALO, CONV_DIM), F32),
                        pltpu.VMEM((seqs, L, CONV_DIM), F32),
                        pltpu.VMEM((seqs, D_INNER, L), F32),
                        pltpu.VMEM((seqs, L, D_INNER), F32)],
        compiler_params=_cparams(("parallel", "arbitrary")),
        name="ssd_chunk_scan",
    )(*operands)


ROUTER_EXP0 = N_EXPERT_GROUPS


def _route(logits):
    lane = lax.broadcasted_iota(jnp.int32, logits.shape, 1)
    big = jnp.int32(2 * LANES)
    is_grp = lane < N_EXPERT_GROUPS
    gl = jnp.where(is_grp, logits, -jnp.inf)
    gmax = jnp.max(gl, axis=-1, keepdims=True)
    gidx = jnp.min(jnp.where(gl == gmax, lane, big), axis=-1, keepdims=True)
    p_grp = 1.0 / jnp.sum(jnp.where(is_grp, jnp.exp(gl - gmax), 0.0), axis=-1, keepdims=True)
    e_rel = lane - ROUTER_EXP0
    in_grp = (e_rel >= gidx * EXPERTS_PER_GROUP) & (e_rel < (gidx + 1) * EXPERTS_PER_GROUP)
    el = jnp.where(in_grp, logits, -jnp.inf)
    m1 = jnp.max(el, axis=-1, keepdims=True)
    i1 = jnp.min(jnp.where(el == m1, lane, big), axis=-1, keepdims=True)
    el2 = jnp.where(lane == i1, -jnp.inf, el)
    m2 = jnp.max(el2, axis=-1, keepdims=True)
    i2 = jnp.min(jnp.where(el2 == m2, lane, big), axis=-1, keepdims=True)
    e2 = jnp.exp(m2 - m1)
    den = 1.0 + e2
    w1 = (1.0 / den) * p_grp
    w2 = (e2 / den) * p_grp
    return jnp.where(lane == i1, w1, jnp.where(lane == i2, w2, 0.0))


def _moe_kernel(h_ref, nw_ref, sc_ref, sh_ref, gf_ref, wr_ref, br_ref, w1_ref, w3_ref, w2_ref,
                o_ref, u_ref, gates_ref, acc_ref):
    step = pl.program_id(2)
    per_step = w1_ref.shape[0]

    @pl.when(step == 0)
    def _():
        u = _norm_mod(h_ref[0], nw_ref[...], sc_ref[0], sh_ref[0])
        u_hi, u_lo = _split_bf16(u)
        w_hi, w_lo = _split_bf16(wr_ref[...])
        logits = (jnp.dot(u_hi, w_hi, preferred_element_type=F32) + jnp.dot(u_lo, w_hi, preferred_element_type=F32)
                  + jnp.dot(u_hi, w_lo, preferred_element_type=F32)) + br_ref[...]
        gates_ref[...] = _route(logits)
        u_ref[...] = u_hi
        acc_ref[...] = jnp.zeros_like(acc_ref)

    u = u_ref[...]
    gates = gates_ref[...]
    lane = lax.broadcasted_iota(jnp.int32, gates.shape, 1)
    acts = []
    for j in range(per_step):
        a = (_silu(jnp.dot(u, w1_ref[j], preferred_element_type=F32))
             * jnp.dot(u, w3_ref[j], preferred_element_type=F32))
        e = step * per_step + j
        gcol = jnp.sum(jnp.where(lane == e + ROUTER_EXP0, gates, 0.0), axis=-1, keepdims=True)
        acts.append((a * gcol).astype(BF16))
    w2 = w2_ref[...].reshape(per_step * w2_ref.shape[1], w2_ref.shape[2])
    acc_ref[...] += jnp.dot(jnp.concatenate(acts, axis=1), w2, preferred_element_type=F32)

    @pl.when(step == pl.num_programs(2) - 1)
    def _():
        o_ref[0] = h_ref[0] + gf_ref[0] * acc_ref[...]


def _moe(h, nw, sc, sh, gf, wr, br, w1, w3, w2, tm):
    bsz, t, d = h.shape
    n_e, _, f = w1.shape
    per_step = MOE_EXPERTS_PER_STEP if sc.shape[1] == 1 else EXPERTS_PER_GROUP
    return pl.pallas_call(
        _moe_kernel,
        grid=(bsz, t // tm, n_e // per_step),
        in_specs=[pl.BlockSpec((1, tm, d), lambda b, i, e: (b, i, 0)),
                  pl.BlockSpec((1, d), lambda b, i, e: (0, 0)),
                  _mod_spec(sc, tm), _mod_spec(sh, tm), _mod_spec(gf, tm),
                  pl.BlockSpec((d, LANES), lambda b, i, e: (0, 0)),
                  pl.BlockSpec((1, LANES), lambda b, i, e: (0, 0)),
                  pl.BlockSpec((per_step, d, f), lambda b, i, e: (e, 0, 0)),
                  pl.BlockSpec((per_step, d, f), lambda b, i, e: (e, 0, 0)),
                  pl.BlockSpec((per_step, f, d), lambda b, i, e: (e, 0, 0))],
        out_specs=pl.BlockSpec((1, tm, d), lambda b, i, e: (b, i, 0)),
        out_shape=jax.ShapeDtypeStruct((bsz, t, d), F32),
        scratch_shapes=[pltpu.VMEM((tm, d), BF16), pltpu.VMEM((tm, LANES), F32), pltpu.VMEM((tm, d), F32)],
        compiler_params=_cparams(("parallel", "parallel", "arbitrary"), VMEM_BIG_MIB),
        name="hmoe",
    )(h, nw, sc, sh, gf, wr, br, w1, w3, w2)


N_SPLIT = 3
FEAT_POS_LO = HEAD_DIM
FEAT_POS_HI = FEAT_POS_LO + N_SPLIT
FEAT_ONE = FEAT_POS_HI + N_SPLIT


def _split3(x):
    h1 = x.astype(BF16).astype(F32)
    r = x - h1
    h2 = r.astype(BF16).astype(F32)
    h3 = (r - h2).astype(BF16).astype(F32)
    return h1, h2, h3


def _feature_lanes(lane, base, parts, other):
    out = other
    for k, part in enumerate(parts):
        out = jnp.where(lane == base + k, part, out)
    return out


def _key_tiles(x, feat):
    lane = lax.broadcasted_iota(jnp.int32, x.shape, 1)
    rolled = pltpu.roll(x, HEAD_DIM, 1)
    lo = lane < HEAD_DIM
    return jnp.where(lo, x, feat), jnp.where(lo, rolled, feat)


def _kv_kernel(h_ref, nw_ref, sc_ref, sh_ref, wk_ref, wv_ref, knw_ref, r_ref, rt_ref,
               kf_ref, vf_ref, *dup_refs, tm, with_dup):
    u = _norm_mod(h_ref[0], nw_ref[...], sc_ref[0], sh_ref[0]).astype(BF16)
    k = jnp.dot(u, wk_ref[...], preferred_element_type=F32)
    v = jnp.dot(u, wv_ref[...], preferred_element_type=F32)
    kn = _head_rmsnorm(k, r_ref[...], rt_ref[...]) * knw_ref[...]
    if not with_dup:
        kf_ref[0] = kn
        vf_ref[0] = v
    else:
        for hh in range(N_KV_HEADS):
            rows = pl.ds(hh, tm, stride=N_KV_HEADS)
            for src, dst in ((kn, kf_ref), (v, vf_ref)):
                tile = src[:, (hh // 2) * LANES:(hh // 2 + 1) * LANES]
                if hh % 2 == 1:
                    tile = pltpu.roll(tile, HEAD_DIM, 1)
                dst[0, rows, :] = tile[:, :HEAD_DIM]
        kd_ref, vt_ref, km_ref = dup_refs
        lane = lax.broadcasted_iota(jnp.int32, (tm, LANES), 1)
        pos = pl.program_id(1) * tm + lax.broadcasted_iota(jnp.int32, (tm, LANES), 0)
        pos_lo = pos % MOBA_BLOCK
        feat = jnp.zeros((tm, LANES), F32)
        feat = _feature_lanes(lane, FEAT_POS_LO, [pos_lo.astype(F32)] * N_SPLIT, feat)
        feat = _feature_lanes(lane, FEAT_POS_HI, [(pos - pos_lo).astype(F32)] * N_SPLIT, feat)
        feat = _feature_lanes(lane, FEAT_ONE, [jnp.ones((tm, LANES), F32)] * N_SPLIT, feat)
        lane_row = lax.broadcasted_iota(jnp.int32, (1, LANES), 1)
        for cidx in range(N_KV_HEADS // 2):
            d0, d1 = _key_tiles(kn[:, cidx * LANES:(cidx + 1) * LANES], feat)
            for hh, dd in ((2 * cidx, d0), (2 * cidx + 1, d1)):
                hsl = slice(hh * LANES, (hh + 1) * LANES)
                kd_ref[0, :, hsl] = dd.astype(BF16)
                for blk in range(tm // MOBA_BLOCK):
                    mean = jnp.mean(dd[blk * MOBA_BLOCK:(blk + 1) * MOBA_BLOCK], axis=0, keepdims=True)
                    for part, piece in enumerate(_split3(jnp.where(lane_row < HEAD_DIM, mean, 0.0))):
                        km_ref[0, part, blk, :, hsl] = piece
        for blk in range(tm // MOBA_BLOCK):
            vt_ref[0, blk] = v[blk * MOBA_BLOCK:(blk + 1) * MOBA_BLOCK].T.astype(BF16)


def _kv_proj(h, nw, sc, sh, wk, wv, knw, r, rt, tm, with_dup):
    bsz, t, d = h.shape
    kvw = wk.shape[1]
    const = lambda shape: pl.BlockSpec(shape, lambda b, i: (0,) * len(shape))
    if not with_dup:
        out_specs = [pl.BlockSpec((1, tm, kvw), lambda b, i: (b, i, 0))] * 2
        out_shape = [jax.ShapeDtypeStruct((bsz, t, kvw), F32)] * 2
    else:
        out_specs = [pl.BlockSpec((1, tm * N_KV_HEADS, HEAD_DIM), lambda b, i: (b, i, 0))] * 2
        out_shape = [jax.ShapeDtypeStruct((bsz, t * N_KV_HEADS, HEAD_DIM), F32)] * 2
        dupw = N_KV_HEADS * LANES
        nblk = tm // MOBA_BLOCK
        out_specs += [pl.BlockSpec((1, tm, dupw), lambda b, i: (b, i, 0)),
                      pl.BlockSpec((1, nblk, kvw, MOBA_BLOCK), lambda b, i: (b, i, 0, 0)),
                      pl.BlockSpec((1, N_SPLIT, nblk, 1, dupw), lambda b, i: (b, 0, i, 0, 0))]
        out_shape += [jax.ShapeDtypeStruct((bsz, t, dupw), BF16),
                      jax.ShapeDtypeStruct((bsz, t // MOBA_BLOCK, kvw, MOBA_BLOCK), BF16),
                      jax.ShapeDtypeStruct((bsz, N_SPLIT, t // MOBA_BLOCK, 1, dupw), F32)]
    return pl.pallas_call(
        functools.partial(_kv_kernel, tm=tm, with_dup=with_dup),
        grid=(bsz, t // tm),
        in_specs=[pl.BlockSpec((1, tm, d), lambda b, i: (b, i, 0)),
                  const((1, d)), _mod_spec(sc, tm), _mod_spec(sh, tm),
                  const((d, kvw)), const((d, kvw)), const((1, kvw)),
                  const((kvw, LANES)), const((LANES, kvw))],
        out_specs=out_specs,
        out_shape=out_shape,
        compiler_params=_cparams(("parallel", "parallel")),
        name="shared_kv",
    )(h, nw, sc, sh, wk, wv, knw, r, rt)


def _q_kernel(h_ref, nw_ref, sc_ref, sh_ref, wq_ref, qnw_ref, r_ref, rt_ref, q_ref, *, out_scale):
    u = _norm_mod(h_ref[0], nw_ref[...], sc_ref[0], sh_ref[0]).astype(BF16)
    q = jnp.dot(u, wq_ref[...], preferred_element_type=F32)
    q = _head_rmsnorm(q, r_ref[...], rt_ref[...]) * qnw_ref[...]
    if out_scale != 1.0:
        q = q * out_scale
    q_ref[0] = q.astype(q_ref.dtype)


def _q_proj(h, nw, sc, sh, wq, qnw, r, rt, tm, out_dtype, out_scale=1.0):
    bsz, t, d = h.shape
    const = lambda shape: pl.BlockSpec(shape, lambda b, i: (0,) * len(shape))
    return pl.pallas_call(
        functools.partial(_q_kernel, out_scale=out_scale),
        grid=(bsz, t // tm),
        in_specs=[pl.BlockSpec((1, tm, d), lambda b, i: (b, i, 0)),
                  const((1, d)), _mod_spec(sc, tm), _mod_spec(sh, tm),
                  const((d, d)), const((1, d)), const((d, LANES)), const((LANES, d))],
        out_specs=pl.BlockSpec((1, tm, d), lambda b, i: (b, i, 0)),
        out_shape=jax.ShapeDtypeStruct((bsz, t, d), out_dtype),
        compiler_params=_cparams(("parallel", "parallel")),
        name="q_proj",
    )(h, nw, sc, sh, wq, qnw, r, rt)


def _top_blocks(gate, n_valid, axis):
    pos = lax.broadcasted_iota(jnp.int32, gate.shape, axis)
    gm = jnp.where(pos < n_valid, gate, -jnp.inf)
    sel = jnp.zeros(gate.shape, F32)
    for _ in range(MOBA_TOPK):
        mx = jnp.max(gm, axis=axis, keepdims=True)
        cand = (gm == mx) & (mx > -jnp.inf)
        idx = jnp.min(jnp.where(cand, pos, jnp.int32(2 * LANES)), axis=axis, keepdims=True)
        pick = pos == idx
        sel = jnp.where(pick, 1.0, sel)
        gm = jnp.where(pick, -jnp.inf, gm)
    return sel


def _alibi_slope(head):
    return jnp.exp2(-8.0 * (head + 1).astype(F32) / N_HEADS)


ATT_GROUP = 2
ACC_ROWS = HEAD_DIM + BF16_ROWS
LOG2_E = 1.4426950408889634


def _attn_prompt_kernel(q_ref, k_ref, vt_ref, km_ref, o_ref,
                        qt_ref, fq_ref, sa_ref, sb_ref, sel_ref, m_ref, acc_ref):
    g = pl.program_id(1)
    i = pl.program_id(2)
    blk = MOBA_BLOCK
    cols = 2 * blk
    grp = ATT_GROUP

    l2 = lax.broadcasted_iota(jnp.int32, (cols, LANES), 1)

    @pl.when(i == 0)
    def _features():
        r2 = lax.broadcasted_iota(jnp.int32, (cols, LANES), 0)
        second = r2 >= blk
        slope = _alibi_slope(2 * g + second.astype(jnp.int32)) * LOG2_E
        offs_q = jnp.where(second, r2 - blk, r2).astype(F32)
        slope_parts = _split3(slope)
        feat = jnp.zeros((cols, LANES), F32)
        feat = _feature_lanes(l2, FEAT_POS_LO, slope_parts, feat)
        feat = _feature_lanes(l2, FEAT_POS_HI, slope_parts, feat)
        fq_ref[...] = _feature_lanes(l2, FEAT_ONE, _split3(-slope * offs_q), feat)

    c1 = lax.broadcasted_iota(jnp.int32, (1, cols), 1)
    col_const = -(_alibi_slope(2 * g + (c1 >= blk).astype(jnp.int32)) * LOG2_E) * (i * blk).astype(F32)

    qt = q_ref[0].astype(F32)
    q2 = jnp.concatenate([qt, pltpu.roll(qt, HEAD_DIM, 1)], axis=0)
    q2t = jnp.where(l2 < HEAD_DIM, q2, fq_ref[...]).T
    is_q = lax.broadcasted_iota(jnp.int32, (LANES, 1), 0) < HEAD_DIM
    qt_ref[...] = (q2t * jnp.where(is_q, HEAD_DIM ** -0.5, 1.0)).astype(BF16)
    gate3 = jnp.dot(km_ref[0], qt_ref[...], preferred_element_type=F32)
    nbp = sel_ref.shape[0]
    gate = gate3[0:nbp] + gate3[nbp:2 * nbp] + gate3[2 * nbp:3 * nbp]
    sel_ref[...] = _top_blocks(gate, i, 0)

    def scores(n):
        start = pl.multiple_of(n * blk, blk)
        return jnp.dot(k_ref[0, pl.ds(start, blk), :], qt_ref[...], preferred_element_type=F32)

    ones_rows = jnp.ones((ACC_ROWS - HEAD_DIM, blk), BF16)

    def block_stats(s, n):
        vt_aug = jnp.concatenate([vt_ref[0, n], ones_rows], axis=0)
        maxima, sums = [], []
        for head in range(2):
            sh = s[:, head * blk:(head + 1) * blk]
            mh = jnp.max(sh, axis=0, keepdims=True)
            maxima.append(mh)
            sums.append(jnp.dot(vt_aug, jnp.exp2((sh - mh).astype(BF16)), preferred_element_type=F32))
        return jnp.concatenate(maxima, axis=1), jnp.concatenate(sums, axis=1)

    def merge(parts):
        m_old = m_ref[...]
        m_new = m_old
        for mb, _ in parts:
            m_new = jnp.maximum(m_new, mb)
        acc_new = jnp.exp2(m_old - m_new) * acc_ref[...]
        for mb, ab in parts:
            acc_new = acc_new + jnp.exp2(mb - m_new) * ab
        m_ref[...] = m_new
        acc_ref[...] = acc_new

    rr = lax.broadcasted_iota(jnp.int32, (blk, cols), 0)
    cc = lax.broadcasted_iota(jnp.int32, (blk, cols), 1)
    visible = jnp.where(cc >= blk, cc - blk, cc) >= rr
    m0, a0 = block_stats(jnp.where(visible, scores(i), -jnp.inf), i)
    m_ref[...] = m0 + col_const
    acc_ref[...] = a0

    n_groups = (i + grp - 1) // grp

    def clamp(n_raw):
        return jnp.maximum(jnp.minimum(n_raw, i - 1), 0)

    def score_group(kk, s_ref):
        for b in range(grp):
            s_ref[b * blk:(b + 1) * blk, :] = scores(clamp(kk * grp + b))

    def softmax_group(kk, s_ref):
        parts = []
        for b in range(grp):
            n_raw = kk * grp + b
            n = clamp(n_raw)
            mb, ab = block_stats(s_ref[b * blk:(b + 1) * blk, :], n)
            counted = sel_ref[pl.ds(n, 1), :] * (n_raw < i).astype(F32) > 0.0
            parts.append((jnp.where(counted, mb + col_const, NEG_BIG), ab))
        merge(parts)

    score_group(0, sa_ref)

    def step(kk, carry):
        score_group(2 * kk + 1, sb_ref)
        softmax_group(2 * kk, sa_ref)
        score_group(2 * kk + 2, sa_ref)
        softmax_group(2 * kk + 1, sb_ref)
        return carry

    lax.fori_loop(0, n_groups // 2, step, 0)

    @pl.when(n_groups % 2 == 1)
    def _last():
        softmax_group(n_groups - 1, sa_ref)

    o = acc_ref[0:HEAD_DIM, :] / acc_ref[HEAD_DIM:HEAD_DIM + 1, :]
    o_ref[0] = jnp.concatenate([o[:, 0:blk], o[:, blk:cols]], axis=0).T.astype(o_ref.dtype)


def _attn_prompt(q, kd, vt, kmd):
    bsz, t, d = q.shape
    blk = MOBA_BLOCK
    nb = t // blk
    nb_pad = kmd.shape[1] // N_SPLIT
    cols = 2 * blk
    return pl.pallas_call(
        _attn_prompt_kernel,
        grid=(bsz, N_KV_HEADS, nb),
        in_specs=[pl.BlockSpec((1, blk, LANES), lambda b, g, i: (b, i, g)),
                  pl.BlockSpec((1, t, LANES), lambda b, g, i: (b, 0, g)),
                  pl.BlockSpec((1, nb, HEAD_DIM, blk), lambda b, g, i: (b, 0, g, 0)),
                  pl.BlockSpec((1, N_SPLIT * nb_pad, LANES), lambda b, g, i: (b, 0, g))],
        out_specs=pl.BlockSpec((1, blk, LANES), lambda b, g, i: (b, i, g)),
        out_shape=jax.ShapeDtypeStruct((bsz, t, d), BF16),
        scratch_shapes=[pltpu.VMEM((LANES, cols), BF16), pltpu.VMEM((cols, LANES), F32),
                        pltpu.VMEM((ATT_GROUP * blk, cols), F32), pltpu.VMEM((ATT_GROUP * blk, cols), F32),
                        pltpu.VMEM((nb_pad, cols), F32), pltpu.VMEM((1, cols), F32),
                        pltpu.VMEM((ACC_ROWS, cols), F32)],
        compiler_params=_cparams(("parallel", "parallel", "arbitrary")),
        name="moba_prompt",
    )(q, kd, vt, kmd)


def _attn_sample_kernel(pt_ref, q_ref, kn_ref, vn_ref, *rest, n_pages, n_seqs):
    del pt_ref
    t_new = q_ref.shape[1] // n_seqs
    k_refs = rest[:n_seqs * n_pages]
    v_refs = rest[n_seqs * n_pages:2 * n_seqs * n_pages]
    o_ref, s_ref = rest[2 * n_seqs * n_pages:]
    for j in range(n_seqs):
        rows = slice(j * t_new, (j + 1) * t_new)
        pages = slice(j * n_pages, (j + 1) * n_pages)
        _attn_sample_one(q_ref.at[0, rows, :], kn_ref.at[0, rows, :], vn_ref.at[0, rows, :],
                         k_refs[pages], v_refs[pages], o_ref.at[0, rows, :], s_ref.at[j])


def _attn_sample_one(q_ref, kn_ref, vn_ref, k_refs, v_refs, o_ref, s_ref):
    n_pages = len(k_refs)
    t_new = q_ref.shape[0]
    past_len = n_pages * PAGE_SIZE
    kvw = N_KV_HEADS * HEAD_DIM
    rows = N_HEADS * t_new
    n_past_blk = past_len // MOBA_BLOCK
    pages_per_blk = MOBA_BLOCK // PAGE_SIZE
    q = q_ref[...]
    lane8 = lax.broadcasted_iota(jnp.int32, (t_new, LANES), 1)
    zeros8 = jnp.zeros((t_new, LANES), F32)
    row_blocks = []
    for h in range(N_HEADS):
        g = h // 2
        tile = q[:, g * LANES:(g + 1) * LANES]
        if h % 2 != g % 2:
            tile = pltpu.roll(tile, HEAD_DIM, 1)
        keep = (lane8 < HEAD_DIM) if g % 2 == 0 else (lane8 >= HEAD_DIM)
        tile = jnp.where(keep, tile, 0.0)
        row_blocks.append(jnp.concatenate([tile if cidx == g // 2 else zeros8 for cidx in range(kvw // LANES)], axis=1))
    qm = jnp.concatenate(row_blocks, axis=0)
    qmb = (qm * (HEAD_DIM ** -0.5)).astype(BF16)

    r1 = lax.broadcasted_iota(jnp.int32, (rows, 1), 0)
    slope = _alibi_slope(r1 // t_new)
    rr = lax.broadcasted_iota(jnp.int32, (rows, PAGE_SIZE), 0)
    cc = lax.broadcasted_iota(jnp.int32, (rows, PAGE_SIZE), 1)
    d_page0 = (past_len + rr % t_new - cc).astype(F32)

    lane_b = lax.broadcasted_iota(jnp.int32, (kvw, LANES), 1)
    kmt = jnp.zeros((kvw, LANES), F32)
    for n in range(n_past_blk):
        acc = k_refs[n * pages_per_blk][0].reshape(kvw, PAGE_SIZE)
        for pp in range(1, pages_per_blk):
            acc = acc + k_refs[n * pages_per_blk + pp][0].reshape(kvw, PAGE_SIZE)
        mean = jnp.sum(acc, axis=1, keepdims=True) * (1.0 / MOBA_BLOCK)
        kmt = jnp.where(lane_b == n, mean, kmt)
    gate = jnp.dot(qm, kmt, precision=HIGHEST, preferred_element_type=F32)
    sel = _top_blocks(gate, n_past_blk, 1)
    lane_s = lax.broadcasted_iota(jnp.int32, sel.shape, 1)

    for p in range(n_pages):
        kp = k_refs[p][0].reshape(kvw, PAGE_SIZE).astype(BF16)
        s = jnp.dot(qmb, kp, preferred_element_type=F32)
        s = s - slope * (d_page0 - float(p * PAGE_SIZE))
        picked = jnp.sum(jnp.where(lane_s == p // pages_per_blk, sel, 0.0), axis=-1, keepdims=True) > 0.0
        s_ref[:, p * PAGE_SIZE:(p + 1) * PAGE_SIZE] = jnp.where(picked, s, -jnp.inf)
    pad = jnp.zeros((LANES - t_new, kvw), F32)
    k_new = jnp.concatenate([kn_ref[...], pad], axis=0).astype(BF16)
    v_new = jnp.concatenate([vn_ref[...], pad], axis=0).astype(BF16)
    rr_own = lax.broadcasted_iota(jnp.int32, (rows, LANES), 0)
    cc_own = lax.broadcasted_iota(jnp.int32, (rows, LANES), 1)
    d_own = (rr_own % t_new - cc_own).astype(F32)
    s = lax.dot_general(qmb, k_new, NT_DIMS, preferred_element_type=F32) - slope * d_own
    s_ref[:, past_len:] = jnp.where((d_own >= 0.0) & (cc_own < t_new), s, -jnp.inf)

    s_all = s_ref[...]
    m = jnp.max(s_all, axis=-1, keepdims=True)
    p_all = jnp.exp(s_all - m)
    l = jnp.sum(p_all, axis=-1, keepdims=True)
    pb = p_all.astype(BF16)
    out = jnp.dot(pb[:, past_len:], v_new, preferred_element_type=F32)
    for p in range(n_pages):
        vp = v_refs[p][0].reshape(kvw, PAGE_SIZE).astype(BF16)
        out = out + lax.dot_general(pb[:, p * PAGE_SIZE:(p + 1) * PAGE_SIZE], vp, NT_DIMS,
                                    preferred_element_type=F32)
    out = out / l

    lo8 = lane8 < HEAD_DIM
    for cidx in range(N_HEADS // 2):
        src = slice((cidx // 2) * LANES, (cidx // 2 + 1) * LANES)
        a = out[(2 * cidx) * t_new:(2 * cidx + 1) * t_new, src]
        b = out[(2 * cidx + 1) * t_new:(2 * cidx + 2) * t_new, src]
        if cidx % 2 == 1:
            a = pltpu.roll(a, HEAD_DIM, 1)
        else:
            b = pltpu.roll(b, HEAD_DIM, 1)
        o_ref[:, cidx * LANES:(cidx + 1) * LANES] = jnp.where(lo8, a, b)


def _attn_sample(q, k_new, v_new, cache_kt, cache_vt, page_table, t_new):
    n_seq, n_pages = page_table.shape
    d = q.shape[-1]
    kvw = k_new.shape[-1]
    rows = N_HEADS * t_new
    n_seqs = 2 if n_seq % 2 == 0 else 1

    def page_spec(j, p):
        return pl.BlockSpec((1, N_KV_HEADS, HEAD_DIM, PAGE_SIZE), lambda s, pt: (pt[n_seqs * s + j, p], 0, 0, 0))

    page_specs = [page_spec(j, p) for j in range(n_seqs) for p in range(n_pages)]
    grid_spec = pltpu.PrefetchScalarGridSpec(
        num_scalar_prefetch=1,
        grid=(n_seq // n_seqs,),
        in_specs=[pl.BlockSpec((1, n_seqs * t_new, d), lambda s, pt: (0, s, 0)),
                  pl.BlockSpec((1, n_seqs * t_new, kvw), lambda s, pt: (0, s, 0)),
                  pl.BlockSpec((1, n_seqs * t_new, kvw), lambda s, pt: (0, s, 0))] + page_specs + page_specs,
        out_specs=pl.BlockSpec((1, n_seqs * t_new, d), lambda s, pt: (0, s, 0)),
        scratch_shapes=[pltpu.VMEM((n_seqs, rows, n_pages * PAGE_SIZE + LANES), F32)],
    )
    n_page_args = n_seqs * n_pages
    return pl.pallas_call(
        functools.partial(_attn_sample_kernel, n_pages=n_pages, n_seqs=n_seqs),
        grid_spec=grid_spec,
        out_shape=jax.ShapeDtypeStruct(q.shape, F32),
        compiler_params=_cparams(("arbitrary",), VMEM_BIG_MIB),
        name="moba_sample",
    )(page_table, q, k_new, v_new, *([cache_kt] * n_page_args), *([cache_vt] * n_page_args))


def _head_sum_matrices(n_heads):
    w = n_heads * HEAD_DIM
    head_of_lane = jnp.arange(w) // HEAD_DIM
    r = (head_of_lane[:, None] == jnp.arange(LANES)[None, :]).astype(BF16)
    return r, r.T


def _prep_weights(p):
    bf = lambda x: x.astype(BF16)
    w_in = p['ssm_w_in']
    wdt = jnp.pad(w_in[:, :, D_INNER + CONV_DIM:], ((0, 0), (0, 0), (0, LANES - SSM_HEADS)))
    pad_heads = lambda x: jnp.pad(x, ((0, 0), (0, LANES - SSM_HEADS)))
    head_params = jnp.stack([pad_heads(p['ssm_dt_bias']), pad_heads(p['ssm_a_log'])], axis=1)
    head_params = jnp.pad(head_params, ((0, 0), (0, SUBLANES - 2), (0, 0)))
    router_w = jnp.concatenate([p['moe_w_grp'], p['moe_w_exp']], axis=-1)
    router_w = jnp.pad(router_w, ((0, 0), (0, 0), (0, LANES - router_w.shape[-1])))
    router_b = jnp.concatenate([p['moe_b_grp'], p['moe_b_exp']], axis=-1)
    router_b = jnp.pad(router_b, ((0, 0), (0, LANES - router_b.shape[-1])))[:, None, :]
    rq, rqt = _head_sum_matrices(N_HEADS)
    rk, rkt = _head_sum_matrices(N_KV_HEADS)
    return dict(
        wz=bf(w_in[:, :, :D_INNER]), wxbc=bf(w_in[:, :, D_INNER:D_INNER + CONV_DIM]), wdt=bf(wdt),
        head_params=head_params, d_skip=jnp.repeat(p['ssm_d'], SSM_HEAD_DIM, axis=-1)[:, None, :],
        conv_w=p['ssm_conv_w'], conv_b=p['ssm_conv_b'][:, None, :], ssm_norm_w=p['ssm_norm_w'][:, None, :],
        w_out=bf(p['ssm_w_out']),
        norm_mix_w=p['norm_mix_w'][:, None, :], norm_ffn_w=p['norm_ffn_w'][:, None, :],
        kv_norm_w=p['kv_norm_w'][None, :], wk=bf(p['w_k']), wv=bf(p['w_v']),
        k_norm_w=jnp.tile(p['k_norm_w'], N_KV_HEADS)[None, :],
        wq=bf(p['attn_w_q']), q_norm_w=jnp.tile(p['q_norm_w'], (1, N_HEADS))[:, None, :], wo=bf(p['attn_w_o']),
        router_w=router_w, router_b=router_b,
        w1=bf(p['moe_w1']), w3=bf(p['moe_w3']), w2=bf(p['moe_w2']),
        rq=rq, rqt=rqt, rk=rk, rkt=rkt,
    )


def _trunk(h, mods, kv_mod, w, *, sample, conv0=None, ssm0=None, cache=None):
    bsz, t, _ = h.shape
    tm = min(t, TOKEN_TILE)
    conv_out = []
    st_all = None
    k_new = v_new = kd = vt = kmd = None
    for layer in range(DEPTH):
        sh_m, sc_m, g_m, sh_f, sc_f, g_f = mods[layer]
        nmw = w['norm_mix_w'][layer]
        if layer < N_A_LAYERS:
            z, xbc, dtr = _in_proj(h, nmw, sc_m, sh_m, w['wz'][layer], w['wxbc'][layer], w['wdt'][layer],
                                   min(t, IN_PROJ_TILE))
            ssd_args = (w['conv_w'][layer], w['conv_b'][layer], w['head_params'][layer], w['d_skip'][layer],
                        w['ssm_norm_w'][layer])
            if sample:
                n_seq, t_new = conv0.shape[1], cache[3]
                seqs = lambda x: x.reshape(n_seq, t_new, x.shape[-1])
                conv_in = jnp.pad(conv0[layer], ((0, 0), (HALO - (CONV_WIDTH - 1), 0), (0, 0)))
                h0_all = ssm0.reshape(N_A_LAYERS, n_seq, D_INNER, D_STATE)
                g, conv8, st_all = _ssd(seqs(z), seqs(xbc), seqs(dtr), conv_in, h0_all, st_all, layer,
                                        *ssd_args, g_dtype=F32)
                g = g.reshape(1, t, D_INNER)
            else:
                conv_in = jnp.zeros((bsz, HALO, CONV_DIM), F32)
                g, conv8, st_all = _ssd(z, xbc, dtr, conv_in, None, st_all, layer, *ssd_args, g_dtype=BF16)
            conv_out.append(conv8[:, HALO - (CONV_WIDTH - 1):])
            h = _mm_res(g, w['w_out'][layer], h, g_m, tm)
        else:
            j = layer - N_A_LAYERS
            if sample:
                q = _q_proj(h, nmw, sc_m, sh_m, w['wq'][j], w['q_norm_w'][j], w['rq'], w['rqt'], tm, F32)
                o = _attn_sample(q, k_new, v_new, cache[0], cache[1], cache[2], cache[3])
            else:
                q = _q_proj(h, nmw, sc_m, sh_m, w['wq'][j], w['q_norm_w'][j], w['rq'], w['rqt'], tm, BF16,
                            out_scale=LOG2_E)
                o = _attn_prompt(q, kd, vt, kmd)
            h = _mm_res(o, w['wo'][j], h, g_m, tm)
        h = _moe(h, w['norm_ffn_w'][layer], sc_f, sh_f, g_f, w['router_w'][layer], w['router_b'][layer],
                 w['w1'][layer], w['w3'][layer], w['w2'][layer], tm)
        if layer == N_A_LAYERS - 1:
            kv_tm = min(t, KV_TILE)
            res = _kv_proj(h, w['kv_norm_w'], kv_mod[1], kv_mod[0], w['wk'], w['wv'], w['k_norm_w'],
                           w['rk'], w['rkt'], kv_tm, with_dup=not sample)
            k_new, v_new = res[0], res[1]
            if not sample:
                kd, vt = res[2], res[3]
                kmd = res[4].reshape(bsz, N_SPLIT, t // MOBA_BLOCK, N_KV_HEADS * LANES)
                kmd = jnp.pad(kmd, ((0, 0), (0, 0), (0, -kmd.shape[2] % BF16_ROWS), (0, 0)))
                kmd = kmd.reshape(bsz, -1, N_KV_HEADS * LANES).astype(BF16)
    ssm_out = st_all.reshape(N_A_LAYERS, st_all.shape[1], SSM_HEADS, SSM_HEAD_DIM, D_STATE)
    return h, jnp.stack(conv_out), ssm_out, k_new, v_new


def kernel(x_prompt, x_sample, state_conv, state_ssm, cache_k, cache_v, page_table, c_prompt, c_sample, w_mod, b_mod, norm_mix_w, norm_ffn_w, ssm_w_in, ssm_conv_w, ssm_conv_b, ssm_dt_bias, ssm_a_log, ssm_d, ssm_norm_w, ssm_w_out, kv_w_mod, kv_b_mod, kv_norm_w, w_k, w_v, k_norm_w, attn_w_q, q_norm_w, attn_w_o, moe_w_grp, moe_b_grp, moe_w_exp, moe_b_exp, moe_w1, moe_w3, moe_w2):
    params = dict(ssm_w_in=ssm_w_in, ssm_conv_w=ssm_conv_w, ssm_conv_b=ssm_conv_b, ssm_dt_bias=ssm_dt_bias,
                  ssm_a_log=ssm_a_log, ssm_d=ssm_d, ssm_norm_w=ssm_norm_w, ssm_w_out=ssm_w_out,
                  norm_mix_w=norm_mix_w, norm_ffn_w=norm_ffn_w, kv_norm_w=kv_norm_w, w_k=w_k, w_v=w_v,
                  k_norm_w=k_norm_w, attn_w_q=attn_w_q, q_norm_w=q_norm_w, attn_w_o=attn_w_o,
                  moe_w_grp=moe_w_grp, moe_b_grp=moe_b_grp, moe_w_exp=moe_w_exp, moe_b_exp=moe_b_exp,
                  moe_w1=moe_w1, moe_w3=moe_w3, moe_w2=moe_w2)
    w = _prep_weights(params)

    bp, seq, d = x_prompt.shape
    n_seq, t_new, _ = x_sample.shape
    n_pages = page_table.shape[1]
    past_len = n_pages * PAGE_SIZE
    assert seq % MOBA_BLOCK == 0
    assert past_len % MOBA_BLOCK == 0 and t_new <= MOBA_BLOCK and t_new % SUBLANES == 0

    n_c = bp + n_seq
    n_c_pad = -(-n_c // SUBLANES) * SUBLANES
    c_all = jnp.pad(jnp.concatenate([c_prompt, c_sample], axis=0), ((0, n_c_pad - n_c), (0, 0)))
    mod_all = _mod_vectors(c_all, w_mod, b_mod[:, None, :], MOD_COLS)
    kv_all = _mod_vectors(c_all, kv_w_mod[None], kv_b_mod[None, None, :], MOD_COLS)[0]

    def group_mods(lo, hi, per_token_repeat):
        def shape(x):
            if per_token_repeat:
                return jnp.repeat(x, per_token_repeat, axis=0)[None]
            return x[:, None, :]
        mods = [[shape(mod_all[l, lo:hi, k * d:(k + 1) * d]) for k in range(6)] for l in range(DEPTH)]
        kvm = [shape(kv_all[lo:hi, k * d:(k + 1) * d]) for k in range(2)]
        return mods, kvm

    mods_p, kv_p = group_mods(0, bp, 0)
    y_p, conv_p, ssm_p, k_p, v_p = _trunk(x_prompt, mods_p, kv_p, w, sample=False)

    mods_s, kv_s = group_mods(bp, n_c, t_new)
    kvw = N_KV_HEADS * HEAD_DIM
    cache = (jnp.transpose(cache_k, (0, 2, 3, 1)), jnp.transpose(cache_v, (0, 2, 3, 1)), page_table, t_new)
    y_s, conv_s, ssm_s, k_s, v_s = _trunk(x_sample.reshape(1, n_seq * t_new, d), mods_s, kv_s, w,
                                          sample=True, conv0=state_conv, ssm0=state_ssm, cache=cache)

    heads = lambda x, b, t: x.reshape(b, t, N_KV_HEADS, HEAD_DIM)
    return (y_p, y_s.reshape(n_seq, t_new, d), conv_p, ssm_p, heads(k_p, bp, seq), heads(v_p, bp, seq),
            conv_s, ssm_s, heads(k_s, n_seq, t_new), heads(v_s, n_seq, t_new))
```
